```python
import math
import jax, jax.numpy as jnp
from jax import lax
import numpy as np

D_MODEL = 1024
BATCH = 8
SEQ = 2048
DEPTH = 2

N_MIXERS = 2
N_HEADS = 16
HEAD_DIM = 64
ATTN_WIDTH = N_HEADS * HEAD_DIM
FOX_BLOCK = 128
FOX_IN = 4 * ATTN_WIDTH + N_HEADS
NSA_GROUPS = 4
NSA_REP = N_HEADS // NSA_GROUPS
KV_WIDTH = NSA_GROUPS * HEAD_DIM
CMP_LEN = 32
CMP_STRIDE = 16
CMP_HIDDEN = 256
SEL_LEN = 64
N_SELECT = 8
SEL_QCHUNK = 32
WINDOW = 512
WIN_BLOCK = 128
N_BRANCH = 3
NSA_IN = 2 * ATTN_WIDTH + 6 * KV_WIDTH + N_BRANCH * N_HEADS
ROPE_THETA = 500000.0
ROPE_DIM = HEAD_DIM // 4
NORM_EPS = 1e-6
NEG = -1e30
FORCE = 1e6
N_FOX = (DEPTH + 1) // 2
N_NSA = DEPTH // 2

kernel_name = "fox_nsa_interleaved_hybrid"


def rmsnorm(x, g):
    xf = x.astype(jnp.float32)
    y = xf * lax.rsqrt(jnp.mean(xf * xf, axis=-1, keepdims=True) + NORM_EPS)
    return (y * g.astype(jnp.float32)).astype(x.dtype)


def rope_partial(x, pos):
    half = ROPE_DIM // 2
    inv_freq = jnp.power(ROPE_THETA, -jnp.arange(half, dtype=jnp.float32) * (2.0 / ROPE_DIM))
    ang = pos.astype(jnp.float32)[:, None] * inv_freq[None, :]
    shape = (pos.shape[0],) + (1,) * (x.ndim - 3) + (half,)
    cos = jnp.cos(ang).reshape(shape)
    sin = jnp.sin(ang).reshape(shape)
    xr = x[..., :ROPE_DIM].astype(jnp.float32)
    x1, x2 = xr[..., :half], xr[..., half:]
    rot = jnp.concatenate([x1 * cos - x2 * sin, x1 * sin + x2 * cos], axis=-1).astype(x.dtype)
    return jnp.concatenate([rot, x[..., ROPE_DIM:]], axis=-1)


def fox_mixer(h, w_in, b_f, w_out):
    B, S, _ = h.shape
    W = ATTN_WIDTH
    proj = h @ w_in
    q, k, v, f, z = jnp.split(proj, [W, 2 * W, 3 * W, 3 * W + N_HEADS], axis=-1)
    q = q.reshape(B, S, N_HEADS, HEAD_DIM)
    k = k.reshape(B, S, N_HEADS, HEAD_DIM)
    v = v.reshape(B, S, N_HEADS, HEAD_DIM)
    log_f = jax.nn.log_sigmoid(f.astype(jnp.float32) + b_f.astype(jnp.float32))
    c = jnp.cumsum(log_f, axis=1).transpose(0, 2, 1)
    scale = HEAD_DIM ** -0.5
    outs = []
    for qb in range(S // FOX_BLOCK):
        s0, s1 = qb * FOX_BLOCK, (qb + 1) * FOX_BLOCK
        logits = jnp.einsum('bqhd,bkhd->bhqk', q[:, s0:s1], k[:, :s1]).astype(jnp.float32) * scale
        logits = logits + c[:, :, s0:s1, None] - c[:, :, None, :s1]
        tq = jnp.arange(s0, s1)
        tk = jnp.arange(s1)
        logits = jnp.where(tk[None, :] <= tq[:, None], logits, NEG)
        p = jax.nn.softmax(logits, axis=-1).astype(v.dtype)
        outs.append(jnp.einsum('bhqk,bkhd->bqhd', p, v[:, :s1]))
    o = jnp.concatenate(outs, axis=1).reshape(B, S, W)
    return (o * jax.nn.silu(z)) @ w_out


def compress_blocks(x, pe, w1, w2):
    B, S, G, D = x.shape
    n_chunk = S // CMP_STRIDE
    r = CMP_LEN // CMP_STRIDE
    n_cmp = n_chunk - r + 1
    ch = x.reshape(B, n_chunk, CMP_STRIDE, G, D)
    blk = jnp.concatenate([ch[:, j:j + n_cmp] for j in range(r)], axis=2)
    blk = blk + pe[None, None, :, None, :]
    blk = blk.transpose(0, 1, 3, 2, 4).reshape(B, n_cmp, G, CMP_LEN * D)
    return jax.nn.silu(blk @ w1) @ w2


def nsa_mixer(h, w_in, pe_k, w_ck1, w_ck2, pe_v, w_cv1, w_cv2, w_out):
    B, S, _ = h.shape
    G, R, D, W = NSA_GROUPS, NSA_REP, HEAD_DIM, ATTN_WIDTH
    proj = h @ w_in
    offs = [W + i * KV_WIDTH for i in range(1, 7)] + [W + 6 * KV_WIDTH + N_BRANCH * N_HEADS]
    q, kc_raw, vc_raw, ks_raw, vs_raw, kw_raw, vw_raw, gate_raw, z = jnp.split(proj, [W] + offs, axis=-1)
    pos = jnp.arange(S)
    q = rope_partial(q.reshape(B, S, G, R, D), pos)
    scale = D ** -0.5

    kc = compress_blocks(kc_raw.reshape(B, S, G, D), pe_k, w_ck1, w_ck2)
    vc = compress_blocks(vc_raw.reshape(B, S, G, D), pe_v, w_cv1, w_cv2)
    n_cmp = kc.shape[1]
    cmp_end = jnp.arange(n_cmp) * CMP_STRIDE + CMP_LEN - 1
    kc = rope_partial(kc, cmp_end)
    lg_c = jnp.einsum('bsgrd,bcgd->bgrsc', q, kc).astype(jnp.float32) * scale
    mask_c = cmp_end[None, :] <= pos[:, None]
    p_c = jax.nn.softmax(jnp.where(mask_c, lg_c, NEG), axis=-1)
    p_c = jnp.where(mask_c, p_c, 0.0)
    o_c = jnp.einsum('bgrsc,bcgd->bsgrd', p_c.astype(vc.dtype), vc)

    n_sel_blk = S // SEL_LEN
    n_sel = min(N_SELECT, n_sel_blk)
    ci = jnp.arange(n_cmp) * CMP_STRIDE
    sj = jnp.arange(n_sel_blk) * SEL_LEN
    overlap = ((ci[:, None] < sj[None, :] + SEL_LEN) & (ci[:, None] + CMP_LEN > sj[None, :])).astype(jnp.float32)
    imp = jnp.einsum('bgsc,cj->bgsj', p_c.sum(axis=2), overlap)
    cur = pos // SEL_LEN
    blk_ids = jnp.arange(n_sel_blk)
    forced = (blk_ids[None, :] == 0) | (blk_ids[None, :] == cur[:, None]) | (blk_ids[None, :] == cur[:, None] - 1)
    causal = blk_ids[None, :] <= cur[:, None]
    imp = jnp.where(forced, FORCE, jnp.where(causal, imp, -1.0))
    _, idx = lax.top_k(imp, n_sel)

    ks = rope_partial(ks_raw.reshape(B, S, G, D), pos)
    vs = vs_raw.reshape(B, S, G, D)
    k_blocks = ks.reshape(B, n_sel_blk, SEL_LEN, G, D).transpose(0, 3, 1, 2, 4)
    v_blocks = vs.reshape(B, n_sel_blk, SEL_LEN, G, D).transpose(0, 3, 1, 2, 4)
    n_qc = S // SEL_QCHUNK
    q_ch = q.reshape(B, n_qc, SEL_QCHUNK, G, R, D).transpose(1, 0, 2, 3, 4, 5)
    idx_ch = idx.transpose(0, 2, 1, 3).reshape(B, n_qc, SEL_QCHUNK, G, n_sel).transpose(1, 0, 2, 3, 4)
    pos_ch = pos.reshape(n_qc, SEL_QCHUNK)
    bi = jnp.arange(B)[:, None, None, None]
    gi = jnp.arange(G)[None, None, :, None]
    l_off = jnp.arange(SEL_LEN)

    def sel_chunk(args):
        qc, ic, tc = args
        kb = k_blocks[bi, gi, ic]
        vb = v_blocks[bi, gi, ic]
        lg = jnp.einsum('btgrd,btgnld->btgrnl', qc, kb).astype(jnp.float32) * scale
        kpos = ic[..., None] * SEL_LEN + l_off
        m = (kpos <= tc[None, :, None, None, None])[:, :, :, None]
        lg = jnp.where(m, lg, NEG)
        p = jax.nn.softmax(lg.reshape(B, SEL_QCHUNK, G, R, n_sel * SEL_LEN), axis=-1)
        p = p.reshape(B, SEL_QCHUNK, G, R, n_sel, SEL_LEN).astype(vb.dtype)
        return jnp.einsum('btgrnl,btgnld->btgrd', p, vb)

    o_s = lax.map(sel_chunk, (q_ch, idx_ch, pos_ch))
    o_s = o_s.transpose(1, 0, 2, 3, 4, 5).reshape(B, S, G, R, D)

    kw = rope_partial(kw_raw.reshape(B, S, G, D), pos)
    vw = vw_raw.reshape(B, S, G, D)
    n_qb = S // WIN_BLOCK
    n_wb = -(-WINDOW // WIN_BLOCK)
    pad = n_wb * WIN_BLOCK
    band = (n_wb + 1) * WIN_BLOCK
    kpad = jnp.pad(kw, ((0, 0), (pad, 0), (0, 0), (0, 0))).reshape(B, n_qb + n_wb, WIN_BLOCK, G, D)
    vpad = jnp.pad(vw, ((0, 0), (pad, 0), (0, 0), (0, 0))).reshape(B, n_qb + n_wb, WIN_BLOCK, G, D)
    k_band = jnp.concatenate([kpad[:, j:j + n_qb] for j in range(n_wb + 1)], axis=2)
    v_band = jnp.concatenate([vpad[:, j:j + n_qb] for j in range(n_wb + 1)], axis=2)
    qb = q.reshape(B, n_qb, WIN_BLOCK, G, R, D)
    lg_w = jnp.einsum('bnqgrd,bnkgd->bngrqk', qb, k_band).astype(jnp.float32) * scale
    tq = pos.reshape(n_qb, WIN_BLOCK)
    tk = jnp.arange(n_qb)[:, None] * WIN_BLOCK - pad + jnp.arange(band)[None, :]
    m_w = (tk[:, None, :] <= tq[:, :, None]) & (tk[:, None, :] > tq[:, :, None] - WINDOW) & (tk[:, None, :] >= 0)
    lg_w = jnp.where(m_w[None, :, None, None], lg_w, NEG)
    p_w = jax.nn.softmax(lg_w, axis=-1).astype(v_band.dtype)
    o_w = jnp.einsum('bngrqk,bnkgd->bnqgrd', p_w, v_band).reshape(B, S, G, R, D)

    g = jax.nn.sigmoid(gate_raw.astype(jnp.float32)).reshape(B, S, G, R, N_BRANCH).astype(o_c.dtype)
    o = g[..., 0:1] * o_c + g[..., 1:2] * o_s + g[..., 2:3] * o_w
    return (o.reshape(B, S, W) * jax.nn.silu(z)) @ w_out


def setup_inputs(seed: int = 0) -> dict:
    key = jax.random.key(seed)
    ks = jax.random.split(key, 16)
    f32 = jnp.float32
    W = ATTN_WIDTH
    cdim = CMP_LEN * HEAD_DIM
    return {
        "x": jax.random.normal(ks[0], (BATCH, SEQ, D_MODEL), f32),
        "norm_g": 1.0 + 0.1 * jax.random.normal(ks[1], (DEPTH, D_MODEL), f32),
        "fox_w_in": jax.random.normal(ks[2], (N_FOX, D_MODEL, FOX_IN), f32) * D_MODEL ** -0.5,
        "fox_b_f": jax.random.uniform(ks[3], (N_FOX, N_HEADS), f32, 1.0, 6.0),
        "fox_w_out": jax.random.normal(ks[4], (N_FOX, W, D_MODEL), f32) * W ** -0.5,
        "nsa_w_in": jax.random.normal(ks[5], (N_NSA, D_MODEL, NSA_IN), f32) * D_MODEL ** -0.5,
        "nsa_pe_k": 0.1 * jax.random.normal(ks[6], (N_NSA, CMP_LEN, HEAD_DIM), f32),
        "nsa_w_ck1": jax.random.normal(ks[7], (N_NSA, cdim, CMP_HIDDEN), f32) * cdim ** -0.5,
        "nsa_w_ck2": jax.random.normal(ks[8], (N_NSA, CMP_HIDDEN, HEAD_DIM), f32) * CMP_HIDDEN ** -0.5,
        "nsa_pe_v": 0.1 * jax.random.normal(ks[9], (N_NSA, CMP_LEN, HEAD_DIM), f32),
        "nsa_w_cv1": jax.random.normal(ks[10], (N_NSA, cdim, CMP_HIDDEN), f32) * cdim ** -0.5,
        "nsa_w_cv2": jax.random.normal(ks[11], (N_NSA, CMP_HIDDEN, HEAD_DIM), f32) * CMP_HIDDEN ** -0.5,
        "nsa_w_out": jax.random.normal(ks[12], (N_NSA, W, D_MODEL), f32) * W ** -0.5,
        "final_g": 1.0 + 0.1 * jax.random.normal(ks[13], (D_MODEL,), f32),
    }


def reference(x, norm_g, fox_w_in, fox_b_f, fox_w_out, nsa_w_in, nsa_pe_k, nsa_w_ck1, nsa_w_ck2,
              nsa_pe_v, nsa_w_cv1, nsa_w_cv2, nsa_w_out, final_g):
    for i in range(DEPTH):
        h = rmsnorm(x, norm_g[i])
        j = i // N_MIXERS
        if i % N_MIXERS == 0:
            y = fox_mixer(h, fox_w_in[j], fox_b_f[j], fox_w_out[j])
        else:
            y = nsa_mixer(h, nsa_w_in[j], nsa_pe_k[j], nsa_w_ck1[j], nsa_w_ck2[j],
                          nsa_pe_v[j], nsa_w_cv1[j], nsa_w_cv2[j], nsa_w_out[j])
        x = x + y
    return rmsnorm(x, final_g)
```

```python
import functools

import jax
import jax.numpy as jnp
from jax import lax
from jax.experimental import pallas as pl
from jax.experimental.pallas import tpu as pltpu

F32 = jnp.float32
BF16 = jnp.bfloat16

D_MODEL = 1024
N_HEADS = 16
HEAD_DIM = 64
ATTN_WIDTH = N_HEADS * HEAD_DIM
N_PAIRS = N_HEADS // 2
NSA_GROUPS = 4
KV_WIDTH = NSA_GROUPS * HEAD_DIM
CMP_LEN = 32
CMP_STRIDE = 16
CMP_HIDDEN = 256
SEL_LEN = 64
N_SELECT = 8
WINDOW = 512
N_BRANCH = 3
ROPE_THETA = 500000.0
ROPE_DIM = HEAD_DIM // 4
ROPE_HALF = ROPE_DIM // 2
NORM_EPS = 1e-6
NEG = -1e30
FORCE = 1e6
SCALE = HEAD_DIM ** -0.5

LANES = 128
PROJ_ROWS = 512
ATTN_TQ = 256
ATTN_TK = 256
VMEM_LIMIT = 56 * 1024 * 1024


def _params(*sem):
    return pltpu.CompilerParams(dimension_semantics=sem, vmem_limit_bytes=VMEM_LIMIT)


def _iota(shape, dim):
    return lax.broadcasted_iota(jnp.int32, shape, dim)


def _split2(x):
    hi = x.astype(BF16)
    lo = (x - hi.astype(F32)).astype(BF16)
    return hi, lo


def _split3(x):
    hi = x.astype(BF16)
    r1 = x - hi.astype(F32)
    mid = r1.astype(BF16)
    lo = (r1 - mid.astype(F32)).astype(BF16)
    return hi, mid, lo


def _dot(a, b):
    return jnp.dot(a, b, preferred_element_type=F32)


def _dot_nt(a, b):
    return lax.dot_general(a, b, (((1,), (1,)), ((), ())), preferred_element_type=F32)


def _rmsnorm(x, g):
    ms = jnp.mean(x * x, axis=-1, keepdims=True)
    return x * lax.rsqrt(ms + NORM_EPS) * g


def _silu(x):
    return x * (1.0 / (1.0 + jnp.exp(-x)))


def _rope(x, cos, sin_lo, sin_hi):
    return (x * cos + pltpu.roll(x, LANES - ROPE_HALF, axis=1) * sin_lo
            + pltpu.roll(x, ROPE_HALF, axis=1) * sin_hi)


def _rope_tables(pos):
    inv_freq = jnp.power(ROPE_THETA, -jnp.arange(ROPE_HALF, dtype=F32) * (2.0 / ROPE_DIM))
    ang = pos.astype(F32)[:, None] * inv_freq[None, :]
    cos, sin = jnp.cos(ang), jnp.sin(ang)
    n = pos.shape[0]
    ones = jnp.ones((n, HEAD_DIM - ROPE_DIM), F32)
    zeros = jnp.zeros((n, HEAD_DIM - ROPE_DIM), F32)
    z8 = jnp.zeros((n, ROPE_HALF), F32)
    c = jnp.concatenate([cos, cos, ones], axis=1)
    s_lo = jnp.concatenate([-sin, z8, zeros], axis=1)
    s_hi = jnp.concatenate([z8, sin, zeros], axis=1)
    tile = lambda t: jnp.concatenate([t, t], axis=1)
    return tile(c), tile(s_lo), tile(s_hi)


def _fox_in_kernel(x_ref, g_ref, w_ref, wfh_ref, wfl_ref, bf_ref, tri_ref,
                   qkvz_ref, cx_ref, carry_ref):
    s = pl.program_id(1)
    h = _rmsnorm(x_ref[0], g_ref[...])
    hb = h.astype(BF16)
    n_out = w_ref.shape[1]
    for n0 in range(0, n_out, 512):
        qkvz_ref[0, :, n0:n0 + 512] = _dot(hb, w_ref[:, n0:n0 + 512]).astype(BF16)

    h_lo = (h - hb.astype(F32)).astype(BF16)
    f = _dot(hb, wfh_ref[...]) + _dot(h_lo, wfh_ref[...]) + _dot(hb, wfl_ref[...])
    f = f + bf_ref[...]
    log_f = jnp.minimum(f, 0.0) - jnp.log1p(jnp.exp(-jnp.abs(f)))

    @pl.when(s == 0)
    def _():
        carry_ref[...] = jnp.zeros_like(carry_ref)

    a0, a1, a2 = _split3(log_f)
    tri = tri_ref[...]
    c = _dot(tri, a0) + _dot(tri, a1) + _dot(tri, a2) + carry_ref[0:1, :]
    rows = c.shape[0]
    carry_ref[...] = jnp.broadcast_to(c[rows - 1:rows, :], carry_ref.shape)

    hi, mid, lo = _split3(c)
    lane = _iota(c.shape, 1)
    zero = jnp.zeros_like(hi)
    cx_ref[0] = jnp.where(lane < N_HEADS, hi,
                          jnp.where(lane < 2 * N_HEADS, mid,
                                    jnp.where(lane < 3 * N_HEADS, lo, zero)))


def _fox_in(x, g, w_main, wf_hi, wf_lo, bf_row):
    B, S, _ = x.shape
    tm = min(PROJ_ROWS, S)
    n_out = w_main.shape[1]
    tri = (jnp.arange(tm)[:, None] >= jnp.arange(tm)[None, :]).astype(BF16)
    const = lambda b, s: (0, 0)
    return pl.pallas_call(
        _fox_in_kernel,
        name="fox_in",
        grid=(B, S // tm),
        in_specs=[
            pl.BlockSpec((1, tm, D_MODEL), lambda b, s: (b, s, 0)),
            pl.BlockSpec((1, D_MODEL), const),
            pl.BlockSpec((D_MODEL, n_out), const),
            pl.BlockSpec((D_MODEL, LANES), const),
            pl.BlockSpec((D_MODEL, LANES), const),
            pl.BlockSpec((1, LANES), const),
            pl.BlockSpec((tm, tm), const),
        ],
        out_specs=[
            pl.BlockSpec((1, tm, n_out), lambda b, s: (b, s, 0)),
            pl.BlockSpec((1, tm, LANES), lambda b, s: (b, s, 0)),
        ],
        out_shape=[
            jax.ShapeDtypeStruct((B, S, n_out), BF16),
            jax.ShapeDtypeStruct((B, S, LANES), BF16),
        ],
        scratch_shapes=[pltpu.VMEM((8, LANES), F32)],
        compiler_params=_params("parallel", "arbitrary"),
    )(x, g, w_main, wf_hi, wf_lo, bf_row, tri)


def _flash_kernel(*refs, tq, tk, has_extra, window):
    if has_extra:
        q_ref, qxa_ref, qxb_ref, k_ref, kx_ref, v_ref, o_ref = refs
    else:
        q_ref, k_ref, v_ref, o_ref = refs
    i = pl.program_id(2)
    q2 = q_ref[0]
    lane = _iota(q2.shape, 1)
    zero = jnp.zeros_like(q2)
    qs = [jnp.where(lane < HEAD_DIM, q2, zero), jnp.where(lane >= HEAD_DIM, q2, zero)]
    if has_extra:
        qs = [jnp.concatenate([qs[0], qxa_ref[0]], axis=1),
              jnp.concatenate([qs[1], qxb_ref[0]], axis=1)]

    def tile(j, carry, masked):
        start = pl.multiple_of(j * tk, tk)
        kt = k_ref[0, pl.ds(start, tk), :]
        if has_extra:
            kt = jnp.concatenate([kt, kx_ref[0, pl.ds(start, tk), :]], axis=1)
        vt = v_ref[0, pl.ds(start, tk), :]
        out = []
        for qq, (m, l, acc) in zip(qs, carry):
            s = _dot_nt(qq, kt)
            if masked:
                row = i * tq + _iota(s.shape, 0)
                col = j * tk + _iota(s.shape, 1)
                ok = col <= row
                if window:
                    ok = jnp.logical_and(ok, col > row - window)
                s = jnp.where(ok, s, NEG)
            m_new = jnp.maximum(m, jnp.max(s, axis=1, keepdims=True))
            alpha = jnp.exp(m - m_new)
            p = jnp.exp(s - m_new)
            l = alpha * l + jnp.sum(p, axis=1, keepdims=True)
            acc = alpha * acc + _dot(p.astype(BF16), vt)
            out.append((m_new, l, acc))
        return tuple(out)

    init = tuple((jnp.full((tq, 1), NEG, F32), jnp.zeros((tq, 1), F32),
                  jnp.zeros((tq, LANES), F32)) for _ in range(2))
    ratio = tq // tk
    hi = i * ratio
    if window:
        lo = jnp.maximum(hi - window // tk, 0)
    else:
        lo = 0
    carry = lax.fori_loop(lo, hi, lambda j, c: tile(j, c, bool(window)), init)
    for d in range(ratio):
        carry = tile(hi + d, carry, True)
    (_, la, acca), (_, lb, accb) = carry
    o = jnp.where(lane < HEAD_DIM, acca * (1.0 / la), accb * (1.0 / lb))
    o_ref[0] = o.astype(o_ref.dtype)


def _flash(q_arr, q_col0, k_arr, k_col_fn, v_arr, v_col_fn, *, extra=None, window=0):
    B, S, _ = q_arr.shape
    tq, tk = min(ATTN_TQ, S), min(ATTN_TK, S)
    in_specs = [pl.BlockSpec((1, tq, LANES), lambda b, p, i: (b, i, q_col0 + p))]
    args = [q_arr]
    if extra is not None:
        qxa, qxa_map, qxb, qxb_map, kx, kx_map = extra
        in_specs += [pl.BlockSpec((1, tq, LANES), qxa_map), pl.BlockSpec((1, tq, LANES), qxb_map)]
        args += [qxa, qxb]
    in_specs.append(pl.BlockSpec((1, S, LANES), lambda b, p, i: (b, 0, k_col_fn(p))))
    args.append(k_arr)
    if extra is not None:
        in_specs.append(pl.BlockSpec((1, S, LANES), kx_map))
        args.append(kx)
    in_specs.append(pl.BlockSpec((1, S, LANES), lambda b, p, i: (b, 0, v_col_fn(p))))
    args.append(v_arr)
    kern = functools.partial(_flash_kernel, tq=tq, tk=tk, has_extra=extra is not None,
                             window=window)
    return pl.pallas_call(
        kern,
        name="flash_window" if window else ("flash_bias" if extra is not None else "flash"),
        grid=(B, N_PAIRS, S // tq),
        in_specs=in_specs,
        out_specs=pl.BlockSpec((1, tq, LANES), lambda b, p, i: (b, i, p)),
        out_shape=jax.ShapeDtypeStruct((B, S, ATTN_WIDTH), BF16),
        compiler_params=_params("parallel", "parallel", "arbitrary"),
    )(*args)


def _fox_out_kernel(o_ref, z_ref, x_ref, w_ref, y_ref):
    gated = o_ref[...].astype(F32) * _silu(z_ref[...].astype(F32))
    y_ref[...] = x_ref[...] + _dot(gated.astype(BF16), w_ref[...])


def _fox_out(o, qkvz, x2d, w_out):
    M = x2d.shape[0]
    tm = min(PROJ_ROWS, M)
    z_col = (qkvz.shape[1] - ATTN_WIDTH) // ATTN_WIDTH
    return pl.pallas_call(
        _fox_out_kernel,
        name="fox_out",
        grid=(M // tm,),
        in_specs=[
            pl.BlockSpec((tm, ATTN_WIDTH), lambda m: (m, 0)),
            pl.BlockSpec((tm, ATTN_WIDTH), lambda m: (m, z_col)),
            pl.BlockSpec((tm, D_MODEL), lambda m: (m, 0)),
            pl.BlockSpec((ATTN_WIDTH, D_MODEL), lambda m: (0, 0)),
        ],
        out_specs=pl.BlockSpec((tm, D_MODEL), lambda m: (m, 0)),
        out_shape=jax.ShapeDtypeStruct((M, D_MODEL), F32),
        compiler_params=_params("parallel"),
    )(o, qkvz, x2d, w_out)


_NSA_Q = (0, ATTN_WIDTH)
_NSA_CMP = (ATTN_WIDTH, 2 * KV_WIDTH)
_NSA_SEL = (_NSA_CMP[0] + _NSA_CMP[1], 4 * KV_WIDTH)
_NSA_WIN = (_NSA_SEL[0] + _NSA_SEL[1], 4 * KV_WIDTH)
_NSA_GATE = (_NSA_WIN[0] + _NSA_WIN[1], LANES)
_NSA_Z = (_NSA_GATE[0] + _NSA_GATE[1], ATTN_WIDTH)
_NSA_COLS = _NSA_Z[0] + _NSA_Z[1]


def _nsa_in_kernel(x_ref, g_ref, w_ref, cos_ref, slo_ref, shi_ref,
                   q_ref, cmp_ref, sel_ref, win_ref, gate_ref, z_ref):
    h = _rmsnorm(x_ref[...], g_ref[...]).astype(BF16)
    cos, slo, shi = cos_ref[...], slo_ref[...], shi_ref[...]

    def project(sec, out_ref, rope_cols):
        col0, width = sec
        for n0 in range(0, width, 512):
            wd = min(512, width - n0)
            acc = _dot(h, w_ref[:, col0 + n0:col0 + n0 + wd])
            for c0 in range(0, wd, LANES):
                blk = acc[:, c0:c0 + LANES]
                if n0 + c0 < rope_cols:
                    blk = _rope(blk, cos, slo, shi)
                out_ref[:, n0 + c0:n0 + c0 + LANES] = blk.astype(out_ref.dtype)

    project(_NSA_Q, q_ref, ATTN_WIDTH)
    project(_NSA_CMP, cmp_ref, 0)
    project(_NSA_SEL, sel_ref, 2 * KV_WIDTH)
    project(_NSA_WIN, win_ref, 2 * KV_WIDTH)
    project(_NSA_Z, z_ref, 0)
    gate = _dot(h, w_ref[:, _NSA_GATE[0]:_NSA_GATE[0] + LANES])
    gate_ref[...] = 1.0 / (1.0 + jnp.exp(-gate))


def _nsa_in(x2d, g, w, seq, tables):
    M = x2d.shape[0]
    tm = min(PROJ_ROWS, seq)
    n_seq = seq // tm
    row = lambda m: (m, 0)
    const = lambda m: (0, 0)
    tab = lambda m: (m % n_seq, 0)
    widths = [ATTN_WIDTH, _NSA_CMP[1], _NSA_SEL[1], _NSA_WIN[1], LANES, ATTN_WIDTH]
    dtypes = [BF16, BF16, BF16, BF16, F32, BF16]
    return pl.pallas_call(
        _nsa_in_kernel,
        name="nsa_in",
        grid=(M // tm,),
        in_specs=[
            pl.BlockSpec((tm, D_MODEL), row),
            pl.BlockSpec((1, D_MODEL), const),
            pl.BlockSpec((D_MODEL, _NSA_COLS), const),
            pl.BlockSpec((tm, LANES), tab),
            pl.BlockSpec((tm, LANES), tab),
            pl.BlockSpec((tm, LANES), tab),
        ],
        out_specs=[pl.BlockSpec((tm, wd), row) for wd in widths],
        out_shape=[jax.ShapeDtypeStruct((M, wd), dt) for wd, dt in zip(widths, dtypes)],
        compiler_params=_params("parallel"),
    )(x2d, g, w, *tables)


def _compress_kernel(chk_ref, chv_ref, w1k_ref, w2k_ref, pek_ref, w1v_ref, w2v_ref, pev_ref,
                     cos_ref, slo_ref, shi_ref, kc_ref, vc_ref):
    half = CMP_STRIDE * HEAD_DIM

    def mlp(ch, w1_ref, w2_ref, pe_ref):
        a = _dot(ch, w1_ref[:half, :])
        b = _dot(ch, w1_ref[half:, :])
        b = pltpu.roll(b, b.shape[0] - 1, axis=0)
        pe = _dot(pe_ref[...], w1_ref[...])[0:1, :]
        hid = a + b + pe
        return _dot(_silu(hid).astype(BF16), w2_ref[...])

    kc = mlp(chk_ref[0, 0], w1k_ref, w2k_ref, pek_ref)
    kc_ref[0, 0] = _rope(kc, cos_ref[...], slo_ref[...], shi_ref[...]).astype(BF16)
    vc_ref[0, 0] = mlp(chv_ref[0, 0], w1v_ref, w2v_ref, pev_ref).astype(BF16)


def _compress(chk, chv, w1k, w2k, pek, w1v, w2v, pev, tables):
    B, G, n_chunk, width = chk.shape
    const = lambda b, g: (0, 0)
    chunk_spec = pl.BlockSpec((1, 1, n_chunk, width), lambda b, g: (b, g, 0, 0))
    w_specs = [pl.BlockSpec(w1k.shape, const), pl.BlockSpec(w2k.shape, const),
               pl.BlockSpec(pek.shape, const)]
    out_spec = pl.BlockSpec((1, 1, n_chunk, LANES), lambda b, g: (b, g, 0, 0))
    out_shape = jax.ShapeDtypeStruct((B, G, n_chunk, LANES), BF16)
    return pl.pallas_call(
        _compress_kernel,
        name="nsa_compress",
        grid=(B, G),
        in_specs=[chunk_spec, chunk_spec] + w_specs + w_specs
                 + [pl.BlockSpec((n_chunk, LANES), const)] * 3,
        out_specs=[out_spec, out_spec],
        out_shape=[out_shape, out_shape],
        compiler_params=_params("parallel", "parallel"),
    )(chk, chv, w1k, w2k, pek, w1v, w2v, pev, *tables)


def _cmp_attn_kernel(q_ref, kc_ref, vc_ref, ovt_ref, oc_ref, bias_ref, *, tq, n_cmp, n_blk):
    i = pl.program_id(2)
    kc = kc_ref[0, 0]
    vc = vc_ref[0, 0]
    n_pad = kc.shape[0]
    row_t = i * tq + _iota((tq, n_pad), 0)
    col_c = _iota((tq, n_pad), 1)
    valid = jnp.logical_and(col_c * CMP_STRIDE + (CMP_LEN - 1) <= row_t, col_c < n_cmp)
    lane = _iota((tq, LANES), 1)
    p_sum = jnp.zeros((tq, n_pad), F32)
    for pair in range(2):
        q2 = q_ref[0, :, pair * LANES:(pair + 1) * LANES]
        zero = jnp.zeros_like(q2)
        outs = []
        for qq in (jnp.where(lane < HEAD_DIM, q2, zero), jnp.where(lane >= HEAD_DIM, q2, zero)):
            s = jnp.where(valid, _dot_nt(qq, kc), NEG)
            e = jnp.exp(s - jnp.max(s, axis=1, keepdims=True))
            p = e * (1.0 / jnp.sum(e, axis=1, keepdims=True))
            p = jnp.where(valid, p, 0.0)
            p_sum = p_sum + p
            outs.append(_dot(p.astype(BF16), vc))
        oc_ref[0, :, pair * LANES:(pair + 1) * LANES] = jnp.where(
            lane < HEAD_DIM, outs[0], outs[1]).astype(oc_ref.dtype)

    ovt = ovt_ref[...]
    p_hi, p_lo = _split2(p_sum)
    imp = _dot_nt(ovt, p_hi) + _dot_nt(ovt, p_lo)
    blk = _iota(imp.shape, 0)
    cur = (i * tq + _iota(imp.shape, 1)) // SEL_LEN
    forced = jnp.logical_or(blk == 0, jnp.logical_or(blk == cur, blk == cur - 1))
    imp = jnp.where(forced, FORCE, jnp.where(blk <= cur, imp, -1.0))
    keep = jnp.zeros(imp.shape, jnp.bool_)
    for _ in range(min(N_SELECT, n_blk)):
        top = jnp.max(imp, axis=0, keepdims=True)
        first = jnp.min(jnp.where(imp == top, blk, n_blk), axis=0, keepdims=True)
        pick = blk == first
        keep = jnp.logical_or(keep, pick)
        imp = jnp.where(pick, -jnp.inf, imp)
    bias = jnp.where(keep, 0.0, NEG)
    bias = jnp.concatenate([bias, jnp.zeros((LANES - n_blk, tq), F32)], axis=0)
    bias_ref[0, 0] = bias.T.astype(bias_ref.dtype)


def _cmp_attn(q, kc, vc, ovt, n_cmp):
    B, S, _ = q.shape
    G = kc.shape[1]
    tq = min(ATTN_TQ, S)
    n_blk = S // SEL_LEN
    gw = ATTN_WIDTH // G
    kv_spec = pl.BlockSpec((1, 1) + kc.shape[2:], lambda b, g, i: (b, g, 0, 0))
    kern = functools.partial(_cmp_attn_kernel, tq=tq, n_cmp=n_cmp, n_blk=n_blk)
    return pl.pallas_call(
        kern,
        name="nsa_cmp_attn",
        grid=(B, G, S // tq),
        in_specs=[
            pl.BlockSpec((1, tq, gw), lambda b, g, i: (b, i, g)),
            kv_spec, kv_spec,
            pl.BlockSpec(ovt.shape, lambda b, g, i: (0, 0)),
        ],
        out_specs=[
            pl.BlockSpec((1, tq, gw), lambda b, g, i: (b, i, g)),
            pl.BlockSpec((1, 1, tq, LANES), lambda b, g, i: (b, g, i, 0)),
        ],
        out_shape=[
            jax.ShapeDtypeStruct((B, S, ATTN_WIDTH), BF16),
            jax.ShapeDtypeStruct((B, G, S, LANES), BF16),
        ],
        compiler_params=_params("parallel", "parallel", "arbitrary"),
    )(q, kc, vc, ovt)


def _nsa_out_kernel(oc_ref, os_ref, ow_ref, gate_ref, z_ref, x_ref, ex_ref, w_ref, g_ref, y_ref):
    g_hi, g_lo = _split2(gate_ref[...])
    gcat = jnp.concatenate([g_hi, g_lo], axis=1)
    o = jnp.zeros(x_ref.shape, F32)
    for b, ref in enumerate((oc_ref, os_ref, ow_ref)):
        o = o + _dot(gcat, ex_ref[b]) * ref[...].astype(F32)
    gated = o * _silu(z_ref[...].astype(F32))
    y = x_ref[...] + _dot(gated.astype(BF16), w_ref[...])
    y_ref[...] = _rmsnorm(y, g_ref[...])


def _nsa_out(oc, os_, ow, gate, z, x2d, ex, w_out, final_g):
    M = x2d.shape[0]
    tm = min(PROJ_ROWS, M)
    row = lambda m: (m, 0)
    wide = pl.BlockSpec((tm, ATTN_WIDTH), row)
    return pl.pallas_call(
        _nsa_out_kernel,
        name="nsa_out",
        grid=(M // tm,),
        in_specs=[
            wide, wide, wide,
            pl.BlockSpec((tm, LANES), row),
            wide,
            pl.BlockSpec((tm, D_MODEL), row),
            pl.BlockSpec(ex.shape, lambda m: (0, 0, 0)),
            pl.BlockSpec((ATTN_WIDTH, D_MODEL), lambda m: (0, 0)),
            pl.BlockSpec((1, D_MODEL), lambda m: (0, 0)),
        ],
        out_specs=pl.BlockSpec((tm, D_MODEL), row),
        out_shape=jax.ShapeDtypeStruct((M, D_MODEL), F32),
        compiler_params=_params("parallel"),
    )(oc, os_, ow, gate, z, x2d, ex, w_out, final_g)


def _dup_groups(w):
    k = w.shape[0]
    w = w.reshape(k, NSA_GROUPS, 1, HEAD_DIM)
    return jnp.broadcast_to(w, (k, NSA_GROUPS, 2, HEAD_DIM)).reshape(k, 2 * KV_WIDTH)


def _fox_layer(x, g, w_in, b_f, w_out):
    B, S, _ = x.shape
    W = ATTN_WIDTH
    w_main = jnp.concatenate(
        [w_in[:, :W] * SCALE, w_in[:, W:3 * W], w_in[:, 3 * W + N_HEADS:]], axis=1).astype(BF16)
    pad = jnp.zeros((D_MODEL, LANES - 3 * N_HEADS), F32)
    w_f = w_in[:, 3 * W:3 * W + N_HEADS]
    w_f = jnp.concatenate([w_f, w_f, w_f, pad], axis=1)
    wf_hi, wf_lo = _split2(w_f)
    bf_row = jnp.concatenate([b_f, b_f, b_f, jnp.zeros((LANES - 3 * N_HEADS,), F32)])[None, :]

    qkvz, cx = _fox_in(x, g[None, :], w_main, wf_hi, wf_lo, bf_row)

    tq = min(ATTN_TQ, S)
    lane = jnp.arange(LANES)
    head = jnp.arange(N_HEADS)
    sel = jnp.logical_and(lane[None, :] % N_HEADS == head[:, None], lane[None, :] < 3 * N_HEADS)
    qx = jnp.broadcast_to(jnp.where(sel, -1.0, 0.0).astype(BF16)[:, None, :], (N_HEADS, tq, LANES))
    extra = (qx, lambda b, p, i: (2 * p, 0, 0), qx, lambda b, p, i: (2 * p + 1, 0, 0),
             cx, lambda b, p, i: (b, 0, 0))
    o = _flash(qkvz, 0, qkvz, lambda p: N_PAIRS + p, qkvz, lambda p: 2 * N_PAIRS + p, extra=extra)
    y = _fox_out(o.reshape(B * S, W), qkvz.reshape(B * S, -1), x.reshape(B * S, D_MODEL),
                 w_out.astype(BF16))
    return y.reshape(B, S, D_MODEL)


def _nsa_layer(x, g, w_in, pe_k, w_ck1, w_ck2, pe_v, w_cv1, w_cv2, w_out, final_g):
    B, S, _ = x.shape
    W, KV, G = ATTN_WIDTH, KV_WIDTH, NSA_GROUPS
    M = B * S
    sec = lambda j: w_in[:, W + j * KV:W + (j + 1) * KV]
    gate_off = W + 6 * KV
    n_gate = N_BRANCH * N_HEADS
    w_gate = jnp.concatenate([w_in[:, gate_off:gate_off + n_gate],
                              jnp.zeros((D_MODEL, LANES - n_gate), F32)], axis=1)
    w_all = jnp.concatenate(
        [w_in[:, :W] * SCALE, sec(0), sec(1), _dup_groups(sec(2)), _dup_groups(sec(3)),
         _dup_groups(sec(4)), _dup_groups(sec(5)), w_gate, w_in[:, gate_off + n_gate:]],
        axis=1).astype(BF16)

    q, cmp_raw, kv_sel, kv_win, gate, z = _nsa_in(
        x.reshape(M, D_MODEL), g[None, :], w_all, S, _rope_tables(jnp.arange(S)))
    q = q.reshape(B, S, W)

    n_chunk = S // CMP_STRIDE
    n_cmp = n_chunk - CMP_LEN // CMP_STRIDE + 1

    def chunks(a):
        a = a.reshape(B, n_chunk, CMP_STRIDE, G, HEAD_DIM).transpose(0, 3, 1, 2, 4)
        return a.reshape(B, G, n_chunk, CMP_STRIDE * HEAD_DIM)

    def flat_pe(pe):
        return jnp.broadcast_to(pe.reshape(1, CMP_LEN * HEAD_DIM), (8, CMP_LEN * HEAD_DIM)).astype(BF16)

    dup = lambda w2: jnp.concatenate([w2, w2], axis=1).astype(BF16)
    cmp_end = jnp.arange(n_chunk) * CMP_STRIDE + CMP_LEN - 1
    kc, vc = _compress(chunks(cmp_raw[:, :KV]), chunks(cmp_raw[:, KV:]),
                       w_ck1.astype(BF16), dup(w_ck2), flat_pe(pe_k),
                       w_cv1.astype(BF16), dup(w_cv2), flat_pe(pe_v), _rope_tables(cmp_end))

    n_blk = S // SEL_LEN
    ci = jnp.arange(n_chunk) * CMP_STRIDE
    sj = jnp.arange(n_blk) * SEL_LEN
    ovt = jnp.logical_and(ci[None, :] < sj[:, None] + SEL_LEN, ci[None, :] + CMP_LEN > sj[:, None])
    ovt = jnp.logical_and(ovt, jnp.arange(n_chunk)[None, :] < n_cmp).astype(BF16)
    o_c, sel_bias = _cmp_attn(q, kc, vc, ovt, n_cmp)

    blk_of_key = (jnp.arange(S)[:, None] // SEL_LEN == jnp.arange(LANES)[None, :]).astype(BF16)
    kv_sel = kv_sel.reshape(B, S, -1)
    extra = (sel_bias.reshape(B * G, S, LANES), lambda b, p, i: (b * G + p // 2, i, 0),
             sel_bias.reshape(B * G, S, LANES), lambda b, p, i: (b * G + p // 2, i, 0),
             blk_of_key[None], lambda b, p, i: (0, 0, 0))
    o_s = _flash(q, 0, kv_sel, lambda p: p // 2, kv_sel, lambda p: G + p // 2, extra=extra)

    kv_win = kv_win.reshape(B, S, -1)
    o_w = _flash(q, 0, kv_win, lambda p: p // 2, kv_win, lambda p: G + p // 2, window=WINDOW)

    j = jnp.arange(2 * LANES)[None, :, None] % LANES
    c = jnp.arange(W)[None, None, :] // HEAD_DIM
    ex = (j == c * N_BRANCH + jnp.arange(N_BRANCH)[:, None, None]).astype(BF16)
    out = _nsa_out(o_c.reshape(M, W), o_s.reshape(M, W), o_w.reshape(M, W), gate, z,
                   x.reshape(M, D_MODEL), ex, w_out.astype(BF16), final_g[None, :])
    return out.reshape(B, S, D_MODEL)


def kernel(x, norm_g, fox_w_in, fox_b_f, fox_w_out, nsa_w_in, nsa_pe_k, nsa_w_ck1, nsa_w_ck2,
           nsa_pe_v, nsa_w_cv1, nsa_w_cv2, nsa_w_out, final_g):
    x = _fox_layer(x, norm_g[0], fox_w_in[0], fox_b_f[0], fox_w_out[0])
    return _nsa_layer(x, norm_g[1], nsa_w_in[0], nsa_pe_k[0], nsa_w_ck1[0], nsa_w_ck2[0],
                      nsa_pe_v[0], nsa_w_cv1[0], nsa_w_cv2[0], nsa_w_out[0], final_g)
```

```python
import functools
import math

import jax
import jax.numpy as jnp
from jax import lax
from jax.experimental import pallas as pl
from jax.experimental.pallas import tpu as pltpu

F32 = jnp.float32
BF16 = jnp.bfloat16

D_MODEL = 1024
N_HEADS = 16
HEAD_DIM = 64
ATTN_WIDTH = N_HEADS * HEAD_DIM
NSA_GROUPS = 4
NSA_REP = N_HEADS // NSA_GROUPS
KV_WIDTH = NSA_GROUPS * HEAD_DIM
CMP_LEN = 32
CMP_STRIDE = 16
CMP_HIDDEN = 256
SEL_LEN = 64
N_SELECT = 8
WINDOW = 512
N_BRANCH = 3
ROPE_THETA = 500000.0
ROPE_DIM = HEAD_DIM // 4
ROPE_HALF = ROPE_DIM // 2
NORM_EPS = 1e-6
NEG = -1e30
FORCE = 1e6
Q_SCALE = HEAD_DIM ** -0.5 * math.log2(math.e)
LOG2E = math.log2(math.e)

LANES = 128
SUBLANES = 8
PROJ_ROWS = 512
ATTN_TILE = 256
HEADS_PER_STEP = 4
VMEM_LIMIT = 56 * 1024 * 1024


def _params(*sem):
    return pltpu.CompilerParams(dimension_semantics=sem, vmem_limit_bytes=VMEM_LIMIT)


def _iota(shape, dim):
    return lax.broadcasted_iota(jnp.int32, shape, dim)


def _split2(x):
    hi = x.astype(BF16)
    lo = (x - hi.astype(F32)).astype(BF16)
    return hi, lo


def _split3(x):
    hi = x.astype(BF16)
    r1 = x - hi.astype(F32)
    mid = r1.astype(BF16)
    lo = (r1 - mid.astype(F32)).astype(BF16)
    return hi, mid, lo


def _dot(a, b):
    return jnp.dot(a, b, preferred_element_type=F32)


def _dot_nt(a, b):
    return lax.dot_general(a, b, (((1,), (1,)), ((), ())), preferred_element_type=F32)


def _dot_tn(a, b):
    return lax.dot_general(a, b, (((0,), (0,)), ((), ())), preferred_element_type=F32)


def _rmsnorm(x, g):
    ms = jnp.mean(x * x, axis=-1, keepdims=True)
    return x * lax.rsqrt(ms + NORM_EPS) * g


def _silu(x):
    return x * (1.0 / (1.0 + jnp.exp(-x)))


def _col_max(x):
    return jnp.max(x, axis=0, keepdims=True)


def _col_sum(x):
    return jnp.sum(x, axis=0, keepdims=True)


def _rope_lanes(x, cos, sin_lo, sin_hi):
    return (x * cos + pltpu.roll(x, LANES - ROPE_HALF, axis=1) * sin_lo
            + pltpu.roll(x, ROPE_HALF, axis=1) * sin_hi)


def _rope_rows(x, cos, sin):
    x1, x2 = x[:ROPE_HALF], x[ROPE_HALF:ROPE_DIM]
    return jnp.concatenate([x1 * cos - x2 * sin, x1 * sin + x2 * cos, x[ROPE_DIM:]], axis=0)


def _rope_angles(pos):
    inv_freq = jnp.power(ROPE_THETA, -jnp.arange(ROPE_HALF, dtype=F32) * (2.0 / ROPE_DIM))
    ang = pos.astype(F32)[:, None] * inv_freq[None, :]
    return jnp.cos(ang), jnp.sin(ang)


def _rope_lane_tables(pos):
    cos, sin = _rope_angles(pos)
    n = pos.shape[0]
    ones = jnp.ones((n, HEAD_DIM - ROPE_DIM), F32)
    zeros = jnp.zeros((n, HEAD_DIM - ROPE_DIM), F32)
    z8 = jnp.zeros((n, ROPE_HALF), F32)
    c = jnp.concatenate([cos, cos, ones], axis=1)
    s_lo = jnp.concatenate([-sin, z8, zeros], axis=1)
    s_hi = jnp.concatenate([z8, sin, zeros], axis=1)
    tile = lambda t: jnp.concatenate([t, t], axis=1)
    return tile(c), tile(s_lo), tile(s_hi)


def _fox_in_kernel(x_ref, g_ref, wk_ref, wt_ref, wfh_ref, wfl_ref, bf_ref, tri_ref,
                   qt_ref, k_ref, vt_ref, zt_ref, cx_ref, carry_ref, *, tk):
    s = pl.program_id(1)
    h = _rmsnorm(x_ref[0], g_ref[...])
    hb = h.astype(BF16)
    tm = hb.shape[0]
    W = ATTN_WIDTH
    for n0 in range(0, W, 512):
        k_ref[0, :, n0:n0 + 512] = _dot(hb, wk_ref[:, n0:n0 + 512]).astype(BF16)
    for r0 in range(0, 3 * W, 512):
        t = _dot_nt(wt_ref[r0:r0 + 512, :], hb).astype(BF16)
        if r0 < W:
            qt_ref[0, r0:r0 + 512, :] = t
        elif r0 < 2 * W:
            for c in range(tm // tk):
                vt_ref[0, c, r0 - W:r0 - W + 512, :] = t[:, c * tk:(c + 1) * tk]
        else:
            zt_ref[0, r0 - 2 * W:r0 - 2 * W + 512, :] = t

    h_lo = (h - hb.astype(F32)).astype(BF16)
    f = _dot(hb, wfh_ref[...]) + _dot(h_lo, wfh_ref[...]) + _dot(hb, wfl_ref[...])
    f = f + bf_ref[...]
    log_f = jnp.minimum(f, 0.0) - jnp.log1p(jnp.exp(-jnp.abs(f)))

    @pl.when(s == 0)
    def _():
        carry_ref[...] = jnp.zeros_like(carry_ref)

    a0, a1, a2 = _split3(log_f)
    tri = tri_ref[...]
    c = _dot(tri, a0) + _dot(tri, a1) + _dot(tri, a2) + carry_ref[0:1, :]
    carry_ref[...] = jnp.broadcast_to(c[tm - 1:tm, :], carry_ref.shape)

    hi, mid, lo = _split3(c * LOG2E)
    lane = _iota(c.shape, 1)
    zero = jnp.zeros_like(hi)
    cx_ref[0] = jnp.where(lane < N_HEADS, hi,
                          jnp.where(lane < 2 * N_HEADS, mid,
                                    jnp.where(lane < 3 * N_HEADS, lo, zero)))


def _fox_in(x, g, w_k, w_t, wf_hi, wf_lo, bf_row):
    B, S, _ = x.shape
    tm = min(PROJ_ROWS, S)
    tk = min(ATTN_TILE, S)
    W = ATTN_WIDTH
    tri = (jnp.arange(tm)[:, None] >= jnp.arange(tm)[None, :]).astype(BF16)
    const = lambda b, s: (0, 0)
    feat = pl.BlockSpec((1, W, tm), lambda b, s: (b, 0, s))
    return pl.pallas_call(
        functools.partial(_fox_in_kernel, tk=tk),
        name="fox_in",
        grid=(B, S // tm),
        in_specs=[
            pl.BlockSpec((1, tm, D_MODEL), lambda b, s: (b, s, 0)),
            pl.BlockSpec((1, D_MODEL), const),
            pl.BlockSpec(w_k.shape, const),
            pl.BlockSpec(w_t.shape, const),
            pl.BlockSpec((D_MODEL, LANES), const),
            pl.BlockSpec((D_MODEL, LANES), const),
            pl.BlockSpec((1, LANES), const),
            pl.BlockSpec((tm, tm), const),
        ],
        out_specs=[
            feat,
            pl.BlockSpec((1, tm, W), lambda b, s: (b, s, 0)),
            pl.BlockSpec((1, tm // tk, W, tk), lambda b, s: (b, s, 0, 0)),
            feat,
            pl.BlockSpec((1, tm, LANES), lambda b, s: (b, s, 0)),
        ],
        out_shape=[
            jax.ShapeDtypeStruct((B, W, S), BF16),
            jax.ShapeDtypeStruct((B, S, W), BF16),
            jax.ShapeDtypeStruct((B, S // tk, W, tk), BF16),
            jax.ShapeDtypeStruct((B, W, S), BF16),
            jax.ShapeDtypeStruct((B, S, LANES), BF16),
        ],
        scratch_shapes=[pltpu.VMEM((SUBLANES, LANES), F32)],
        compiler_params=_params("parallel", "arbitrary"),
    )(x, g, w_k, w_t, wf_hi, wf_lo, bf_row, tri)


def _causal_flash(i, t, q_scr, key_tile, value_rows, s_scr, acc_scr):
    heads = range(q_scr.shape[0])

    def logits(j):
        return [_dot(key_tile(j, hh), q_scr[hh]) for hh in heads]

    def pv(j, hh, p):
        return _dot(value_rows(j, hh), p.astype(BF16))

    s_diag = logits(i)
    for hh, s in enumerate(logits(0)):
        s_scr[hh] = s
    causal = _iota((t, t), 0) <= _iota((t, t), 1)
    stats = []
    for hh in heads:
        s = jnp.where(causal, s_diag[hh], NEG)
        m = _col_max(s)
        p = jnp.exp2(s - m)
        acc_scr[hh] = pv(i, hh, p)
        stats.append((m, _col_sum(p)))
    carry = []
    for hh in heads:
        m, l = stats[hh]
        m_new = jnp.maximum(m, _col_max(s_scr[hh]))
        carry.append((m_new, jnp.exp2(m - m_new), l))

    def body(j, carry):
        s_next = logits(jnp.minimum(j + 1, i - 1))
        ls, pvs = [], []
        for hh in heads:
            m, alpha, l = carry[hh]
            p = jnp.exp2(s_scr[hh] - m)
            ls.append(alpha * l + _col_sum(p))
            pvs.append(pv(j, hh, p))
        out = []
        for hh in heads:
            m, alpha, _ = carry[hh]
            s_scr[hh] = s_next[hh]
            m_new = jnp.maximum(m, _col_max(s_next[hh]))
            out.append((m_new, jnp.exp2(m - m_new), ls[hh]))
        for hh in heads:
            acc_scr[hh] = carry[hh][1] * acc_scr[hh] + pvs[hh]
        return tuple(out)

    carry = lax.fori_loop(0, i, body, tuple(carry))
    return [c[2] for c in carry]


def _fox_attn_kernel(q_ref, qx_ref, k_ref, kx_ref, v_ref, o_ref, q_scr, s_scr, acc_scr, *, t):
    i = pl.program_id(2)
    nh = q_scr.shape[0]
    row = _iota((LANES, t), 0)
    for hh in range(nh):
        pair, sub = divmod(hh, 2)
        qt = q_ref[0, pair * LANES:(pair + 1) * LANES, :]
        own = jnp.logical_and(row >= sub * HEAD_DIM, row < (sub + 1) * HEAD_DIM)
        q_scr[hh] = jnp.concatenate([jnp.where(own, qt, jnp.zeros_like(qt)), qx_ref[hh]], axis=0)

    def key_tile(j, hh):
        start = pl.multiple_of(j * t, t)
        pair = hh // 2
        return jnp.concatenate([k_ref[0, pl.ds(start, t), pair * LANES:(pair + 1) * LANES],
                                kx_ref[0, pl.ds(start, t), :]], axis=1)

    def value_rows(j, hh):
        return v_ref[0, j, hh * HEAD_DIM:(hh + 1) * HEAD_DIM, :]

    ls = _causal_flash(i, t, q_scr, key_tile, value_rows, s_scr, acc_scr)
    for hh in range(nh):
        o_ref[0, hh * HEAD_DIM:(hh + 1) * HEAD_DIM, :] = (
            acc_scr[hh] * (1.0 / ls[hh])).astype(o_ref.dtype)


def _attn_scratch(nh, k_depth, t):
    return [pltpu.VMEM((nh, k_depth, t), BF16), pltpu.VMEM((nh, t, t), F32),
            pltpu.VMEM((nh, HEAD_DIM, t), F32)]


def _fox_attn(qt, qx, k, kx, vt):
    B, W, S = qt.shape
    t = min(ATTN_TILE, S)
    nh = HEADS_PER_STEP
    rows = nh * HEAD_DIM
    return pl.pallas_call(
        functools.partial(_fox_attn_kernel, t=t),
        name="fox_attn",
        grid=(B, N_HEADS // nh, S // t),
        in_specs=[
            pl.BlockSpec((1, rows, t), lambda b, p, i: (b, p, i)),
            pl.BlockSpec((nh, LANES, t), lambda b, p, i: (p, 0, 0)),
            pl.BlockSpec((1, S, rows), lambda b, p, i: (b, 0, p)),
            pl.BlockSpec((1, S, LANES), lambda b, p, i: (b, 0, 0)),
            pl.BlockSpec((1, S // t, rows, t), lambda b, p, i: (b, 0, p, 0)),
        ],
        out_specs=pl.BlockSpec((1, rows, t), lambda b, p, i: (b, p, i)),
        out_shape=jax.ShapeDtypeStruct((B, W, S), BF16),
        scratch_shapes=_attn_scratch(nh, 2 * LANES, t),
        compiler_params=_params("parallel", "parallel", "parallel"),
    )(qt, qx, k, kx, vt)


def _fox_out_kernel(ot_ref, zt_ref, x_ref, w_ref, y_ref):
    gated = ot_ref[0].astype(F32) * _silu(zt_ref[0].astype(F32))
    y_ref[0] = x_ref[0] + _dot_tn(gated.astype(BF16), w_ref[...])


def _fox_out(ot, zt, x, w_out):
    B, S, _ = x.shape
    tm = min(PROJ_ROWS, S)
    feat = pl.BlockSpec((1, ATTN_WIDTH, tm), lambda b, s: (b, 0, s))
    tok = pl.BlockSpec((1, tm, D_MODEL), lambda b, s: (b, s, 0))
    return pl.pallas_call(
        _fox_out_kernel,
        name="fox_out",
        grid=(B, S // tm),
        in_specs=[feat, feat, tok, pl.BlockSpec((ATTN_WIDTH, D_MODEL), lambda b, s: (0, 0))],
        out_specs=tok,
        out_shape=jax.ShapeDtypeStruct((B, S, D_MODEL), F32),
        compiler_params=_params("parallel", "parallel"),
    )(ot, zt, x, w_out)


_TOK_SEL = 0
_TOK_WIN = NSA_GROUPS * LANES
_TOK_CMP = 2 * NSA_GROUPS * LANES
_TOK_COLS = _TOK_CMP + 2 * KV_WIDTH
_FEAT_Q = 0
_FEAT_VSEL = ATTN_WIDTH
_FEAT_VWIN = _FEAT_VSEL + KV_WIDTH
_FEAT_Z = _FEAT_VWIN + KV_WIDTH
_FEAT_GATE = _FEAT_Z + ATTN_WIDTH
_FEAT_ROWS = _FEAT_GATE + LANES


def _nsa_in_kernel(x_ref, g_ref, wk_ref, wt_ref, cos_ref, slo_ref, shi_ref, cost_ref, sint_ref,
                   qt_ref, ksel_ref, kwin_ref, cmp_ref, vsel_ref, vwin_ref, zt_ref, gt_ref, *, tk):
    s = pl.program_id(1)
    hb = _rmsnorm(x_ref[0], g_ref[...]).astype(BF16)
    tm = hb.shape[0]
    G = NSA_GROUPS

    cos, slo, shi = cos_ref[...], slo_ref[...], shi_ref[...]
    tok = s * tm + _iota((tm, LANES), 0)
    lane = _iota((tm, LANES), 1)
    block_id = jnp.where(lane - HEAD_DIM == tok // SEL_LEN, 1.0, 0.0)
    ksel = _dot(hb, wk_ref[:, _TOK_SEL:_TOK_SEL + G * LANES])
    kwin = _dot(hb, wk_ref[:, _TOK_WIN:_TOK_WIN + G * LANES])
    for g in range(G):
        blk = _rope_lanes(ksel[:, g * LANES:(g + 1) * LANES], cos, slo, shi)
        ksel_ref[0, g] = (blk + block_id).astype(BF16)
        kwin_ref[0, g] = _rope_lanes(kwin[:, g * LANES:(g + 1) * LANES], cos, slo, shi).astype(BF16)
    cmp_ref[0] = _dot(hb, wk_ref[:, _TOK_CMP:_TOK_CMP + 2 * KV_WIDTH]).astype(BF16)

    cost, sint = cost_ref[...], sint_ref[...]
    for r0 in range(0, ATTN_WIDTH, 512):
        t = _dot_nt(wt_ref[_FEAT_Q + r0:_FEAT_Q + r0 + 512, :], hb)
        for h0 in range(0, 512, HEAD_DIM):
            qt_ref[0, r0 + h0:r0 + h0 + HEAD_DIM, :] = _rope_rows(
                t[h0:h0 + HEAD_DIM], cost, sint).astype(BF16)
        zt_ref[0, r0:r0 + 512, :] = _dot_nt(wt_ref[_FEAT_Z + r0:_FEAT_Z + r0 + 512, :], hb).astype(BF16)
    for row0, out_ref in ((_FEAT_VSEL, vsel_ref), (_FEAT_VWIN, vwin_ref)):
        t = _dot_nt(wt_ref[row0:row0 + KV_WIDTH, :], hb).astype(BF16)
        for c in range(tm // tk):
            out_ref[0, c] = t[:, c * tk:(c + 1) * tk]
    gate = _dot_nt(wt_ref[_FEAT_GATE:_FEAT_GATE + LANES, :], hb)
    gt_ref[0] = 1.0 / (1.0 + jnp.exp(-gate))


def _nsa_in(x, g, w_k, w_t, lane_tables, row_tables):
    B, S, _ = x.shape
    tm = min(PROJ_ROWS, S)
    tk = min(ATTN_TILE, S)
    W, G = ATTN_WIDTH, NSA_GROUPS
    const = lambda b, s: (0, 0)
    feat = lambda rows: pl.BlockSpec((1, rows, tm), lambda b, s: (b, 0, s))
    tiles = pl.BlockSpec((1, tm // tk, KV_WIDTH, tk), lambda b, s: (b, s, 0, 0))
    keys = pl.BlockSpec((1, G, tm, LANES), lambda b, s: (b, 0, s, 0))
    return pl.pallas_call(
        functools.partial(_nsa_in_kernel, tk=tk),
        name="nsa_in",
        grid=(B, S // tm),
        in_specs=[
            pl.BlockSpec((1, tm, D_MODEL), lambda b, s: (b, s, 0)),
            pl.BlockSpec((1, D_MODEL), const),
            pl.BlockSpec(w_k.shape, const),
            pl.BlockSpec(w_t.shape, const),
        ] + [pl.BlockSpec((tm, LANES), lambda b, s: (s, 0))] * 3
          + [pl.BlockSpec((ROPE_HALF, tm), lambda b, s: (0, s))] * 2,
        out_specs=[feat(W), keys, keys,
                   pl.BlockSpec((1, tm, 2 * KV_WIDTH), lambda b, s: (b, s, 0)),
                   tiles, tiles, feat(W), feat(LANES)],
        out_shape=[
            jax.ShapeDtypeStruct((B, W, S), BF16),
            jax.ShapeDtypeStruct((B, G, S, LANES), BF16),
            jax.ShapeDtypeStruct((B, G, S, LANES), BF16),
            jax.ShapeDtypeStruct((B, S, 2 * KV_WIDTH), BF16),
            jax.ShapeDtypeStruct((B, S // tk, KV_WIDTH, tk), BF16),
            jax.ShapeDtypeStruct((B, S // tk, KV_WIDTH, tk), BF16),
            jax.ShapeDtypeStruct((B, W, S), BF16),
            jax.ShapeDtypeStruct((B, LANES, S), F32),
        ],
        compiler_params=_params("parallel", "parallel"),
    )(x, g, w_k, w_t, *lane_tables, *row_tables)


def _compress_kernel(chk_ref, chv_ref, w1k_ref, w2k_ref, pek_ref, w1v_ref, w2vt_ref, pev_ref,
                     cos_ref, slo_ref, shi_ref, kc_ref, vct_ref):
    half = CMP_STRIDE * HEAD_DIM

    def hidden(ch, w1_ref, pe_ref):
        a = _dot(ch, w1_ref[:half, :])
        b = _dot(ch, w1_ref[half:, :])
        b = pltpu.roll(b, b.shape[0] - 1, axis=0)
        pe = _dot(pe_ref[...], w1_ref[...])[0:1, :]
        return _silu(a + b + pe).astype(BF16)

    kc = _dot(hidden(chk_ref[0, 0], w1k_ref, pek_ref), w2k_ref[...])
    kc_ref[0, 0] = _rope_lanes(kc, cos_ref[...], slo_ref[...], shi_ref[...]).astype(BF16)
    vct_ref[0, 0] = _dot_nt(w2vt_ref[...], hidden(chv_ref[0, 0], w1v_ref, pev_ref)).astype(BF16)


def _compress(chk, chv, w1k, w2k, pek, w1v, w2vt, pev, tables):
    B, G, n_chunk, width = chk.shape
    const = lambda b, g: (0, 0)
    chunk_spec = pl.BlockSpec((1, 1, n_chunk, width), lambda b, g: (b, g, 0, 0))
    specs = lambda ws: [pl.BlockSpec(w.shape, const) for w in ws]
    return pl.pallas_call(
        _compress_kernel,
        name="nsa_compress",
        grid=(B, G),
        in_specs=[chunk_spec, chunk_spec] + specs((w1k, w2k, pek, w1v, w2vt, pev))
                 + [pl.BlockSpec((n_chunk, LANES), const)] * 3,
        out_specs=[pl.BlockSpec((1, 1, n_chunk, LANES), lambda b, g: (b, g, 0, 0)),
                   pl.BlockSpec((1, 1, HEAD_DIM, n_chunk), lambda b, g: (b, g, 0, 0))],
        out_shape=[jax.ShapeDtypeStruct((B, G, n_chunk, LANES), BF16),
                   jax.ShapeDtypeStruct((B, G, HEAD_DIM, n_chunk), BF16)],
        compiler_params=_params("parallel", "parallel"),
    )(chk, chv, w1k, w2k, pek, w1v, w2vt, pev, *tables)


def _pad_rows(x, rows):
    return jnp.concatenate([x, jnp.zeros((rows - x.shape[0], x.shape[1]), x.dtype)], axis=0)


def _cmp_attn_kernel(q_ref, kc_ref, vct_ref, ovt_ref, oc_ref, bias_ref, *, t, n_cmp, n_blk):
    i = pl.program_id(2)
    kc = kc_ref[0, 0]
    vct = vct_ref[0, 0]
    n_pad = kc.shape[0]
    blk_c = _iota((n_pad, t), 0)
    qry = i * t + _iota((n_pad, t), 1)
    valid = jnp.logical_and(blk_c * CMP_STRIDE + (CMP_LEN - 1) <= qry, blk_c < n_cmp)
    p_sum = jnp.zeros((n_pad, t), F32)
    for r in range(NSA_REP):
        q_aug = _pad_rows(q_ref[0, r * HEAD_DIM:(r + 1) * HEAD_DIM, :], LANES)
        s = jnp.where(valid, _dot(kc, q_aug), NEG)
        e = jnp.exp2(s - _col_max(s))
        p = jnp.where(valid, e * (1.0 / _col_sum(e)), 0.0)
        p_sum = p_sum + p
        oc_ref[0, r * HEAD_DIM:(r + 1) * HEAD_DIM, :] = _dot(vct, p.astype(BF16)).astype(oc_ref.dtype)

    ovt = ovt_ref[...]
    p_hi, p_lo = _split2(p_sum)
    imp = _dot(ovt, p_hi) + _dot(ovt, p_lo)
    blk = _iota(imp.shape, 0)
    cur = (i * t + _iota(imp.shape, 1)) // SEL_LEN
    forced = jnp.logical_or(blk == 0, jnp.logical_or(blk == cur, blk == cur - 1))
    imp = jnp.where(forced, FORCE, jnp.where(blk <= cur, imp, -1.0))
    keep = jnp.zeros(imp.shape, jnp.bool_)
    for _ in range(min(N_SELECT, n_blk)):
        top = _col_max(imp)
        first = jnp.min(jnp.where(imp == top, blk, n_blk), axis=0, keepdims=True)
        pick = blk == first
        keep = jnp.logical_or(keep, pick)
        imp = jnp.where(pick, -jnp.inf, imp)
    bias_ref[0, 0] = jnp.where(keep, 0.0, NEG).astype(bias_ref.dtype)


def _cmp_attn(qt, kc, vct, ovt, n_cmp):
    B, W, S = qt.shape
    G = kc.shape[1]
    t = min(ATTN_TILE, S)
    n_blk = ovt.shape[0]
    gw = W // G
    kern = functools.partial(_cmp_attn_kernel, t=t, n_cmp=n_cmp, n_blk=n_blk)
    return pl.pallas_call(
        kern,
        name="nsa_cmp_attn",
        grid=(B, G, S // t),
        in_specs=[
            pl.BlockSpec((1, gw, t), lambda b, g, i: (b, g, i)),
            pl.BlockSpec((1, 1) + kc.shape[2:], lambda b, g, i: (b, g, 0, 0)),
            pl.BlockSpec((1, 1) + vct.shape[2:], lambda b, g, i: (b, g, 0, 0)),
            pl.BlockSpec(ovt.shape, lambda b, g, i: (0, 0)),
        ],
        out_specs=[
            pl.BlockSpec((1, gw, t), lambda b, g, i: (b, g, i)),
            pl.BlockSpec((1, 1, n_blk, t), lambda b, g, i: (b, g, 0, i)),
        ],
        out_shape=[
            jax.ShapeDtypeStruct((B, W, S), BF16),
            jax.ShapeDtypeStruct((B, G, n_blk, S), BF16),
        ],
        compiler_params=_params("parallel", "parallel", "parallel"),
    )(qt, kc, vct, ovt)


def _sel_attn_kernel(q_ref, bias_ref, k_ref, v_ref, o_ref, q_scr, s_scr, acc_scr, *, t):
    i = pl.program_id(2)
    nh = q_scr.shape[0]
    bias = _pad_rows(bias_ref[0, 0], HEAD_DIM)
    for hh in range(nh):
        q_scr[hh] = jnp.concatenate([q_ref[0, hh * HEAD_DIM:(hh + 1) * HEAD_DIM, :], bias], axis=0)

    def key_tile(j, hh):
        return k_ref[0, 0, pl.ds(pl.multiple_of(j * t, t), t), :]

    def value_rows(j, hh):
        return v_ref[0, j]

    ls = _causal_flash(i, t, q_scr, key_tile, value_rows, s_scr, acc_scr)
    for hh in range(nh):
        o_ref[0, hh * HEAD_DIM:(hh + 1) * HEAD_DIM, :] = (
            acc_scr[hh] * (1.0 / ls[hh])).astype(o_ref.dtype)


def _group_attn_specs(S, t, gw, extra_in=()):
    G = NSA_GROUPS
    return dict(
        in_specs=[pl.BlockSpec((1, gw, t), lambda b, g, i: (b, g, i))] + list(extra_in) + [
            pl.BlockSpec((1, 1, S, LANES), lambda b, g, i: (b, g, 0, 0)),
            pl.BlockSpec((1, S // t, HEAD_DIM, t), lambda b, g, i: (b, 0, g, 0)),
        ],
        out_specs=pl.BlockSpec((1, gw, t), lambda b, g, i: (b, g, i)),
    )


def _sel_attn(qt, bias, k_aug, vt):
    B, W, S = qt.shape
    t = min(ATTN_TILE, S)
    nh = NSA_REP
    n_blk = bias.shape[2]
    bias_spec = pl.BlockSpec((1, 1, n_blk, t), lambda b, g, i: (b, g, 0, i))
    return pl.pallas_call(
        functools.partial(_sel_attn_kernel, t=t),
        name="nsa_sel_attn",
        grid=(B, NSA_GROUPS, S // t),
        out_shape=jax.ShapeDtypeStruct((B, W, S), BF16),
        scratch_shapes=_attn_scratch(nh, LANES, t),
        compiler_params=_params("parallel", "parallel", "parallel"),
        **_group_attn_specs(S, t, nh * HEAD_DIM, [bias_spec]),
    )(qt, bias, k_aug, vt)


def _win_attn_kernel(q_ref, k_ref, v_ref, o_ref, *, t, n_back):
    i = pl.program_id(2)
    above = _iota((t, t), 0) > _iota((t, t), 1)
    tiles = []
    for d in range(n_back + 1):
        j = i - d
        jc = jnp.maximum(j, 0)
        k = k_ref[0, 0, pl.ds(pl.multiple_of(jc * t, t), t), :]
        if d == 0:
            ok = jnp.logical_not(above)
        elif d == n_back:
            ok = jnp.logical_and(above, j >= 0)
        else:
            ok = jnp.broadcast_to(j >= 0, (t, t))
        tiles.append((k, v_ref[0, jc], ok))
    for hh in range(NSA_REP):
        q_aug = _pad_rows(q_ref[0, hh * HEAD_DIM:(hh + 1) * HEAD_DIM, :], LANES)
        ss = [jnp.where(ok, _dot(k, q_aug), NEG) for k, _, ok in tiles]
        m = functools.reduce(jnp.maximum, [_col_max(s) for s in ss])
        ps = [jnp.exp2(s - m) for s in ss]
        l = functools.reduce(jnp.add, [_col_sum(p) for p in ps])
        acc = functools.reduce(jnp.add, [_dot(v, p.astype(BF16)) for (_, v, _), p in zip(tiles, ps)])
        o_ref[0, hh * HEAD_DIM:(hh + 1) * HEAD_DIM, :] = (acc * (1.0 / l)).astype(o_ref.dtype)


def _win_attn(qt, k_aug, vt):
    B, W, S = qt.shape
    t = min(ATTN_TILE, S)
    assert WINDOW % t == 0
    return pl.pallas_call(
        functools.partial(_win_attn_kernel, t=t, n_back=WINDOW // t),
        name="nsa_win_attn",
        grid=(B, NSA_GROUPS, S // t),
        out_shape=jax.ShapeDtypeStruct((B, W, S), BF16),
        compiler_params=_params("parallel", "parallel", "parallel"),
        **_group_attn_specs(S, t, NSA_REP * HEAD_DIM),
    )(qt, k_aug, vt)


def _nsa_out_kernel(oc_ref, os_ref, ow_ref, gt_ref, zt_ref, x_ref, w_ref, g_ref, y_ref, gated_scr):
    for h in range(N_HEADS):
        rows = slice(h * HEAD_DIM, (h + 1) * HEAD_DIM)
        o = jnp.zeros((HEAD_DIM, gated_scr.shape[1]), F32)
        for b, ref in enumerate((oc_ref, os_ref, ow_ref)):
            gate = gt_ref[0, N_BRANCH * h + b:N_BRANCH * h + b + 1, :]
            o = o + gate * ref[0, rows, :].astype(F32)
        gated_scr[rows, :] = (o * _silu(zt_ref[0, rows, :].astype(F32))).astype(BF16)
    y = x_ref[0] + _dot_tn(gated_scr[...], w_ref[...])
    y_ref[0] = _rmsnorm(y, g_ref[...])


def _nsa_out(oc, os_, ow, gt, zt, x, w_out, final_g):
    B, S, _ = x.shape
    tm = min(PROJ_ROWS, S)
    feat = lambda rows: pl.BlockSpec((1, rows, tm), lambda b, s: (b, 0, s))
    tok = pl.BlockSpec((1, tm, D_MODEL), lambda b, s: (b, s, 0))
    wide = feat(ATTN_WIDTH)
    return pl.pallas_call(
        _nsa_out_kernel,
        name="nsa_out",
        grid=(B, S // tm),
        in_specs=[wide, wide, wide, feat(LANES), wide, tok,
                  pl.BlockSpec((ATTN_WIDTH, D_MODEL), lambda b, s: (0, 0)),
                  pl.BlockSpec((1, D_MODEL), lambda b, s: (0, 0))],
        out_specs=tok,
        out_shape=jax.ShapeDtypeStruct((B, S, D_MODEL), F32),
        scratch_shapes=[pltpu.VMEM((ATTN_WIDTH, tm), BF16)],
        compiler_params=_params("parallel", "parallel"),
    )(oc, os_, ow, gt, zt, x, w_out, final_g)


def _fox_layer(x, g, w_in, b_f, w_out):
    B, S, _ = x.shape
    W = ATTN_WIDTH
    w_k = w_in[:, W:2 * W].astype(BF16)
    w_t = jnp.concatenate([w_in[:, :W] * Q_SCALE, w_in[:, 2 * W:3 * W], w_in[:, 3 * W + N_HEADS:]],
                          axis=1).T.astype(BF16)
    pad = jnp.zeros((D_MODEL, LANES - 3 * N_HEADS), F32)
    w_f = w_in[:, 3 * W:3 * W + N_HEADS]
    w_f = jnp.concatenate([w_f, w_f, w_f, pad], axis=1)
    wf_hi, wf_lo = _split2(w_f)
    bf_row = jnp.concatenate([b_f, b_f, b_f, jnp.zeros((LANES - 3 * N_HEADS,), F32)])[None, :]

    qt, k, vt, zt, cx = _fox_in(x, g[None, :], w_k, w_t, wf_hi, wf_lo, bf_row)

    t = min(ATTN_TILE, S)
    r = jnp.arange(LANES)
    head = jnp.arange(N_HEADS)
    sel = jnp.logical_and(r[None, :] % N_HEADS == head[:, None], r[None, :] < 3 * N_HEADS)
    qx = jnp.broadcast_to(jnp.where(sel, -1.0, 0.0).astype(BF16)[:, :, None], (N_HEADS, LANES, t))
    ot = _fox_attn(qt, qx, k, cx, vt)
    return _fox_out(ot, zt, x, w_out.astype(BF16))


def _pad_groups(w):
    k = w.shape[0]
    w = w.reshape(k, NSA_GROUPS, HEAD_DIM)
    return jnp.concatenate([w, jnp.zeros_like(w)], axis=2).reshape(k, NSA_GROUPS * LANES)


def _nsa_layer(x, g, w_in, pe_k, w_ck1, w_ck2, pe_v, w_cv1, w_cv2, w_out, final_g):
    B, S, _ = x.shape
    W, KV, G = ATTN_WIDTH, KV_WIDTH, NSA_GROUPS
    sec = lambda j: w_in[:, W + j * KV:W + (j + 1) * KV]
    gate_off = W + 6 * KV
    n_gate = N_BRANCH * N_HEADS
    w_gate = jnp.concatenate([w_in[:, gate_off:gate_off + n_gate],
                              jnp.zeros((D_MODEL, LANES - n_gate), F32)], axis=1)
    w_k = jnp.concatenate([_pad_groups(sec(2)), _pad_groups(sec(4)), sec(0), sec(1)],
                          axis=1).astype(BF16)
    w_t = jnp.concatenate([w_in[:, :W] * Q_SCALE, sec(3), sec(5), w_in[:, gate_off + n_gate:], w_gate],
                          axis=1).T.astype(BF16)
    pos = jnp.arange(S)
    cos, sin = _rope_angles(pos)
    qt, ksel, kwin, cmp_raw, vsel, vwin, zt, gt = _nsa_in(
        x, g[None, :], w_k, w_t, _rope_lane_tables(pos), (cos.T, sin.T))

    n_chunk = S // CMP_STRIDE
    n_cmp = n_chunk - CMP_LEN // CMP_STRIDE + 1

    def chunks(a):
        a = a.reshape(B, n_chunk, CMP_STRIDE, G, HEAD_DIM).transpose(0, 3, 1, 2, 4)
        return a.reshape(B, G, n_chunk, CMP_STRIDE * HEAD_DIM)

    def flat_pe(pe):
        pe = pe.reshape(1, CMP_LEN * HEAD_DIM)
        return jnp.broadcast_to(pe, (SUBLANES, CMP_LEN * HEAD_DIM)).astype(BF16)

    w2k = jnp.concatenate([w_ck2, jnp.zeros_like(w_ck2)], axis=1).astype(BF16)
    cmp_end = jnp.arange(n_chunk) * CMP_STRIDE + CMP_LEN - 1
    kc, vct = _compress(chunks(cmp_raw[..., :KV]), chunks(cmp_raw[..., KV:]),
                        w_ck1.astype(BF16), w2k, flat_pe(pe_k),
                        w_cv1.astype(BF16), w_cv2.T.astype(BF16), flat_pe(pe_v),
                        _rope_lane_tables(cmp_end))

    n_blk = S // SEL_LEN
    ci = jnp.arange(n_chunk) * CMP_STRIDE
    sj = jnp.arange(n_blk) * SEL_LEN
    ovt = jnp.logical_and(ci[None, :] < sj[:, None] + SEL_LEN, ci[None, :] + CMP_LEN > sj[:, None])
    ovt = jnp.logical_and(ovt, jnp.arange(n_chunk)[None, :] < n_cmp).astype(BF16)
    oc, bias = _cmp_attn(qt, kc, vct, ovt, n_cmp)
    os_ = _sel_attn(qt, bias, ksel, vsel)
    ow = _win_attn(qt, kwin, vwin)
    return _nsa_out(oc, os_, ow, gt, zt, x, w_out.astype(BF16), final_g[None, :])


def kernel(x, norm_g, fox_w_in, fox_b_f, fox_w_out, nsa_w_in, nsa_pe_k, nsa_w_ck1, nsa_w_ck2,
           nsa_pe_v, nsa_w_cv1, nsa_w_cv2, nsa_w_out, final_g):
    x = _fox_layer(x, norm_g[0], fox_w_in[0], fox_b_f[0], fox_w_out[0])
    return _nsa_layer(x, norm_g[1], nsa_w_in[0], nsa_pe_k[0], nsa_w_ck1[0], nsa_w_ck2[0],
                      nsa_pe_v[0], nsa_w_cv1[0], nsa_w_cv2[0], nsa_w_out[0], final_g)
```

```python
import functools
import math

import jax
import jax.numpy as jnp
from jax import lax
from jax.experimental import pallas as pl
from jax.experimental.pallas import tpu as pltpu

F32 = jnp.float32
BF16 = jnp.bfloat16

D_MODEL = 1024
N_HEADS = 16
HEAD_DIM = 64
ATTN_WIDTH = N_HEADS * HEAD_DIM
NSA_GROUPS = 4
NSA_REP = N_HEADS // NSA_GROUPS
KV_WIDTH = NSA_GROUPS * HEAD_DIM
CMP_LEN = 32
CMP_STRIDE = 16
CMP_HIDDEN = 256
SEL_LEN = 64
N_SELECT = 8
WINDOW = 512
N_BRANCH = 3
ROPE_THETA = 500000.0
ROPE_DIM = HEAD_DIM // 4
ROPE_HALF = ROPE_DIM // 2
NORM_EPS = 1e-6
NEG = -1e30
FORCE = 1e6
Q_SCALE = HEAD_DIM ** -0.5 * math.log2(math.e)
LOG2E = math.log2(math.e)

LANES = 128
SUBLANES = 8
PROJ_ROWS = 512
ATTN_TILE = 256
HEADS_PER_STEP = 8
VMEM_LIMIT = 56 * 1024 * 1024


def _params(*sem):
    return pltpu.CompilerParams(dimension_semantics=sem, vmem_limit_bytes=VMEM_LIMIT)


def _iota(shape, dim):
    return lax.broadcasted_iota(jnp.int32, shape, dim)


def _split2(x):
    hi = x.astype(BF16)
    lo = (x - hi.astype(F32)).astype(BF16)
    return hi, lo


def _split3(x):
    hi = x.astype(BF16)
    r1 = x - hi.astype(F32)
    mid = r1.astype(BF16)
    lo = (r1 - mid.astype(F32)).astype(BF16)
    return hi, mid, lo


def _dot(a, b):
    return jnp.dot(a, b, preferred_element_type=F32)


def _dot_nt(a, b):
    return lax.dot_general(a, b, (((1,), (1,)), ((), ())), preferred_element_type=F32)


def _dot_tn(a, b):
    return lax.dot_general(a, b, (((0,), (0,)), ((), ())), preferred_element_type=F32)


def _rmsnorm(x, g):
    ms = jnp.mean(x * x, axis=-1, keepdims=True)
    return x * lax.rsqrt(ms + NORM_EPS) * g


def _silu(x):
    return x * (1.0 / (1.0 + jnp.exp(-x)))


def _col_max(x):
    return jnp.max(x, axis=0, keepdims=True)


def _col_sum(x):
    return jnp.sum(x, axis=0, keepdims=True)


def _rope_lanes(x, cos, sin_lo, sin_hi):
    return (x * cos + pltpu.roll(x, LANES - ROPE_HALF, axis=1) * sin_lo
            + pltpu.roll(x, ROPE_HALF, axis=1) * sin_hi)


def _rope_rows(x, cos, sin):
    x1, x2 = x[:ROPE_HALF], x[ROPE_HALF:ROPE_DIM]
    return jnp.concatenate([x1 * cos - x2 * sin, x1 * sin + x2 * cos, x[ROPE_DIM:]], axis=0)


def _rope_angles(pos):
    inv_freq = jnp.power(ROPE_THETA, -jnp.arange(ROPE_HALF, dtype=F32) * (2.0 / ROPE_DIM))
    ang = pos.astype(F32)[:, None] * inv_freq[None, :]
    return jnp.cos(ang), jnp.sin(ang)


def _rope_lane_tables(pos):
    cos, sin = _rope_angles(pos)
    n = pos.shape[0]
    ones = jnp.ones((n, HEAD_DIM - ROPE_DIM), F32)
    zeros = jnp.zeros((n, HEAD_DIM - ROPE_DIM), F32)
    z8 = jnp.zeros((n, ROPE_HALF), F32)
    c = jnp.concatenate([cos, cos, ones], axis=1)
    s_lo = jnp.concatenate([-sin, z8, zeros], axis=1)
    s_hi = jnp.concatenate([z8, sin, zeros], axis=1)
    tile = lambda t: jnp.concatenate([t, t], axis=1)
    return tile(c), tile(s_lo), tile(s_hi)


def _fox_in_kernel(x_ref, g_ref, wk_ref, wt_ref, wfh_ref, wfl_ref, bf_ref, tri_ref,
                   qt_ref, k_ref, vt_ref, zt_ref, cx_ref, carry_ref, *, tk):
    s = pl.program_id(1)
    h = _rmsnorm(x_ref[0], g_ref[...])
    hb = h.astype(BF16)
    tm = hb.shape[0]
    W = ATTN_WIDTH
    for n0 in range(0, W, 512):
        k_ref[0, :, n0:n0 + 512] = _dot(hb, wk_ref[:, n0:n0 + 512]).astype(BF16)
    for r0 in range(0, 3 * W, 512):
        t = _dot_nt(wt_ref[r0:r0 + 512, :], hb).astype(BF16)
        if r0 < W:
            qt_ref[0, r0:r0 + 512, :] = t
        elif r0 < 2 * W:
            for c in range(tm // tk):
                vt_ref[0, c, r0 - W:r0 - W + 512, :] = t[:, c * tk:(c + 1) * tk]
        else:
            zt_ref[0, r0 - 2 * W:r0 - 2 * W + 512, :] = t

    h_lo = (h - hb.astype(F32)).astype(BF16)
    f = _dot(hb, wfh_ref[...]) + _dot(h_lo, wfh_ref[...]) + _dot(hb, wfl_ref[...])
    f = f + bf_ref[...]
    log_f = jnp.minimum(f, 0.0) - jnp.log1p(jnp.exp(-jnp.abs(f)))

    @pl.when(s == 0)
    def _():
        carry_ref[...] = jnp.zeros_like(carry_ref)

    a0, a1, a2 = _split3(log_f)
    tri = tri_ref[...]
    c = _dot(tri, a0) + _dot(tri, a1) + _dot(tri, a2) + carry_ref[0:1, :]
    carry_ref[...] = jnp.broadcast_to(c[tm - 1:tm, :], carry_ref.shape)

    hi, mid, lo = _split3(c * LOG2E)
    lane = _iota(c.shape, 1)
    zero = jnp.zeros_like(hi)
    cx_ref[0] = jnp.where(lane < N_HEADS, hi,
                          jnp.where(lane < 2 * N_HEADS, mid,
                                    jnp.where(lane < 3 * N_HEADS, lo, zero)))


def _fox_in(x, g, w_k, w_t, wf_hi, wf_lo, bf_row):
    B, S, _ = x.shape
    tm = min(PROJ_ROWS, S)
    tk = min(ATTN_TILE, S)
    W = ATTN_WIDTH
    tri = (jnp.arange(tm)[:, None] >= jnp.arange(tm)[None, :]).astype(BF16)
    const = lambda b, s: (0, 0)
    feat = pl.BlockSpec((1, W, tm), lambda b, s: (b, 0, s))
    return pl.pallas_call(
        functools.partial(_fox_in_kernel, tk=tk),
        name="fox_in",
        grid=(B, S // tm),
        in_specs=[
            pl.BlockSpec((1, tm, D_MODEL), lambda b, s: (b, s, 0)),
            pl.BlockSpec((1, D_MODEL), const),
            pl.BlockSpec(w_k.shape, const),
            pl.BlockSpec(w_t.shape, const),
            pl.BlockSpec((D_MODEL, LANES), const),
            pl.BlockSpec((D_MODEL, LANES), const),
            pl.BlockSpec((1, LANES), const),
            pl.BlockSpec((tm, tm), const),
        ],
        out_specs=[
            feat,
            pl.BlockSpec((1, tm, W), lambda b, s: (b, s, 0)),
            pl.BlockSpec((1, tm // tk, W, tk), lambda b, s: (b, s, 0, 0)),
            feat,
            pl.BlockSpec((1, tm, LANES), lambda b, s: (b, s, 0)),
        ],
        out_shape=[
            jax.ShapeDtypeStruct((B, W, S), BF16),
            jax.ShapeDtypeStruct((B, S, W), BF16),
            jax.ShapeDtypeStruct((B, S // tk, W, tk), BF16),
            jax.ShapeDtypeStruct((B, W, S), BF16),
            jax.ShapeDtypeStruct((B, S, LANES), BF16),
        ],
        scratch_shapes=[pltpu.VMEM((SUBLANES, LANES), F32)],
        compiler_params=_params("parallel", "arbitrary"),
    )(x, g, w_k, w_t, wf_hi, wf_lo, bf_row, tri)


def _causal_flash(i, t, q_scr, key_tile, value_rows, s_scr, acc_scr):
    heads = range(q_scr.shape[0])

    def logits(j):
        return [_dot(key_tile(j, hh), q_scr[hh]) for hh in heads]

    def pv(j, hh, p):
        return _dot(value_rows(j, hh), p.astype(BF16))

    s_diag = logits(i)
    for hh, s in enumerate(logits(0)):
        s_scr[hh] = s
    causal = _iota((t, t), 0) <= _iota((t, t), 1)
    stats = []
    for hh in heads:
        s = jnp.where(causal, s_diag[hh], NEG)
        m = _col_max(s)
        p = jnp.exp2(s - m)
        acc_scr[hh] = pv(i, hh, p)
        stats.append((m, _col_sum(p)))
    carry = []
    for hh in heads:
        m, l = stats[hh]
        m_new = jnp.maximum(m, _col_max(s_scr[hh]))
        carry.append((m_new, jnp.exp2(m - m_new), l))

    def body(j, carry):
        s_next = logits(jnp.minimum(j + 1, i - 1))
        ls, pvs = [], []
        for hh in heads:
            m, alpha, l = carry[hh]
            p = jnp.exp2(s_scr[hh] - m)
            ls.append(alpha * l + _col_sum(p))
            pvs.append(pv(j, hh, p))
        out = []
        for hh in heads:
            m, alpha, _ = carry[hh]
            s_scr[hh] = s_next[hh]
            m_new = jnp.maximum(m, _col_max(s_next[hh]))
            out.append((m_new, jnp.exp2(m - m_new), ls[hh]))
        for hh in heads:
            acc_scr[hh] = carry[hh][1] * acc_scr[hh] + pvs[hh]
        return tuple(out)

    carry = lax.fori_loop(0, i, body, tuple(carry))
    return [c[2] for c in carry]


def _fox_attn_kernel(q_ref, qx_ref, k_ref, kx_ref, v_ref, o_ref, q_scr, s_scr, acc_scr, *, t):
    i = pl.program_id(2)
    nh = q_scr.shape[0]
    row = _iota((LANES, t), 0)
    for hh in range(nh):
        pair, sub = divmod(hh, 2)
        qt = q_ref[0, pair * LANES:(pair + 1) * LANES, :]
        own = jnp.logical_and(row >= sub * HEAD_DIM, row < (sub + 1) * HEAD_DIM)
        q_scr[hh] = jnp.concatenate([jnp.where(own, qt, jnp.zeros_like(qt)), qx_ref[hh]], axis=0)

    def key_tile(j, hh):
        start = pl.multiple_of(j * t, t)
        pair = hh // 2
        return jnp.concatenate([k_ref[0, pl.ds(start, t), pair * LANES:(pair + 1) * LANES],
                                kx_ref[0, pl.ds(start, t), :]], axis=1)

    def value_rows(j, hh):
        return v_ref[0, j, hh * HEAD_DIM:(hh + 1) * HEAD_DIM, :]

    ls = _causal_flash(i, t, q_scr, key_tile, value_rows, s_scr, acc_scr)
    for hh in range(nh):
        o_ref[0, hh * HEAD_DIM:(hh + 1) * HEAD_DIM, :] = (
            acc_scr[hh] * (1.0 / ls[hh])).astype(o_ref.dtype)


def _attn_scratch(nh, k_depth, t):
    return [pltpu.VMEM((nh, k_depth, t), BF16), pltpu.VMEM((nh, t, t), F32),
            pltpu.VMEM((nh, HEAD_DIM, t), F32)]


def _fox_attn(qt, qx, k, kx, vt):
    B, W, S = qt.shape
    t = min(ATTN_TILE, S)
    nh = HEADS_PER_STEP
    rows = nh * HEAD_DIM
    return pl.pallas_call(
        functools.partial(_fox_attn_kernel, t=t),
        name="fox_attn",
        grid=(B, N_HEADS // nh, S // t),
        in_specs=[
            pl.BlockSpec((1, rows, t), lambda b, p, i: (b, p, i)),
            pl.BlockSpec((nh, LANES, t), lambda b, p, i: (p, 0, 0)),
            pl.BlockSpec((1, S, rows), lambda b, p, i: (b, 0, p)),
            pl.BlockSpec((1, S, LANES), lambda b, p, i: (b, 0, 0)),
            pl.BlockSpec((1, S // t, rows, t), lambda b, p, i: (b, 0, p, 0)),
        ],
        out_specs=pl.BlockSpec((1, rows, t), lambda b, p, i: (b, p, i)),
        out_shape=jax.ShapeDtypeStruct((B, W, S), BF16),
        scratch_shapes=_attn_scratch(nh, 2 * LANES, t),
        compiler_params=_params("parallel", "parallel", "parallel"),
    )(qt, qx, k, kx, vt)


def _fox_out_kernel(ot_ref, zt_ref, x_ref, w_ref, y_ref):
    gated = ot_ref[0].astype(F32) * _silu(zt_ref[0].astype(F32))
    y_ref[0] = x_ref[0] + _dot_tn(gated.astype(BF16), w_ref[...])


def _fox_out(ot, zt, x, w_out):
    B, S, _ = x.shape
    tm = min(PROJ_ROWS, S)
    feat = pl.BlockSpec((1, ATTN_WIDTH, tm), lambda b, s: (b, 0, s))
    tok = pl.BlockSpec((1, tm, D_MODEL), lambda b, s: (b, s, 0))
    return pl.pallas_call(
        _fox_out_kernel,
        name="fox_out",
        grid=(B, S // tm),
        in_specs=[feat, feat, tok, pl.BlockSpec((ATTN_WIDTH, D_MODEL), lambda b, s: (0, 0))],
        out_specs=tok,
        out_shape=jax.ShapeDtypeStruct((B, S, D_MODEL), F32),
        compiler_params=_params("parallel", "parallel"),
    )(ot, zt, x, w_out)


_TOK_SEL = 0
_TOK_WIN = NSA_GROUPS * LANES
_TOK_CMP = 2 * NSA_GROUPS * LANES
_TOK_COLS = _TOK_CMP + 2 * KV_WIDTH
_FEAT_Q = 0
_FEAT_VSEL = ATTN_WIDTH
_FEAT_VWIN = _FEAT_VSEL + KV_WIDTH
_FEAT_Z = _FEAT_VWIN + KV_WIDTH
_FEAT_GATE = _FEAT_Z + ATTN_WIDTH
_FEAT_ROWS = _FEAT_GATE + LANES


def _nsa_in_kernel(x_ref, g_ref, wk_ref, wt_ref, cos_ref, slo_ref, shi_ref, cost_ref, sint_ref,
                   qt_ref, ksel_ref, kwin_ref, cmp_ref, vsel_ref, vwin_ref, zt_ref, gt_ref, *, tk):
    s = pl.program_id(1)
    hb = _rmsnorm(x_ref[0], g_ref[...]).astype(BF16)
    tm = hb.shape[0]
    G = NSA_GROUPS

    cos, slo, shi = cos_ref[...], slo_ref[...], shi_ref[...]
    tok = s * tm + _iota((tm, LANES), 0)
    lane = _iota((tm, LANES), 1)
    block_id = jnp.where(lane - HEAD_DIM == tok // SEL_LEN, 1.0, 0.0)
    ksel = _dot(hb, wk_ref[:, _TOK_SEL:_TOK_SEL + G * LANES])
    kwin = _dot(hb, wk_ref[:, _TOK_WIN:_TOK_WIN + G * LANES])
    for g in range(G):
        blk = _rope_lanes(ksel[:, g * LANES:(g + 1) * LANES], cos, slo, shi)
        ksel_ref[0, g] = (blk + block_id).astype(BF16)
        kwin_ref[0, g] = _rope_lanes(kwin[:, g * LANES:(g + 1) * LANES], cos, slo, shi).astype(BF16)
    cmp_ref[0] = _dot(hb, wk_ref[:, _TOK_CMP:_TOK_CMP + 2 * KV_WIDTH]).astype(BF16)

    cost, sint = cost_ref[...], sint_ref[...]
    for r0 in range(0, ATTN_WIDTH, 512):
        t = _dot_nt(wt_ref[_FEAT_Q + r0:_FEAT_Q + r0 + 512, :], hb)
        for h0 in range(0, 512, HEAD_DIM):
            qt_ref[0, r0 + h0:r0 + h0 + HEAD_DIM, :] = _rope_rows(
                t[h0:h0 + HEAD_DIM], cost, sint).astype(BF16)
        zt_ref[0, r0:r0 + 512, :] = _dot_nt(wt_ref[_FEAT_Z + r0:_FEAT_Z + r0 + 512, :], hb).astype(BF16)
    for row0, out_ref in ((_FEAT_VSEL, vsel_ref), (_FEAT_VWIN, vwin_ref)):
        t = _dot_nt(wt_ref[row0:row0 + KV_WIDTH, :], hb).astype(BF16)
        for c in range(tm // tk):
            out_ref[0, c] = t[:, c * tk:(c + 1) * tk]
    gate = _dot_nt(wt_ref[_FEAT_GATE:_FEAT_GATE + LANES, :], hb)
    gt_ref[0] = 1.0 / (1.0 + jnp.exp(-gate))


def _nsa_in(x, g, w_k, w_t, lane_tables, row_tables):
    B, S, _ = x.shape
    tm = min(PROJ_ROWS, S)
    tk = min(ATTN_TILE, S)
    W, G = ATTN_WIDTH, NSA_GROUPS
    const = lambda b, s: (0, 0)
    feat = lambda rows: pl.BlockSpec((1, rows, tm), lambda b, s: (b, 0, s))
    tiles = pl.BlockSpec((1, tm // tk, KV_WIDTH, tk), lambda b, s: (b, s, 0, 0))
    keys = pl.BlockSpec((1, G, tm, LANES), lambda b, s: (b, 0, s, 0))
    return pl.pallas_call(
        functools.partial(_nsa_in_kernel, tk=tk),
        name="nsa_in",
        grid=(B, S // tm),
        in_specs=[
            pl.BlockSpec((1, tm, D_MODEL), lambda b, s: (b, s, 0)),
            pl.BlockSpec((1, D_MODEL), const),
            pl.BlockSpec(w_k.shape, const),
            pl.BlockSpec(w_t.shape, const),
        ] + [pl.BlockSpec((tm, LANES), lambda b, s: (s, 0))] * 3
          + [pl.BlockSpec((ROPE_HALF, tm), lambda b, s: (0, s))] * 2,
        out_specs=[feat(W), keys, keys,
                   pl.BlockSpec((1, tm, 2 * KV_WIDTH), lambda b, s: (b, s, 0)),
                   tiles, tiles, feat(W), feat(LANES)],
        out_shape=[
            jax.ShapeDtypeStruct((B, W, S), BF16),
            jax.ShapeDtypeStruct((B, G, S, LANES), BF16),
            jax.ShapeDtypeStruct((B, G, S, LANES), BF16),
            jax.ShapeDtypeStruct((B, S, 2 * KV_WIDTH), BF16),
            jax.ShapeDtypeStruct((B, S // tk, KV_WIDTH, tk), BF16),
            jax.ShapeDtypeStruct((B, S // tk, KV_WIDTH, tk), BF16),
            jax.ShapeDtypeStruct((B, W, S), BF16),
            jax.ShapeDtypeStruct((B, LANES, S), F32),
        ],
        compiler_params=_params("parallel", "parallel"),
    )(x, g, w_k, w_t, *lane_tables, *row_tables)


def _compress_kernel(chk_ref, chv_ref, w1k_ref, w2k_ref, pek_ref, w1v_ref, w2vt_ref, pev_ref,
                     cos_ref, slo_ref, shi_ref, kc_ref, vct_ref):
    half = CMP_STRIDE * HEAD_DIM

    def hidden(ch, w1_ref, pe_ref):
        a = _dot(ch, w1_ref[:half, :])
        b = _dot(ch, w1_ref[half:, :])
        b = pltpu.roll(b, b.shape[0] - 1, axis=0)
        pe = _dot(pe_ref[...], w1_ref[...])[0:1, :]
        return _silu(a + b + pe).astype(BF16)

    kc = _dot(hidden(chk_ref[0, 0], w1k_ref, pek_ref), w2k_ref[...])
    kc_ref[0, 0] = _rope_lanes(kc, cos_ref[...], slo_ref[...], shi_ref[...]).astype(BF16)
    vct_ref[0, 0] = _dot_nt(w2vt_ref[...], hidden(chv_ref[0, 0], w1v_ref, pev_ref)).astype(BF16)


def _compress(chk, chv, w1k, w2k, pek, w1v, w2vt, pev, tables):
    B, G, n_chunk, width = chk.shape
    const = lambda b, g: (0, 0)
    chunk_spec = pl.BlockSpec((1, 1, n_chunk, width), lambda b, g: (b, g, 0, 0))
    specs = lambda ws: [pl.BlockSpec(w.shape, const) for w in ws]
    return pl.pallas_call(
        _compress_kernel,
        name="nsa_compress",
        grid=(B, G),
        in_specs=[chunk_spec, chunk_spec] + specs((w1k, w2k, pek, w1v, w2vt, pev))
                 + [pl.BlockSpec((n_chunk, LANES), const)] * 3,
        out_specs=[pl.BlockSpec((1, 1, n_chunk, LANES), lambda b, g: (b, g, 0, 0)),
                   pl.BlockSpec((1, 1, HEAD_DIM, n_chunk), lambda b, g: (b, g, 0, 0))],
        out_shape=[jax.ShapeDtypeStruct((B, G, n_chunk, LANES), BF16),
                   jax.ShapeDtypeStruct((B, G, HEAD_DIM, n_chunk), BF16)],
        compiler_params=_params("parallel", "parallel"),
    )(chk, chv, w1k, w2k, pek, w1v, w2vt, pev, *tables)


def _pad_rows(x, rows):
    return jnp.concatenate([x, jnp.zeros((rows - x.shape[0], x.shape[1]), x.dtype)], axis=0)


def _cmp_attn_kernel(q_ref, kc_ref, vct_ref, ovt_ref, oc_ref, bias_ref, *, t, n_cmp, n_blk):
    i = pl.program_id(2)
    kc = kc_ref[0, 0]
    vct = vct_ref[0, 0]
    n_pad = kc.shape[0]
    blk_c = _iota((n_pad, t), 0)
    qry = i * t + _iota((n_pad, t), 1)
    valid = jnp.logical_and(blk_c * CMP_STRIDE + (CMP_LEN - 1) <= qry, blk_c < n_cmp)
    p_sum = jnp.zeros((n_pad, t), F32)
    logits = [_dot(kc, _pad_rows(q_ref[0, r * HEAD_DIM:(r + 1) * HEAD_DIM, :], LANES))
              for r in range(NSA_REP)]
    for r in range(NSA_REP):
        s = jnp.where(valid, logits[r], NEG)
        e = jnp.exp2(s - _col_max(s))
        p = jnp.where(valid, e * (1.0 / _col_sum(e)), 0.0)
        p_sum = p_sum + p
        oc_ref[0, r * HEAD_DIM:(r + 1) * HEAD_DIM, :] = _dot(vct, p.astype(BF16)).astype(oc_ref.dtype)

    ovt = ovt_ref[...]
    p_hi, p_lo = _split2(p_sum)
    imp = _dot(ovt, p_hi) + _dot(ovt, p_lo)
    blk = _iota(imp.shape, 0)
    cur = (i * t + _iota(imp.shape, 1)) // SEL_LEN
    forced = jnp.logical_or(blk == 0, jnp.logical_or(blk == cur, blk == cur - 1))
    imp = jnp.where(forced, FORCE, jnp.where(blk <= cur, imp, -1.0))
    keep = jnp.zeros(imp.shape, jnp.bool_)
    for _ in range(min(N_SELECT, n_blk)):
        top = _col_max(imp)
        first = jnp.min(jnp.where(imp == top, blk, n_blk), axis=0, keepdims=True)
        pick = blk == first
        keep = jnp.logical_or(keep, pick)
        imp = jnp.where(pick, -jnp.inf, imp)
    bias_ref[0, 0] = jnp.where(keep, 0.0, NEG).astype(bias_ref.dtype)


def _cmp_attn(qt, kc, vct, ovt, n_cmp):
    B, W, S = qt.shape
    G = kc.shape[1]
    t = min(ATTN_TILE, S)
    n_blk = ovt.shape[0]
    gw = W // G
    kern = functools.partial(_cmp_attn_kernel, t=t, n_cmp=n_cmp, n_blk=n_blk)
    return pl.pallas_call(
        kern,
        name="nsa_cmp_attn",
        grid=(B, G, S // t),
        in_specs=[
            pl.BlockSpec((1, gw, t), lambda b, g, i: (b, g, i)),
            pl.BlockSpec((1, 1) + kc.shape[2:], lambda b, g, i: (b, g, 0, 0)),
            pl.BlockSpec((1, 1) + vct.shape[2:], lambda b, g, i: (b, g, 0, 0)),
            pl.BlockSpec(ovt.shape, lambda b, g, i: (0, 0)),
        ],
        out_specs=[
            pl.BlockSpec((1, gw, t), lambda b, g, i: (b, g, i)),
            pl.BlockSpec((1, 1, n_blk, t), lambda b, g, i: (b, g, 0, i)),
        ],
        out_shape=[
            jax.ShapeDtypeStruct((B, W, S), BF16),
            jax.ShapeDtypeStruct((B, G, n_blk, S), BF16),
        ],
        compiler_params=_params("parallel", "parallel", "parallel"),
    )(qt, kc, vct, ovt)


def _sel_attn_kernel(q_ref, bias_ref, k_ref, v_ref, o_ref, q_scr, s_scr, acc_scr, *, t):
    i = pl.program_id(2)
    nh = q_scr.shape[0]
    for hh in range(nh):
        bias = _pad_rows(bias_ref[0, hh // NSA_REP], HEAD_DIM)
        q_scr[hh] = jnp.concatenate([q_ref[0, hh * HEAD_DIM:(hh + 1) * HEAD_DIM, :], bias], axis=0)

    def key_tile(j, hh):
        return k_ref[0, hh // NSA_REP, pl.ds(pl.multiple_of(j * t, t), t), :]

    def value_rows(j, hh):
        g = hh // NSA_REP
        return v_ref[0, j, g * HEAD_DIM:(g + 1) * HEAD_DIM, :]

    ls = _causal_flash(i, t, q_scr, key_tile, value_rows, s_scr, acc_scr)
    for hh in range(nh):
        o_ref[0, hh * HEAD_DIM:(hh + 1) * HEAD_DIM, :] = (
            acc_scr[hh] * (1.0 / ls[hh])).astype(o_ref.dtype)


def _group_attn_specs(S, t, ng, extra_in=()):
    rows = ng * NSA_REP * HEAD_DIM
    return dict(
        in_specs=[pl.BlockSpec((1, rows, t), lambda b, g, i: (b, g, i))] + list(extra_in) + [
            pl.BlockSpec((1, ng, S, LANES), lambda b, g, i: (b, g, 0, 0)),
            pl.BlockSpec((1, S // t, ng * HEAD_DIM, t), lambda b, g, i: (b, 0, g, 0)),
        ],
        out_specs=pl.BlockSpec((1, rows, t), lambda b, g, i: (b, g, i)),
    )


def _sel_attn(qt, bias, k_aug, vt):
    B, W, S = qt.shape
    t = min(ATTN_TILE, S)
    ng = HEADS_PER_STEP // NSA_REP
    n_blk = bias.shape[2]
    bias_spec = pl.BlockSpec((1, ng, n_blk, t), lambda b, g, i: (b, g, 0, i))
    return pl.pallas_call(
        functools.partial(_sel_attn_kernel, t=t),
        name="nsa_sel_attn",
        grid=(B, NSA_GROUPS // ng, S // t),
        out_shape=jax.ShapeDtypeStruct((B, W, S), BF16),
        scratch_shapes=_attn_scratch(HEADS_PER_STEP, LANES, t),
        compiler_params=_params("parallel", "parallel", "parallel"),
        **_group_attn_specs(S, t, ng, [bias_spec]),
    )(qt, bias, k_aug, vt)


def _win_attn_kernel(q_ref, k_ref, v_ref, o_ref, *, t, n_back):
    i = pl.program_id(2)
    above = _iota((t, t), 0) > _iota((t, t), 1)
    tiles = []
    for d in range(n_back + 1):
        j = i - d
        jc = jnp.maximum(j, 0)
        k = k_ref[0, 0, pl.ds(pl.multiple_of(jc * t, t), t), :]
        if d == 0:
            ok = jnp.logical_not(above)
        elif d == n_back:
            ok = jnp.logical_and(above, j >= 0)
        else:
            ok = jnp.broadcast_to(j >= 0, (t, t))
        tiles.append((k, v_ref[0, jc], ok))
    logits = []
    for hh in range(NSA_REP):
        q_aug = _pad_rows(q_ref[0, hh * HEAD_DIM:(hh + 1) * HEAD_DIM, :], LANES)
        logits.append([_dot(k, q_aug) for k, _, _ in tiles])
    for hh in range(NSA_REP):
        ss = [jnp.where(ok, s, NEG) for s, (_, _, ok) in zip(logits[hh], tiles)]
        m = functools.reduce(jnp.maximum, [_col_max(s) for s in ss])
        ps = [jnp.exp2(s - m) for s in ss]
        l = functools.reduce(jnp.add, [_col_sum(p) for p in ps])
        acc = functools.reduce(jnp.add, [_dot(v, p.astype(BF16)) for (_, v, _), p in zip(tiles, ps)])
        o_ref[0, hh * HEAD_DIM:(hh + 1) * HEAD_DIM, :] = (acc * (1.0 / l)).astype(o_ref.dtype)


def _win_attn(qt, k_aug, vt):
    B, W, S = qt.shape
    t = min(ATTN_TILE, S)
    assert WINDOW % t == 0
    return pl.pallas_call(
        functools.partial(_win_attn_kernel, t=t, n_back=WINDOW // t),
        name="nsa_win_attn",
        grid=(B, NSA_GROUPS, S // t),
        out_shape=jax.ShapeDtypeStruct((B, W, S), BF16),
        compiler_params=_params("parallel", "parallel", "parallel"),
        **_group_attn_specs(S, t, 1),
    )(qt, k_aug, vt)


def _nsa_out_kernel(oc_ref, os_ref, ow_ref, gt_ref, zt_ref, x_ref, w_ref, g_ref, y_ref, gated_scr):
    for h in range(N_HEADS):
        rows = slice(h * HEAD_DIM, (h + 1) * HEAD_DIM)
        o = jnp.zeros((HEAD_DIM, gated_scr.shape[1]), F32)
        for b, ref in enumerate((oc_ref, os_ref, ow_ref)):
            gate = gt_ref[0, N_BRANCH * h + b:N_BRANCH * h + b + 1, :]
            o = o + gate * ref[0, rows, :].astype(F32)
        gated_scr[rows, :] = (o * _silu(zt_ref[0, rows, :].astype(F32))).astype(BF16)
    y = x_ref[0] + _dot_tn(gated_scr[...], w_ref[...])
    y_ref[0] = _rmsnorm(y, g_ref[...])


def _nsa_out(oc, os_, ow, gt, zt, x, w_out, final_g):
    B, S, _ = x.shape
    tm = min(PROJ_ROWS, S)
    feat = lambda rows: pl.BlockSpec((1, rows, tm), lambda b, s: (b, 0, s))
    tok = pl.BlockSpec((1, tm, D_MODEL), lambda b, s: (b, s, 0))
    wide = feat(ATTN_WIDTH)
    return pl.pallas_call(
        _nsa_out_kernel,
        name="nsa_out",
        grid=(B, S // tm),
        in_specs=[wide, wide, wide, feat(LANES), wide, tok,
                  pl.BlockSpec((ATTN_WIDTH, D_MODEL), lambda b, s: (0, 0)),
                  pl.BlockSpec((1, D_MODEL), lambda b, s: (0, 0))],
        out_specs=tok,
        out_shape=jax.ShapeDtypeStruct((B, S, D_MODEL), F32),
        scratch_shapes=[pltpu.VMEM((ATTN_WIDTH, tm), BF16)],
        compiler_params=_params("parallel", "parallel"),
    )(oc, os_, ow, gt, zt, x, w_out, final_g)


def _fox_layer(x, g, w_in, b_f, w_out):
    B, S, _ = x.shape
    W = ATTN_WIDTH
    w_k = w_in[:, W:2 * W].astype(BF16)
    w_t = jnp.concatenate([w_in[:, :W] * Q_SCALE, w_in[:, 2 * W:3 * W], w_in[:, 3 * W + N_HEADS:]],
                          axis=1).T.astype(BF16)
    pad = jnp.zeros((D_MODEL, LANES - 3 * N_HEADS), F32)
    w_f = w_in[:, 3 * W:3 * W + N_HEADS]
    w_f = jnp.concatenate([w_f, w_f, w_f, pad], axis=1)
    wf_hi, wf_lo = _split2(w_f)
    bf_row = jnp.concatenate([b_f, b_f, b_f, jnp.zeros((LANES - 3 * N_HEADS,), F32)])[None, :]

    qt, k, vt, zt, cx = _fox_in(x, g[None, :], w_k, w_t, wf_hi, wf_lo, bf_row)

    t = min(ATTN_TILE, S)
    r = jnp.arange(LANES)
    head = jnp.arange(N_HEADS)
    sel = jnp.logical_and(r[None, :] % N_HEADS == head[:, None], r[None, :] < 3 * N_HEADS)
    qx = jnp.broadcast_to(jnp.where(sel, -1.0, 0.0).astype(BF16)[:, :, None], (N_HEADS, LANES, t))
    ot = _fox_attn(qt, qx, k, cx, vt)
    return _fox_out(ot, zt, x, w_out.astype(BF16))


def _pad_groups(w):
    k = w.shape[0]
    w = w.reshape(k, NSA_GROUPS, HEAD_DIM)
    return jnp.concatenate([w, jnp.zeros_like(w)], axis=2).reshape(k, NSA_GROUPS * LANES)


def _nsa_layer(x, g, w_in, pe_k, w_ck1, w_ck2, pe_v, w_cv1, w_cv2, w_out, final_g):
    B, S, _ = x.shape
    W, KV, G = ATTN_WIDTH, KV_WIDTH, NSA_GROUPS
    sec = lambda j: w_in[:, W + j * KV:W + (j + 1) * KV]
    gate_off = W + 6 * KV
    n_gate = N_BRANCH * N_HEADS
    w_gate = jnp.concatenate([w_in[:, gate_off:gate_off + n_gate],
                              jnp.zeros((D_MODEL, LANES - n_gate), F32)], axis=1)
    w_k = jnp.concatenate([_pad_groups(sec(2)), _pad_groups(sec(4)), sec(0), sec(1)],
                          axis=1).astype(BF16)
    w_t = jnp.concatenate([w_in[:, :W] * Q_SCALE, sec(3), sec(5), w_in[:, gate_off + n_gate:], w_gate],
                          axis=1).T.astype(BF16)
    pos = jnp.arange(S)
    cos, sin = _rope_angles(pos)
    qt, ksel, kwin, cmp_raw, vsel, vwin, zt, gt = _nsa_in(
        x, g[None, :], w_k, w_t, _rope_lane_tables(pos), (cos.T, sin.T))

    n_chunk = S // CMP_STRIDE
    n_cmp = n_chunk - CMP_LEN // CMP_STRIDE + 1

    def chunks(a):
        a = a.reshape(B, n_chunk, CMP_STRIDE, G, HEAD_DIM).transpose(0, 3, 1, 2, 4)
        return a.reshape(B, G, n_chunk, CMP_STRIDE * HEAD_DIM)

    def flat_pe(pe):
        pe = pe.reshape(1, CMP_LEN * HEAD_DIM)
        return jnp.broadcast_to(pe, (SUBLANES, CMP_LEN * HEAD_DIM)).astype(BF16)

    w2k = jnp.concatenate([w_ck2, jnp.zeros_like(w_ck2)], axis=1).astype(BF16)
    cmp_end = jnp.arange(n_chunk) * CMP_STRIDE + CMP_LEN - 1
    kc, vct = _compress(chunks(cmp_raw[..., :KV]), chunks(cmp_raw[..., KV:]),
                        w_ck1.astype(BF16), w2k, flat_pe(pe_k),
                        w_cv1.astype(BF16), w_cv2.T.astype(BF16), flat_pe(pe_v),
                        _rope_lane_tables(cmp_end))

    n_blk = S // SEL_LEN
    ci = jnp.arange(n_chunk) * CMP_STRIDE
    sj = jnp.arange(n_blk) * SEL_LEN
    ovt = jnp.logical_and(ci[None, :] < sj[:, None] + SEL_LEN, ci[None, :] + CMP_LEN > sj[:, None])
    ovt = jnp.logical_and(ovt, jnp.arange(n_chunk)[None, :] < n_cmp).astype(BF16)
    oc, bias = _cmp_attn(qt, kc, vct, ovt, n_cmp)
    os_ = _sel_attn(qt, bias, ksel, vsel)
    ow = _win_attn(qt, kwin, vwin)
    return _nsa_out(oc, os_, ow, gt, zt, x, w_out.astype(BF16), final_g[None, :])


def kernel(x, norm_g, fox_w_in, fox_b_f, fox_w_out, nsa_w_in, nsa_pe_k, nsa_w_ck1, nsa_w_ck2,
           nsa_pe_v, nsa_w_cv1, nsa_w_cv2, nsa_w_out, final_g):
    x = _fox_layer(x, norm_g[0], fox_w_in[0], fox_b_f[0], fox_w_out[0])
    return _nsa_layer(x, norm_g[1], nsa_w_in[0], nsa_pe_k[0], nsa_w_ck1[0], nsa_w_ck2[0],
                      nsa_pe_v[0], nsa_w_cv1[0], nsa_w_cv2[0], nsa_w_out[0], final_g)
```

```python
import functools
import math

import jax
import jax.numpy as jnp
from jax import lax
from jax.experimental import pallas as pl
from jax.experimental.pallas import tpu as pltpu

F32 = jnp.float32
BF16 = jnp.bfloat16

D_MODEL = 1024
N_HEADS = 16
HEAD_DIM = 64
ATTN_WIDTH = N_HEADS * HEAD_DIM
NSA_GROUPS = 4
NSA_REP = N_HEADS // NSA_GROUPS
KV_WIDTH = NSA_GROUPS * HEAD_DIM
CMP_LEN = 32
CMP_STRIDE = 16
CMP_HIDDEN = 256
SEL_LEN = 64
N_SELECT = 8
WINDOW = 512
N_BRANCH = 3
ROPE_THETA = 500000.0
ROPE_DIM = HEAD_DIM // 4
ROPE_HALF = ROPE_DIM // 2
NORM_EPS = 1e-6
NEG = -1e30
FORCE = 1e6
Q_SCALE = HEAD_DIM ** -0.5 * math.log2(math.e)
LOG2E = math.log2(math.e)

LANES = 128
SUBLANES = 8
ONES_ROWS = 2 * SUBLANES
PROJ_ROWS = 512
ATTN_TILE = 256
HEADS_PER_STEP = 8
VMEM_LIMIT = 56 * 1024 * 1024


def _params(*sem):
    return pltpu.CompilerParams(dimension_semantics=sem, vmem_limit_bytes=VMEM_LIMIT)


def _iota(shape, dim):
    return lax.broadcasted_iota(jnp.int32, shape, dim)


def _split2(x):
    hi = x.astype(BF16)
    lo = (x - hi.astype(F32)).astype(BF16)
    return hi, lo


def _split3(x):
    hi = x.astype(BF16)
    r1 = x - hi.astype(F32)
    mid = r1.astype(BF16)
    lo = (r1 - mid.astype(F32)).astype(BF16)
    return hi, mid, lo


def _dot(a, b):
    return jnp.dot(a, b, preferred_element_type=F32)


def _dot_nt(a, b):
    return lax.dot_general(a, b, (((1,), (1,)), ((), ())), preferred_element_type=F32)


def _dot_tn(a, b):
    return lax.dot_general(a, b, (((0,), (0,)), ((), ())), preferred_element_type=F32)


def _rmsnorm(x, g):
    ms = jnp.mean(x * x, axis=-1, keepdims=True)
    return x * lax.rsqrt(ms + NORM_EPS) * g


def _silu(x):
    return x * (1.0 / (1.0 + jnp.exp(-x)))


def _col_max(x):
    return jnp.max(x, axis=0, keepdims=True)


def _col_sum(x):
    return jnp.sum(x, axis=0, keepdims=True)


def _rope_lanes(x, cos, sin_lo, sin_hi):
    return (x * cos + pltpu.roll(x, LANES - ROPE_HALF, axis=1) * sin_lo
            + pltpu.roll(x, ROPE_HALF, axis=1) * sin_hi)


def _rope_rows(x, cos, sin):
    x1, x2 = x[:ROPE_HALF], x[ROPE_HALF:ROPE_DIM]
    return jnp.concatenate([x1 * cos - x2 * sin, x1 * sin + x2 * cos, x[ROPE_DIM:]], axis=0)


def _rope_angles(pos):
    inv_freq = jnp.power(ROPE_THETA, -jnp.arange(ROPE_HALF, dtype=F32) * (2.0 / ROPE_DIM))
    ang = pos.astype(F32)[:, None] * inv_freq[None, :]
    return jnp.cos(ang), jnp.sin(ang)


def _rope_lane_tables(pos):
    cos, sin = _rope_angles(pos)
    n = pos.shape[0]
    ones = jnp.ones((n, HEAD_DIM - ROPE_DIM), F32)
    zeros = jnp.zeros((n, HEAD_DIM - ROPE_DIM), F32)
    z8 = jnp.zeros((n, ROPE_HALF), F32)
    c = jnp.concatenate([cos, cos, ones], axis=1)
    s_lo = jnp.concatenate([-sin, z8, zeros], axis=1)
    s_hi = jnp.concatenate([z8, sin, zeros], axis=1)
    tile = lambda t: jnp.concatenate([t, t], axis=1)
    return tile(c), tile(s_lo), tile(s_hi)


def _fox_in_kernel(x_ref, g_ref, wk_ref, wt_ref, wfh_ref, wfl_ref, bf_ref, tri_ref,
                   qt_ref, k_ref, vt_ref, zt_ref, cx_ref, carry_ref, *, tk):
    s = pl.program_id(1)
    h = _rmsnorm(x_ref[0], g_ref[...])
    hb = h.astype(BF16)
    tm = hb.shape[0]
    W = ATTN_WIDTH
    for n0 in range(0, W, 512):
        k_ref[0, :, n0:n0 + 512] = _dot(hb, wk_ref[:, n0:n0 + 512]).astype(BF16)
    for r0 in range(0, 3 * W, 512):
        t = _dot_nt(wt_ref[r0:r0 + 512, :], hb).astype(BF16)
        if r0 < W:
            qt_ref[0, r0:r0 + 512, :] = t
        elif r0 < 2 * W:
            for c in range(tm // tk):
                vt_ref[0, c, r0 - W:r0 - W + 512, :] = t[:, c * tk:(c + 1) * tk]
        else:
            zt_ref[0, r0 - 2 * W:r0 - 2 * W + 512, :] = t

    h_lo = (h - hb.astype(F32)).astype(BF16)
    f = _dot(hb, wfh_ref[...]) + _dot(h_lo, wfh_ref[...]) + _dot(hb, wfl_ref[...])
    f = f + bf_ref[...]
    log_f = jnp.minimum(f, 0.0) - jnp.log1p(jnp.exp(-jnp.abs(f)))

    @pl.when(s == 0)
    def _():
        carry_ref[...] = jnp.zeros_like(carry_ref)

    a0, a1, a2 = _split3(log_f)
    tri = tri_ref[...]
    c = _dot(tri, a0) + _dot(tri, a1) + _dot(tri, a2) + carry_ref[0:1, :]
    carry_ref[...] = jnp.broadcast_to(c[tm - 1:tm, :], carry_ref.shape)

    hi, mid, lo = _split3(c * LOG2E)
    lane = _iota(c.shape, 1)
    zero = jnp.zeros_like(hi)
    cx_ref[0] = jnp.where(lane < N_HEADS, hi,
                          jnp.where(lane < 2 * N_HEADS, mid,
                                    jnp.where(lane < 3 * N_HEADS, lo, zero)))


def _fox_in(x, g, w_k, w_t, wf_hi, wf_lo, bf_row):
    B, S, _ = x.shape
    tm = min(PROJ_ROWS, S)
    tk = min(ATTN_TILE, S)
    W = ATTN_WIDTH
    tri = (jnp.arange(tm)[:, None] >= jnp.arange(tm)[None, :]).astype(BF16)
    const = lambda b, s: (0, 0)
    feat = pl.BlockSpec((1, W, tm), lambda b, s: (b, 0, s))
    return pl.pallas_call(
        functools.partial(_fox_in_kernel, tk=tk),
        name="fox_in",
        grid=(B, S // tm),
        in_specs=[
            pl.BlockSpec((1, tm, D_MODEL), lambda b, s: (b, s, 0)),
            pl.BlockSpec((1, D_MODEL), const),
            pl.BlockSpec(w_k.shape, const),
            pl.BlockSpec(w_t.shape, const),
            pl.BlockSpec((D_MODEL, LANES), const),
            pl.BlockSpec((D_MODEL, LANES), const),
            pl.BlockSpec((1, LANES), const),
            pl.BlockSpec((tm, tm), const),
        ],
        out_specs=[
            feat,
            pl.BlockSpec((1, tm, W), lambda b, s: (b, s, 0)),
            pl.BlockSpec((1, tm // tk, W, tk), lambda b, s: (b, s, 0, 0)),
            feat,
            pl.BlockSpec((1, tm, LANES), lambda b, s: (b, s, 0)),
        ],
        out_shape=[
            jax.ShapeDtypeStruct((B, W, S), BF16),
            jax.ShapeDtypeStruct((B, S, W), BF16),
            jax.ShapeDtypeStruct((B, S // tk, W, tk), BF16),
            jax.ShapeDtypeStruct((B, W, S), BF16),
            jax.ShapeDtypeStruct((B, S, LANES), BF16),
        ],
        scratch_shapes=[pltpu.VMEM((SUBLANES, LANES), F32)],
        compiler_params=_params("parallel", "arbitrary"),
    )(x, g, w_k, w_t, wf_hi, wf_lo, bf_row, tri)


def _causal_flash(i, t, q_scr, key_tile, value_rows, s_scr, acc_scr):
    heads = range(q_scr.shape[0])
    ones = jnp.ones((ONES_ROWS, t), BF16)

    def logits(j):
        return [_dot(key_tile(j, hh), q_scr[hh]) for hh in heads]

    def pv(j, hh, p):
        return _dot(jnp.concatenate([value_rows(j, hh), ones], axis=0), p.astype(BF16))

    def consume(j, carry):
        return [pv(j, hh, jnp.exp2(s_scr[hh] - carry[hh][0])) for hh in heads]

    def accumulate(carry, pvs):
        for hh in heads:
            acc_scr[hh] = carry[hh][1] * acc_scr[hh] + pvs[hh]

    s_diag = logits(i)
    for hh, s in enumerate(logits(0)):
        s_scr[hh] = s
    causal = _iota((t, t), 0) <= _iota((t, t), 1)
    m_diag = []
    for hh in heads:
        s = jnp.where(causal, s_diag[hh], NEG)
        m = _col_max(s)
        acc_scr[hh] = pv(i, hh, jnp.exp2(s - m))
        m_diag.append(m)
    carry = []
    for hh in heads:
        m_new = jnp.maximum(m_diag[hh], _col_max(s_scr[hh]))
        carry.append((m_new, jnp.exp2(m_diag[hh] - m_new)))

    def body(j, carry):
        s_next = logits(j + 1)
        pvs = consume(j, carry)
        out = []
        for hh in heads:
            m = carry[hh][0]
            s_scr[hh] = s_next[hh]
            m_new = jnp.maximum(m, _col_max(s_next[hh]))
            out.append((m_new, jnp.exp2(m - m_new)))
        accumulate(carry, pvs)
        return tuple(out)

    carry = lax.fori_loop(0, i - 1, body, tuple(carry))

    @pl.when(i >= 1)
    def _():
        accumulate(carry, consume(i - 1, carry))


def _normalised(acc):
    return acc[:HEAD_DIM] * (1.0 / acc[HEAD_DIM:HEAD_DIM + 1])


def _fox_attn_kernel(q_ref, qx_ref, k_ref, kx_ref, v_ref, o_ref, q_scr, s_scr, acc_scr, *, t):
    i = pl.program_id(2)
    nh = q_scr.shape[0]
    row = _iota((LANES, t), 0)
    for hh in range(nh):
        pair, sub = divmod(hh, 2)
        qt = q_ref[0, pair * LANES:(pair + 1) * LANES, :]
        own = jnp.logical_and(row >= sub * HEAD_DIM, row < (sub + 1) * HEAD_DIM)
        q_scr[hh] = jnp.concatenate([jnp.where(own, qt, jnp.zeros_like(qt)), qx_ref[hh]], axis=0)

    def key_tile(j, hh):
        start = pl.multiple_of(j * t, t)
        pair = hh // 2
        return jnp.concatenate([k_ref[0, pl.ds(start, t), pair * LANES:(pair + 1) * LANES],
                                kx_ref[0, pl.ds(start, t), :]], axis=1)

    def value_rows(j, hh):
        return v_ref[0, j, hh * HEAD_DIM:(hh + 1) * HEAD_DIM, :]

    _causal_flash(i, t, q_scr, key_tile, value_rows, s_scr, acc_scr)
    for hh in range(nh):
        o_ref[0, hh * HEAD_DIM:(hh + 1) * HEAD_DIM, :] = _normalised(acc_scr[hh]).astype(o_ref.dtype)


def _attn_scratch(nh, k_depth, t):
    return [pltpu.VMEM((nh, k_depth, t), BF16), pltpu.VMEM((nh, t, t), F32),
            pltpu.VMEM((nh, HEAD_DIM + ONES_ROWS, t), F32)]


def _fox_attn(qt, qx, k, kx, vt):
    B, W, S = qt.shape
    t = min(ATTN_TILE, S)
    nh = HEADS_PER_STEP
    rows = nh * HEAD_DIM
    return pl.pallas_call(
        functools.partial(_fox_attn_kernel, t=t),
        name="fox_attn",
        grid=(B, N_HEADS // nh, S // t),
        in_specs=[
            pl.BlockSpec((1, rows, t), lambda b, p, i: (b, p, i)),
            pl.BlockSpec((nh, LANES, t), lambda b, p, i: (p, 0, 0)),
            pl.BlockSpec((1, S, rows), lambda b, p, i: (b, 0, p)),
            pl.BlockSpec((1, S, LANES), lambda b, p, i: (b, 0, 0)),
            pl.BlockSpec((1, S // t, rows, t), lambda b, p, i: (b, 0, p, 0)),
        ],
        out_specs=pl.BlockSpec((1, rows, t), lambda b, p, i: (b, p, i)),
        out_shape=jax.ShapeDtypeStruct((B, W, S), BF16),
        scratch_shapes=_attn_scratch(nh, 2 * LANES, t),
        compiler_params=_params("parallel", "parallel", "parallel"),
    )(qt, qx, k, kx, vt)


def _fox_out_kernel(ot_ref, zt_ref, x_ref, w_ref, y_ref):
    gated = ot_ref[0].astype(F32) * _silu(zt_ref[0].astype(F32))
    y_ref[0] = x_ref[0] + _dot_tn(gated.astype(BF16), w_ref[...])


def _fox_out(ot, zt, x, w_out):
    B, S, _ = x.shape
    tm = min(PROJ_ROWS, S)
    feat = pl.BlockSpec((1, ATTN_WIDTH, tm), lambda b, s: (b, 0, s))
    tok = pl.BlockSpec((1, tm, D_MODEL), lambda b, s: (b, s, 0))
    return pl.pallas_call(
        _fox_out_kernel,
        name="fox_out",
        grid=(B, S // tm),
        in_specs=[feat, feat, tok, pl.BlockSpec((ATTN_WIDTH, D_MODEL), lambda b, s: (0, 0))],
        out_specs=tok,
        out_shape=jax.ShapeDtypeStruct((B, S, D_MODEL), F32),
        compiler_params=_params("parallel", "parallel"),
    )(ot, zt, x, w_out)


_TOK_SEL = 0
_TOK_WIN = NSA_GROUPS * LANES
_TOK_CMP = 2 * NSA_GROUPS * LANES
_TOK_COLS = _TOK_CMP + 2 * KV_WIDTH
_FEAT_Q = 0
_FEAT_VSEL = ATTN_WIDTH
_FEAT_VWIN = _FEAT_VSEL + KV_WIDTH
_FEAT_Z = _FEAT_VWIN + KV_WIDTH
_FEAT_GATE = _FEAT_Z + ATTN_WIDTH
_FEAT_ROWS = _FEAT_GATE + LANES


def _nsa_in_kernel(x_ref, g_ref, wk_ref, wt_ref, cos_ref, slo_ref, shi_ref, cost_ref, sint_ref,
                   qt_ref, ksel_ref, kwin_ref, cmp_ref, vsel_ref, vwin_ref, zt_ref, gt_ref, *, tk):
    s = pl.program_id(1)
    hb = _rmsnorm(x_ref[0], g_ref[...]).astype(BF16)
    tm = hb.shape[0]
    G = NSA_GROUPS

    cos, slo, shi = cos_ref[...], slo_ref[...], shi_ref[...]
    tok = s * tm + _iota((tm, LANES), 0)
    lane = _iota((tm, LANES), 1)
    block_id = jnp.where(lane - HEAD_DIM == tok // SEL_LEN, 1.0, 0.0)
    ksel = _dot(hb, wk_ref[:, _TOK_SEL:_TOK_SEL + G * LANES])
    kwin = _dot(hb, wk_ref[:, _TOK_WIN:_TOK_WIN + G * LANES])
    for g in range(G):
        blk = _rope_lanes(ksel[:, g * LANES:(g + 1) * LANES], cos, slo, shi)
        ksel_ref[0, g] = (blk + block_id).astype(BF16)
        kwin_ref[0, g] = _rope_lanes(kwin[:, g * LANES:(g + 1) * LANES], cos, slo, shi).astype(BF16)
    cmp_ref[0] = _dot(hb, wk_ref[:, _TOK_CMP:_TOK_CMP + 2 * KV_WIDTH]).astype(BF16)

    cost, sint = cost_ref[...], sint_ref[...]
    for r0 in range(0, ATTN_WIDTH, 512):
        t = _dot_nt(wt_ref[_FEAT_Q + r0:_FEAT_Q + r0 + 512, :], hb)
        for h0 in range(0, 512, HEAD_DIM):
            qt_ref[0, r0 + h0:r0 + h0 + HEAD_DIM, :] = _rope_rows(
                t[h0:h0 + HEAD_DIM], cost, sint).astype(BF16)
        zt_ref[0, r0:r0 + 512, :] = _dot_nt(wt_ref[_FEAT_Z + r0:_FEAT_Z + r0 + 512, :], hb).astype(BF16)
    for row0, out_ref in ((_FEAT_VSEL, vsel_ref), (_FEAT_VWIN, vwin_ref)):
        t = _dot_nt(wt_ref[row0:row0 + KV_WIDTH, :], hb).astype(BF16)
        for c in range(tm // tk):
            out_ref[0, c] = t[:, c * tk:(c + 1) * tk]
    gate = _dot_nt(wt_ref[_FEAT_GATE:_FEAT_GATE + LANES, :], hb)
    gt_ref[0] = 1.0 / (1.0 + jnp.exp(-gate))


def _nsa_in(x, g, w_k, w_t, lane_tables, row_tables):
    B, S, _ = x.shape
    tm = min(PROJ_ROWS, S)
    tk = min(ATTN_TILE, S)
    W, G = ATTN_WIDTH, NSA_GROUPS
    const = lambda b, s: (0, 0)
    feat = lambda rows: pl.BlockSpec((1, rows, tm), lambda b, s: (b, 0, s))
    tiles = pl.BlockSpec((1, tm // tk, KV_WIDTH, tk), lambda b, s: (b, s, 0, 0))
    keys = pl.BlockSpec((1, G, tm, LANES), lambda b, s: (b, 0, s, 0))
    return pl.pallas_call(
        functools.partial(_nsa_in_kernel, tk=tk),
        name="nsa_in",
        grid=(B, S // tm),
        in_specs=[
            pl.BlockSpec((1, tm, D_MODEL), lambda b, s: (b, s, 0)),
            pl.BlockSpec((1, D_MODEL), const),
            pl.BlockSpec(w_k.shape, const),
            pl.BlockSpec(w_t.shape, const),
        ] + [pl.BlockSpec((tm, LANES), lambda b, s: (s, 0))] * 3
          + [pl.BlockSpec((ROPE_HALF, tm), lambda b, s: (0, s))] * 2,
        out_specs=[feat(W), keys, keys,
                   pl.BlockSpec((1, tm, 2 * KV_WIDTH), lambda b, s: (b, s, 0)),
                   tiles, tiles, feat(W), feat(LANES)],
        out_shape=[
            jax.ShapeDtypeStruct((B, W, S), BF16),
            jax.ShapeDtypeStruct((B, G, S, LANES), BF16),
            jax.ShapeDtypeStruct((B, G, S, LANES), BF16),
            jax.ShapeDtypeStruct((B, S, 2 * KV_WIDTH), BF16),
            jax.ShapeDtypeStruct((B, S // tk, KV_WIDTH, tk), BF16),
            jax.ShapeDtypeStruct((B, S // tk, KV_WIDTH, tk), BF16),
            jax.ShapeDtypeStruct((B, W, S), BF16),
            jax.ShapeDtypeStruct((B, LANES, S), F32),
        ],
        compiler_params=_params("parallel", "parallel"),
    )(x, g, w_k, w_t, *lane_tables, *row_tables)


def _compress_kernel(chk_ref, chv_ref, w1k_ref, w2k_ref, pek_ref, w1v_ref, w2vt_ref, pev_ref,
                     cos_ref, slo_ref, shi_ref, kc_ref, vct_ref):
    half = CMP_STRIDE * HEAD_DIM

    def hidden(ch, w1_ref, pe_ref):
        a = _dot(ch, w1_ref[:half, :])
        b = _dot(ch, w1_ref[half:, :])
        b = pltpu.roll(b, b.shape[0] - 1, axis=0)
        pe = _dot(pe_ref[...], w1_ref[...])[0:1, :]
        return _silu(a + b + pe).astype(BF16)

    kc = _dot(hidden(chk_ref[0, 0], w1k_ref, pek_ref), w2k_ref[...])
    kc_ref[0, 0] = _rope_lanes(kc, cos_ref[...], slo_ref[...], shi_ref[...]).astype(BF16)
    vct_ref[0, 0] = _dot_nt(w2vt_ref[...], hidden(chv_ref[0, 0], w1v_ref, pev_ref)).astype(BF16)


def _compress(chk, chv, w1k, w2k, pek, w1v, w2vt, pev, tables):
    B, G, n_chunk, width = chk.shape
    const = lambda b, g: (0, 0)
    chunk_spec = pl.BlockSpec((1, 1, n_chunk, width), lambda b, g: (b, g, 0, 0))
    specs = lambda ws: [pl.BlockSpec(w.shape, const) for w in ws]
    return pl.pallas_call(
        _compress_kernel,
        name="nsa_compress",
        grid=(B, G),
        in_specs=[chunk_spec, chunk_spec] + specs((w1k, w2k, pek, w1v, w2vt, pev))
                 + [pl.BlockSpec((n_chunk, LANES), const)] * 3,
        out_specs=[pl.BlockSpec((1, 1, n_chunk, LANES), lambda b, g: (b, g, 0, 0)),
                   pl.BlockSpec((1, 1, HEAD_DIM, n_chunk), lambda b, g: (b, g, 0, 0))],
        out_shape=[jax.ShapeDtypeStruct((B, G, n_chunk, LANES), BF16),
                   jax.ShapeDtypeStruct((B, G, HEAD_DIM, n_chunk), BF16)],
        compiler_params=_params("parallel", "parallel"),
    )(chk, chv, w1k, w2k, pek, w1v, w2vt, pev, *tables)


def _pad_rows(x, rows):
    return jnp.concatenate([x, jnp.zeros((rows - x.shape[0], x.shape[1]), x.dtype)], axis=0)


def _cmp_attn_kernel(q_ref, kc_ref, vct_ref, ovt_ref, oc_ref, bias_ref, *, t, n_cmp, n_blk):
    i = pl.program_id(2)
    kc = kc_ref[0, 0]
    vct = vct_ref[0, 0]
    n_pad = kc.shape[0]
    blk_c = _iota((n_pad, t), 0)
    qry = i * t + _iota((n_pad, t), 1)
    valid = jnp.logical_and(blk_c * CMP_STRIDE + (CMP_LEN - 1) <= qry, blk_c < n_cmp)
    p_sum = jnp.zeros((n_pad, t), F32)
    logits = [_dot(kc, _pad_rows(q_ref[0, r * HEAD_DIM:(r + 1) * HEAD_DIM, :], LANES))
              for r in range(NSA_REP)]
    for r in range(NSA_REP):
        s = jnp.where(valid, logits[r], NEG)
        e = jnp.exp2(s - _col_max(s))
        p = jnp.where(valid, e * (1.0 / _col_sum(e)), 0.0)
        p_sum = p_sum + p
        oc_ref[0, r * HEAD_DIM:(r + 1) * HEAD_DIM, :] = _dot(vct, p.astype(BF16)).astype(oc_ref.dtype)

    ovt = ovt_ref[...]
    p_hi, p_lo = _split2(p_sum)
    imp = _dot(ovt, p_hi) + _dot(ovt, p_lo)
    blk = _iota(imp.shape, 0)
    cur = (i * t + _iota(imp.shape, 1)) // SEL_LEN
    forced = jnp.logical_or(blk == 0, jnp.logical_or(blk == cur, blk == cur - 1))
    imp = jnp.where(forced, FORCE, jnp.where(blk <= cur, imp, -1.0))
    keep = jnp.zeros(imp.shape, jnp.bool_)
    for _ in range(min(N_SELECT, n_blk)):
        top = _col_max(imp)
        first = jnp.min(jnp.where(imp == top, blk, n_blk), axis=0, keepdims=True)
        pick = blk == first
        keep = jnp.logical_or(keep, pick)
        imp = jnp.where(pick, -jnp.inf, imp)
    bias_ref[0, 0] = jnp.where(keep, 0.0, NEG).astype(bias_ref.dtype)


def _cmp_attn(qt, kc, vct, ovt, n_cmp):
    B, W, S = qt.shape
    G = kc.shape[1]
    t = min(ATTN_TILE, S)
    n_blk = ovt.shape[0]
    gw = W // G
    kern = functools.partial(_cmp_attn_kernel, t=t, n_cmp=n_cmp, n_blk=n_blk)
    return pl.pallas_call(
        kern,
        name="nsa_cmp_attn",
        grid=(B, G, S // t),
        in_specs=[
            pl.BlockSpec((1, gw, t), lambda b, g, i: (b, g, i)),
            pl.BlockSpec((1, 1) + kc.shape[2:], lambda b, g, i: (b, g, 0, 0)),
            pl.BlockSpec((1, 1) + vct.shape[2:], lambda b, g, i: (b, g, 0, 0)),
            pl.BlockSpec(ovt.shape, lambda b, g, i: (0, 0)),
        ],
        out_specs=[
            pl.BlockSpec((1, gw, t), lambda b, g, i: (b, g, i)),
            pl.BlockSpec((1, 1, n_blk, t), lambda b, g, i: (b, g, 0, i)),
        ],
        out_shape=[
            jax.ShapeDtypeStruct((B, W, S), BF16),
            jax.ShapeDtypeStruct((B, G, n_blk, S), BF16),
        ],
        compiler_params=_params("parallel", "parallel", "parallel"),
    )(qt, kc, vct, ovt)


def _sel_attn_kernel(q_ref, bias_ref, k_ref, v_ref, o_ref, q_scr, s_scr, acc_scr, *, t):
    i = pl.program_id(2)
    nh = q_scr.shape[0]
    for hh in range(nh):
        bias = _pad_rows(bias_ref[0, hh // NSA_REP], HEAD_DIM)
        q_scr[hh] = jnp.concatenate([q_ref[0, hh * HEAD_DIM:(hh + 1) * HEAD_DIM, :], bias], axis=0)

    def key_tile(j, hh):
        return k_ref[0, hh // NSA_REP, pl.ds(pl.multiple_of(j * t, t), t), :]

    def value_rows(j, hh):
        g = hh // NSA_REP
        return v_ref[0, j, g * HEAD_DIM:(g + 1) * HEAD_DIM, :]

    _causal_flash(i, t, q_scr, key_tile, value_rows, s_scr, acc_scr)
    for hh in range(nh):
        o_ref[0, hh * HEAD_DIM:(hh + 1) * HEAD_DIM, :] = _normalised(acc_scr[hh]).astype(o_ref.dtype)


def _group_attn_specs(S, t, ng, extra_in=()):
    rows = ng * NSA_REP * HEAD_DIM
    return dict(
        in_specs=[pl.BlockSpec((1, rows, t), lambda b, g, i: (b, g, i))] + list(extra_in) + [
            pl.BlockSpec((1, ng, S, LANES), lambda b, g, i: (b, g, 0, 0)),
            pl.BlockSpec((1, S // t, ng * HEAD_DIM, t), lambda b, g, i: (b, 0, g, 0)),
        ],
        out_specs=pl.BlockSpec((1, rows, t), lambda b, g, i: (b, g, i)),
    )


def _sel_attn(qt, bias, k_aug, vt):
    B, W, S = qt.shape
    t = min(ATTN_TILE, S)
    ng = HEADS_PER_STEP // NSA_REP
    n_blk = bias.shape[2]
    bias_spec = pl.BlockSpec((1, ng, n_blk, t), lambda b, g, i: (b, g, 0, i))
    return pl.pallas_call(
        functools.partial(_sel_attn_kernel, t=t),
        name="nsa_sel_attn",
        grid=(B, NSA_GROUPS // ng, S // t),
        out_shape=jax.ShapeDtypeStruct((B, W, S), BF16),
        scratch_shapes=_attn_scratch(HEADS_PER_STEP, LANES, t),
        compiler_params=_params("parallel", "parallel", "parallel"),
        **_group_attn_specs(S, t, ng, [bias_spec]),
    )(qt, bias, k_aug, vt)


def _win_attn_kernel(q_ref, k_ref, v_ref, o_ref, *, t, n_back):
    i = pl.program_id(2)
    above = _iota((t, t), 0) > _iota((t, t), 1)
    tiles = []
    for d in range(n_back + 1):
        j = i - d
        jc = jnp.maximum(j, 0)
        k = k_ref[0, 0, pl.ds(pl.multiple_of(jc * t, t), t), :]
        if d == 0:
            ok = jnp.logical_not(above)
        elif d == n_back:
            ok = jnp.logical_and(above, j >= 0)
        else:
            ok = jnp.broadcast_to(j >= 0, (t, t))
        v = jnp.concatenate([v_ref[0, jc], jnp.ones((ONES_ROWS, t), BF16)], axis=0)
        tiles.append((k, v, ok))
    logits = []
    for hh in range(NSA_REP):
        q_aug = _pad_rows(q_ref[0, hh * HEAD_DIM:(hh + 1) * HEAD_DIM, :], LANES)
        logits.append([_dot(k, q_aug) for k, _, _ in tiles])
    for hh in range(NSA_REP):
        ss = [jnp.where(ok, s, NEG) for s, (_, _, ok) in zip(logits[hh], tiles)]
        m = functools.reduce(jnp.maximum, [_col_max(s) for s in ss])
        acc = functools.reduce(jnp.add, [_dot(v, jnp.exp2(s - m).astype(BF16))
                                         for (_, v, _), s in zip(tiles, ss)])
        o_ref[0, hh * HEAD_DIM:(hh + 1) * HEAD_DIM, :] = _normalised(acc).astype(o_ref.dtype)


def _win_attn(qt, k_aug, vt):
    B, W, S = qt.shape
    t = min(ATTN_TILE, S)
    assert WINDOW % t == 0
    return pl.pallas_call(
        functools.partial(_win_attn_kernel, t=t, n_back=WINDOW // t),
        name="nsa_win_attn",
        grid=(B, NSA_GROUPS, S // t),
        out_shape=jax.ShapeDtypeStruct((B, W, S), BF16),
        compiler_params=_params("parallel", "parallel", "parallel"),
        **_group_attn_specs(S, t, 1),
    )(qt, k_aug, vt)


def _local_attn_kernel(q_ref, kc_ref, vct_ref, ovt_ref, kw_ref, vw_ref, oc_ref, bias_ref, ow_ref,
                       *, t, n_cmp, n_blk, n_back):
    i = pl.program_id(2)
    heads = range(NSA_REP)
    q_aug = [_pad_rows(q_ref[0, r * HEAD_DIM:(r + 1) * HEAD_DIM, :], LANES) for r in heads]

    kc = kc_ref[0, 0]
    cmp_logits = [_dot(kc, q_aug[r]) for r in heads]
    above = _iota((t, t), 0) > _iota((t, t), 1)
    ones = jnp.ones((ONES_ROWS, t), BF16)
    tiles = []
    for d in range(n_back + 1):
        j = i - d
        jc = jnp.maximum(j, 0)
        k = kw_ref[0, 0, pl.ds(pl.multiple_of(jc * t, t), t), :]
        if d == 0:
            ok = jnp.logical_not(above)
        elif d == n_back:
            ok = jnp.logical_and(above, j >= 0)
        else:
            ok = jnp.broadcast_to(j >= 0, (t, t))
        tiles.append((jnp.concatenate([vw_ref[0, jc], ones], axis=0), ok))
        for r in heads:
            tiles[-1] += (_dot(k, q_aug[r]),)

    vct = vct_ref[0, 0]
    n_pad = kc.shape[0]
    blk_c = _iota((n_pad, t), 0)
    qry = i * t + _iota((n_pad, t), 1)
    valid = jnp.logical_and(blk_c * CMP_STRIDE + (CMP_LEN - 1) <= qry, blk_c < n_cmp)
    p_sum = jnp.zeros((n_pad, t), F32)
    for r in heads:
        s = jnp.where(valid, cmp_logits[r], NEG)
        e = jnp.exp2(s - _col_max(s))
        p = jnp.where(valid, e * (1.0 / _col_sum(e)), 0.0)
        p_sum = p_sum + p
        oc_ref[0, r * HEAD_DIM:(r + 1) * HEAD_DIM, :] = _dot(vct, p.astype(BF16)).astype(oc_ref.dtype)

    ovt = ovt_ref[...]
    p_hi, p_lo = _split2(p_sum)
    imp = _dot(ovt, p_hi) + _dot(ovt, p_lo)
    blk = _iota(imp.shape, 0)
    cur = (i * t + _iota(imp.shape, 1)) // SEL_LEN
    forced = jnp.logical_or(blk == 0, jnp.logical_or(blk == cur, blk == cur - 1))
    imp = jnp.where(forced, FORCE, jnp.where(blk <= cur, imp, -1.0))
    keep = jnp.zeros(imp.shape, jnp.bool_)

    n_rounds = min(N_SELECT, n_blk)
    per_head = -(-n_rounds // NSA_REP)
    done = 0
    for r in heads:
        ss = [jnp.where(ok, logits[r], NEG) for _, ok, *logits in tiles]
        m = functools.reduce(jnp.maximum, [_col_max(s) for s in ss])
        acc = functools.reduce(jnp.add, [_dot(v, jnp.exp2(s - m).astype(BF16))
                                         for (v, *_), s in zip(tiles, ss)])
        ow_ref[0, r * HEAD_DIM:(r + 1) * HEAD_DIM, :] = _normalised(acc).astype(ow_ref.dtype)
        for _ in range(min(per_head, n_rounds - done)):
            top = _col_max(imp)
            first = jnp.min(jnp.where(imp == top, blk, n_blk), axis=0, keepdims=True)
            pick = blk == first
            keep = jnp.logical_or(keep, pick)
            imp = jnp.where(pick, -jnp.inf, imp)
            done += 1
    bias_ref[0, 0] = jnp.where(keep, 0.0, NEG).astype(bias_ref.dtype)


def _local_attn(qt, kc, vct, ovt, n_cmp, kw, vw):
    B, W, S = qt.shape
    G = kc.shape[1]
    t = min(ATTN_TILE, S)
    assert WINDOW % t == 0
    n_blk = ovt.shape[0]
    gw = W // G
    kern = functools.partial(_local_attn_kernel, t=t, n_cmp=n_cmp, n_blk=n_blk, n_back=WINDOW // t)
    feat = pl.BlockSpec((1, gw, t), lambda b, g, i: (b, g, i))
    return pl.pallas_call(
        kern,
        name="nsa_local_attn",
        grid=(B, G, S // t),
        in_specs=[
            feat,
            pl.BlockSpec((1, 1) + kc.shape[2:], lambda b, g, i: (b, g, 0, 0)),
            pl.BlockSpec((1, 1) + vct.shape[2:], lambda b, g, i: (b, g, 0, 0)),
            pl.BlockSpec(ovt.shape, lambda b, g, i: (0, 0)),
            pl.BlockSpec((1, 1, S, LANES), lambda b, g, i: (b, g, 0, 0)),
            pl.BlockSpec((1, S // t, HEAD_DIM, t), lambda b, g, i: (b, 0, g, 0)),
        ],
        out_specs=[feat, pl.BlockSpec((1, 1, n_blk, t), lambda b, g, i: (b, g, 0, i)), feat],
        out_shape=[
            jax.ShapeDtypeStruct((B, W, S), BF16),
            jax.ShapeDtypeStruct((B, G, n_blk, S), BF16),
            jax.ShapeDtypeStruct((B, W, S), BF16),
        ],
        compiler_params=_params("parallel", "parallel", "parallel"),
    )(qt, kc, vct, ovt, kw, vw)


def _nsa_out_kernel(oc_ref, os_ref, ow_ref, gt_ref, zt_ref, x_ref, w_ref, g_ref, y_ref, gated_scr):
    for h in range(N_HEADS):
        rows = slice(h * HEAD_DIM, (h + 1) * HEAD_DIM)
        o = jnp.zeros((HEAD_DIM, gated_scr.shape[1]), F32)
        for b, ref in enumerate((oc_ref, os_ref, ow_ref)):
            gate = gt_ref[0, N_BRANCH * h + b:N_BRANCH * h + b + 1, :]
            o = o + gate * ref[0, rows, :].astype(F32)
        gated_scr[rows, :] = (o * _silu(zt_ref[0, rows, :].astype(F32))).astype(BF16)
    y = x_ref[0] + _dot_tn(gated_scr[...], w_ref[...])
    y_ref[0] = _rmsnorm(y, g_ref[...])


def _nsa_out(oc, os_, ow, gt, zt, x, w_out, final_g):
    B, S, _ = x.shape
    tm = min(PROJ_ROWS, S)
    feat = lambda rows: pl.BlockSpec((1, rows, tm), lambda b, s: (b, 0, s))
    tok = pl.BlockSpec((1, tm, D_MODEL), lambda b, s: (b, s, 0))
    wide = feat(ATTN_WIDTH)
    return pl.pallas_call(
        _nsa_out_kernel,
        name="nsa_out",
        grid=(B, S // tm),
        in_specs=[wide, wide, wide, feat(LANES), wide, tok,
                  pl.BlockSpec((ATTN_WIDTH, D_MODEL), lambda b, s: (0, 0)),
                  pl.BlockSpec((1, D_MODEL), lambda b, s: (0, 0))],
        out_specs=tok,
        out_shape=jax.ShapeDtypeStruct((B, S, D_MODEL), F32),
        scratch_shapes=[pltpu.VMEM((ATTN_WIDTH, tm), BF16)],
        compiler_params=_params("parallel", "parallel"),
    )(oc, os_, ow, gt, zt, x, w_out, final_g)


def _fox_layer(x, g, w_in, b_f, w_out):
    B, S, _ = x.shape
    W = ATTN_WIDTH
    w_k = w_in[:, W:2 * W].astype(BF16)
    w_t = jnp.concatenate([w_in[:, :W] * Q_SCALE, w_in[:, 2 * W:3 * W], w_in[:, 3 * W + N_HEADS:]],
                          axis=1).T.astype(BF16)
    pad = jnp.zeros((D_MODEL, LANES - 3 * N_HEADS), F32)
    w_f = w_in[:, 3 * W:3 * W + N_HEADS]
    w_f = jnp.concatenate([w_f, w_f, w_f, pad], axis=1)
    wf_hi, wf_lo = _split2(w_f)
    bf_row = jnp.concatenate([b_f, b_f, b_f, jnp.zeros((LANES - 3 * N_HEADS,), F32)])[None, :]

    qt, k, vt, zt, cx = _fox_in(x, g[None, :], w_k, w_t, wf_hi, wf_lo, bf_row)

    t = min(ATTN_TILE, S)
    r = jnp.arange(LANES)
    head = jnp.arange(N_HEADS)
    sel = jnp.logical_and(r[None, :] % N_HEADS == head[:, None], r[None, :] < 3 * N_HEADS)
    qx = jnp.broadcast_to(jnp.where(sel, -1.0, 0.0).astype(BF16)[:, :, None], (N_HEADS, LANES, t))
    ot = _fox_attn(qt, qx, k, cx, vt)
    return _fox_out(ot, zt, x, w_out.astype(BF16))


def _pad_groups(w):
    k = w.shape[0]
    w = w.reshape(k, NSA_GROUPS, HEAD_DIM)
    return jnp.concatenate([w, jnp.zeros_like(w)], axis=2).reshape(k, NSA_GROUPS * LANES)


def _nsa_layer(x, g, w_in, pe_k, w_ck1, w_ck2, pe_v, w_cv1, w_cv2, w_out, final_g):
    B, S, _ = x.shape
    W, KV, G = ATTN_WIDTH, KV_WIDTH, NSA_GROUPS
    sec = lambda j: w_in[:, W + j * KV:W + (j + 1) * KV]
    gate_off = W + 6 * KV
    n_gate = N_BRANCH * N_HEADS
    w_gate = jnp.concatenate([w_in[:, gate_off:gate_off + n_gate],
                              jnp.zeros((D_MODEL, LANES - n_gate), F32)], axis=1)
    w_k = jnp.concatenate([_pad_groups(sec(2)), _pad_groups(sec(4)), sec(0), sec(1)],
                          axis=1).astype(BF16)
    w_t = jnp.concatenate([w_in[:, :W] * Q_SCALE, sec(3), sec(5), w_in[:, gate_off + n_gate:], w_gate],
                          axis=1).T.astype(BF16)
    pos = jnp.arange(S)
    cos, sin = _rope_angles(pos)
    qt, ksel, kwin, cmp_raw, vsel, vwin, zt, gt = _nsa_in(
        x, g[None, :], w_k, w_t, _rope_lane_tables(pos), (cos.T, sin.T))

    n_chunk = S // CMP_STRIDE
    n_cmp = n_chunk - CMP_LEN // CMP_STRIDE + 1

    def chunks(a):
        a = a.reshape(B, n_chunk, CMP_STRIDE, G, HEAD_DIM).transpose(0, 3, 1, 2, 4)
        return a.reshape(B, G, n_chunk, CMP_STRIDE * HEAD_DIM)

    def flat_pe(pe):
        pe = pe.reshape(1, CMP_LEN * HEAD_DIM)
        return jnp.broadcast_to(pe, (SUBLANES, CMP_LEN * HEAD_DIM)).astype(BF16)

    w2k = jnp.concatenate([w_ck2, jnp.zeros_like(w_ck2)], axis=1).astype(BF16)
    cmp_end = jnp.arange(n_chunk) * CMP_STRIDE + CMP_LEN - 1
    kc, vct = _compress(chunks(cmp_raw[..., :KV]), chunks(cmp_raw[..., KV:]),
                        w_ck1.astype(BF16), w2k, flat_pe(pe_k),
                        w_cv1.astype(BF16), w_cv2.T.astype(BF16), flat_pe(pe_v),
                        _rope_lane_tables(cmp_end))

    n_blk = S // SEL_LEN
    ci = jnp.arange(n_chunk) * CMP_STRIDE
    sj = jnp.arange(n_blk) * SEL_LEN
    ovt = jnp.logical_and(ci[None, :] < sj[:, None] + SEL_LEN, ci[None, :] + CMP_LEN > sj[:, None])
    ovt = jnp.logical_and(ovt, jnp.arange(n_chunk)[None, :] < n_cmp).astype(BF16)
    oc, bias, ow = _local_attn(qt, kc, vct, ovt, n_cmp, kwin, vwin)
    os_ = _sel_attn(qt, bias, ksel, vsel)
    return _nsa_out(oc, os_, ow, gt, zt, x, w_out.astype(BF16), final_g[None, :])


def kernel(x, norm_g, fox_w_in, fox_b_f, fox_w_out, nsa_w_in, nsa_pe_k, nsa_w_ck1, nsa_w_ck2,
           nsa_pe_v, nsa_w_cv1, nsa_w_cv2, nsa_w_out, final_g):
    x = _fox_layer(x, norm_g[0], fox_w_in[0], fox_b_f[0], fox_w_out[0])
    return _nsa_layer(x, norm_g[1], nsa_w_in[0], nsa_pe_k[0], nsa_w_ck1[0], nsa_w_ck2[0],
                      nsa_pe_v[0], nsa_w_cv1[0], nsa_w_cv2[0], nsa_w_out[0], final_g)
```

```python
import functools
import math

import jax
import jax.numpy as jnp
from jax import lax
from jax.experimental import pallas as pl
from jax.experimental.pallas import tpu as pltpu

F32 = jnp.float32
BF16 = jnp.bfloat16

D_MODEL = 1024
N_HEADS = 16
HEAD_DIM = 64
ATTN_WIDTH = N_HEADS * HEAD_DIM
NSA_GROUPS = 4
NSA_REP = N_HEADS // NSA_GROUPS
KV_WIDTH = NSA_GROUPS * HEAD_DIM
CMP_LEN = 32
CMP_STRIDE = 16
CMP_HIDDEN = 256
SEL_LEN = 64
N_SELECT = 8
WINDOW = 512
N_BRANCH = 3
ROPE_THETA = 500000.0
ROPE_DIM = HEAD_DIM // 4
ROPE_HALF = ROPE_DIM // 2
NORM_EPS = 1e-6
NEG = -1e30
FORCE = 1e6
Q_SCALE = HEAD_DIM ** -0.5 * math.log2(math.e)
LOG2E = math.log2(math.e)

LANES = 128
SUBLANES = 8
ONES_ROWS = 2 * SUBLANES
PROJ_ROWS = 512
ATTN_TILE = 256
HEADS_PER_STEP = 8
GATE_ROWS = -(-N_BRANCH * HEADS_PER_STEP // SUBLANES) * SUBLANES
VMEM_LIMIT = 56 * 1024 * 1024


def _params(*sem):
    return pltpu.CompilerParams(dimension_semantics=sem, vmem_limit_bytes=VMEM_LIMIT)


def _iota(shape, dim):
    return lax.broadcasted_iota(jnp.int32, shape, dim)


def _split2(x):
    hi = x.astype(BF16)
    lo = (x - hi.astype(F32)).astype(BF16)
    return hi, lo


def _split3(x):
    hi = x.astype(BF16)
    r1 = x - hi.astype(F32)
    mid = r1.astype(BF16)
    lo = (r1 - mid.astype(F32)).astype(BF16)
    return hi, mid, lo


def _dot(a, b):
    return jnp.dot(a, b, preferred_element_type=F32)


def _dot_nt(a, b):
    return lax.dot_general(a, b, (((1,), (1,)), ((), ())), preferred_element_type=F32)


def _dot_tn(a, b):
    return lax.dot_general(a, b, (((0,), (0,)), ((), ())), preferred_element_type=F32)


def _rmsnorm(x, g):
    ms = jnp.mean(x * x, axis=-1, keepdims=True)
    return x * lax.rsqrt(ms + NORM_EPS) * g


def _silu(x):
    return x * (1.0 / (1.0 + jnp.exp(-x)))


def _col_max(x):
    return jnp.max(x, axis=0, keepdims=True)


def _col_sum(x):
    return jnp.sum(x, axis=0, keepdims=True)


def _rope_lanes(x, cos, sin_lo, sin_hi):
    return (x * cos + pltpu.roll(x, LANES - ROPE_HALF, axis=1) * sin_lo
            + pltpu.roll(x, ROPE_HALF, axis=1) * sin_hi)


def _rope_rows(x, cos, sin):
    x1, x2 = x[:ROPE_HALF], x[ROPE_HALF:ROPE_DIM]
    return jnp.concatenate([x1 * cos - x2 * sin, x1 * sin + x2 * cos, x[ROPE_DIM:]], axis=0)


def _rope_angles(pos):
    inv_freq = jnp.power(ROPE_THETA, -jnp.arange(ROPE_HALF, dtype=F32) * (2.0 / ROPE_DIM))
    ang = pos.astype(F32)[:, None] * inv_freq[None, :]
    return jnp.cos(ang), jnp.sin(ang)


def _rope_lane_tables(pos):
    cos, sin = _rope_angles(pos)
    n = pos.shape[0]
    ones = jnp.ones((n, HEAD_DIM - ROPE_DIM), F32)
    zeros = jnp.zeros((n, HEAD_DIM - ROPE_DIM), F32)
    z8 = jnp.zeros((n, ROPE_HALF), F32)
    c = jnp.concatenate([cos, cos, ones], axis=1)
    s_lo = jnp.concatenate([-sin, z8, zeros], axis=1)
    s_hi = jnp.concatenate([z8, sin, zeros], axis=1)
    tile = lambda t: jnp.concatenate([t, t], axis=1)
    return tile(c), tile(s_lo), tile(s_hi)


def _fox_in_kernel(x_ref, g_ref, wk_ref, wt_ref, wfh_ref, wfl_ref, bf_ref, tri_ref,
                   qt_ref, k_ref, vt_ref, zt_ref, cx_ref, carry_ref, *, tk):
    s = pl.program_id(1)
    h = _rmsnorm(x_ref[0], g_ref[...])
    hb = h.astype(BF16)
    tm = hb.shape[0]
    W = ATTN_WIDTH
    for n0 in range(0, W, 512):
        k_ref[0, :, n0:n0 + 512] = _dot(hb, wk_ref[:, n0:n0 + 512]).astype(BF16)
    for r0 in range(0, 3 * W, 512):
        t = _dot_nt(wt_ref[r0:r0 + 512, :], hb).astype(BF16)
        if r0 < W:
            qt_ref[0, r0:r0 + 512, :] = t
        elif r0 < 2 * W:
            for c in range(tm // tk):
                vt_ref[0, c, r0 - W:r0 - W + 512, :] = t[:, c * tk:(c + 1) * tk]
        else:
            zt_ref[0, r0 - 2 * W:r0 - 2 * W + 512, :] = t

    h_lo = (h - hb.astype(F32)).astype(BF16)
    f = _dot(hb, wfh_ref[...]) + _dot(h_lo, wfh_ref[...]) + _dot(hb, wfl_ref[...])
    f = f + bf_ref[...]
    log_f = jnp.minimum(f, 0.0) - jnp.log1p(jnp.exp(-jnp.abs(f)))

    @pl.when(s == 0)
    def _():
        carry_ref[...] = jnp.zeros_like(carry_ref)

    a0, a1, a2 = _split3(log_f)
    tri = tri_ref[...]
    c = _dot(tri, a0) + _dot(tri, a1) + _dot(tri, a2) + carry_ref[0:1, :]
    carry_ref[...] = jnp.broadcast_to(c[tm - 1:tm, :], carry_ref.shape)

    hi, mid, lo = _split3(c * LOG2E)
    lane = _iota(c.shape, 1)
    zero = jnp.zeros_like(hi)
    cx_ref[0] = jnp.where(lane < N_HEADS, hi,
                          jnp.where(lane < 2 * N_HEADS, mid,
                                    jnp.where(lane < 3 * N_HEADS, lo, zero)))


def _fox_in(x, g, w_k, w_t, wf_hi, wf_lo, bf_row):
    B, S, _ = x.shape
    tm = min(PROJ_ROWS, S)
    tk = min(ATTN_TILE, S)
    W = ATTN_WIDTH
    tri = (jnp.arange(tm)[:, None] >= jnp.arange(tm)[None, :]).astype(BF16)
    const = lambda b, s: (0, 0)
    feat = pl.BlockSpec((1, W, tm), lambda b, s: (b, 0, s))
    return pl.pallas_call(
        functools.partial(_fox_in_kernel, tk=tk),
        name="fox_in",
        grid=(B, S // tm),
        in_specs=[
            pl.BlockSpec((1, tm, D_MODEL), lambda b, s: (b, s, 0)),
            pl.BlockSpec((1, D_MODEL), const),
            pl.BlockSpec(w_k.shape, const),
            pl.BlockSpec(w_t.shape, const),
            pl.BlockSpec((D_MODEL, LANES), const),
            pl.BlockSpec((D_MODEL, LANES), const),
            pl.BlockSpec((1, LANES), const),
            pl.BlockSpec((tm, tm), const),
        ],
        out_specs=[
            feat,
            pl.BlockSpec((1, tm, W), lambda b, s: (b, s, 0)),
            pl.BlockSpec((1, tm // tk, W, tk), lambda b, s: (b, s, 0, 0)),
            feat,
            pl.BlockSpec((1, tm, LANES), lambda b, s: (b, s, 0)),
        ],
        out_shape=[
            jax.ShapeDtypeStruct((B, W, S), BF16),
            jax.ShapeDtypeStruct((B, S, W), BF16),
            jax.ShapeDtypeStruct((B, S // tk, W, tk), BF16),
            jax.ShapeDtypeStruct((B, W, S), BF16),
            jax.ShapeDtypeStruct((B, S, LANES), BF16),
        ],
        scratch_shapes=[pltpu.VMEM((SUBLANES, LANES), F32)],
        compiler_params=_params("parallel", "arbitrary"),
    )(x, g, w_k, w_t, wf_hi, wf_lo, bf_row, tri)


def _causal_flash(i, t, q_scr, key_tile, value_rows, s_scr, acc_scr):
    heads = range(q_scr.shape[0])
    ones = jnp.ones((ONES_ROWS, t), BF16)

    def logits(j):
        return [_dot(key_tile(j, hh), q_scr[hh]) for hh in heads]

    def pv(j, hh, p):
        return _dot(jnp.concatenate([value_rows(j, hh), ones], axis=0), p.astype(BF16))

    def consume(j, carry):
        return [pv(j, hh, jnp.exp2(s_scr[hh] - carry[hh][0])) for hh in heads]

    def accumulate(carry, pvs):
        for hh in heads:
            acc_scr[hh] = carry[hh][1] * acc_scr[hh] + pvs[hh]

    s_diag = logits(i)
    for hh, s in enumerate(logits(0)):
        s_scr[hh] = s
    causal = _iota((t, t), 0) <= _iota((t, t), 1)
    m_diag = []
    for hh in heads:
        s = jnp.where(causal, s_diag[hh], NEG)
        m = _col_max(s)
        acc_scr[hh] = pv(i, hh, jnp.exp2(s - m))
        m_diag.append(m)
    carry = []
    for hh in heads:
        m_new = jnp.maximum(m_diag[hh], _col_max(s_scr[hh]))
        carry.append((m_new, jnp.exp2(m_diag[hh] - m_new)))

    def body(j, carry):
        s_next = logits(j + 1)
        pvs = consume(j, carry)
        out = []
        for hh in heads:
            m = carry[hh][0]
            s_scr[hh] = s_next[hh]
            m_new = jnp.maximum(m, _col_max(s_next[hh]))
            out.append((m_new, jnp.exp2(m - m_new)))
        accumulate(carry, pvs)
        return tuple(out)

    carry = lax.fori_loop(0, i - 1, body, tuple(carry))

    @pl.when(i >= 1)
    def _():
        accumulate(carry, consume(i - 1, carry))


def _normalised(acc):
    return acc[:HEAD_DIM] * (1.0 / acc[HEAD_DIM:HEAD_DIM + 1])


def _fox_attn_kernel(q_ref, qx_ref, k_ref, kx_ref, v_ref, z_ref, o_ref, q_scr, s_scr, acc_scr, *, t):
    i = pl.program_id(2)
    nh = q_scr.shape[0]
    row = _iota((LANES, t), 0)
    for hh in range(nh):
        pair, sub = divmod(hh, 2)
        qt = q_ref[0, pair * LANES:(pair + 1) * LANES, :]
        own = jnp.logical_and(row >= sub * HEAD_DIM, row < (sub + 1) * HEAD_DIM)
        q_scr[hh] = jnp.concatenate([jnp.where(own, qt, jnp.zeros_like(qt)), qx_ref[hh]], axis=0)

    def key_tile(j, hh):
        start = pl.multiple_of(j * t, t)
        pair = hh // 2
        return jnp.concatenate([k_ref[0, pl.ds(start, t), pair * LANES:(pair + 1) * LANES],
                                kx_ref[0, pl.ds(start, t), :]], axis=1)

    def value_rows(j, hh):
        return v_ref[0, j, hh * HEAD_DIM:(hh + 1) * HEAD_DIM, :]

    _causal_flash(i, t, q_scr, key_tile, value_rows, s_scr, acc_scr)
    for hh in range(nh):
        rows = slice(hh * HEAD_DIM, (hh + 1) * HEAD_DIM)
        gated = _normalised(acc_scr[hh]) * _silu(z_ref[0, rows, :].astype(F32))
        o_ref[0, rows, :] = gated.astype(o_ref.dtype)


def _attn_scratch(nh, k_depth, t):
    return [pltpu.VMEM((nh, k_depth, t), BF16), pltpu.VMEM((nh, t, t), F32),
            pltpu.VMEM((nh, HEAD_DIM + ONES_ROWS, t), F32)]


def _fox_attn(qt, qx, k, kx, vt, zt):
    B, W, S = qt.shape
    t = min(ATTN_TILE, S)
    nh = HEADS_PER_STEP
    rows = nh * HEAD_DIM
    feat = pl.BlockSpec((1, rows, t), lambda b, p, i: (b, p, i))
    return pl.pallas_call(
        functools.partial(_fox_attn_kernel, t=t),
        name="fox_attn",
        grid=(B, N_HEADS // nh, S // t),
        in_specs=[
            feat,
            pl.BlockSpec((nh, LANES, t), lambda b, p, i: (p, 0, 0)),
            pl.BlockSpec((1, S, rows), lambda b, p, i: (b, 0, p)),
            pl.BlockSpec((1, S, LANES), lambda b, p, i: (b, 0, 0)),
            pl.BlockSpec((1, S // t, rows, t), lambda b, p, i: (b, 0, p, 0)),
            feat,
        ],
        out_specs=feat,
        out_shape=jax.ShapeDtypeStruct((B, W, S), BF16),
        scratch_shapes=_attn_scratch(nh, 2 * LANES, t),
        compiler_params=_params("parallel", "parallel", "parallel"),
    )(qt, qx, k, kx, vt, zt)


def _out_proj_kernel(*refs, final_norm):
    if final_norm:
        ot_ref, x_ref, w_ref, g_ref, y_ref = refs
    else:
        ot_ref, x_ref, w_ref, y_ref = refs
    y = x_ref[0] + _dot_tn(ot_ref[0], w_ref[...])
    y_ref[0] = _rmsnorm(y, g_ref[...]) if final_norm else y


def _out_proj(ot, x, w_out, final_g=None):
    B, S, _ = x.shape
    tm = min(PROJ_ROWS, S)
    const = lambda b, s: (0, 0)
    tok = pl.BlockSpec((1, tm, D_MODEL), lambda b, s: (b, s, 0))
    in_specs = [pl.BlockSpec((1, ATTN_WIDTH, tm), lambda b, s: (b, 0, s)), tok,
                pl.BlockSpec((ATTN_WIDTH, D_MODEL), const)]
    args = [ot, x, w_out]
    if final_g is not None:
        in_specs.append(pl.BlockSpec((1, D_MODEL), const))
        args.append(final_g)
    return pl.pallas_call(
        functools.partial(_out_proj_kernel, final_norm=final_g is not None),
        name="out_proj",
        grid=(B, S // tm),
        in_specs=in_specs,
        out_specs=tok,
        out_shape=jax.ShapeDtypeStruct((B, S, D_MODEL), F32),
        compiler_params=_params("parallel", "parallel"),
    )(*args)


_TOK_SEL = 0
_TOK_WIN = NSA_GROUPS * LANES
_TOK_CMP = 2 * NSA_GROUPS * LANES
_TOK_COLS = _TOK_CMP + 2 * KV_WIDTH
_FEAT_Q = 0
_FEAT_VSEL = ATTN_WIDTH
_FEAT_VWIN = _FEAT_VSEL + KV_WIDTH
_FEAT_Z = _FEAT_VWIN + KV_WIDTH
_FEAT_GATE = _FEAT_Z + ATTN_WIDTH
_FEAT_ROWS = _FEAT_GATE + LANES


def _nsa_in_kernel(x_ref, g_ref, wk_ref, wt_ref, cos_ref, slo_ref, shi_ref, cost_ref, sint_ref,
                   qt_ref, ksel_ref, kwin_ref, chk_ref, chv_ref, vsel_ref, vwin_ref, zt_ref, gt_ref,
                   cmp_scr, *, tk):
    s = pl.program_id(1)
    hb = _rmsnorm(x_ref[0], g_ref[...]).astype(BF16)
    tm = hb.shape[0]
    G = NSA_GROUPS

    cos, slo, shi = cos_ref[...], slo_ref[...], shi_ref[...]
    tok = s * tm + _iota((tm, LANES), 0)
    lane = _iota((tm, LANES), 1)
    block_id = jnp.where(lane - HEAD_DIM == tok // SEL_LEN, 1.0, 0.0)
    ksel = _dot(hb, wk_ref[:, _TOK_SEL:_TOK_SEL + G * LANES])
    kwin = _dot(hb, wk_ref[:, _TOK_WIN:_TOK_WIN + G * LANES])
    for g in range(G):
        blk = _rope_lanes(ksel[:, g * LANES:(g + 1) * LANES], cos, slo, shi)
        ksel_ref[0, g] = (blk + block_id).astype(BF16)
        kwin_ref[0, g] = _rope_lanes(kwin[:, g * LANES:(g + 1) * LANES], cos, slo, shi).astype(BF16)

    raw = _dot(hb, wk_ref[:, _TOK_CMP:_TOK_CMP + 2 * KV_WIDTH])
    for c in range(cmp_scr.shape[0]):
        cmp_scr[c] = raw[:, c * LANES:(c + 1) * LANES]
    n_ch = tm // CMP_STRIDE
    low = _iota((n_ch, LANES), 1) < HEAD_DIM
    for kind, out_ref in enumerate((chk_ref, chv_ref)):
        for gp in range(G // 2):
            c = kind * (G // 2) + gp
            for l in range(0, CMP_STRIDE, 2):
                a0 = cmp_scr[c, pl.ds(l, n_ch, stride=CMP_STRIDE), :]
                a1 = cmp_scr[c, pl.ds(l + 1, n_ch, stride=CMP_STRIDE), :]
                dst = slice(l * HEAD_DIM, (l + 2) * HEAD_DIM)
                out_ref[0, 2 * gp, :, dst] = jnp.where(
                    low, a0, pltpu.roll(a1, HEAD_DIM, axis=1)).astype(BF16)
                out_ref[0, 2 * gp + 1, :, dst] = jnp.where(
                    low, pltpu.roll(a0, HEAD_DIM, axis=1), a1).astype(BF16)

    cost, sint = cost_ref[...], sint_ref[...]
    for r0 in range(0, ATTN_WIDTH, 512):
        t = _dot_nt(wt_ref[_FEAT_Q + r0:_FEAT_Q + r0 + 512, :], hb)
        for h0 in range(0, 512, HEAD_DIM):
            qt_ref[0, r0 + h0:r0 + h0 + HEAD_DIM, :] = _rope_rows(
                t[h0:h0 + HEAD_DIM], cost, sint).astype(BF16)
        zt_ref[0, r0:r0 + 512, :] = _dot_nt(wt_ref[_FEAT_Z + r0:_FEAT_Z + r0 + 512, :], hb).astype(BF16)
    for row0, out_ref in ((_FEAT_VSEL, vsel_ref), (_FEAT_VWIN, vwin_ref)):
        t = _dot_nt(wt_ref[row0:row0 + KV_WIDTH, :], hb).astype(BF16)
        for c in range(tm // tk):
            out_ref[0, c] = t[:, c * tk:(c + 1) * tk]
    gate = _dot_nt(wt_ref[_FEAT_GATE:_FEAT_GATE + LANES, :], hb)
    gate = 1.0 / (1.0 + jnp.exp(-gate))
    rows = N_BRANCH * HEADS_PER_STEP
    for hg in range(N_HEADS // HEADS_PER_STEP):
        gt_ref[0, hg] = _pad_rows(gate[hg * rows:(hg + 1) * rows], GATE_ROWS)


def _nsa_in(x, g, w_k, w_t, lane_tables, row_tables):
    B, S, _ = x.shape
    tm = min(PROJ_ROWS, S)
    tk = min(ATTN_TILE, S)
    W, G = ATTN_WIDTH, NSA_GROUPS
    const = lambda b, s: (0, 0)
    feat = lambda rows: pl.BlockSpec((1, rows, tm), lambda b, s: (b, 0, s))
    tiles = pl.BlockSpec((1, tm // tk, KV_WIDTH, tk), lambda b, s: (b, s, 0, 0))
    keys = pl.BlockSpec((1, G, tm, LANES), lambda b, s: (b, 0, s, 0))
    chunk_w = CMP_STRIDE * HEAD_DIM
    chunk = pl.BlockSpec((1, G, tm // CMP_STRIDE, chunk_w), lambda b, s: (b, 0, s, 0))
    n_hg = N_HEADS // HEADS_PER_STEP
    return pl.pallas_call(
        functools.partial(_nsa_in_kernel, tk=tk),
        name="nsa_in",
        grid=(B, S // tm),
        in_specs=[
            pl.BlockSpec((1, tm, D_MODEL), lambda b, s: (b, s, 0)),
            pl.BlockSpec((1, D_MODEL), const),
            pl.BlockSpec(w_k.shape, const),
            pl.BlockSpec(w_t.shape, const),
        ] + [pl.BlockSpec((tm, LANES), lambda b, s: (s, 0))] * 3
          + [pl.BlockSpec((ROPE_HALF, tm), lambda b, s: (0, s))] * 2,
        out_specs=[feat(W), keys, keys, chunk, chunk, tiles, tiles, feat(W),
                   pl.BlockSpec((1, n_hg, GATE_ROWS, tm), lambda b, s: (b, 0, 0, s))],
        out_shape=[
            jax.ShapeDtypeStruct((B, W, S), BF16),
            jax.ShapeDtypeStruct((B, G, S, LANES), BF16),
            jax.ShapeDtypeStruct((B, G, S, LANES), BF16),
            jax.ShapeDtypeStruct((B, G, S // CMP_STRIDE, chunk_w), BF16),
            jax.ShapeDtypeStruct((B, G, S // CMP_STRIDE, chunk_w), BF16),
            jax.ShapeDtypeStruct((B, S // tk, KV_WIDTH, tk), BF16),
            jax.ShapeDtypeStruct((B, S // tk, KV_WIDTH, tk), BF16),
            jax.ShapeDtypeStruct((B, W, S), BF16),
            jax.ShapeDtypeStruct((B, n_hg, GATE_ROWS, S), F32),
        ],
        scratch_shapes=[pltpu.VMEM((2 * KV_WIDTH // LANES, tm, LANES), F32)],
        compiler_params=_params("parallel", "parallel"),
    )(x, g, w_k, w_t, *lane_tables, *row_tables)


def _compress_kernel(chk_ref, chv_ref, w1k_ref, w2k_ref, pek_ref, w1v_ref, w2vt_ref, pev_ref,
                     cos_ref, slo_ref, shi_ref, kc_ref, vct_ref):
    half = CMP_STRIDE * HEAD_DIM

    def hidden(ch, w1_ref, pe_ref):
        a = _dot(ch, w1_ref[:half, :])
        b = _dot(ch, w1_ref[half:, :])
        b = pltpu.roll(b, b.shape[0] - 1, axis=0)
        pe = _dot(pe_ref[...], w1_ref[...])[0:1, :]
        return _silu(a + b + pe).astype(BF16)

    kc = _dot(hidden(chk_ref[0, 0], w1k_ref, pek_ref), w2k_ref[...])
    kc_ref[0, 0] = _rope_lanes(kc, cos_ref[...], slo_ref[...], shi_ref[...]).astype(BF16)
    vct_ref[0, 0] = _dot_nt(w2vt_ref[...], hidden(chv_ref[0, 0], w1v_ref, pev_ref)).astype(BF16)


def _compress(chk, chv, w1k, w2k, pek, w1v, w2vt, pev, tables):
    B, G, n_chunk, width = chk.shape
    const = lambda b, g: (0, 0)
    chunk_spec = pl.BlockSpec((1, 1, n_chunk, width), lambda b, g: (b, g, 0, 0))
    specs = lambda ws: [pl.BlockSpec(w.shape, const) for w in ws]
    return pl.pallas_call(
        _compress_kernel,
        name="nsa_compress",
        grid=(B, G),
        in_specs=[chunk_spec, chunk_spec] + specs((w1k, w2k, pek, w1v, w2vt, pev))
                 + [pl.BlockSpec((n_chunk, LANES), const)] * 3,
        out_specs=[pl.BlockSpec((1, 1, n_chunk, LANES), lambda b, g: (b, g, 0, 0)),
                   pl.BlockSpec((1, 1, HEAD_DIM, n_chunk), lambda b, g: (b, g, 0, 0))],
        out_shape=[jax.ShapeDtypeStruct((B, G, n_chunk, LANES), BF16),
                   jax.ShapeDtypeStruct((B, G, HEAD_DIM, n_chunk), BF16)],
        compiler_params=_params("parallel", "parallel"),
    )(chk, chv, w1k, w2k, pek, w1v, w2vt, pev, *tables)


def _pad_rows(x, rows):
    if rows == x.shape[0]:
        return x
    return jnp.concatenate([x, jnp.zeros((rows - x.shape[0], x.shape[1]), x.dtype)], axis=0)


def _cmp_attn_kernel(q_ref, kc_ref, vct_ref, ovt_ref, oc_ref, bias_ref, *, t, n_cmp, n_blk):
    i = pl.program_id(2)
    kc = kc_ref[0, 0]
    vct = vct_ref[0, 0]
    n_pad = kc.shape[0]
    blk_c = _iota((n_pad, t), 0)
    qry = i * t + _iota((n_pad, t), 1)
    valid = jnp.logical_and(blk_c * CMP_STRIDE + (CMP_LEN - 1) <= qry, blk_c < n_cmp)
    p_sum = jnp.zeros((n_pad, t), F32)
    logits = [_dot(kc, _pad_rows(q_ref[0, r * HEAD_DIM:(r + 1) * HEAD_DIM, :], LANES))
              for r in range(NSA_REP)]
    for r in range(NSA_REP):
        s = jnp.where(valid, logits[r], NEG)
        e = jnp.exp2(s - _col_max(s))
        p = jnp.where(valid, e * (1.0 / _col_sum(e)), 0.0)
        p_sum = p_sum + p
        oc_ref[0, r * HEAD_DIM:(r + 1) * HEAD_DIM, :] = _dot(vct, p.astype(BF16)).astype(oc_ref.dtype)

    ovt = ovt_ref[...]
    p_hi, p_lo = _split2(p_sum)
    imp = _dot(ovt, p_hi) + _dot(ovt, p_lo)
    blk = _iota(imp.shape, 0)
    cur = (i * t + _iota(imp.shape, 1)) // SEL_LEN
    forced = jnp.logical_or(blk == 0, jnp.logical_or(blk == cur, blk == cur - 1))
    imp = jnp.where(forced, FORCE, jnp.where(blk <= cur, imp, -1.0))
    keep = jnp.zeros(imp.shape, jnp.bool_)
    for _ in range(min(N_SELECT, n_blk)):
        top = _col_max(imp)
        first = jnp.min(jnp.where(imp == top, blk, n_blk), axis=0, keepdims=True)
        pick = blk == first
        keep = jnp.logical_or(keep, pick)
        imp = jnp.where(pick, -jnp.inf, imp)
    bias_ref[0, 0] = jnp.where(keep, 0.0, NEG).astype(bias_ref.dtype)


def _cmp_attn(qt, kc, vct, ovt, n_cmp):
    B, W, S = qt.shape
    G = kc.shape[1]
    t = min(ATTN_TILE, S)
    n_blk = ovt.shape[0]
    gw = W // G
    kern = functools.partial(_cmp_attn_kernel, t=t, n_cmp=n_cmp, n_blk=n_blk)
    return pl.pallas_call(
        kern,
        name="nsa_cmp_attn",
        grid=(B, G, S // t),
        in_specs=[
            pl.BlockSpec((1, gw, t), lambda b, g, i: (b, g, i)),
            pl.BlockSpec((1, 1) + kc.shape[2:], lambda b, g, i: (b, g, 0, 0)),
            pl.BlockSpec((1, 1) + vct.shape[2:], lambda b, g, i: (b, g, 0, 0)),
            pl.BlockSpec(ovt.shape, lambda b, g, i: (0, 0)),
        ],
        out_specs=[
            pl.BlockSpec((1, gw, t), lambda b, g, i: (b, g, i)),
            pl.BlockSpec((1, 1, n_blk, t), lambda b, g, i: (b, g, 0, i)),
        ],
        out_shape=[
            jax.ShapeDtypeStruct((B, W, S), BF16),
            jax.ShapeDtypeStruct((B, G, n_blk, S), BF16),
        ],
        compiler_params=_params("parallel", "parallel", "parallel"),
    )(qt, kc, vct, ovt)


def _sel_attn_kernel(q_ref, bias_ref, k_ref, v_ref, oc_ref, ow_ref, gt_ref, z_ref, o_ref,
                     q_scr, s_scr, acc_scr, *, t):
    i = pl.program_id(2)
    nh = q_scr.shape[0]
    for hh in range(nh):
        bias = _pad_rows(bias_ref[0, hh // NSA_REP], HEAD_DIM)
        q_scr[hh] = jnp.concatenate([q_ref[0, hh * HEAD_DIM:(hh + 1) * HEAD_DIM, :], bias], axis=0)

    def key_tile(j, hh):
        return k_ref[0, hh // NSA_REP, pl.ds(pl.multiple_of(j * t, t), t), :]

    def value_rows(j, hh):
        g = hh // NSA_REP
        return v_ref[0, j, g * HEAD_DIM:(g + 1) * HEAD_DIM, :]

    _causal_flash(i, t, q_scr, key_tile, value_rows, s_scr, acc_scr)
    for hh in range(nh):
        rows = slice(hh * HEAD_DIM, (hh + 1) * HEAD_DIM)
        gate = lambda b: gt_ref[0, 0, N_BRANCH * hh + b:N_BRANCH * hh + b + 1, :]
        o = (gate(0) * oc_ref[0, rows, :].astype(F32) + gate(1) * _normalised(acc_scr[hh])
             + gate(2) * ow_ref[0, rows, :].astype(F32))
        o_ref[0, rows, :] = (o * _silu(z_ref[0, rows, :].astype(F32))).astype(o_ref.dtype)


def _sel_attn(qt, bias, k_aug, vt, oc, ow, gt, zt):
    B, W, S = qt.shape
    t = min(ATTN_TILE, S)
    ng = HEADS_PER_STEP // NSA_REP
    n_blk = bias.shape[2]
    feat = pl.BlockSpec((1, HEADS_PER_STEP * HEAD_DIM, t), lambda b, g, i: (b, g, i))
    return pl.pallas_call(
        functools.partial(_sel_attn_kernel, t=t),
        name="nsa_sel_attn",
        grid=(B, NSA_GROUPS // ng, S // t),
        in_specs=[
            feat,
            pl.BlockSpec((1, ng, n_blk, t), lambda b, g, i: (b, g, 0, i)),
            pl.BlockSpec((1, ng, S, LANES), lambda b, g, i: (b, g, 0, 0)),
            pl.BlockSpec((1, S // t, ng * HEAD_DIM, t), lambda b, g, i: (b, 0, g, 0)),
            feat, feat,
            pl.BlockSpec((1, 1, GATE_ROWS, t), lambda b, g, i: (b, g, 0, i)),
            feat,
        ],
        out_specs=feat,
        out_shape=jax.ShapeDtypeStruct((B, W, S), BF16),
        scratch_shapes=_attn_scratch(HEADS_PER_STEP, LANES, t),
        compiler_params=_params("parallel", "parallel", "parallel"),
    )(qt, bias, k_aug, vt, oc, ow, gt, zt)


def _win_attn_kernel(q_ref, k_ref, v_ref, o_ref, *, t, n_back):
    i = pl.program_id(2)
    above = _iota((t, t), 0) > _iota((t, t), 1)
    tiles = []
    for d in range(n_back + 1):
        j = i - d
        jc = jnp.maximum(j, 0)
        k = k_ref[0, 0, pl.ds(pl.multiple_of(jc * t, t), t), :]
        if d == 0:
            ok = jnp.logical_not(above)
        elif d == n_back:
            ok = jnp.logical_and(above, j >= 0)
        else:
            ok = jnp.broadcast_to(j >= 0, (t, t))
        v = jnp.concatenate([v_ref[0, jc], jnp.ones((ONES_ROWS, t), BF16)], axis=0)
        tiles.append((k, v, ok))
    logits = []
    for hh in range(NSA_REP):
        q_aug = _pad_rows(q_ref[0, hh * HEAD_DIM:(hh + 1) * HEAD_DIM, :], LANES)
        logits.append([_dot(k, q_aug) for k, _, _ in tiles])
    for hh in range(NSA_REP):
        ss = [jnp.where(ok, s, NEG) for s, (_, _, ok) in zip(logits[hh], tiles)]
        m = functools.reduce(jnp.maximum, [_col_max(s) for s in ss])
        acc = functools.reduce(jnp.add, [_dot(v, jnp.exp2(s - m).astype(BF16))
                                         for (_, v, _), s in zip(tiles, ss)])
        o_ref[0, hh * HEAD_DIM:(hh + 1) * HEAD_DIM, :] = _normalised(acc).astype(o_ref.dtype)


def _win_attn(qt, k_aug, vt):
    B, W, S = qt.shape
    t = min(ATTN_TILE, S)
    assert WINDOW % t == 0
    return pl.pallas_call(
        functools.partial(_win_attn_kernel, t=t, n_back=WINDOW // t),
        name="nsa_win_attn",
        grid=(B, NSA_GROUPS, S // t),
        out_shape=jax.ShapeDtypeStruct((B, W, S), BF16),
        compiler_params=_params("parallel", "parallel", "parallel"),
        **_group_attn_specs(S, t, 1),
    )(qt, k_aug, vt)


def _local_attn_kernel(q_ref, kc_ref, vct_ref, ovt_ref, kw_ref, vw_ref, oc_ref, bias_ref, ow_ref,
                       *, t, n_cmp, n_blk, n_back):
    i = pl.program_id(2)
    heads = range(NSA_REP)
    q_aug = [_pad_rows(q_ref[0, r * HEAD_DIM:(r + 1) * HEAD_DIM, :], LANES) for r in heads]

    kc = kc_ref[0, 0]
    cmp_logits = [_dot(kc, q_aug[r]) for r in heads]
    above = _iota((t, t), 0) > _iota((t, t), 1)
    ones = jnp.ones((ONES_ROWS, t), BF16)
    tiles = []
    for d in range(n_back + 1):
        j = i - d
        jc = jnp.maximum(j, 0)
        k = kw_ref[0, 0, pl.ds(pl.multiple_of(jc * t, t), t), :]
        if d == 0:
            ok = jnp.logical_not(above)
        elif d == n_back:
            ok = jnp.logical_and(above, j >= 0)
        else:
            ok = jnp.broadcast_to(j >= 0, (t, t))
        tiles.append((jnp.concatenate([vw_ref[0, jc], ones], axis=0), ok))
        for r in heads:
            tiles[-1] += (_dot(k, q_aug[r]),)

    vct = vct_ref[0, 0]
    n_pad = kc.shape[0]
    blk_c = _iota((n_pad, t), 0)
    qry = i * t + _iota((n_pad, t), 1)
    valid = jnp.logical_and(blk_c * CMP_STRIDE + (CMP_LEN - 1) <= qry, blk_c < n_cmp)
    p_sum = jnp.zeros((n_pad, t), F32)
    for r in heads:
        s = jnp.where(valid, cmp_logits[r], NEG)
        e = jnp.exp2(s - _col_max(s))
        p = jnp.where(valid, e * (1.0 / _col_sum(e)), 0.0)
        p_sum = p_sum + p
        oc_ref[0, r * HEAD_DIM:(r + 1) * HEAD_DIM, :] = _dot(vct, p.astype(BF16)).astype(oc_ref.dtype)

    ovt = ovt_ref[...]
    p_hi, p_lo = _split2(p_sum)
    imp = _dot(ovt, p_hi) + _dot(ovt, p_lo)
    blk = _iota(imp.shape, 0)
    cur = (i * t + _iota(imp.shape, 1)) // SEL_LEN
    forced = jnp.logical_or(blk == 0, jnp.logical_or(blk == cur, blk == cur - 1))
    imp = jnp.where(forced, FORCE, jnp.where(blk <= cur, imp, -1.0))
    keep = jnp.zeros(imp.shape, jnp.bool_)

    n_rounds = min(N_SELECT, n_blk)
    per_head = -(-n_rounds // NSA_REP)
    done = 0
    for r in heads:
        ss = [jnp.where(ok, logits[r], NEG) for _, ok, *logits in tiles]
        m = functools.reduce(jnp.maximum, [_col_max(s) for s in ss])
        acc = functools.reduce(jnp.add, [_dot(v, jnp.exp2(s - m).astype(BF16))
                                         for (v, *_), s in zip(tiles, ss)])
        ow_ref[0, r * HEAD_DIM:(r + 1) * HEAD_DIM, :] = _normalised(acc).astype(ow_ref.dtype)
        for _ in range(min(per_head, n_rounds - done)):
            top = _col_max(imp)
            first = jnp.min(jnp.where(imp == top, blk, n_blk), axis=0, keepdims=True)
            pick = blk == first
            keep = jnp.logical_or(keep, pick)
            imp = jnp.where(pick, -jnp.inf, imp)
            done += 1
    bias_ref[0, 0] = jnp.where(keep, 0.0, NEG).astype(bias_ref.dtype)


def _local_attn(qt, kc, vct, ovt, n_cmp, kw, vw):
    B, W, S = qt.shape
    G = kc.shape[1]
    t = min(ATTN_TILE, S)
    assert WINDOW % t == 0
    n_blk = ovt.shape[0]
    gw = W // G
    kern = functools.partial(_local_attn_kernel, t=t, n_cmp=n_cmp, n_blk=n_blk, n_back=WINDOW // t)
    feat = pl.BlockSpec((1, gw, t), lambda b, g, i: (b, g, i))
    return pl.pallas_call(
        kern,
        name="nsa_local_attn",
        grid=(B, G, S // t),
        in_specs=[
            feat,
            pl.BlockSpec((1, 1) + kc.shape[2:], lambda b, g, i: (b, g, 0, 0)),
            pl.BlockSpec((1, 1) + vct.shape[2:], lambda b, g, i: (b, g, 0, 0)),
            pl.BlockSpec(ovt.shape, lambda b, g, i: (0, 0)),
            pl.BlockSpec((1, 1, S, LANES), lambda b, g, i: (b, g, 0, 0)),
            pl.BlockSpec((1, S // t, HEAD_DIM, t), lambda b, g, i: (b, 0, g, 0)),
        ],
        out_specs=[feat, pl.BlockSpec((1, 1, n_blk, t), lambda b, g, i: (b, g, 0, i)), feat],
        out_shape=[
            jax.ShapeDtypeStruct((B, W, S), BF16),
            jax.ShapeDtypeStruct((B, G, n_blk, S), BF16),
            jax.ShapeDtypeStruct((B, W, S), BF16),
        ],
        compiler_params=_params("parallel", "parallel", "parallel"),
    )(qt, kc, vct, ovt, kw, vw)


def _nsa_out_kernel(oc_ref, os_ref, ow_ref, gt_ref, zt_ref, x_ref, w_ref, g_ref, y_ref, gated_scr):
    for h in range(N_HEADS):
        rows = slice(h * HEAD_DIM, (h + 1) * HEAD_DIM)
        o = jnp.zeros((HEAD_DIM, gated_scr.shape[1]), F32)
        for b, ref in enumerate((oc_ref, os_ref, ow_ref)):
            gate = gt_ref[0, N_BRANCH * h + b:N_BRANCH * h + b + 1, :]
            o = o + gate * ref[0, rows, :].astype(F32)
        gated_scr[rows, :] = (o * _silu(zt_ref[0, rows, :].astype(F32))).astype(BF16)
    y = x_ref[0] + _dot_tn(gated_scr[...], w_ref[...])
    y_ref[0] = _rmsnorm(y, g_ref[...])


def _nsa_out(oc, os_, ow, gt, zt, x, w_out, final_g):
    B, S, _ = x.shape
    tm = min(PROJ_ROWS, S)
    feat = lambda rows: pl.BlockSpec((1, rows, tm), lambda b, s: (b, 0, s))
    tok = pl.BlockSpec((1, tm, D_MODEL), lambda b, s: (b, s, 0))
    wide = feat(ATTN_WIDTH)
    return pl.pallas_call(
        _nsa_out_kernel,
        name="nsa_out",
        grid=(B, S // tm),
        in_specs=[wide, wide, wide, feat(LANES), wide, tok,
                  pl.BlockSpec((ATTN_WIDTH, D_MODEL), lambda b, s: (0, 0)),
                  pl.BlockSpec((1, D_MODEL), lambda b, s: (0, 0))],
        out_specs=tok,
        out_shape=jax.ShapeDtypeStruct((B, S, D_MODEL), F32),
        scratch_shapes=[pltpu.VMEM((ATTN_WIDTH, tm), BF16)],
        compiler_params=_params("parallel", "parallel"),
    )(oc, os_, ow, gt, zt, x, w_out, final_g)


def _fox_layer(x, g, w_in, b_f, w_out):
    B, S, _ = x.shape
    W = ATTN_WIDTH
    w_k = w_in[:, W:2 * W].astype(BF16)
    w_t = jnp.concatenate([w_in[:, :W] * Q_SCALE, w_in[:, 2 * W:3 * W], w_in[:, 3 * W + N_HEADS:]],
                          axis=1).T.astype(BF16)
    pad = jnp.zeros((D_MODEL, LANES - 3 * N_HEADS), F32)
    w_f = w_in[:, 3 * W:3 * W + N_HEADS]
    w_f = jnp.concatenate([w_f, w_f, w_f, pad], axis=1)
    wf_hi, wf_lo = _split2(w_f)
    bf_row = jnp.concatenate([b_f, b_f, b_f, jnp.zeros((LANES - 3 * N_HEADS,), F32)])[None, :]

    qt, k, vt, zt, cx = _fox_in(x, g[None, :], w_k, w_t, wf_hi, wf_lo, bf_row)

    t = min(ATTN_TILE, S)
    r = jnp.arange(LANES)
    head = jnp.arange(N_HEADS)
    sel = jnp.logical_and(r[None, :] % N_HEADS == head[:, None], r[None, :] < 3 * N_HEADS)
    qx = jnp.broadcast_to(jnp.where(sel, -1.0, 0.0).astype(BF16)[:, :, None], (N_HEADS, LANES, t))
    ot = _fox_attn(qt, qx, k, cx, vt, zt)
    return _out_proj(ot, x, w_out.astype(BF16))


def _pad_groups(w):
    k = w.shape[0]
    w = w.reshape(k, NSA_GROUPS, HEAD_DIM)
    return jnp.concatenate([w, jnp.zeros_like(w)], axis=2).reshape(k, NSA_GROUPS * LANES)


def _nsa_layer(x, g, w_in, pe_k, w_ck1, w_ck2, pe_v, w_cv1, w_cv2, w_out, final_g):
    B, S, _ = x.shape
    W, KV, G = ATTN_WIDTH, KV_WIDTH, NSA_GROUPS
    sec = lambda j: w_in[:, W + j * KV:W + (j + 1) * KV]
    gate_off = W + 6 * KV
    n_gate = N_BRANCH * N_HEADS
    w_gate = jnp.concatenate([w_in[:, gate_off:gate_off + n_gate],
                              jnp.zeros((D_MODEL, LANES - n_gate), F32)], axis=1)
    w_k = jnp.concatenate([_pad_groups(sec(2)), _pad_groups(sec(4)), sec(0), sec(1)],
                          axis=1).astype(BF16)
    w_t = jnp.concatenate([w_in[:, :W] * Q_SCALE, sec(3), sec(5), w_in[:, gate_off + n_gate:], w_gate],
                          axis=1).T.astype(BF16)
    pos = jnp.arange(S)
    cos, sin = _rope_angles(pos)
    qt, ksel, kwin, chk, chv, vsel, vwin, zt, gt = _nsa_in(
        x, g[None, :], w_k, w_t, _rope_lane_tables(pos), (cos.T, sin.T))

    n_chunk = S // CMP_STRIDE
    n_cmp = n_chunk - CMP_LEN // CMP_STRIDE + 1

    def flat_pe(pe):
        pe = pe.reshape(1, CMP_LEN * HEAD_DIM)
        return jnp.broadcast_to(pe, (SUBLANES, CMP_LEN * HEAD_DIM)).astype(BF16)

    w2k = jnp.concatenate([w_ck2, jnp.zeros_like(w_ck2)], axis=1).astype(BF16)
    cmp_end = jnp.arange(n_chunk) * CMP_STRIDE + CMP_LEN - 1
    kc, vct = _compress(chk, chv, w_ck1.astype(BF16), w2k, flat_pe(pe_k),
                        w_cv1.astype(BF16), w_cv2.T.astype(BF16), flat_pe(pe_v),
                        _rope_lane_tables(cmp_end))

    n_blk = S // SEL_LEN
    ci = jnp.arange(n_chunk) * CMP_STRIDE
    sj = jnp.arange(n_blk) * SEL_LEN
    ovt = jnp.logical_and(ci[None, :] < sj[:, None] + SEL_LEN, ci[None, :] + CMP_LEN > sj[:, None])
    ovt = jnp.logical_and(ovt, jnp.arange(n_chunk)[None, :] < n_cmp).astype(BF16)
    oc, bias, ow = _local_attn(qt, kc, vct, ovt, n_cmp, kwin, vwin)
    ot = _sel_attn(qt, bias, ksel, vsel, oc, ow, gt, zt)
    return _out_proj(ot, x, w_out.astype(BF16), final_g[None, :])


def kernel(x, norm_g, fox_w_in, fox_b_f, fox_w_out, nsa_w_in, nsa_pe_k, nsa_w_ck1, nsa_w_ck2,
           nsa_pe_v, nsa_w_cv1, nsa_w_cv2, nsa_w_out, final_g):
    x = _fox_layer(x, norm_g[0], fox_w_in[0], fox_b_f[0], fox_w_out[0])
    return _nsa_layer(x, norm_g[1], nsa_w_in[0], nsa_pe_k[0], nsa_w_ck1[0], nsa_w_ck2[0],
                      nsa_pe_v[0], nsa_w_cv1[0], nsa_w_cv2[0], nsa_w_out[0], final_g)
```

```python
import functools
import math

import jax
import jax.numpy as jnp
from jax import lax
from jax.experimental import pallas as pl
from jax.experimental.pallas import tpu as pltpu

F32 = jnp.float32
BF16 = jnp.bfloat16

D_MODEL = 1024
N_HEADS = 16
HEAD_DIM = 64
ATTN_WIDTH = N_HEADS * HEAD_DIM
NSA_GROUPS = 4
NSA_REP = N_HEADS // NSA_GROUPS
KV_WIDTH = NSA_GROUPS * HEAD_DIM
CMP_LEN = 32
CMP_STRIDE = 16
CMP_HIDDEN = 256
SEL_LEN = 64
N_SELECT = 8
WINDOW = 512
N_BRANCH = 3
ROPE_THETA = 500000.0
ROPE_DIM = HEAD_DIM // 4
ROPE_HALF = ROPE_DIM // 2
NORM_EPS = 1e-6
NEG = -1e30
FORCE = 1e6
Q_SCALE = HEAD_DIM ** -0.5 * math.log2(math.e)
LOG2E = math.log2(math.e)

LANES = 128
SUBLANES = 8
ONES_ROWS = 2 * SUBLANES
PROJ_ROWS = 512
ATTN_TILE = 256
QUERY_TILE = 256
HEADS_PER_STEP = 8
GATE_ROWS = -(-N_BRANCH * HEADS_PER_STEP // SUBLANES) * SUBLANES
VMEM_LIMIT = 56 * 1024 * 1024


def _params(*sem):
    return pltpu.CompilerParams(dimension_semantics=sem, vmem_limit_bytes=VMEM_LIMIT)


def _iota(shape, dim):
    return lax.broadcasted_iota(jnp.int32, shape, dim)


def _split2(x):
    hi = x.astype(BF16)
    lo = (x - hi.astype(F32)).astype(BF16)
    return hi, lo


def _split3(x):
    hi = x.astype(BF16)
    r1 = x - hi.astype(F32)
    mid = r1.astype(BF16)
    lo = (r1 - mid.astype(F32)).astype(BF16)
    return hi, mid, lo


def _dot(a, b):
    return jnp.dot(a, b, preferred_element_type=F32)


def _dot_nt(a, b):
    return lax.dot_general(a, b, (((1,), (1,)), ((), ())), preferred_element_type=F32)


def _dot_tn(a, b):
    return lax.dot_general(a, b, (((0,), (0,)), ((), ())), preferred_element_type=F32)


def _rmsnorm(x, g):
    ms = jnp.mean(x * x, axis=-1, keepdims=True)
    return x * lax.rsqrt(ms + NORM_EPS) * g


def _silu(x):
    return x * (1.0 / (1.0 + jnp.exp(-x)))


def _col_max(x):
    return jnp.max(x, axis=0, keepdims=True)


def _col_sum(x):
    return jnp.sum(x, axis=0, keepdims=True)


def _rope_lanes(x, cos, sin_lo, sin_hi):
    return (x * cos + pltpu.roll(x, LANES - ROPE_HALF, axis=1) * sin_lo
            + pltpu.roll(x, ROPE_HALF, axis=1) * sin_hi)


def _rope_rows(x, cos, sin):
    x1, x2 = x[:ROPE_HALF], x[ROPE_HALF:ROPE_DIM]
    return jnp.concatenate([x1 * cos - x2 * sin, x1 * sin + x2 * cos, x[ROPE_DIM:]], axis=0)


def _rope_angles(pos):
    inv_freq = jnp.power(ROPE_THETA, -jnp.arange(ROPE_HALF, dtype=F32) * (2.0 / ROPE_DIM))
    ang = pos.astype(F32)[:, None] * inv_freq[None, :]
    return jnp.cos(ang), jnp.sin(ang)


def _rope_lane_tables(pos):
    cos, sin = _rope_angles(pos)
    n = pos.shape[0]
    ones = jnp.ones((n, HEAD_DIM - ROPE_DIM), F32)
    zeros = jnp.zeros((n, HEAD_DIM - ROPE_DIM), F32)
    z8 = jnp.zeros((n, ROPE_HALF), F32)
    c = jnp.concatenate([cos, cos, ones], axis=1)
    s_lo = jnp.concatenate([-sin, z8, zeros], axis=1)
    s_hi = jnp.concatenate([z8, sin, zeros], axis=1)
    tile = lambda t: jnp.concatenate([t, t], axis=1)
    return tile(c), tile(s_lo), tile(s_hi)


def _fox_in_kernel(x_ref, g_ref, wk_ref, wt_ref, wfh_ref, wfl_ref, bf_ref, tri_ref,
                   qt_ref, k_ref, vt_ref, zt_ref, cx_ref, carry_ref, *, tk):
    s = pl.program_id(1)
    h = _rmsnorm(x_ref[0], g_ref[...])
    hb = h.astype(BF16)
    tm = hb.shape[0]
    W = ATTN_WIDTH
    for n0 in range(0, W, 512):
        k_ref[0, :, n0:n0 + 512] = _dot(hb, wk_ref[:, n0:n0 + 512]).astype(BF16)
    for r0 in range(0, 3 * W, 512):
        t = _dot_nt(wt_ref[r0:r0 + 512, :], hb).astype(BF16)
        if r0 < W:
            qt_ref[0, r0:r0 + 512, :] = t
        elif r0 < 2 * W:
            for c in range(tm // tk):
                vt_ref[0, c, r0 - W:r0 - W + 512, :] = t[:, c * tk:(c + 1) * tk]
        else:
            zt_ref[0, r0 - 2 * W:r0 - 2 * W + 512, :] = t

    h_lo = (h - hb.astype(F32)).astype(BF16)
    f = _dot(hb, wfh_ref[...]) + _dot(h_lo, wfh_ref[...]) + _dot(hb, wfl_ref[...])
    f = f + bf_ref[...]
    log_f = jnp.minimum(f, 0.0) - jnp.log1p(jnp.exp(-jnp.abs(f)))

    @pl.when(s == 0)
    def _():
        carry_ref[...] = jnp.zeros_like(carry_ref)

    a0, a1, a2 = _split3(log_f)
    tri = tri_ref[...]
    c = _dot(tri, a0) + _dot(tri, a1) + _dot(tri, a2) + carry_ref[0:1, :]
    carry_ref[...] = jnp.broadcast_to(c[tm - 1:tm, :], carry_ref.shape)

    hi, mid, lo = _split3(c * LOG2E)
    lane = _iota(c.shape, 1)
    zero = jnp.zeros_like(hi)
    cx_ref[0] = jnp.where(lane < N_HEADS, hi,
                          jnp.where(lane < 2 * N_HEADS, mid,
                                    jnp.where(lane < 3 * N_HEADS, lo, zero)))


def _fox_in(x, g, w_k, w_t, wf_hi, wf_lo, bf_row):
    B, S, _ = x.shape
    tm = min(PROJ_ROWS, S)
    tk = min(ATTN_TILE, S)
    W = ATTN_WIDTH
    tri = (jnp.arange(tm)[:, None] >= jnp.arange(tm)[None, :]).astype(BF16)
    const = lambda b, s: (0, 0)
    feat = pl.BlockSpec((1, W, tm), lambda b, s: (b, 0, s))
    return pl.pallas_call(
        functools.partial(_fox_in_kernel, tk=tk),
        name="fox_in",
        grid=(B, S // tm),
        in_specs=[
            pl.BlockSpec((1, tm, D_MODEL), lambda b, s: (b, s, 0)),
            pl.BlockSpec((1, D_MODEL), const),
            pl.BlockSpec(w_k.shape, const),
            pl.BlockSpec(w_t.shape, const),
            pl.BlockSpec((D_MODEL, LANES), const),
            pl.BlockSpec((D_MODEL, LANES), const),
            pl.BlockSpec((1, LANES), const),
            pl.BlockSpec((tm, tm), const),
        ],
        out_specs=[
            feat,
            pl.BlockSpec((1, tm, W), lambda b, s: (b, s, 0)),
            pl.BlockSpec((1, tm // tk, W, tk), lambda b, s: (b, s, 0, 0)),
            feat,
            pl.BlockSpec((1, tm, LANES), lambda b, s: (b, s, 0)),
        ],
        out_shape=[
            jax.ShapeDtypeStruct((B, W, S), BF16),
            jax.ShapeDtypeStruct((B, S, W), BF16),
            jax.ShapeDtypeStruct((B, S // tk, W, tk), BF16),
            jax.ShapeDtypeStruct((B, W, S), BF16),
            jax.ShapeDtypeStruct((B, S, LANES), BF16),
        ],
        scratch_shapes=[pltpu.VMEM((SUBLANES, LANES), F32)],
        compiler_params=_params("parallel", "arbitrary"),
    )(x, g, w_k, w_t, wf_hi, wf_lo, bf_row, tri)


def _causal_flash(i, tq, tk, q_scr, key_tile, value_rows, s_scr, acc_scr):
    heads = range(q_scr.shape[0])
    ratio = tq // tk
    n_off = ratio * i
    ones = jnp.ones((ONES_ROWS, tk), BF16)

    def logits(j):
        return [_dot(key_tile(j, hh), q_scr[hh]) for hh in heads]

    def pv(j, hh, p):
        return _dot(jnp.concatenate([value_rows(j, hh), ones], axis=0), p.astype(BF16))

    s_diag = [logits(n_off + d) for d in range(ratio)]
    for hh, s in enumerate(logits(0)):
        s_scr[hh] = s
    key = _iota((tk, tq), 0)
    qry = _iota((tk, tq), 1)
    m_diag = []
    for hh in heads:
        ss = [jnp.where(key + d * tk <= qry, s_diag[d][hh], NEG) for d in range(ratio)]
        m = functools.reduce(jnp.maximum, [_col_max(s) for s in ss])
        acc_scr[hh] = functools.reduce(
            jnp.add, [pv(n_off + d, hh, jnp.exp2(s - m)) for d, s in enumerate(ss)])
        m_diag.append(m)
    carry = []
    for hh in heads:
        m_new = jnp.maximum(m_diag[hh], _col_max(s_scr[hh]))
        carry.append((m_new, jnp.exp2(m_diag[hh] - m_new)))

    def body(j, carry):
        s_next = logits(j + 1)
        pvs = [pv(j, hh, jnp.exp2(s_scr[hh] - carry[hh][0])) for hh in heads]
        out = []
        for hh in heads:
            m = carry[hh][0]
            s_scr[hh] = s_next[hh]
            m_new = jnp.maximum(m, _col_max(s_next[hh]))
            out.append((m_new, jnp.exp2(m - m_new)))
        for hh in heads:
            acc_scr[hh] = carry[hh][1] * acc_scr[hh] + pvs[hh]
        return tuple(out)

    carry = lax.fori_loop(0, n_off - 1, body, tuple(carry))

    some = n_off >= 1
    last = jnp.maximum(n_off - 1, 0)
    live = jnp.where(some, 1.0, 0.0)
    accs = []
    for hh in heads:
        m, alpha = carry[hh]
        p = jnp.exp2(s_scr[hh] - m) * live
        accs.append(jnp.where(some, alpha, 1.0) * acc_scr[hh] + pv(last, hh, p))
    return accs


def _causal_attention(n_q, t, load_q, key_tile, value_rows, emit, q_scr, s_scr, acc_scr):
    heads = range(q_scr.shape[1])
    ones = jnp.ones((ONES_ROWS, t), BF16)
    causal = _iota((t, t), 0) <= _iota((t, t), 1)

    def logits(slot, j):
        return [_dot(key_tile(j, hh), q_scr[slot, hh]) for hh in heads]

    def pv(j, hh, p):
        return _dot(jnp.concatenate([value_rows(j, hh), ones], axis=0), p.astype(BF16))

    def finish(i, slot, carry):
        accs = []
        for hh in heads:
            m, alpha = carry[hh]
            accs.append(alpha * acc_scr[hh] + pv(i - 1, hh, jnp.exp2(s_scr[slot, hh] - m)))
        emit(i, accs)

    pending = None
    for i in range(n_q):
        slot = i % 2
        load_q(i, slot)
        s_diag = logits(slot, i)
        s_first = logits(slot, 0) if i else None
        if pending is not None:
            finish(*pending)
        m_diag = []
        for hh in heads:
            s = jnp.where(causal, s_diag[hh], NEG)
            m = _col_max(s)
            acc_scr[hh] = pv(i, hh, jnp.exp2(s - m))
            m_diag.append(m)
        if i == 0:
            emit(0, [acc_scr[hh] for hh in heads])
            continue
        carry = []
        for hh in heads:
            s_scr[slot, hh] = s_first[hh]
            m_new = jnp.maximum(m_diag[hh], _col_max(s_first[hh]))
            carry.append((m_new, jnp.exp2(m_diag[hh] - m_new)))

        def body(j, carry, slot=slot):
            s_next = logits(slot, j + 1)
            pvs = [pv(j, hh, jnp.exp2(s_scr[slot, hh] - carry[hh][0])) for hh in heads]
            out = []
            for hh in heads:
                m = carry[hh][0]
                s_scr[slot, hh] = s_next[hh]
                m_new = jnp.maximum(m, _col_max(s_next[hh]))
                out.append((m_new, jnp.exp2(m - m_new)))
            for hh in heads:
                acc_scr[hh] = carry[hh][1] * acc_scr[hh] + pvs[hh]
            return tuple(out)

        carry = lax.fori_loop(0, i - 1, body, tuple(carry))
        pending = (i, slot, carry)
    if pending is not None:
        finish(*pending)


def _normalised(acc):
    return acc[:HEAD_DIM] * (1.0 / acc[HEAD_DIM:HEAD_DIM + 1])


def _fox_attn_kernel(q_ref, qx_ref, k_ref, kx_ref, v_ref, z_ref, o_ref, q_scr, s_scr, acc_scr,
                     *, tq, t):
    del tq
    nh = q_scr.shape[1]
    n_q = q_ref.shape[2] // t
    row = _iota((LANES, t), 0)

    def load_q(i, slot):
        for hh in range(nh):
            pair, sub = divmod(hh, 2)
            qt = q_ref[0, pair * LANES:(pair + 1) * LANES, i * t:(i + 1) * t]
            own = jnp.logical_and(row >= sub * HEAD_DIM, row < (sub + 1) * HEAD_DIM)
            q_scr[slot, hh] = jnp.concatenate(
                [jnp.where(own, qt, jnp.zeros_like(qt)), qx_ref[hh]], axis=0)

    def key_tile(j, hh):
        pair = hh // 2
        return jnp.concatenate([k_ref[0, _tile(j, t), pair * LANES:(pair + 1) * LANES],
                                kx_ref[0, _tile(j, t), :]], axis=1)

    def value_rows(j, hh):
        return v_ref[0, j, hh * HEAD_DIM:(hh + 1) * HEAD_DIM, :]

    def emit(i, accs):
        for hh in range(nh):
            rows, cols = slice(hh * HEAD_DIM, (hh + 1) * HEAD_DIM), slice(i * t, (i + 1) * t)
            gated = _normalised(accs[hh]) * _silu(z_ref[0, rows, cols].astype(F32))
            o_ref[0, rows, cols] = gated.astype(o_ref.dtype)

    _causal_attention(n_q, t, load_q, key_tile, value_rows, emit, q_scr, s_scr, acc_scr)


def _tile(j, t):
    return pl.ds(j * t if isinstance(j, int) else pl.multiple_of(j * t, t), t)


def _attn_scratch(nh, k_depth, t):
    return [pltpu.VMEM((2, nh, k_depth, t), BF16), pltpu.VMEM((2, nh, t, t), F32),
            pltpu.VMEM((nh, HEAD_DIM + ONES_ROWS, t), F32)]


def _fox_attn(qt, qx, k, kx, vt, zt):
    B, W, S = qt.shape
    t = min(ATTN_TILE, S)
    nh = HEADS_PER_STEP
    rows = nh * HEAD_DIM
    feat = pl.BlockSpec((1, rows, S), lambda b, p: (b, p, 0))
    return pl.pallas_call(
        functools.partial(_fox_attn_kernel, tq=t, t=t),
        name="fox_attn",
        grid=(B, N_HEADS // nh),
        in_specs=[
            feat,
            pl.BlockSpec((nh, LANES, t), lambda b, p: (p, 0, 0)),
            pl.BlockSpec((1, S, rows), lambda b, p: (b, 0, p)),
            pl.BlockSpec((1, S, LANES), lambda b, p: (b, 0, 0)),
            pl.BlockSpec((1, S // t, rows, t), lambda b, p: (b, 0, p, 0)),
            feat,
        ],
        out_specs=feat,
        out_shape=jax.ShapeDtypeStruct((B, W, S), BF16),
        scratch_shapes=_attn_scratch(nh, 2 * LANES, t),
        compiler_params=_params("parallel", "parallel"),
    )(qt, qx, k, kx, vt, zt)


def _out_proj_kernel(*refs, final_norm):
    if final_norm:
        ot_ref, x_ref, w_ref, g_ref, y_ref = refs
    else:
        ot_ref, x_ref, w_ref, y_ref = refs
    y = x_ref[0] + _dot_tn(ot_ref[0], w_ref[...])
    y_ref[0] = _rmsnorm(y, g_ref[...]) if final_norm else y


def _out_proj(ot, x, w_out, final_g=None):
    B, S, _ = x.shape
    tm = min(PROJ_ROWS, S)
    const = lambda b, s: (0, 0)
    tok = pl.BlockSpec((1, tm, D_MODEL), lambda b, s: (b, s, 0))
    in_specs = [pl.BlockSpec((1, ATTN_WIDTH, tm), lambda b, s: (b, 0, s)), tok,
                pl.BlockSpec((ATTN_WIDTH, D_MODEL), const)]
    args = [ot, x, w_out]
    if final_g is not None:
        in_specs.append(pl.BlockSpec((1, D_MODEL), const))
        args.append(final_g)
    return pl.pallas_call(
        functools.partial(_out_proj_kernel, final_norm=final_g is not None),
        name="out_proj",
        grid=(B, S // tm),
        in_specs=in_specs,
        out_specs=tok,
        out_shape=jax.ShapeDtypeStruct((B, S, D_MODEL), F32),
        compiler_params=_params("parallel", "parallel"),
    )(*args)


_TOK_SEL = 0
_TOK_WIN = NSA_GROUPS * LANES
_TOK_CMP = 2 * NSA_GROUPS * LANES
_TOK_COLS = _TOK_CMP + 2 * KV_WIDTH
_FEAT_Q = 0
_FEAT_VSEL = ATTN_WIDTH
_FEAT_VWIN = _FEAT_VSEL + KV_WIDTH
_FEAT_Z = _FEAT_VWIN + KV_WIDTH
_FEAT_GATE = _FEAT_Z + ATTN_WIDTH
_FEAT_ROWS = _FEAT_GATE + LANES


def _nsa_in_kernel(x_ref, g_ref, wk_ref, wt_ref, cos_ref, slo_ref, shi_ref, cost_ref, sint_ref,
                   qt_ref, ksel_ref, kwin_ref, chk_ref, chv_ref, vsel_ref, vwin_ref, zt_ref, gt_ref,
                   cmp_scr, *, tk):
    s = pl.program_id(1)
    hb = _rmsnorm(x_ref[0], g_ref[...]).astype(BF16)
    tm = hb.shape[0]
    G = NSA_GROUPS

    cos, slo, shi = cos_ref[...], slo_ref[...], shi_ref[...]
    tok = s * tm + _iota((tm, LANES), 0)
    lane = _iota((tm, LANES), 1)
    block_id = jnp.where(lane - HEAD_DIM == tok // SEL_LEN, 1.0, 0.0)
    ksel = _dot(hb, wk_ref[:, _TOK_SEL:_TOK_SEL + G * LANES])
    kwin = _dot(hb, wk_ref[:, _TOK_WIN:_TOK_WIN + G * LANES])
    for g in range(G):
        blk = _rope_lanes(ksel[:, g * LANES:(g + 1) * LANES], cos, slo, shi)
        ksel_ref[0, g] = (blk + block_id).astype(BF16)
        kwin_ref[0, g] = _rope_lanes(kwin[:, g * LANES:(g + 1) * LANES], cos, slo, shi).astype(BF16)

    raw = _dot(hb, wk_ref[:, _TOK_CMP:_TOK_CMP + 2 * KV_WIDTH])
    for c in range(cmp_scr.shape[0]):
        cmp_scr[c] = raw[:, c * LANES:(c + 1) * LANES]
    n_ch = tm // CMP_STRIDE
    low = _iota((n_ch, LANES), 1) < HEAD_DIM
    for kind, out_ref in enumerate((chk_ref, chv_ref)):
        for gp in range(G // 2):
            c = kind * (G // 2) + gp
            for l in range(0, CMP_STRIDE, 2):
                a0 = cmp_scr[c, pl.ds(l, n_ch, stride=CMP_STRIDE), :]
                a1 = cmp_scr[c, pl.ds(l + 1, n_ch, stride=CMP_STRIDE), :]
                dst = slice(l * HEAD_DIM, (l + 2) * HEAD_DIM)
                out_ref[0, 2 * gp, :, dst] = jnp.where(
                    low, a0, pltpu.roll(a1, HEAD_DIM, axis=1)).astype(BF16)
                out_ref[0, 2 * gp + 1, :, dst] = jnp.where(
                    low, pltpu.roll(a0, HEAD_DIM, axis=1), a1).astype(BF16)

    cost, sint = cost_ref[...], sint_ref[...]
    for r0 in range(0, ATTN_WIDTH, 512):
        t = _dot_nt(wt_ref[_FEAT_Q + r0:_FEAT_Q + r0 + 512, :], hb)
        for h0 in range(0, 512, HEAD_DIM):
            qt_ref[0, r0 + h0:r0 + h0 + HEAD_DIM, :] = _rope_rows(
                t[h0:h0 + HEAD_DIM], cost, sint).astype(BF16)
        zt_ref[0, r0:r0 + 512, :] = _dot_nt(wt_ref[_FEAT_Z + r0:_FEAT_Z + r0 + 512, :], hb).astype(BF16)
    for row0, out_ref in ((_FEAT_VSEL, vsel_ref), (_FEAT_VWIN, vwin_ref)):
        t = _dot_nt(wt_ref[row0:row0 + KV_WIDTH, :], hb).astype(BF16)
        for c in range(tm // tk):
            out_ref[0, c] = t[:, c * tk:(c + 1) * tk]
    gate = _dot_nt(wt_ref[_FEAT_GATE:_FEAT_GATE + LANES, :], hb)
    gate = 1.0 / (1.0 + jnp.exp(-gate))
    rows = N_BRANCH * HEADS_PER_STEP
    for hg in range(N_HEADS // HEADS_PER_STEP):
        gt_ref[0, hg] = _pad_rows(gate[hg * rows:(hg + 1) * rows], GATE_ROWS)


def _nsa_in(x, g, w_k, w_t, lane_tables, row_tables):
    B, S, _ = x.shape
    tm = min(PROJ_ROWS, S)
    tk = min(ATTN_TILE, S)
    W, G = ATTN_WIDTH, NSA_GROUPS
    const = lambda b, s: (0, 0)
    feat = lambda rows: pl.BlockSpec((1, rows, tm), lambda b, s: (b, 0, s))
    tiles = pl.BlockSpec((1, tm // tk, KV_WIDTH, tk), lambda b, s: (b, s, 0, 0))
    keys = pl.BlockSpec((1, G, tm, LANES), lambda b, s: (b, 0, s, 0))
    chunk_w = CMP_STRIDE * HEAD_DIM
    chunk = pl.BlockSpec((1, G, tm // CMP_STRIDE, chunk_w), lambda b, s: (b, 0, s, 0))
    n_hg = N_HEADS // HEADS_PER_STEP
    return pl.pallas_call(
        functools.partial(_nsa_in_kernel, tk=tk),
        name="nsa_in",
        grid=(B, S // tm),
        in_specs=[
            pl.BlockSpec((1, tm, D_MODEL), lambda b, s: (b, s, 0)),
            pl.BlockSpec((1, D_MODEL), const),
            pl.BlockSpec(w_k.shape, const),
            pl.BlockSpec(w_t.shape, const),
        ] + [pl.BlockSpec((tm, LANES), lambda b, s: (s, 0))] * 3
          + [pl.BlockSpec((ROPE_HALF, tm), lambda b, s: (0, s))] * 2,
        out_specs=[feat(W), keys, keys, chunk, chunk, tiles, tiles, feat(W),
                   pl.BlockSpec((1, n_hg, GATE_ROWS, tm), lambda b, s: (b, 0, 0, s))],
        out_shape=[
            jax.ShapeDtypeStruct((B, W, S), BF16),
            jax.ShapeDtypeStruct((B, G, S, LANES), BF16),
            jax.ShapeDtypeStruct((B, G, S, LANES), BF16),
            jax.ShapeDtypeStruct((B, G, S // CMP_STRIDE, chunk_w), BF16),
            jax.ShapeDtypeStruct((B, G, S // CMP_STRIDE, chunk_w), BF16),
            jax.ShapeDtypeStruct((B, S // tk, KV_WIDTH, tk), BF16),
            jax.ShapeDtypeStruct((B, S // tk, KV_WIDTH, tk), BF16),
            jax.ShapeDtypeStruct((B, W, S), BF16),
            jax.ShapeDtypeStruct((B, n_hg, GATE_ROWS, S), F32),
        ],
        scratch_shapes=[pltpu.VMEM((2 * KV_WIDTH // LANES, tm, LANES), F32)],
        compiler_params=_params("parallel", "parallel"),
    )(x, g, w_k, w_t, *lane_tables, *row_tables)


def _compress_kernel(chk_ref, chv_ref, w1k_ref, w2k_ref, pek_ref, w1v_ref, w2vt_ref, pev_ref,
                     cos_ref, slo_ref, shi_ref, kc_ref, vct_ref):
    half = CMP_STRIDE * HEAD_DIM

    def hidden(ch, w1_ref, pe_ref):
        a = _dot(ch, w1_ref[:half, :])
        b = _dot(ch, w1_ref[half:, :])
        b = pltpu.roll(b, b.shape[0] - 1, axis=0)
        pe = _dot(pe_ref[...], w1_ref[...])[0:1, :]
        return _silu(a + b + pe).astype(BF16)

    kc = _dot(hidden(chk_ref[0, 0], w1k_ref, pek_ref), w2k_ref[...])
    kc_ref[0, 0] = _rope_lanes(kc, cos_ref[...], slo_ref[...], shi_ref[...]).astype(BF16)
    vct_ref[0, 0] = _dot_nt(w2vt_ref[...], hidden(chv_ref[0, 0], w1v_ref, pev_ref)).astype(BF16)


def _compress(chk, chv, w1k, w2k, pek, w1v, w2vt, pev, tables):
    B, G, n_chunk, width = chk.shape
    const = lambda b, g: (0, 0)
    chunk_spec = pl.BlockSpec((1, 1, n_chunk, width), lambda b, g: (b, g, 0, 0))
    specs = lambda ws: [pl.BlockSpec(w.shape, const) for w in ws]
    return pl.pallas_call(
        _compress_kernel,
        name="nsa_compress",
        grid=(B, G),
        in_specs=[chunk_spec, chunk_spec] + specs((w1k, w2k, pek, w1v, w2vt, pev))
                 + [pl.BlockSpec((n_chunk, LANES), const)] * 3,
        out_specs=[pl.BlockSpec((1, 1, n_chunk, LANES), lambda b, g: (b, g, 0, 0)),
                   pl.BlockSpec((1, 1, HEAD_DIM, n_chunk), lambda b, g: (b, g, 0, 0))],
        out_shape=[jax.ShapeDtypeStruct((B, G, n_chunk, LANES), BF16),
                   jax.ShapeDtypeStruct((B, G, HEAD_DIM, n_chunk), BF16)],
        compiler_params=_params("parallel", "parallel"),
    )(chk, chv, w1k, w2k, pek, w1v, w2vt, pev, *tables)


def _pad_rows(x, rows):
    if rows == x.shape[0]:
        return x
    return jnp.concatenate([x, jnp.zeros((rows - x.shape[0], x.shape[1]), x.dtype)], axis=0)


def _cmp_attn_kernel(q_ref, kc_ref, vct_ref, ovt_ref, oc_ref, bias_ref, *, t, n_cmp, n_blk):
    i = pl.program_id(2)
    kc = kc_ref[0, 0]
    vct = vct_ref[0, 0]
    n_pad = kc.shape[0]
    blk_c = _iota((n_pad, t), 0)
    qry = i * t + _iota((n_pad, t), 1)
    valid = jnp.logical_and(blk_c * CMP_STRIDE + (CMP_LEN - 1) <= qry, blk_c < n_cmp)
    p_sum = jnp.zeros((n_pad, t), F32)
    logits = [_dot(kc, _pad_rows(q_ref[0, r * HEAD_DIM:(r + 1) * HEAD_DIM, :], LANES))
              for r in range(NSA_REP)]
    for r in range(NSA_REP):
        s = jnp.where(valid, logits[r], NEG)
        e = jnp.exp2(s - _col_max(s))
        p = jnp.where(valid, e * (1.0 / _col_sum(e)), 0.0)
        p_sum = p_sum + p
        oc_ref[0, r * HEAD_DIM:(r + 1) * HEAD_DIM, :] = _dot(vct, p.astype(BF16)).astype(oc_ref.dtype)

    ovt = ovt_ref[...]
    p_hi, p_lo = _split2(p_sum)
    imp = _dot(ovt, p_hi) + _dot(ovt, p_lo)
    blk = _iota(imp.shape, 0)
    cur = (i * t + _iota(imp.shape, 1)) // SEL_LEN
    forced = jnp.logical_or(blk == 0, jnp.logical_or(blk == cur, blk == cur - 1))
    imp = jnp.where(forced, FORCE, jnp.where(blk <= cur, imp, -1.0))
    keep = jnp.zeros(imp.shape, jnp.bool_)
    for _ in range(min(N_SELECT, n_blk)):
        top = _col_max(imp)
        first = jnp.min(jnp.where(imp == top, blk, n_blk), axis=0, keepdims=True)
        pick = blk == first
        keep = jnp.logical_or(keep, pick)
        imp = jnp.where(pick, -jnp.inf, imp)
    bias_ref[0, 0] = jnp.where(keep, 0.0, NEG).astype(bias_ref.dtype)


def _cmp_attn(qt, kc, vct, ovt, n_cmp):
    B, W, S = qt.shape
    G = kc.shape[1]
    t = min(ATTN_TILE, S)
    n_blk = ovt.shape[0]
    gw = W // G
    kern = functools.partial(_cmp_attn_kernel, t=t, n_cmp=n_cmp, n_blk=n_blk)
    return pl.pallas_call(
        kern,
        name="nsa_cmp_attn",
        grid=(B, G, S // t),
        in_specs=[
            pl.BlockSpec((1, gw, t), lambda b, g, i: (b, g, i)),
            pl.BlockSpec((1, 1) + kc.shape[2:], lambda b, g, i: (b, g, 0, 0)),
            pl.BlockSpec((1, 1) + vct.shape[2:], lambda b, g, i: (b, g, 0, 0)),
            pl.BlockSpec(ovt.shape, lambda b, g, i: (0, 0)),
        ],
        out_specs=[
            pl.BlockSpec((1, gw, t), lambda b, g, i: (b, g, i)),
            pl.BlockSpec((1, 1, n_blk, t), lambda b, g, i: (b, g, 0, i)),
        ],
        out_shape=[
            jax.ShapeDtypeStruct((B, W, S), BF16),
            jax.ShapeDtypeStruct((B, G, n_blk, S), BF16),
        ],
        compiler_params=_params("parallel", "parallel", "parallel"),
    )(qt, kc, vct, ovt)


def _sel_attn_kernel(q_ref, bias_ref, k_ref, v_ref, oc_ref, ow_ref, gt_ref, z_ref, o_ref,
                     q_scr, s_scr, acc_scr, *, tq, t):
    del tq
    nh = q_scr.shape[1]
    n_q = q_ref.shape[2] // t

    def load_q(i, slot):
        cols = slice(i * t, (i + 1) * t)
        for hh in range(nh):
            bias = _pad_rows(bias_ref[0, hh // NSA_REP, :, cols], HEAD_DIM)
            q_scr[slot, hh] = jnp.concatenate(
                [q_ref[0, hh * HEAD_DIM:(hh + 1) * HEAD_DIM, cols], bias], axis=0)

    def key_tile(j, hh):
        return k_ref[0, hh // NSA_REP, _tile(j, t), :]

    def value_rows(j, hh):
        g = hh // NSA_REP
        return v_ref[0, j, g * HEAD_DIM:(g + 1) * HEAD_DIM, :]

    def emit(i, accs):
        cols = slice(i * t, (i + 1) * t)
        for hh in range(nh):
            rows = slice(hh * HEAD_DIM, (hh + 1) * HEAD_DIM)
            gate = lambda b: gt_ref[0, 0, N_BRANCH * hh + b:N_BRANCH * hh + b + 1, cols]
            o = (gate(0) * oc_ref[0, rows, cols].astype(F32) + gate(1) * _normalised(accs[hh])
                 + gate(2) * ow_ref[0, rows, cols].astype(F32))
            o_ref[0, rows, cols] = (o * _silu(z_ref[0, rows, cols].astype(F32))).astype(o_ref.dtype)

    _causal_attention(n_q, t, load_q, key_tile, value_rows, emit, q_scr, s_scr, acc_scr)


def _sel_attn(qt, bias, k_aug, vt, oc, ow, gt, zt):
    B, W, S = qt.shape
    t = min(ATTN_TILE, S)
    ng = HEADS_PER_STEP // NSA_REP
    n_blk = bias.shape[2]
    feat = pl.BlockSpec((1, HEADS_PER_STEP * HEAD_DIM, S), lambda b, g: (b, g, 0))
    return pl.pallas_call(
        functools.partial(_sel_attn_kernel, tq=t, t=t),
        name="nsa_sel_attn",
        grid=(B, NSA_GROUPS // ng),
        in_specs=[
            feat,
            pl.BlockSpec((1, ng, n_blk, S), lambda b, g: (b, g, 0, 0)),
            pl.BlockSpec((1, ng, S, LANES), lambda b, g: (b, g, 0, 0)),
            pl.BlockSpec((1, S // t, ng * HEAD_DIM, t), lambda b, g: (b, 0, g, 0)),
            feat, feat,
            pl.BlockSpec((1, 1, GATE_ROWS, S), lambda b, g: (b, g, 0, 0)),
            feat,
        ],
        out_specs=feat,
        out_shape=jax.ShapeDtypeStruct((B, W, S), BF16),
        scratch_shapes=_attn_scratch(HEADS_PER_STEP, LANES, t),
        compiler_params=_params("parallel", "parallel"),
    )(qt, bias, k_aug, vt, oc, ow, gt, zt)


def _win_attn_kernel(q_ref, k_ref, v_ref, o_ref, *, t, n_back):
    i = pl.program_id(2)
    above = _iota((t, t), 0) > _iota((t, t), 1)
    tiles = []
    for d in range(n_back + 1):
        j = i - d
        jc = jnp.maximum(j, 0)
        k = k_ref[0, 0, pl.ds(pl.multiple_of(jc * t, t), t), :]
        if d == 0:
            ok = jnp.logical_not(above)
        elif d == n_back:
            ok = jnp.logical_and(above, j >= 0)
        else:
            ok = jnp.broadcast_to(j >= 0, (t, t))
        v = jnp.concatenate([v_ref[0, jc], jnp.ones((ONES_ROWS, t), BF16)], axis=0)
        tiles.append((k, v, ok))
    logits = []
    for hh in range(NSA_REP):
        q_aug = _pad_rows(q_ref[0, hh * HEAD_DIM:(hh + 1) * HEAD_DIM, :], LANES)
        logits.append([_dot(k, q_aug) for k, _, _ in tiles])
    for hh in range(NSA_REP):
        ss = [jnp.where(ok, s, NEG) for s, (_, _, ok) in zip(logits[hh], tiles)]
        m = functools.reduce(jnp.maximum, [_col_max(s) for s in ss])
        acc = functools.reduce(jnp.add, [_dot(v, jnp.exp2(s - m).astype(BF16))
                                         for (_, v, _), s in zip(tiles, ss)])
        o_ref[0, hh * HEAD_DIM:(hh + 1) * HEAD_DIM, :] = _normalised(acc).astype(o_ref.dtype)


def _win_attn(qt, k_aug, vt):
    B, W, S = qt.shape
    t = min(ATTN_TILE, S)
    assert WINDOW % t == 0
    return pl.pallas_call(
        functools.partial(_win_attn_kernel, t=t, n_back=WINDOW // t),
        name="nsa_win_attn",
        grid=(B, NSA_GROUPS, S // t),
        out_shape=jax.ShapeDtypeStruct((B, W, S), BF16),
        compiler_params=_params("parallel", "parallel", "parallel"),
        **_group_attn_specs(S, t, 1),
    )(qt, k_aug, vt)


def _local_attn_kernel(q_ref, kc_ref, vct_ref, ovt_ref, kw_ref, vw_ref, oc_ref, bias_ref, ow_ref,
                       *, t, n_cmp, n_blk, n_back):
    i = pl.program_id(2)
    heads = range(NSA_REP)
    q_aug = [_pad_rows(q_ref[0, r * HEAD_DIM:(r + 1) * HEAD_DIM, :], LANES) for r in heads]

    kc = kc_ref[0, 0]
    cmp_logits = [_dot(kc, q_aug[r]) for r in heads]
    above = _iota((t, t), 0) > _iota((t, t), 1)
    ones = jnp.ones((ONES_ROWS, t), BF16)
    tiles = []
    for d in range(n_back + 1):
        j = i - d
        jc = jnp.maximum(j, 0)
        k = kw_ref[0, 0, pl.ds(pl.multiple_of(jc * t, t), t), :]
        if d == 0:
            ok = jnp.logical_not(above)
        elif d == n_back:
            ok = jnp.logical_and(above, j >= 0)
        else:
            ok = jnp.broadcast_to(j >= 0, (t, t))
        tiles.append((jnp.concatenate([vw_ref[0, jc], ones], axis=0), ok))
        for r in heads:
            tiles[-1] += (_dot(k, q_aug[r]),)

    vct = vct_ref[0, 0]
    n_pad = kc.shape[0]
    blk_c = _iota((n_pad, t), 0)
    qry = i * t + _iota((n_pad, t), 1)
    valid = jnp.logical_and(blk_c * CMP_STRIDE + (CMP_LEN - 1) <= qry, blk_c < n_cmp)
    p_sum = jnp.zeros((n_pad, t), F32)
    for r in heads:
        s = jnp.where(valid, cmp_logits[r], NEG)
        e = jnp.exp2(s - _col_max(s))
        p = jnp.where(valid, e * (1.0 / _col_sum(e)), 0.0)
        p_sum = p_sum + p
        oc_ref[0, r * HEAD_DIM:(r + 1) * HEAD_DIM, :] = _dot(vct, p.astype(BF16)).astype(oc_ref.dtype)

    ovt = ovt_ref[...]
    p_hi, p_lo = _split2(p_sum)
    imp = _dot(ovt, p_hi) + _dot(ovt, p_lo)
    blk = _iota(imp.shape, 0)
    cur = (i * t + _iota(imp.shape, 1)) // SEL_LEN
    forced = jnp.logical_or(blk == 0, jnp.logical_or(blk == cur, blk == cur - 1))
    imp = jnp.where(forced, FORCE, jnp.where(blk <= cur, imp, -1.0))
    keep = jnp.zeros(imp.shape, jnp.bool_)

    n_rounds = min(N_SELECT, n_blk)
    per_head = -(-n_rounds // NSA_REP)
    done = 0
    for r in heads:
        ss = [jnp.where(ok, logits[r], NEG) for _, ok, *logits in tiles]
        m = functools.reduce(jnp.maximum, [_col_max(s) for s in ss])
        acc = functools.reduce(jnp.add, [_dot(v, jnp.exp2(s - m).astype(BF16))
                                         for (v, *_), s in zip(tiles, ss)])
        ow_ref[0, r * HEAD_DIM:(r + 1) * HEAD_DIM, :] = _normalised(acc).astype(ow_ref.dtype)
        for _ in range(min(per_head, n_rounds - done)):
            top = _col_max(imp)
            first = jnp.min(jnp.where(imp == top, blk, n_blk), axis=0, keepdims=True)
            pick = blk == first
            keep = jnp.logical_or(keep, pick)
            imp = jnp.where(pick, -jnp.inf, imp)
            done += 1
    bias_ref[0, 0] = jnp.where(keep, 0.0, NEG).astype(bias_ref.dtype)


def _local_attn(qt, kc, vct, ovt, n_cmp, kw, vw):
    B, W, S = qt.shape
    G = kc.shape[1]
    t = min(ATTN_TILE, S)
    assert WINDOW % t == 0
    n_blk = ovt.shape[0]
    gw = W // G
    kern = functools.partial(_local_attn_kernel, t=t, n_cmp=n_cmp, n_blk=n_blk, n_back=WINDOW // t)
    feat = pl.BlockSpec((1, gw, t), lambda b, g, i: (b, g, i))
    return pl.pallas_call(
        kern,
        name="nsa_local_attn",
        grid=(B, G, S // t),
        in_specs=[
            feat,
            pl.BlockSpec((1, 1) + kc.shape[2:], lambda b, g, i: (b, g, 0, 0)),
            pl.BlockSpec((1, 1) + vct.shape[2:], lambda b, g, i: (b, g, 0, 0)),
            pl.BlockSpec(ovt.shape, lambda b, g, i: (0, 0)),
            pl.BlockSpec((1, 1, S, LANES), lambda b, g, i: (b, g, 0, 0)),
            pl.BlockSpec((1, S // t, HEAD_DIM, t), lambda b, g, i: (b, 0, g, 0)),
        ],
        out_specs=[feat, pl.BlockSpec((1, 1, n_blk, t), lambda b, g, i: (b, g, 0, i)), feat],
        out_shape=[
            jax.ShapeDtypeStruct((B, W, S), BF16),
            jax.ShapeDtypeStruct((B, G, n_blk, S), BF16),
            jax.ShapeDtypeStruct((B, W, S), BF16),
        ],
        compiler_params=_params("parallel", "parallel", "parallel"),
    )(qt, kc, vct, ovt, kw, vw)


def _nsa_out_kernel(oc_ref, os_ref, ow_ref, gt_ref, zt_ref, x_ref, w_ref, g_ref, y_ref, gated_scr):
    for h in range(N_HEADS):
        rows = slice(h * HEAD_DIM, (h + 1) * HEAD_DIM)
        o = jnp.zeros((HEAD_DIM, gated_scr.shape[1]), F32)
        for b, ref in enumerate((oc_ref, os_ref, ow_ref)):
            gate = gt_ref[0, N_BRANCH * h + b:N_BRANCH * h + b + 1, :]
            o = o + gate * ref[0, rows, :].astype(F32)
        gated_scr[rows, :] = (o * _silu(zt_ref[0, rows, :].astype(F32))).astype(BF16)
    y = x_ref[0] + _dot_tn(gated_scr[...], w_ref[...])
    y_ref[0] = _rmsnorm(y, g_ref[...])


def _nsa_out(oc, os_, ow, gt, zt, x, w_out, final_g):
    B, S, _ = x.shape
    tm = min(PROJ_ROWS, S)
    feat = lambda rows: pl.BlockSpec((1, rows, tm), lambda b, s: (b, 0, s))
    tok = pl.BlockSpec((1, tm, D_MODEL), lambda b, s: (b, s, 0))
    wide = feat(ATTN_WIDTH)
    return pl.pallas_call(
        _nsa_out_kernel,
        name="nsa_out",
        grid=(B, S // tm),
        in_specs=[wide, wide, wide, feat(LANES), wide, tok,
                  pl.BlockSpec((ATTN_WIDTH, D_MODEL), lambda b, s: (0, 0)),
                  pl.BlockSpec((1, D_MODEL), lambda b, s: (0, 0))],
        out_specs=tok,
        out_shape=jax.ShapeDtypeStruct((B, S, D_MODEL), F32),
        scratch_shapes=[pltpu.VMEM((ATTN_WIDTH, tm), BF16)],
        compiler_params=_params("parallel", "parallel"),
    )(oc, os_, ow, gt, zt, x, w_out, final_g)


def _fox_layer(x, g, w_in, b_f, w_out):
    B, S, _ = x.shape
    W = ATTN_WIDTH
    w_k = w_in[:, W:2 * W].astype(BF16)
    w_t = jnp.concatenate([w_in[:, :W] * Q_SCALE, w_in[:, 2 * W:3 * W], w_in[:, 3 * W + N_HEADS:]],
                          axis=1).T.astype(BF16)
    pad = jnp.zeros((D_MODEL, LANES - 3 * N_HEADS), F32)
    w_f = w_in[:, 3 * W:3 * W + N_HEADS]
    w_f = jnp.concatenate([w_f, w_f, w_f, pad], axis=1)
    wf_hi, wf_lo = _split2(w_f)
    bf_row = jnp.concatenate([b_f, b_f, b_f, jnp.zeros((LANES - 3 * N_HEADS,), F32)])[None, :]

    qt, k, vt, zt, cx = _fox_in(x, g[None, :], w_k, w_t, wf_hi, wf_lo, bf_row)

    t = min(QUERY_TILE, S)
    r = jnp.arange(LANES)
    head = jnp.arange(N_HEADS)
    sel = jnp.logical_and(r[None, :] % N_HEADS == head[:, None], r[None, :] < 3 * N_HEADS)
    qx = jnp.broadcast_to(jnp.where(sel, -1.0, 0.0).astype(BF16)[:, :, None], (N_HEADS, LANES, t))
    ot = _fox_attn(qt, qx, k, cx, vt, zt)
    return _out_proj(ot, x, w_out.astype(BF16))


def _pad_groups(w):
    k = w.shape[0]
    w = w.reshape(k, NSA_GROUPS, HEAD_DIM)
    return jnp.concatenate([w, jnp.zeros_like(w)], axis=2).reshape(k, NSA_GROUPS * LANES)


def _nsa_layer(x, g, w_in, pe_k, w_ck1, w_ck2, pe_v, w_cv1, w_cv2, w_out, final_g):
    B, S, _ = x.shape
    W, KV, G = ATTN_WIDTH, KV_WIDTH, NSA_GROUPS
    sec = lambda j: w_in[:, W + j * KV:W + (j + 1) * KV]
    gate_off = W + 6 * KV
    n_gate = N_BRANCH * N_HEADS
    w_gate = jnp.concatenate([w_in[:, gate_off:gate_off + n_gate],
                              jnp.zeros((D_MODEL, LANES - n_gate), F32)], axis=1)
    w_k = jnp.concatenate([_pad_groups(sec(2)), _pad_groups(sec(4)), sec(0), sec(1)],
                          axis=1).astype(BF16)
    w_t = jnp.concatenate([w_in[:, :W] * Q_SCALE, sec(3), sec(5), w_in[:, gate_off + n_gate:], w_gate],
                          axis=1).T.astype(BF16)
    pos = jnp.arange(S)
    cos, sin = _rope_angles(pos)
    qt, ksel, kwin, chk, chv, vsel, vwin, zt, gt = _nsa_in(
        x, g[None, :], w_k, w_t, _rope_lane_tables(pos), (cos.T, sin.T))

    n_chunk = S // CMP_STRIDE
    n_cmp = n_chunk - CMP_LEN // CMP_STRIDE + 1

    def flat_pe(pe):
        pe = pe.reshape(1, CMP_LEN * HEAD_DIM)
        return jnp.broadcast_to(pe, (SUBLANES, CMP_LEN * HEAD_DIM)).astype(BF16)

    w2k = jnp.concatenate([w_ck2, jnp.zeros_like(w_ck2)], axis=1).astype(BF16)
    cmp_end = jnp.arange(n_chunk) * CMP_STRIDE + CMP_LEN - 1
    kc, vct = _compress(chk, chv, w_ck1.astype(BF16), w2k, flat_pe(pe_k),
                        w_cv1.astype(BF16), w_cv2.T.astype(BF16), flat_pe(pe_v),
                        _rope_lane_tables(cmp_end))

    n_blk = S // SEL_LEN
    ci = jnp.arange(n_chunk) * CMP_STRIDE
    sj = jnp.arange(n_blk) * SEL_LEN
    ovt = jnp.logical_and(ci[None, :] < sj[:, None] + SEL_LEN, ci[None, :] + CMP_LEN > sj[:, None])
    ovt = jnp.logical_and(ovt, jnp.arange(n_chunk)[None, :] < n_cmp).astype(BF16)
    oc, bias, ow = _local_attn(qt, kc, vct, ovt, n_cmp, kwin, vwin)
    ot = _sel_attn(qt, bias, ksel, vsel, oc, ow, gt, zt)
    return _out_proj(ot, x, w_out.astype(BF16), final_g[None, :])


def kernel(x, norm_g, fox_w_in, fox_b_f, fox_w_out, nsa_w_in, nsa_pe_k, nsa_w_ck1, nsa_w_ck2,
           nsa_pe_v, nsa_w_cv1, nsa_w_cv2, nsa_w_out, final_g):
    x = _fox_layer(x, norm_g[0], fox_w_in[0], fox_b_f[0], fox_w_out[0])
    return _nsa_layer(x, norm_g[1], nsa_w_in[0], nsa_pe_k[0], nsa_w_ck1[0], nsa_w_ck2[0],
                      nsa_pe_v[0], nsa_w_cv1[0], nsa_w_cv2[0], nsa_w_out[0], final_g)
```

```python
import functools
import math

import jax
import jax.numpy as jnp
import numpy as np
from jax import lax
from jax.experimental import pallas as pl
from jax.experimental.pallas import tpu as pltpu

F32 = jnp.float32
BF16 = jnp.bfloat16

D_MODEL = 1024
N_HEADS = 16
HEAD_DIM = 64
ATTN_WIDTH = N_HEADS * HEAD_DIM
NSA_GROUPS = 4
NSA_REP = N_HEADS // NSA_GROUPS
KV_WIDTH = NSA_GROUPS * HEAD_DIM
CMP_LEN = 32
CMP_STRIDE = 16
CMP_HIDDEN = 256
SEL_LEN = 64
N_SELECT = 8
WINDOW = 512
N_BRANCH = 3
ROPE_THETA = 500000.0
ROPE_DIM = HEAD_DIM // 4
ROPE_HALF = ROPE_DIM // 2
NORM_EPS = 1e-6
NEG = -1e30
FORCE = 1e6
LOG2E = math.log2(math.e)
Q_SCALE = HEAD_DIM ** -0.5 * LOG2E

LANES = 128
SUBLANES = 8
ONES_ROWS = 2 * SUBLANES
PROJ_ROWS = 512
ATTN_TILE = 256
HEADS_PER_STEP = 8
GATE_ROWS = -(-N_BRANCH * HEADS_PER_STEP // SUBLANES) * SUBLANES
VMEM_LIMIT = 56 * 1024 * 1024


def _params(*sem):
    return pltpu.CompilerParams(dimension_semantics=sem, vmem_limit_bytes=VMEM_LIMIT)


def _iota(shape, dim):
    return lax.broadcasted_iota(jnp.int32, shape, dim)


def _split2(x):
    hi = x.astype(BF16)
    lo = (x - hi.astype(F32)).astype(BF16)
    return hi, lo


def _split3(x):
    hi = x.astype(BF16)
    r1 = x - hi.astype(F32)
    mid = r1.astype(BF16)
    lo = (r1 - mid.astype(F32)).astype(BF16)
    return hi, mid, lo


def _dot(a, b):
    return jnp.dot(a, b, preferred_element_type=F32)


def _dot_nt(a, b):
    return lax.dot_general(a, b, (((1,), (1,)), ((), ())), preferred_element_type=F32)


def _dot_tn(a, b):
    return lax.dot_general(a, b, (((0,), (0,)), ((), ())), preferred_element_type=F32)


def _rmsnorm(x, g):
    ms = jnp.mean(x * x, axis=-1, keepdims=True)
    return x * lax.rsqrt(ms + NORM_EPS) * g


def _silu(x):
    return x * (1.0 / (1.0 + jnp.exp(-x)))


def _col_max(x):
    return jnp.max(x, axis=0, keepdims=True)


def _col_sum(x):
    return jnp.sum(x, axis=0, keepdims=True)


def _pad_rows(x, rows):
    if rows == x.shape[0]:
        return x
    return jnp.concatenate([x, jnp.zeros((rows - x.shape[0], x.shape[1]), x.dtype)], axis=0)


def _tile(j, t):
    return pl.ds(j * t if isinstance(j, int) else pl.multiple_of(j * t, t), t)


def _normalised(acc):
    return acc[:HEAD_DIM] * (1.0 / acc[HEAD_DIM:HEAD_DIM + 1])


def _rope_lanes(x, cos, sin_lo, sin_hi):
    return (x * cos + pltpu.roll(x, LANES - ROPE_HALF, axis=1) * sin_lo
            + pltpu.roll(x, ROPE_HALF, axis=1) * sin_hi)


def _rope_rows(x, cos, sin):
    x1, x2 = x[:ROPE_HALF], x[ROPE_HALF:ROPE_DIM]
    return jnp.concatenate([x1 * cos - x2 * sin, x1 * sin + x2 * cos, x[ROPE_DIM:]], axis=0)


def _rope_angles(pos):
    inv_freq = np.power(np.float32(ROPE_THETA),
                        -np.arange(ROPE_HALF, dtype=np.float32) * np.float32(2.0 / ROPE_DIM))
    ang = pos.astype(np.float32)[:, None] * inv_freq[None, :].astype(np.float32)
    return np.cos(ang).astype(np.float32), np.sin(ang).astype(np.float32)


def _rope_lane_tables(pos):
    cos, sin = _rope_angles(pos)
    n = pos.shape[0]
    ones = np.ones((n, HEAD_DIM - ROPE_DIM), np.float32)
    zeros = np.zeros((n, HEAD_DIM - ROPE_DIM), np.float32)
    z8 = np.zeros((n, ROPE_HALF), np.float32)
    c = np.concatenate([cos, cos, ones], axis=1)
    s_lo = np.concatenate([-sin, z8, zeros], axis=1)
    s_hi = np.concatenate([z8, sin, zeros], axis=1)
    tile = lambda t: np.concatenate([t, t], axis=1)
    return tile(c), tile(s_lo), tile(s_hi)


def _fox_in_kernel(x_ref, g_ref, wk_ref, wt_ref, wfa_ref, wfb_ref, bf_ref, tri_ref,
                   qt_ref, k_ref, vt_ref, zt_ref, cx_ref, carry_ref, *, tk):
    s = pl.program_id(1)
    h = _rmsnorm(x_ref[0], g_ref[...])
    hb = h.astype(BF16)
    tm = hb.shape[0]
    W, H = ATTN_WIDTH, N_HEADS

    @pl.when(s == 0)
    def _():
        carry_ref[...] = jnp.zeros_like(carry_ref)

    h_lo = (h - hb.astype(F32)).astype(BF16)
    fa = _dot(hb, wfa_ref[...])
    f = fa + pltpu.roll(fa, LANES - 3 * H, axis=1) + _dot(h_lo, wfb_ref[...]) + bf_ref[...]

    for n0 in range(0, W, 512):
        k_ref[0, :, n0:n0 + 512] = _dot(hb, wk_ref[:, n0:n0 + 512]).astype(BF16)

    log_f = jnp.minimum(f, 0.0) - jnp.log1p(jnp.exp(-jnp.abs(f)))
    lane = _iota(log_f.shape, 1)
    a0, a1, a2 = _split3(log_f)
    pieces = jnp.where(lane < H, a0, jnp.where(lane < 2 * H, a1, a2))
    c = _dot(tri_ref[...], pieces)

    for r0 in range(0, 3 * W, 512):
        t = _dot_nt(wt_ref[r0:r0 + 512, :], hb).astype(BF16)
        if r0 < W:
            qt_ref[0, r0:r0 + 512, :] = t
        elif r0 < 2 * W:
            for ct in range(tm // tk):
                vt_ref[0, ct, r0 - W:r0 - W + 512, :] = t[:, ct * tk:(ct + 1) * tk]
        else:
            zt_ref[0, r0 - 2 * W:r0 - 2 * W + 512, :] = t

    c = c + pltpu.roll(c, LANES - H, axis=1) + pltpu.roll(c, LANES - 2 * H, axis=1)
    c = c + carry_ref[0:1, :]
    carry_ref[...] = jnp.broadcast_to(c[tm - 1:tm, :], carry_ref.shape)
    c = jnp.where(lane < H, c, jnp.where(lane < 2 * H, pltpu.roll(c, H, axis=1),
                                         pltpu.roll(c, 2 * H, axis=1)))

    hi, mid, lo = _split3(c * LOG2E)
    zero = jnp.zeros_like(hi)
    cx_ref[0] = jnp.where(lane < H, hi,
                          jnp.where(lane < 2 * H, mid, jnp.where(lane < 3 * H, lo, zero)))


def _fox_in(x, g, w_k, w_t, wf_a, wf_b, bf_row):
    B, S, _ = x.shape
    tm = min(PROJ_ROWS, S)
    tk = min(ATTN_TILE, S)
    W = ATTN_WIDTH
    tri = np.tril(np.ones((tm, tm), np.float32)).astype(BF16)
    const = lambda b, s: (0, 0)
    feat = pl.BlockSpec((1, W, tm), lambda b, s: (b, 0, s))
    return pl.pallas_call(
        functools.partial(_fox_in_kernel, tk=tk),
        name="fox_in",
        grid=(B, S // tm),
        in_specs=[
            pl.BlockSpec((1, tm, D_MODEL), lambda b, s: (b, s, 0)),
            pl.BlockSpec((1, D_MODEL), const),
            pl.BlockSpec(w_k.shape, const),
            pl.BlockSpec(w_t.shape, const),
            pl.BlockSpec((D_MODEL, LANES), const),
            pl.BlockSpec((D_MODEL, LANES), const),
            pl.BlockSpec((1, LANES), const),
            pl.BlockSpec((tm, tm), const),
        ],
        out_specs=[
            feat,
            pl.BlockSpec((1, tm, W), lambda b, s: (b, s, 0)),
            pl.BlockSpec((1, tm // tk, W, tk), lambda b, s: (b, s, 0, 0)),
            feat,
            pl.BlockSpec((1, tm, LANES), lambda b, s: (b, s, 0)),
        ],
        out_shape=[
            jax.ShapeDtypeStruct((B, W, S), BF16),
            jax.ShapeDtypeStruct((B, S, W), BF16),
            jax.ShapeDtypeStruct((B, S // tk, W, tk), BF16),
            jax.ShapeDtypeStruct((B, W, S), BF16),
            jax.ShapeDtypeStruct((B, S, LANES), BF16),
        ],
        scratch_shapes=[pltpu.VMEM((SUBLANES, LANES), F32)],
        compiler_params=_params("parallel", "arbitrary"),
    )(x, g, w_k, w_t, wf_a, wf_b, bf_row, tri)


def _causal_attention(n_q, t, load_q, key_tile, value_rows, emit, q_scr, s_scr, acc_scr):
    heads = range(q_scr.shape[1])
    ones = jnp.ones((ONES_ROWS, t), BF16)
    causal = _iota((t, t), 0) <= _iota((t, t), 1)

    def logits(slot, j):
        return [_dot(key_tile(j, hh), q_scr[slot, hh]) for hh in heads]

    def pv(j, hh, p):
        return _dot(jnp.concatenate([value_rows(j, hh), ones], axis=0), p.astype(BF16))

    def finish(i, slot, carry):
        accs = []
        for hh in heads:
            m, alpha = carry[hh]
            accs.append(alpha * acc_scr[hh] + pv(i - 1, hh, jnp.exp2(s_scr[slot, hh] - m)))
        emit(i, accs)

    pending = None
    for i in range(n_q):
        slot = i % 2
        load_q(i, slot)
        s_diag = logits(slot, i)
        s_first = logits(slot, 0) if i else None
        if pending is not None:
            finish(*pending)
        m_diag = []
        for hh in heads:
            s = jnp.where(causal, s_diag[hh], NEG)
            m = _col_max(s)
            acc_scr[hh] = pv(i, hh, jnp.exp2(s - m))
            m_diag.append(m)
        if i == 0:
            emit(0, [acc_scr[hh] for hh in heads])
            continue
        carry = []
        for hh in heads:
            s_scr[slot, hh] = s_first[hh]
            m_new = jnp.maximum(m_diag[hh], _col_max(s_first[hh]))
            carry.append((m_new, jnp.exp2(m_diag[hh] - m_new)))

        def body(j, carry, slot=slot):
            s_next = logits(slot, j + 1)
            pvs = [pv(j, hh, jnp.exp2(s_scr[slot, hh] - carry[hh][0])) for hh in heads]
            out = []
            for hh in heads:
                m = carry[hh][0]
                s_scr[slot, hh] = s_next[hh]
                m_new = jnp.maximum(m, _col_max(s_next[hh]))
                out.append((m_new, jnp.exp2(m - m_new)))
            for hh in heads:
                acc_scr[hh] = carry[hh][1] * acc_scr[hh] + pvs[hh]
            return tuple(out)

        carry = lax.fori_loop(0, i - 1, body, tuple(carry))
        pending = (i, slot, carry)
    if pending is not None:
        finish(*pending)


def _attn_scratch(nh, k_depth, t):
    return [pltpu.VMEM((2, nh, k_depth, t), BF16), pltpu.VMEM((2, nh, t, t), F32),
            pltpu.VMEM((nh, HEAD_DIM + ONES_ROWS, t), F32)]


def _fox_attn_kernel(q_ref, qx_ref, k_ref, kx_ref, v_ref, z_ref, o_ref, q_scr, s_scr, acc_scr, *, t):
    nh = q_scr.shape[1]
    n_q = q_ref.shape[2] // t
    row = _iota((LANES, t), 0)

    def load_q(i, slot):
        for hh in range(nh):
            pair, sub = divmod(hh, 2)
            qt = q_ref[0, pair * LANES:(pair + 1) * LANES, i * t:(i + 1) * t]
            own = jnp.logical_and(row >= sub * HEAD_DIM, row < (sub + 1) * HEAD_DIM)
            q_scr[slot, hh] = jnp.concatenate(
                [jnp.where(own, qt, jnp.zeros_like(qt)), qx_ref[hh]], axis=0)

    def key_tile(j, hh):
        pair = hh // 2
        return jnp.concatenate([k_ref[0, _tile(j, t), pair * LANES:(pair + 1) * LANES],
                                kx_ref[0, _tile(j, t), :]], axis=1)

    def value_rows(j, hh):
        return v_ref[0, j, hh * HEAD_DIM:(hh + 1) * HEAD_DIM, :]

    def emit(i, accs):
        for hh in range(nh):
            rows, cols = slice(hh * HEAD_DIM, (hh + 1) * HEAD_DIM), slice(i * t, (i + 1) * t)
            gated = _normalised(accs[hh]) * _silu(z_ref[0, rows, cols].astype(F32))
            o_ref[0, rows, cols] = gated.astype(o_ref.dtype)

    _causal_attention(n_q, t, load_q, key_tile, value_rows, emit, q_scr, s_scr, acc_scr)


def _fox_attn(qt, qx, k, kx, vt, zt):
    B, W, S = qt.shape
    t = min(ATTN_TILE, S)
    nh = HEADS_PER_STEP
    rows = nh * HEAD_DIM
    feat = pl.BlockSpec((1, rows, S), lambda b, p: (b, p, 0))
    return pl.pallas_call(
        functools.partial(_fox_attn_kernel, t=t),
        name="fox_attn",
        grid=(B, N_HEADS // nh),
        in_specs=[
            feat,
            pl.BlockSpec((nh, LANES, t), lambda b, p: (p, 0, 0)),
            pl.BlockSpec((1, S, rows), lambda b, p: (b, 0, p)),
            pl.BlockSpec((1, S, LANES), lambda b, p: (b, 0, 0)),
            pl.BlockSpec((1, S // t, rows, t), lambda b, p: (b, 0, p, 0)),
            feat,
        ],
        out_specs=feat,
        out_shape=jax.ShapeDtypeStruct((B, W, S), BF16),
        scratch_shapes=_attn_scratch(nh, 2 * LANES, t),
        compiler_params=_params("parallel", "parallel"),
    )(qt, qx, k, kx, vt, zt)


def _out_proj_kernel(*refs, final_norm):
    if final_norm:
        ot_ref, x_ref, w_ref, g_ref, y_ref = refs
    else:
        ot_ref, x_ref, w_ref, y_ref = refs
    y = x_ref[0] + _dot_tn(ot_ref[0], w_ref[...])
    y_ref[0] = _rmsnorm(y, g_ref[...]) if final_norm else y


def _out_proj(ot, x, w_out, final_g=None):
    B, S, _ = x.shape
    tm = min(PROJ_ROWS, S)
    const = lambda b, s: (0, 0)
    tok = pl.BlockSpec((1, tm, D_MODEL), lambda b, s: (b, s, 0))
    in_specs = [pl.BlockSpec((1, ATTN_WIDTH, tm), lambda b, s: (b, 0, s)), tok,
                pl.BlockSpec((ATTN_WIDTH, D_MODEL), const)]
    args = [ot, x, w_out]
    if final_g is not None:
        in_specs.append(pl.BlockSpec((1, D_MODEL), const))
        args.append(final_g)
    return pl.pallas_call(
        functools.partial(_out_proj_kernel, final_norm=final_g is not None),
        name="out_proj",
        grid=(B, S // tm),
        in_specs=in_specs,
        out_specs=tok,
        out_shape=jax.ShapeDtypeStruct((B, S, D_MODEL), F32),
        compiler_params=_params("parallel", "parallel"),
    )(*args)


_TOK_SEL = 0
_TOK_WIN = KV_WIDTH
_TOK_CMP = 2 * KV_WIDTH
_TOK_COLS = 4 * KV_WIDTH
_FEAT_Q = 0
_FEAT_VSEL = ATTN_WIDTH
_FEAT_VWIN = _FEAT_VSEL + KV_WIDTH
_FEAT_Z = _FEAT_VWIN + KV_WIDTH
_FEAT_GATE = _FEAT_Z + ATTN_WIDTH
_FEAT_ROWS = _FEAT_GATE + N_BRANCH * N_HEADS


def _nsa_in_kernel(x_ref, g_ref, wk_ref, wt_ref, cos_ref, slo_ref, shi_ref, cost_ref, sint_ref,
                   qt_ref, ksel_ref, kwin_ref, chk_ref, chv_ref, vsel_ref, vwin_ref, zt_ref, gt_ref,
                   cmp_scr, *, tk):
    s = pl.program_id(1)
    hb = _rmsnorm(x_ref[0], g_ref[...]).astype(BF16)
    tm = hb.shape[0]
    G = NSA_GROUPS

    cos, slo, shi = cos_ref[...], slo_ref[...], shi_ref[...]
    tok = s * tm + _iota((tm, LANES), 0)
    lane = _iota((tm, LANES), 1)
    low = lane < HEAD_DIM
    block_id = jnp.where(lane - HEAD_DIM == tok // SEL_LEN, 1.0, 0.0)
    keys = _dot(hb, wk_ref[:, _TOK_SEL:_TOK_CMP])
    for col0, out_ref, extra in ((_TOK_SEL, ksel_ref, block_id), (_TOK_WIN, kwin_ref, 0.0)):
        for gp in range(G // 2):
            pair = _rope_lanes(keys[:, col0 + gp * LANES:col0 + (gp + 1) * LANES], cos, slo, shi)
            out_ref[0, 2 * gp] = jnp.where(low, pair, extra).astype(BF16)
            out_ref[0, 2 * gp + 1] = jnp.where(
                low, pltpu.roll(pair, HEAD_DIM, axis=1), extra).astype(BF16)

    raw = _dot(hb, wk_ref[:, _TOK_CMP:_TOK_COLS])
    for c in range(cmp_scr.shape[0]):
        cmp_scr[c] = raw[:, c * LANES:(c + 1) * LANES]

    cost, sint = cost_ref[...], sint_ref[...]
    for r0 in range(0, ATTN_WIDTH, 512):
        t = _dot_nt(wt_ref[_FEAT_Q + r0:_FEAT_Q + r0 + 512, :], hb)
        for h0 in range(0, 512, HEAD_DIM):
            qt_ref[0, r0 + h0:r0 + h0 + HEAD_DIM, :] = _rope_rows(
                t[h0:h0 + HEAD_DIM], cost, sint).astype(BF16)
        zt_ref[0, r0:r0 + 512, :] = _dot_nt(wt_ref[_FEAT_Z + r0:_FEAT_Z + r0 + 512, :], hb).astype(BF16)
    for row0, out_ref in ((_FEAT_VSEL, vsel_ref), (_FEAT_VWIN, vwin_ref)):
        t = _dot_nt(wt_ref[row0:row0 + KV_WIDTH, :], hb).astype(BF16)
        for c in range(tm // tk):
            out_ref[0, c] = t[:, c * tk:(c + 1) * tk]
    gate = _dot_nt(wt_ref[_FEAT_GATE:_FEAT_ROWS, :], hb)
    gate = 1.0 / (1.0 + jnp.exp(-gate))
    rows = N_BRANCH * HEADS_PER_STEP
    for hg in range(N_HEADS // HEADS_PER_STEP):
        gt_ref[0, hg] = _pad_rows(gate[hg * rows:(hg + 1) * rows], GATE_ROWS)

    n_ch = tm // CMP_STRIDE
    low_ch = _iota((n_ch, LANES), 1) < HEAD_DIM
    for kind, out_ref in enumerate((chk_ref, chv_ref)):
        for gp in range(G // 2):
            c = kind * (G // 2) + gp
            for l in range(0, CMP_STRIDE, 2):
                a0 = cmp_scr[c, pl.ds(l, n_ch, stride=CMP_STRIDE), :]
                a1 = cmp_scr[c, pl.ds(l + 1, n_ch, stride=CMP_STRIDE), :]
                dst = slice(l * HEAD_DIM, (l + 2) * HEAD_DIM)
                out_ref[0, 2 * gp, :, dst] = jnp.where(
                    low_ch, a0, pltpu.roll(a1, HEAD_DIM, axis=1)).astype(BF16)
                out_ref[0, 2 * gp + 1, :, dst] = jnp.where(
                    low_ch, pltpu.roll(a0, HEAD_DIM, axis=1), a1).astype(BF16)


def _nsa_in(x, g, w_k, w_t, lane_tables, row_tables):
    B, S, _ = x.shape
    tm = min(PROJ_ROWS, S)
    tk = min(ATTN_TILE, S)
    W, G = ATTN_WIDTH, NSA_GROUPS
    const = lambda b, s: (0, 0)
    feat = lambda rows: pl.BlockSpec((1, rows, tm), lambda b, s: (b, 0, s))
    tiles = pl.BlockSpec((1, tm // tk, KV_WIDTH, tk), lambda b, s: (b, s, 0, 0))
    keys = pl.BlockSpec((1, G, tm, LANES), lambda b, s: (b, 0, s, 0))
    chunk_w = CMP_STRIDE * HEAD_DIM
    chunk = pl.BlockSpec((1, G, tm // CMP_STRIDE, chunk_w), lambda b, s: (b, 0, s, 0))
    n_hg = N_HEADS // HEADS_PER_STEP
    return pl.pallas_call(
        functools.partial(_nsa_in_kernel, tk=tk),
        name="nsa_in",
        grid=(B, S // tm),
        in_specs=[
            pl.BlockSpec((1, tm, D_MODEL), lambda b, s: (b, s, 0)),
            pl.BlockSpec((1, D_MODEL), const),
            pl.BlockSpec(w_k.shape, const),
            pl.BlockSpec(w_t.shape, const),
        ] + [pl.BlockSpec((tm, LANES), lambda b, s: (s, 0))] * 3
          + [pl.BlockSpec((ROPE_HALF, tm), lambda b, s: (0, s))] * 2,
        out_specs=[feat(W), keys, keys, chunk, chunk, tiles, tiles, feat(W),
                   pl.BlockSpec((1, n_hg, GATE_ROWS, tm), lambda b, s: (b, 0, 0, s))],
        out_shape=[
            jax.ShapeDtypeStruct((B, W, S), BF16),
            jax.ShapeDtypeStruct((B, G, S, LANES), BF16),
            jax.ShapeDtypeStruct((B, G, S, LANES), BF16),
            jax.ShapeDtypeStruct((B, G, S // CMP_STRIDE, chunk_w), BF16),
            jax.ShapeDtypeStruct((B, G, S // CMP_STRIDE, chunk_w), BF16),
            jax.ShapeDtypeStruct((B, S // tk, KV_WIDTH, tk), BF16),
            jax.ShapeDtypeStruct((B, S // tk, KV_WIDTH, tk), BF16),
            jax.ShapeDtypeStruct((B, W, S), BF16),
            jax.ShapeDtypeStruct((B, n_hg, GATE_ROWS, S), F32),
        ],
        scratch_shapes=[pltpu.VMEM((2 * KV_WIDTH // LANES, tm, LANES), F32)],
        compiler_params=_params("parallel", "parallel"),
    )(x, g, w_k, w_t, *lane_tables, *row_tables)


def _compress_kernel(chk_ref, chv_ref, w1k_ref, w2k_ref, pek_ref, w1v_ref, w2vt_ref, pev_ref,
                     cos_ref, slo_ref, shi_ref, kc_ref, vct_ref):
    half = CMP_STRIDE * HEAD_DIM

    def hidden(ch, w1_ref, pe_ref):
        a = _dot(ch, w1_ref[:half, :])
        b = _dot(ch, w1_ref[half:, :])
        b = pltpu.roll(b, b.shape[0] - 1, axis=0)
        pe = _dot(pe_ref[...], w1_ref[...])[0:1, :]
        return _silu(a + b + pe).astype(BF16)

    kc = _dot(hidden(chk_ref[0, 0], w1k_ref, pek_ref), w2k_ref[...])
    kc_ref[0, 0] = _rope_lanes(kc, cos_ref[...], slo_ref[...], shi_ref[...]).astype(BF16)
    vct_ref[0, 0] = _dot_nt(w2vt_ref[...], hidden(chv_ref[0, 0], w1v_ref, pev_ref)).astype(BF16)


def _compress(chk, chv, w1k, w2k, pek, w1v, w2vt, pev, tables):
    B, G, n_chunk, width = chk.shape
    const = lambda b, g: (0, 0)
    chunk_spec = pl.BlockSpec((1, 1, n_chunk, width), lambda b, g: (b, g, 0, 0))
    specs = lambda ws: [pl.BlockSpec(w.shape, const) for w in ws]
    return pl.pallas_call(
        _compress_kernel,
        name="nsa_compress",
        grid=(B, G),
        in_specs=[chunk_spec, chunk_spec] + specs((w1k, w2k, pek, w1v, w2vt, pev))
                 + [pl.BlockSpec((n_chunk, LANES), const)] * 3,
        out_specs=[pl.BlockSpec((1, 1, n_chunk, LANES), lambda b, g: (b, g, 0, 0)),
                   pl.BlockSpec((1, 1, HEAD_DIM, n_chunk), lambda b, g: (b, g, 0, 0))],
        out_shape=[jax.ShapeDtypeStruct((B, G, n_chunk, LANES), BF16),
                   jax.ShapeDtypeStruct((B, G, HEAD_DIM, n_chunk), BF16)],
        compiler_params=_params("parallel", "parallel"),
    )(chk, chv, w1k, w2k, pek, w1v, w2vt, pev, *tables)


def _local_attn_kernel(q_ref, kc_ref, vct_ref, ovt_ref, kw_ref, vw_ref, oc_ref, bias_ref, ow_ref,
                       *, t, n_cmp, n_blk, n_back):
    i = pl.program_id(2)
    heads = range(NSA_REP)
    q_aug = [_pad_rows(q_ref[0, r * HEAD_DIM:(r + 1) * HEAD_DIM, :], LANES) for r in heads]

    kc = kc_ref[0, 0]
    cmp_logits = [_dot(kc, q_aug[r]) for r in heads]
    above = _iota((t, t), 0) > _iota((t, t), 1)
    ones = jnp.ones((ONES_ROWS, t), BF16)
    tiles = []
    for d in range(n_back + 1):
        j = i - d
        jc = jnp.maximum(j, 0)
        k = kw_ref[0, 0, _tile(jc, t), :]
        if d == 0:
            ok = jnp.logical_not(above)
        elif d == n_back:
            ok = jnp.logical_and(above, j >= 0)
        else:
            ok = jnp.broadcast_to(j >= 0, (t, t))
        tiles.append((jnp.concatenate([vw_ref[0, jc], ones], axis=0), ok))
        for r in heads:
            tiles[-1] += (_dot(k, q_aug[r]),)

    vct = vct_ref[0, 0]
    n_pad = kc.shape[0]
    blk_c = _iota((n_pad, t), 0)
    qry = i * t + _iota((n_pad, t), 1)
    valid = jnp.logical_and(blk_c * CMP_STRIDE + (CMP_LEN - 1) <= qry, blk_c < n_cmp)
    p_sum = jnp.zeros((n_pad, t), F32)
    for r in heads:
        s = jnp.where(valid, cmp_logits[r], NEG)
        e = jnp.exp2(s - _col_max(s))
        p = jnp.where(valid, e * (1.0 / _col_sum(e)), 0.0)
        p_sum = p_sum + p
        oc_ref[0, r * HEAD_DIM:(r + 1) * HEAD_DIM, :] = _dot(vct, p.astype(BF16)).astype(oc_ref.dtype)

    ovt = ovt_ref[...]
    p_hi, p_lo = _split2(p_sum)
    imp = _dot(ovt, p_hi) + _dot(ovt, p_lo)
    blk = _iota(imp.shape, 0)
    cur = (i * t + _iota(imp.shape, 1)) // SEL_LEN
    forced = jnp.logical_or(blk == 0, jnp.logical_or(blk == cur, blk == cur - 1))
    imp = jnp.where(forced, FORCE, jnp.where(blk <= cur, imp, -1.0))
    keep = jnp.zeros(imp.shape, jnp.bool_)

    n_rounds = min(N_SELECT, n_blk)
    per_head = -(-n_rounds // NSA_REP)
    done = 0
    for r in heads:
        ss = [jnp.where(ok, logits[r], NEG) for _, ok, *logits in tiles]
        m = functools.reduce(jnp.maximum, [_col_max(s) for s in ss])
        acc = functools.reduce(jnp.add, [_dot(v, jnp.exp2(s - m).astype(BF16))
                                         for (v, *_), s in zip(tiles, ss)])
        ow_ref[0, r * HEAD_DIM:(r + 1) * HEAD_DIM, :] = _normalised(acc).astype(ow_ref.dtype)
        for _ in range(min(per_head, n_rounds - done)):
            top = _col_max(imp)
            first = jnp.min(jnp.where(imp == top, blk, n_blk), axis=0, keepdims=True)
            pick = blk == first
            keep = jnp.logical_or(keep, pick)
            imp = jnp.where(pick, -jnp.inf, imp)
            done += 1
    bias_ref[0, 0] = jnp.where(keep, 0.0, NEG).astype(bias_ref.dtype)


def _local_attn(qt, kc, vct, ovt, n_cmp, kw, vw):
    B, W, S = qt.shape
    G = kc.shape[1]
    t = min(ATTN_TILE, S)
    assert WINDOW % t == 0
    n_blk = ovt.shape[0]
    gw = W // G
    kern = functools.partial(_local_attn_kernel, t=t, n_cmp=n_cmp, n_blk=n_blk, n_back=WINDOW // t)
    feat = pl.BlockSpec((1, gw, t), lambda b, g, i: (b, g, i))
    return pl.pallas_call(
        kern,
        name="nsa_local_attn",
        grid=(B, G, S // t),
        in_specs=[
            feat,
            pl.BlockSpec((1, 1) + kc.shape[2:], lambda b, g, i: (b, g, 0, 0)),
            pl.BlockSpec((1, 1) + vct.shape[2:], lambda b, g, i: (b, g, 0, 0)),
            pl.BlockSpec(ovt.shape, lambda b, g, i: (0, 0)),
            pl.BlockSpec((1, 1, S, LANES), lambda b, g, i: (b, g, 0, 0)),
            pl.BlockSpec((1, S // t, HEAD_DIM, t), lambda b, g, i: (b, 0, g, 0)),
        ],
        out_specs=[feat, pl.BlockSpec((1, 1, n_blk, t), lambda b, g, i: (b, g, 0, i)), feat],
        out_shape=[
            jax.ShapeDtypeStruct((B, W, S), BF16),
            jax.ShapeDtypeStruct((B, G, n_blk, S), BF16),
            jax.ShapeDtypeStruct((B, W, S), BF16),
        ],
        compiler_params=_params("parallel", "parallel", "parallel"),
    )(qt, kc, vct, ovt, kw, vw)


def _sel_attn_kernel(q_ref, bias_ref, k_ref, v_ref, oc_ref, ow_ref, gt_ref, z_ref, o_ref,
                     q_scr, s_scr, acc_scr, *, t):
    nh = q_scr.shape[1]
    n_q = q_ref.shape[2] // t

    def load_q(i, slot):
        cols = slice(i * t, (i + 1) * t)
        for hh in range(nh):
            bias = _pad_rows(bias_ref[0, hh // NSA_REP, :, cols], HEAD_DIM)
            q_scr[slot, hh] = jnp.concatenate(
                [q_ref[0, hh * HEAD_DIM:(hh + 1) * HEAD_DIM, cols], bias], axis=0)

    def key_tile(j, hh):
        return k_ref[0, hh // NSA_REP, _tile(j, t), :]

    def value_rows(j, hh):
        g = hh // NSA_REP
        return v_ref[0, j, g * HEAD_DIM:(g + 1) * HEAD_DIM, :]

    def emit(i, accs):
        cols = slice(i * t, (i + 1) * t)
        for hh in range(nh):
            rows = slice(hh * HEAD_DIM, (hh + 1) * HEAD_DIM)
            gate = lambda b: gt_ref[0, 0, N_BRANCH * hh + b:N_BRANCH * hh + b + 1, cols]
            o = (gate(0) * oc_ref[0, rows, cols].astype(F32) + gate(1) * _normalised(accs[hh])
                 + gate(2) * ow_ref[0, rows, cols].astype(F32))
            o_ref[0, rows, cols] = (o * _silu(z_ref[0, rows, cols].astype(F32))).astype(o_ref.dtype)

    _causal_attention(n_q, t, load_q, key_tile, value_rows, emit, q_scr, s_scr, acc_scr)


def _sel_attn(qt, bias, k_aug, vt, oc, ow, gt, zt):
    B, W, S = qt.shape
    t = min(ATTN_TILE, S)
    ng = HEADS_PER_STEP // NSA_REP
    n_blk = bias.shape[2]
    feat = pl.BlockSpec((1, HEADS_PER_STEP * HEAD_DIM, S), lambda b, g: (b, g, 0))
    return pl.pallas_call(
        functools.partial(_sel_attn_kernel, t=t),
        name="nsa_sel_attn",
        grid=(B, NSA_GROUPS // ng),
        in_specs=[
            feat,
            pl.BlockSpec((1, ng, n_blk, S), lambda b, g: (b, g, 0, 0)),
            pl.BlockSpec((1, ng, S, LANES), lambda b, g: (b, g, 0, 0)),
            pl.BlockSpec((1, S // t, ng * HEAD_DIM, t), lambda b, g: (b, 0, g, 0)),
            feat, feat,
            pl.BlockSpec((1, 1, GATE_ROWS, S), lambda b, g: (b, g, 0, 0)),
            feat,
        ],
        out_specs=feat,
        out_shape=jax.ShapeDtypeStruct((B, W, S), BF16),
        scratch_shapes=_attn_scratch(HEADS_PER_STEP, LANES, t),
        compiler_params=_params("parallel", "parallel"),
    )(qt, bias, k_aug, vt, oc, ow, gt, zt)


def _fox_layer(x, g, w_in, b_f, w_out):
    B, S, _ = x.shape
    W, H = ATTN_WIDTH, N_HEADS
    w_k = w_in[:, W:2 * W].astype(BF16)
    w_t = jnp.concatenate([w_in[:, :W] * Q_SCALE, w_in[:, 2 * W:3 * W], w_in[:, 3 * W + H:]],
                          axis=1).T.astype(BF16)
    wf_hi, wf_lo = _split2(w_in[:, 3 * W:3 * W + H])
    pad = lambda n: jnp.zeros((D_MODEL, LANES - n * H), BF16)
    wf_a = jnp.concatenate([wf_hi] * 3 + [wf_lo] * 3 + [pad(6)], axis=1)
    wf_b = jnp.concatenate([wf_hi] * 3 + [pad(3)], axis=1)
    bf_row = jnp.concatenate([b_f, b_f, b_f, jnp.zeros((LANES - 3 * H,), F32)])[None, :]

    qt, k, vt, zt, cx = _fox_in(x, g[None, :], w_k, w_t, wf_a, wf_b, bf_row)

    t = min(ATTN_TILE, S)
    r = np.arange(LANES)
    sel = np.logical_and(r[None, :] % H == np.arange(H)[:, None], r[None, :] < 3 * H)
    qx = np.broadcast_to(np.where(sel, -1.0, 0.0).astype(np.float32)[:, :, None], (H, LANES, t))
    ot = _fox_attn(qt, jnp.asarray(qx, BF16), k, cx, vt, zt)
    return _out_proj(ot, x, w_out.astype(BF16))


def _nsa_layer(x, g, w_in, pe_k, w_ck1, w_ck2, pe_v, w_cv1, w_cv2, w_out, final_g):
    B, S, _ = x.shape
    W, KV, G = ATTN_WIDTH, KV_WIDTH, NSA_GROUPS
    sec = lambda j: w_in[:, W + j * KV:W + (j + 1) * KV]
    gate_off = W + 6 * KV
    n_gate = N_BRANCH * N_HEADS
    w_k = jnp.concatenate([sec(2), sec(4), sec(0), sec(1)], axis=1).astype(BF16)
    w_t = jnp.concatenate([w_in[:, :W] * Q_SCALE, sec(3), sec(5), w_in[:, gate_off + n_gate:],
                           w_in[:, gate_off:gate_off + n_gate]], axis=1).T.astype(BF16)
    pos = np.arange(S)
    cos, sin = _rope_angles(pos)
    qt, ksel, kwin, chk, chv, vsel, vwin, zt, gt = _nsa_in(
        x, g[None, :], w_k, w_t, _rope_lane_tables(pos),
        (np.ascontiguousarray(cos.T), np.ascontiguousarray(sin.T)))

    n_chunk = S // CMP_STRIDE
    n_cmp = n_chunk - CMP_LEN // CMP_STRIDE + 1

    def flat_pe(pe):
        pe = pe.reshape(1, CMP_LEN * HEAD_DIM)
        return jnp.broadcast_to(pe, (SUBLANES, CMP_LEN * HEAD_DIM)).astype(BF16)

    w2k = jnp.concatenate([w_ck2, jnp.zeros_like(w_ck2)], axis=1).astype(BF16)
    cmp_end = np.arange(n_chunk) * CMP_STRIDE + CMP_LEN - 1
    kc, vct = _compress(chk, chv, w_ck1.astype(BF16), w2k, flat_pe(pe_k),
                        w_cv1.astype(BF16), w_cv2.T.astype(BF16), flat_pe(pe_v),
                        _rope_lane_tables(cmp_end))

    n_blk = S // SEL_LEN
    ci = np.arange(n_chunk) * CMP_STRIDE
    sj = np.arange(n_blk) * SEL_LEN
    ovt = np.logical_and(ci[None, :] < sj[:, None] + SEL_LEN, ci[None, :] + CMP_LEN > sj[:, None])
    ovt = np.logical_and(ovt, np.arange(n_chunk)[None, :] < n_cmp).astype(np.float32)
    oc, bias, ow = _local_attn(qt, kc, vct, jnp.asarray(ovt, BF16), n_cmp, kwin, vwin)
    ot = _sel_attn(qt, bias, ksel, vsel, oc, ow, gt, zt)
    return _out_proj(ot, x, w_out.astype(BF16), final_g[None, :])


def kernel(x, norm_g, fox_w_in, fox_b_f, fox_w_out, nsa_w_in, nsa_pe_k, nsa_w_ck1, nsa_w_ck2,
           nsa_pe_v, nsa_w_cv1, nsa_w_cv2, nsa_w_out, final_g):
    x = _fox_layer(x, norm_g[0], fox_w_in[0], fox_b_f[0], fox_w_out[0])
    return _nsa_layer(x, norm_g[1], nsa_w_in[0], nsa_pe_k[0], nsa_w_ck1[0], nsa_w_ck2[0],
                      nsa_pe_v[0], nsa_w_cv1[0], nsa_w_cv2[0], nsa_w_out[0], final_g)
```

```python
import functools
import math

import jax
import jax.numpy as jnp
import numpy as np
from jax import lax
from jax.experimental import pallas as pl
from jax.experimental.pallas import tpu as pltpu

F32 = jnp.float32
BF16 = jnp.bfloat16

D_MODEL = 1024
N_HEADS = 16
HEAD_DIM = 64
ATTN_WIDTH = N_HEADS * HEAD_DIM
NSA_GROUPS = 4
NSA_REP = N_HEADS // NSA_GROUPS
KV_WIDTH = NSA_GROUPS * HEAD_DIM
CMP_LEN = 32
CMP_STRIDE = 16
CMP_HIDDEN = 256
SEL_LEN = 64
N_SELECT = 8
WINDOW = 512
N_BRANCH = 3
ROPE_THETA = 500000.0
ROPE_DIM = HEAD_DIM // 4
ROPE_HALF = ROPE_DIM // 2
NORM_EPS = 1e-6
NEG = -1e30
FORCE = 1e6
LOG2E = math.log2(math.e)
Q_SCALE = HEAD_DIM ** -0.5 * LOG2E

LANES = 128
SUBLANES = 8
ONES_ROWS = 2 * SUBLANES
PROJ_ROWS = 512
ATTN_TILE = 256
HEADS_PER_STEP = 8
GATE_ROWS = -(-N_BRANCH * HEADS_PER_STEP // SUBLANES) * SUBLANES
VMEM_LIMIT = 56 * 1024 * 1024


def _params(*sem):
    return pltpu.CompilerParams(dimension_semantics=sem, vmem_limit_bytes=VMEM_LIMIT)


def _iota(shape, dim):
    return lax.broadcasted_iota(jnp.int32, shape, dim)


def _split2(x):
    hi = x.astype(BF16)
    lo = (x - hi.astype(F32)).astype(BF16)
    return hi, lo


def _split3(x):
    hi = x.astype(BF16)
    r1 = x - hi.astype(F32)
    mid = r1.astype(BF16)
    lo = (r1 - mid.astype(F32)).astype(BF16)
    return hi, mid, lo


def _dot(a, b):
    return jnp.dot(a, b, preferred_element_type=F32)


def _dot_nt(a, b):
    return lax.dot_general(a, b, (((1,), (1,)), ((), ())), preferred_element_type=F32)


def _dot_tn(a, b):
    return lax.dot_general(a, b, (((0,), (0,)), ((), ())), preferred_element_type=F32)


def _rmsnorm(x, g):
    ms = jnp.mean(x * x, axis=-1, keepdims=True)
    return x * lax.rsqrt(ms + NORM_EPS) * g


def _silu(x):
    return x * (1.0 / (1.0 + jnp.exp(-x)))


def _col_max(x):
    return jnp.max(x, axis=0, keepdims=True)


def _col_sum(x):
    return jnp.sum(x, axis=0, keepdims=True)


def _pad_rows(x, rows):
    if rows == x.shape[0]:
        return x
    return jnp.concatenate([x, jnp.zeros((rows - x.shape[0], x.shape[1]), x.dtype)], axis=0)


def _tile(j, t):
    return pl.ds(j * t if isinstance(j, int) else pl.multiple_of(j * t, t), t)


def _normalised(acc):
    return acc[:HEAD_DIM] * (1.0 / acc[HEAD_DIM:HEAD_DIM + 1])


def _rope_lanes(x, cos, sin_lo, sin_hi):
    return (x * cos + pltpu.roll(x, LANES - ROPE_HALF, axis=1) * sin_lo
            + pltpu.roll(x, ROPE_HALF, axis=1) * sin_hi)


def _rope_rows(x, cos, sin):
    x1, x2 = x[:ROPE_HALF], x[ROPE_HALF:ROPE_DIM]
    return jnp.concatenate([x1 * cos - x2 * sin, x1 * sin + x2 * cos, x[ROPE_DIM:]], axis=0)


def _rope_angles(pos):
    inv_freq = np.power(np.float32(ROPE_THETA),
                        -np.arange(ROPE_HALF, dtype=np.float32) * np.float32(2.0 / ROPE_DIM))
    ang = pos.astype(np.float32)[:, None] * inv_freq[None, :].astype(np.float32)
    return np.cos(ang).astype(np.float32), np.sin(ang).astype(np.float32)


def _rope_lane_tables(pos):
    cos, sin = _rope_angles(pos)
    n = pos.shape[0]
    ones = np.ones((n, HEAD_DIM - ROPE_DIM), np.float32)
    zeros = np.zeros((n, HEAD_DIM - ROPE_DIM), np.float32)
    z8 = np.zeros((n, ROPE_HALF), np.float32)
    c = np.concatenate([cos, cos, ones], axis=1)
    s_lo = np.concatenate([-sin, z8, zeros], axis=1)
    s_hi = np.concatenate([z8, sin, zeros], axis=1)
    tile = lambda t: np.concatenate([t, t], axis=1)
    return tile(c), tile(s_lo), tile(s_hi)


def _fox_in_kernel(x_ref, g_ref, wk_ref, wt_ref, wfa_ref, wfb_ref, bf_ref, tri_ref,
                   qt_ref, k_ref, vt_ref, zt_ref, cx_ref, carry_ref, *, tk):
    s = pl.program_id(1)
    h = _rmsnorm(x_ref[0], g_ref[...])
    hb = h.astype(BF16)
    tm = hb.shape[0]
    W, H = ATTN_WIDTH, N_HEADS

    @pl.when(s == 0)
    def _():
        carry_ref[...] = jnp.zeros_like(carry_ref)

    h_lo = (h - hb.astype(F32)).astype(BF16)
    fa = _dot(hb, wfa_ref[...])
    f = fa + pltpu.roll(fa, LANES - 3 * H, axis=1) + _dot(h_lo, wfb_ref[...]) + bf_ref[...]

    for n0 in range(0, W, 512):
        k_ref[0, :, n0:n0 + 512] = _dot(hb, wk_ref[:, n0:n0 + 512]).astype(BF16)

    log_f = jnp.minimum(f, 0.0) - jnp.log1p(jnp.exp(-jnp.abs(f)))
    lane = _iota(log_f.shape, 1)
    a0, a1, a2 = _split3(log_f)
    pieces = jnp.where(lane < H, a0, jnp.where(lane < 2 * H, a1, a2))
    c = _dot(tri_ref[...], pieces)

    for r0 in range(0, 3 * W, 512):
        t = _dot_nt(wt_ref[r0:r0 + 512, :], hb).astype(BF16)
        if r0 < W:
            qt_ref[0, r0:r0 + 512, :] = t
        elif r0 < 2 * W:
            for ct in range(tm // tk):
                vt_ref[0, ct, r0 - W:r0 - W + 512, :] = t[:, ct * tk:(ct + 1) * tk]
        else:
            zt_ref[0, r0 - 2 * W:r0 - 2 * W + 512, :] = t

    c = c + pltpu.roll(c, LANES - H, axis=1) + pltpu.roll(c, LANES - 2 * H, axis=1)
    c = c + carry_ref[0:1, :]
    carry_ref[...] = jnp.broadcast_to(c[tm - 1:tm, :], carry_ref.shape)
    c = jnp.where(lane < H, c, jnp.where(lane < 2 * H, pltpu.roll(c, H, axis=1),
                                         pltpu.roll(c, 2 * H, axis=1)))

    hi, mid, lo = _split3(c * LOG2E)
    zero = jnp.zeros_like(hi)
    cx_ref[0] = jnp.where(lane < H, hi,
                          jnp.where(lane < 2 * H, mid, jnp.where(lane < 3 * H, lo, zero)))


def _fox_in(x, g, w_k, w_t, wf_a, wf_b, bf_row):
    B, S, _ = x.shape
    tm = min(PROJ_ROWS, S)
    tk = min(ATTN_TILE, S)
    W = ATTN_WIDTH
    tri = np.tril(np.ones((tm, tm), np.float32)).astype(BF16)
    const = lambda b, s: (0, 0)
    feat = pl.BlockSpec((1, W, tm), lambda b, s: (b, 0, s))
    return pl.pallas_call(
        functools.partial(_fox_in_kernel, tk=tk),
        name="fox_in",
        grid=(B, S // tm),
        in_specs=[
            pl.BlockSpec((1, tm, D_MODEL), lambda b, s: (b, s, 0)),
            pl.BlockSpec((1, D_MODEL), const),
            pl.BlockSpec(w_k.shape, const),
            pl.BlockSpec(w_t.shape, const),
            pl.BlockSpec((D_MODEL, LANES), const),
            pl.BlockSpec((D_MODEL, LANES), const),
            pl.BlockSpec((1, LANES), const),
            pl.BlockSpec((tm, tm), const),
        ],
        out_specs=[
            feat,
            pl.BlockSpec((1, tm, W), lambda b, s: (b, s, 0)),
            pl.BlockSpec((1, tm // tk, W, tk), lambda b, s: (b, s, 0, 0)),
            feat,
            pl.BlockSpec((1, tm, LANES), lambda b, s: (b, s, 0)),
        ],
        out_shape=[
            jax.ShapeDtypeStruct((B, W, S), BF16),
            jax.ShapeDtypeStruct((B, S, W), BF16),
            jax.ShapeDtypeStruct((B, S // tk, W, tk), BF16),
            jax.ShapeDtypeStruct((B, W, S), BF16),
            jax.ShapeDtypeStruct((B, S, LANES), BF16),
        ],
        scratch_shapes=[pltpu.VMEM((SUBLANES, LANES), F32)],
        compiler_params=_params("parallel", "arbitrary"),
    )(x, g, w_k, w_t, wf_a, wf_b, bf_row, tri)


def _causal_attention(n_q, t, load_q, key_tile, value_rows, emit, q_scr, s_scr, acc_scr):
    heads = range(q_scr.shape[1])
    ones = jnp.ones((ONES_ROWS, t), BF16)
    causal = _iota((t, t), 0) <= _iota((t, t), 1)

    def logits(slot, j):
        return [_dot(key_tile(j, hh), q_scr[slot, hh]) for hh in heads]

    def pv(j, hh, p):
        return _dot(jnp.concatenate([value_rows(j, hh), ones], axis=0), p.astype(BF16))

    def finish(i, slot, carry):
        accs = []
        for hh in heads:
            m, alpha = carry[hh]
            accs.append(alpha * acc_scr[hh] + pv(i - 1, hh, jnp.exp2(s_scr[slot, hh] - m)))
        emit(i, accs)

    pending = None
    for i in range(n_q):
        slot = i % 2
        load_q(i, slot)
        s_diag = logits(slot, i)
        s_first = logits(slot, 0) if i else None
        if pending is not None:
            finish(*pending)
        m_diag = []
        for hh in heads:
            s = jnp.where(causal, s_diag[hh], NEG)
            m = _col_max(s)
            acc_scr[hh] = pv(i, hh, jnp.exp2(s - m))
            m_diag.append(m)
        if i == 0:
            emit(0, [acc_scr[hh] for hh in heads])
            continue
        carry = []
        for hh in heads:
            s_scr[slot, hh] = s_first[hh]
            m_new = jnp.maximum(m_diag[hh], _col_max(s_first[hh]))
            carry.append((m_new, jnp.exp2(m_diag[hh] - m_new)))

        def body(j, carry, slot=slot):
            s_next = logits(slot, j + 1)
            pvs = [pv(j, hh, jnp.exp2(s_scr[slot, hh] - carry[hh][0])) for hh in heads]
            out = []
            for hh in heads:
                m = carry[hh][0]
                s_scr[slot, hh] = s_next[hh]
                m_new = jnp.maximum(m, _col_max(s_next[hh]))
                out.append((m_new, jnp.exp2(m - m_new)))
            for hh in heads:
                acc_scr[hh] = carry[hh][1] * acc_scr[hh] + pvs[hh]
            return tuple(out)

        carry = lax.fori_loop(0, i - 1, body, tuple(carry))
        pending = (i, slot, carry)
    if pending is not None:
        finish(*pending)


def _attn_scratch(nh, k_depth, t):
    return [pltpu.VMEM((2, nh, k_depth, t), BF16), pltpu.VMEM((2, nh, t, t), F32),
            pltpu.VMEM((nh, HEAD_DIM + ONES_ROWS, t), F32)]


def _fox_attn_kernel(q_ref, qx_ref, k_ref, kx_ref, v_ref, z_ref, o_ref, q_scr, s_scr, acc_scr, *, t):
    nh = q_scr.shape[1]
    n_q = q_ref.shape[2] // t
    row = _iota((LANES, t), 0)

    def load_q(i, slot):
        for hh in range(nh):
            pair, sub = divmod(hh, 2)
            qt = q_ref[0, pair * LANES:(pair + 1) * LANES, i * t:(i + 1) * t]
            own = jnp.logical_and(row >= sub * HEAD_DIM, row < (sub + 1) * HEAD_DIM)
            q_scr[slot, hh] = jnp.concatenate(
                [jnp.where(own, qt, jnp.zeros_like(qt)), qx_ref[hh]], axis=0)

    def key_tile(j, hh):
        pair = hh // 2
        return jnp.concatenate([k_ref[0, _tile(j, t), pair * LANES:(pair + 1) * LANES],
                                kx_ref[0, _tile(j, t), :]], axis=1)

    def value_rows(j, hh):
        return v_ref[0, j, hh * HEAD_DIM:(hh + 1) * HEAD_DIM, :]

    def emit(i, accs):
        for hh in range(nh):
            rows, cols = slice(hh * HEAD_DIM, (hh + 1) * HEAD_DIM), slice(i * t, (i + 1) * t)
            gated = _normalised(accs[hh]) * _silu(z_ref[0, rows, cols].astype(F32))
            o_ref[0, rows, cols] = gated.astype(o_ref.dtype)

    _causal_attention(n_q, t, load_q, key_tile, value_rows, emit, q_scr, s_scr, acc_scr)


def _fox_attn(qt, qx, k, kx, vt, zt):
    B, W, S = qt.shape
    t = min(ATTN_TILE, S)
    nh = HEADS_PER_STEP
    rows = nh * HEAD_DIM
    feat = pl.BlockSpec((1, rows, S), lambda b, p: (b, p, 0))
    return pl.pallas_call(
        functools.partial(_fox_attn_kernel, t=t),
        name="fox_attn",
        grid=(B, N_HEADS // nh),
        in_specs=[
            feat,
            pl.BlockSpec((nh, LANES, t), lambda b, p: (p, 0, 0)),
            pl.BlockSpec((1, S, rows), lambda b, p: (b, 0, p)),
            pl.BlockSpec((1, S, LANES), lambda b, p: (b, 0, 0)),
            pl.BlockSpec((1, S // t, rows, t), lambda b, p: (b, 0, p, 0)),
            feat,
        ],
        out_specs=feat,
        out_shape=jax.ShapeDtypeStruct((B, W, S), BF16),
        scratch_shapes=_attn_scratch(nh, 2 * LANES, t),
        compiler_params=_params("parallel", "parallel"),
    )(qt, qx, k, kx, vt, zt)


def _out_proj_kernel(*refs, final_norm):
    if final_norm:
        ot_ref, x_ref, w_ref, g_ref, y_ref = refs
    else:
        ot_ref, x_ref, w_ref, y_ref = refs
    y = x_ref[0] + _dot_tn(ot_ref[0], w_ref[...])
    y_ref[0] = _rmsnorm(y, g_ref[...]) if final_norm else y


def _out_proj(ot, x, w_out, final_g=None):
    B, S, _ = x.shape
    tm = min(PROJ_ROWS, S)
    const = lambda b, s: (0, 0)
    tok = pl.BlockSpec((1, tm, D_MODEL), lambda b, s: (b, s, 0))
    in_specs = [pl.BlockSpec((1, ATTN_WIDTH, tm), lambda b, s: (b, 0, s)), tok,
                pl.BlockSpec((ATTN_WIDTH, D_MODEL), const)]
    args = [ot, x, w_out]
    if final_g is not None:
        in_specs.append(pl.BlockSpec((1, D_MODEL), const))
        args.append(final_g)
    return pl.pallas_call(
        functools.partial(_out_proj_kernel, final_norm=final_g is not None),
        name="out_proj",
        grid=(B, S // tm),
        in_specs=in_specs,
        out_specs=tok,
        out_shape=jax.ShapeDtypeStruct((B, S, D_MODEL), F32),
        compiler_params=_params("parallel", "parallel"),
    )(*args)


_TOK_SEL = 0
_TOK_WIN = KV_WIDTH
_TOK_CMP = 2 * KV_WIDTH
_TOK_COLS = 4 * KV_WIDTH
_FEAT_Q = 0
_FEAT_VSEL = ATTN_WIDTH
_FEAT_VWIN = _FEAT_VSEL + KV_WIDTH
_FEAT_Z = _FEAT_VWIN + KV_WIDTH
_FEAT_GATE = _FEAT_Z + ATTN_WIDTH
_FEAT_ROWS = _FEAT_GATE + N_BRANCH * N_HEADS


def _nsa_in_kernel(x_ref, g_ref, wk_ref, wt_ref, cos_ref, slo_ref, shi_ref, cost_ref, sint_ref,
                   qt_ref, ksel_ref, kwin_ref, chk_ref, chv_ref, vsel_ref, vwin_ref, zt_ref, gt_ref,
                   cmp_scr, *, tk):
    s = pl.program_id(1)
    hb = _rmsnorm(x_ref[0], g_ref[...]).astype(BF16)
    tm = hb.shape[0]
    G = NSA_GROUPS

    cos, slo, shi = cos_ref[...], slo_ref[...], shi_ref[...]
    tok = s * tm + _iota((tm, LANES), 0)
    lane = _iota((tm, LANES), 1)
    low = lane < HEAD_DIM
    block_id = jnp.where(lane - HEAD_DIM == tok // SEL_LEN, 1.0, 0.0)
    keys = _dot(hb, wk_ref[:, _TOK_SEL:_TOK_CMP])
    for col0, out_ref, extra in ((_TOK_SEL, ksel_ref, block_id), (_TOK_WIN, kwin_ref, 0.0)):
        for gp in range(G // 2):
            pair = _rope_lanes(keys[:, col0 + gp * LANES:col0 + (gp + 1) * LANES], cos, slo, shi)
            out_ref[0, 2 * gp] = jnp.where(low, pair, extra).astype(BF16)
            out_ref[0, 2 * gp + 1] = jnp.where(
                low, pltpu.roll(pair, HEAD_DIM, axis=1), extra).astype(BF16)

    raw = _dot(hb, wk_ref[:, _TOK_CMP:_TOK_COLS])
    for c in range(cmp_scr.shape[0]):
        cmp_scr[c] = raw[:, c * LANES:(c + 1) * LANES]

    cost, sint = cost_ref[...], sint_ref[...]
    for r0 in range(0, ATTN_WIDTH, 512):
        t = _dot_nt(wt_ref[_FEAT_Q + r0:_FEAT_Q + r0 + 512, :], hb)
        for h0 in range(0, 512, HEAD_DIM):
            qt_ref[0, r0 + h0:r0 + h0 + HEAD_DIM, :] = _rope_rows(
                t[h0:h0 + HEAD_DIM], cost, sint).astype(BF16)
        zt_ref[0, r0:r0 + 512, :] = _dot_nt(wt_ref[_FEAT_Z + r0:_FEAT_Z + r0 + 512, :], hb).astype(BF16)
    for row0, out_ref in ((_FEAT_VSEL, vsel_ref), (_FEAT_VWIN, vwin_ref)):
        t = _dot_nt(wt_ref[row0:row0 + KV_WIDTH, :], hb).astype(BF16)
        for c in range(tm // tk):
            out_ref[0, c] = t[:, c * tk:(c + 1) * tk]
    gate = _dot_nt(wt_ref[_FEAT_GATE:_FEAT_ROWS, :], hb)
    gate = 1.0 / (1.0 + jnp.exp(-gate))
    rows = N_BRANCH * HEADS_PER_STEP
    for hg in range(N_HEADS // HEADS_PER_STEP):
        gt_ref[0, hg] = _pad_rows(gate[hg * rows:(hg + 1) * rows], GATE_ROWS)

    n_ch = tm // CMP_STRIDE
    low_ch = _iota((n_ch, LANES), 1) < HEAD_DIM
    for kind, out_ref in enumerate((chk_ref, chv_ref)):
        for gp in range(G // 2):
            c = kind * (G // 2) + gp
            for l in range(0, CMP_STRIDE, 2):
                a0 = cmp_scr[c, pl.ds(l, n_ch, stride=CMP_STRIDE), :]
                a1 = cmp_scr[c, pl.ds(l + 1, n_ch, stride=CMP_STRIDE), :]
                dst = slice(l * HEAD_DIM, (l + 2) * HEAD_DIM)
                out_ref[0, 2 * gp, :, dst] = jnp.where(
                    low_ch, a0, pltpu.roll(a1, HEAD_DIM, axis=1)).astype(BF16)
                out_ref[0, 2 * gp + 1, :, dst] = jnp.where(
                    low_ch, pltpu.roll(a0, HEAD_DIM, axis=1), a1).astype(BF16)


def _nsa_in(x, g, w_k, w_t, lane_tables, row_tables):
    B, S, _ = x.shape
    tm = min(PROJ_ROWS, S)
    tk = min(ATTN_TILE, S)
    W, G = ATTN_WIDTH, NSA_GROUPS
    const = lambda b, s: (0, 0)
    feat = lambda rows: pl.BlockSpec((1, rows, tm), lambda b, s: (b, 0, s))
    tiles = pl.BlockSpec((1, tm // tk, KV_WIDTH, tk), lambda b, s: (b, s, 0, 0))
    keys = pl.BlockSpec((1, G, tm, LANES), lambda b, s: (b, 0, s, 0))
    chunk_w = CMP_STRIDE * HEAD_DIM
    chunk = pl.BlockSpec((1, G, tm // CMP_STRIDE, chunk_w), lambda b, s: (b, 0, s, 0))
    n_hg = N_HEADS // HEADS_PER_STEP
    return pl.pallas_call(
        functools.partial(_nsa_in_kernel, tk=tk),
        name="nsa_in",
        grid=(B, S // tm),
        in_specs=[
            pl.BlockSpec((1, tm, D_MODEL), lambda b, s: (b, s, 0)),
            pl.BlockSpec((1, D_MODEL), const),
            pl.BlockSpec(w_k.shape, const),
            pl.BlockSpec(w_t.shape, const),
        ] + [pl.BlockSpec((tm, LANES), lambda b, s: (s, 0))] * 3
          + [pl.BlockSpec((ROPE_HALF, tm), lambda b, s: (0, s))] * 2,
        out_specs=[feat(W), keys, keys, chunk, chunk, tiles, tiles, feat(W),
                   pl.BlockSpec((1, n_hg, GATE_ROWS, tm), lambda b, s: (b, 0, 0, s))],
        out_shape=[
            jax.ShapeDtypeStruct((B, W, S), BF16),
            jax.ShapeDtypeStruct((B, G, S, LANES), BF16),
            jax.ShapeDtypeStruct((B, G, S, LANES), BF16),
            jax.ShapeDtypeStruct((B, G, S // CMP_STRIDE, chunk_w), BF16),
            jax.ShapeDtypeStruct((B, G, S // CMP_STRIDE, chunk_w), BF16),
            jax.ShapeDtypeStruct((B, S // tk, KV_WIDTH, tk), BF16),
            jax.ShapeDtypeStruct((B, S // tk, KV_WIDTH, tk), BF16),
            jax.ShapeDtypeStruct((B, W, S), BF16),
            jax.ShapeDtypeStruct((B, n_hg, GATE_ROWS, S), F32),
        ],
        scratch_shapes=[pltpu.VMEM((2 * KV_WIDTH // LANES, tm, LANES), F32)],
        compiler_params=_params("parallel", "parallel"),
    )(x, g, w_k, w_t, *lane_tables, *row_tables)


def _compress_kernel(chk_ref, chv_ref, w1k_ref, w2k_ref, pek_ref, w1v_ref, w2vt_ref, pev_ref,
                     cos_ref, slo_ref, shi_ref, kc_ref, vct_ref):
    half = CMP_STRIDE * HEAD_DIM

    def hidden(ch, w1_ref, pe_ref):
        a = _dot(ch, w1_ref[:half, :])
        b = _dot(ch, w1_ref[half:, :])
        b = pltpu.roll(b, b.shape[0] - 1, axis=0)
        pe = _dot(pe_ref[...], w1_ref[...])[0:1, :]
        return _silu(a + b + pe).astype(BF16)

    kc = _dot(hidden(chk_ref[0, 0], w1k_ref, pek_ref), w2k_ref[...])
    kc_ref[0, 0] = _rope_lanes(kc, cos_ref[...], slo_ref[...], shi_ref[...]).astype(BF16)
    vct_ref[0, 0] = _dot_nt(w2vt_ref[...], hidden(chv_ref[0, 0], w1v_ref, pev_ref)).astype(BF16)


def _compress(chk, chv, w1k, w2k, pek, w1v, w2vt, pev, tables):
    B, G, n_chunk, width = chk.shape
    const = lambda b, g: (0, 0)
    chunk_spec = pl.BlockSpec((1, 1, n_chunk, width), lambda b, g: (b, g, 0, 0))
    specs = lambda ws: [pl.BlockSpec(w.shape, const) for w in ws]
    return pl.pallas_call(
        _compress_kernel,
        name="nsa_compress",
        grid=(B, G),
        in_specs=[chunk_spec, chunk_spec] + specs((w1k, w2k, pek, w1v, w2vt, pev))
                 + [pl.BlockSpec((n_chunk, LANES), const)] * 3,
        out_specs=[pl.BlockSpec((1, 1, n_chunk, LANES), lambda b, g: (b, g, 0, 0)),
                   pl.BlockSpec((1, 1, HEAD_DIM, n_chunk), lambda b, g: (b, g, 0, 0))],
        out_shape=[jax.ShapeDtypeStruct((B, G, n_chunk, LANES), BF16),
                   jax.ShapeDtypeStruct((B, G, HEAD_DIM, n_chunk), BF16)],
        compiler_params=_params("parallel", "parallel"),
    )(chk, chv, w1k, w2k, pek, w1v, w2vt, pev, *tables)


def _local_attn_kernel(q_ref, kc_ref, vct_ref, ovt_ref, kw_ref, vw_ref, oc_ref, bias_ref, ow_ref,
                       *, t, n_cmp, n_blk, n_back):
    n_q = q_ref.shape[2] // t
    heads = range(NSA_REP)
    kc = kc_ref[0, 0]
    vct = vct_ref[0, 0]
    ovt = ovt_ref[...]
    n_pad = kc.shape[0]
    above = _iota((t, t), 0) > _iota((t, t), 1)
    ones = jnp.ones((ONES_ROWS, t), BF16)
    blk_c = _iota((n_pad, t), 0)
    blk = _iota((n_blk, t), 0)
    n_rounds = min(N_SELECT, n_blk)
    per_head = -(-n_rounds // NSA_REP)

    def q_aug(i, r):
        return _pad_rows(q_ref[0, r * HEAD_DIM:(r + 1) * HEAD_DIM, i * t:(i + 1) * t], LANES)

    masks = {0: jnp.logical_not(above), n_back: above}

    def win_tiles(i):
        return [(i - d, masks.get(d)) for d in range(n_back + 1) if i - d >= 0]

    def win_logits(i, r):
        return [_dot(kw_ref[0, 0, _tile(j, t), :], q_aug(i, r)) for j, _ in win_tiles(i)]

    def cmp_logits(i):
        return [_dot(kc, q_aug(i, r)) for r in heads]

    def importance(i, logits):
        cols = slice(i * t, (i + 1) * t)
        qry = i * t + _iota((n_pad, t), 1)
        valid = jnp.logical_and(blk_c * CMP_STRIDE + (CMP_LEN - 1) <= qry, blk_c < n_cmp)
        p_sum = jnp.zeros((n_pad, t), F32)
        for r in heads:
            s = jnp.where(valid, logits[r], NEG)
            e = jnp.exp2(s - _col_max(s))
            p = jnp.where(valid, e * (1.0 / _col_sum(e)), 0.0)
            p_sum = p_sum + p
            oc_ref[0, r * HEAD_DIM:(r + 1) * HEAD_DIM, cols] = _dot(
                vct, p.astype(BF16)).astype(oc_ref.dtype)
        p_hi, p_lo = _split2(p_sum)
        imp = _dot(ovt, p_hi) + _dot(ovt, p_lo)
        cur = (i * t + _iota(imp.shape, 1)) // SEL_LEN
        forced = jnp.logical_or(blk == 0, jnp.logical_or(blk == cur, blk == cur - 1))
        return jnp.where(forced, FORCE, jnp.where(blk <= cur, imp, -1.0))

    items = [(i, r) for i in range(n_q) for r in heads]
    next_win = win_logits(*items[0])
    next_cmp = cmp_logits(0)
    for n, (i, r) in enumerate(items):
        logits, tile_cmp = next_win, next_cmp
        if n + 1 < len(items):
            next_win = win_logits(*items[n + 1])
            next_cmp = cmp_logits(items[n + 1][0]) if items[n + 1][1] == 0 else None
        if r == 0:
            imp = importance(i, tile_cmp)
            keep = jnp.zeros(imp.shape, jnp.bool_)
            done = 0
        ss = [s if mask is None else jnp.where(mask, s, NEG)
              for s, (_, mask) in zip(logits, win_tiles(i))]
        m = functools.reduce(jnp.maximum, [_col_max(s) for s in ss])
        acc = functools.reduce(jnp.add, [
            _dot(jnp.concatenate([vw_ref[0, j], ones], axis=0), jnp.exp2(s - m).astype(BF16))
            for s, (j, _) in zip(ss, win_tiles(i))])
        ow_ref[0, r * HEAD_DIM:(r + 1) * HEAD_DIM, i * t:(i + 1) * t] = _normalised(acc).astype(
            ow_ref.dtype)
        for _ in range(min(per_head, n_rounds - done)):
            top = _col_max(imp)
            first = jnp.min(jnp.where(imp == top, blk, n_blk), axis=0, keepdims=True)
            pick = blk == first
            keep = jnp.logical_or(keep, pick)
            imp = jnp.where(pick, -jnp.inf, imp)
            done += 1
        if r == NSA_REP - 1:
            bias_ref[0, 0, :, i * t:(i + 1) * t] = jnp.where(keep, 0.0, NEG).astype(bias_ref.dtype)


def _local_attn(qt, kc, vct, ovt, n_cmp, kw, vw):
    B, W, S = qt.shape
    G = kc.shape[1]
    t = min(ATTN_TILE, S)
    assert WINDOW % t == 0
    n_blk = ovt.shape[0]
    kern = functools.partial(_local_attn_kernel, t=t, n_cmp=n_cmp, n_blk=n_blk, n_back=WINDOW // t)
    feat = pl.BlockSpec((1, W // G, S), lambda b, g: (b, g, 0))
    return pl.pallas_call(
        kern,
        name="nsa_local_attn",
        grid=(B, G),
        in_specs=[
            feat,
            pl.BlockSpec((1, 1) + kc.shape[2:], lambda b, g: (b, g, 0, 0)),
            pl.BlockSpec((1, 1) + vct.shape[2:], lambda b, g: (b, g, 0, 0)),
            pl.BlockSpec(ovt.shape, lambda b, g: (0, 0)),
            pl.BlockSpec((1, 1, S, LANES), lambda b, g: (b, g, 0, 0)),
            pl.BlockSpec((1, S // t, HEAD_DIM, t), lambda b, g: (b, 0, g, 0)),
        ],
        out_specs=[feat, pl.BlockSpec((1, 1, n_blk, S), lambda b, g: (b, g, 0, 0)), feat],
        out_shape=[
            jax.ShapeDtypeStruct((B, W, S), BF16),
            jax.ShapeDtypeStruct((B, G, n_blk, S), BF16),
            jax.ShapeDtypeStruct((B, W, S), BF16),
        ],
        compiler_params=_params("parallel", "parallel"),
    )(qt, kc, vct, ovt, kw, vw)


def _sel_attn_kernel(q_ref, bias_ref, k_ref, v_ref, oc_ref, ow_ref, gt_ref, z_ref, o_ref,
                     q_scr, s_scr, acc_scr, *, t):
    nh = q_scr.shape[1]
    n_q = q_ref.shape[2] // t

    def load_q(i, slot):
        cols = slice(i * t, (i + 1) * t)
        for hh in range(nh):
            bias = _pad_rows(bias_ref[0, hh // NSA_REP, :, cols], HEAD_DIM)
            q_scr[slot, hh] = jnp.concatenate(
                [q_ref[0, hh * HEAD_DIM:(hh + 1) * HEAD_DIM, cols], bias], axis=0)

    def key_tile(j, hh):
        return k_ref[0, hh // NSA_REP, _tile(j, t), :]

    def value_rows(j, hh):
        g = hh // NSA_REP
        return v_ref[0, j, g * HEAD_DIM:(g + 1) * HEAD_DIM, :]

    def emit(i, accs):
        cols = slice(i * t, (i + 1) * t)
        for hh in range(nh):
            rows = slice(hh * HEAD_DIM, (hh + 1) * HEAD_DIM)
            gate = lambda b: gt_ref[0, 0, N_BRANCH * hh + b:N_BRANCH * hh + b + 1, cols]
            o = (gate(0) * oc_ref[0, rows, cols].astype(F32) + gate(1) * _normalised(accs[hh])
                 + gate(2) * ow_ref[0, rows, cols].astype(F32))
            o_ref[0, rows, cols] = (o * _silu(z_ref[0, rows, cols].astype(F32))).astype(o_ref.dtype)

    _causal_attention(n_q, t, load_q, key_tile, value_rows, emit, q_scr, s_scr, acc_scr)


def _sel_attn(qt, bias, k_aug, vt, oc, ow, gt, zt):
    B, W, S = qt.shape
    t = min(ATTN_TILE, S)
    ng = HEADS_PER_STEP // NSA_REP
    n_blk = bias.shape[2]
    feat = pl.BlockSpec((1, HEADS_PER_STEP * HEAD_DIM, S), lambda b, g: (b, g, 0))
    return pl.pallas_call(
        functools.partial(_sel_attn_kernel, t=t),
        name="nsa_sel_attn",
        grid=(B, NSA_GROUPS // ng),
        in_specs=[
            feat,
            pl.BlockSpec((1, ng, n_blk, S), lambda b, g: (b, g, 0, 0)),
            pl.BlockSpec((1, ng, S, LANES), lambda b, g: (b, g, 0, 0)),
            pl.BlockSpec((1, S // t, ng * HEAD_DIM, t), lambda b, g: (b, 0, g, 0)),
            feat, feat,
            pl.BlockSpec((1, 1, GATE_ROWS, S), lambda b, g: (b, g, 0, 0)),
            feat,
        ],
        out_specs=feat,
        out_shape=jax.ShapeDtypeStruct((B, W, S), BF16),
        scratch_shapes=_attn_scratch(HEADS_PER_STEP, LANES, t),
        compiler_params=_params("parallel", "parallel"),
    )(qt, bias, k_aug, vt, oc, ow, gt, zt)


def _fox_layer(x, g, w_in, b_f, w_out):
    B, S, _ = x.shape
    W, H = ATTN_WIDTH, N_HEADS
    w_k = w_in[:, W:2 * W].astype(BF16)
    w_t = jnp.concatenate([w_in[:, :W] * Q_SCALE, w_in[:, 2 * W:3 * W], w_in[:, 3 * W + H:]],
                          axis=1).T.astype(BF16)
    wf_hi, wf_lo = _split2(w_in[:, 3 * W:3 * W + H])
    pad = lambda n: jnp.zeros((D_MODEL, LANES - n * H), BF16)
    wf_a = jnp.concatenate([wf_hi] * 3 + [wf_lo] * 3 + [pad(6)], axis=1)
    wf_b = jnp.concatenate([wf_hi] * 3 + [pad(3)], axis=1)
    bf_row = jnp.concatenate([b_f, b_f, b_f, jnp.zeros((LANES - 3 * H,), F32)])[None, :]

    qt, k, vt, zt, cx = _fox_in(x, g[None, :], w_k, w_t, wf_a, wf_b, bf_row)

    t = min(ATTN_TILE, S)
    r = np.arange(LANES)
    sel = np.logical_and(r[None, :] % H == np.arange(H)[:, None], r[None, :] < 3 * H)
    qx = np.broadcast_to(np.where(sel, -1.0, 0.0).astype(np.float32)[:, :, None], (H, LANES, t))
    ot = _fox_attn(qt, jnp.asarray(qx, BF16), k, cx, vt, zt)
    return _out_proj(ot, x, w_out.astype(BF16))


def _nsa_layer(x, g, w_in, pe_k, w_ck1, w_ck2, pe_v, w_cv1, w_cv2, w_out, final_g):
    B, S, _ = x.shape
    W, KV, G = ATTN_WIDTH, KV_WIDTH, NSA_GROUPS
    sec = lambda j: w_in[:, W + j * KV:W + (j + 1) * KV]
    gate_off = W + 6 * KV
    n_gate = N_BRANCH * N_HEADS
    w_k = jnp.concatenate([sec(2), sec(4), sec(0), sec(1)], axis=1).astype(BF16)
    w_t = jnp.concatenate([w_in[:, :W] * Q_SCALE, sec(3), sec(5), w_in[:, gate_off + n_gate:],
                           w_in[:, gate_off:gate_off + n_gate]], axis=1).T.astype(BF16)
    pos = np.arange(S)
    cos, sin = _rope_angles(pos)
    qt, ksel, kwin, chk, chv, vsel, vwin, zt, gt = _nsa_in(
        x, g[None, :], w_k, w_t, _rope_lane_tables(pos),
        (np.ascontiguousarray(cos.T), np.ascontiguousarray(sin.T)))

    n_chunk = S // CMP_STRIDE
    n_cmp = n_chunk - CMP_LEN // CMP_STRIDE + 1

    def flat_pe(pe):
        pe = pe.reshape(1, CMP_LEN * HEAD_DIM)
        return jnp.broadcast_to(pe, (SUBLANES, CMP_LEN * HEAD_DIM)).astype(BF16)

    w2k = jnp.concatenate([w_ck2, jnp.zeros_like(w_ck2)], axis=1).astype(BF16)
    cmp_end = np.arange(n_chunk) * CMP_STRIDE + CMP_LEN - 1
    kc, vct = _compress(chk, chv, w_ck1.astype(BF16), w2k, flat_pe(pe_k),
                        w_cv1.astype(BF16), w_cv2.T.astype(BF16), flat_pe(pe_v),
                        _rope_lane_tables(cmp_end))

    n_blk = S // SEL_LEN
    ci = np.arange(n_chunk) * CMP_STRIDE
    sj = np.arange(n_blk) * SEL_LEN
    ovt = np.logical_and(ci[None, :] < sj[:, None] + SEL_LEN, ci[None, :] + CMP_LEN > sj[:, None])
    ovt = np.logical_and(ovt, np.arange(n_chunk)[None, :] < n_cmp).astype(np.float32)
    oc, bias, ow = _local_attn(qt, kc, vct, jnp.asarray(ovt, BF16), n_cmp, kwin, vwin)
    ot = _sel_attn(qt, bias, ksel, vsel, oc, ow, gt, zt)
    return _out_proj(ot, x, w_out.astype(BF16), final_g[None, :])


def kernel(x, norm_g, fox_w_in, fox_b_f, fox_w_out, nsa_w_in, nsa_pe_k, nsa_w_ck1, nsa_w_ck2,
           nsa_pe_v, nsa_w_cv1, nsa_w_cv2, nsa_w_out, final_g):
    x = _fox_layer(x, norm_g[0], fox_w_in[0], fox_b_f[0], fox_w_out[0])
    return _nsa_layer(x, norm_g[1], nsa_w_in[0], nsa_pe_k[0], nsa_w_ck1[0], nsa_w_ck2[0],
                      nsa_pe_v[0], nsa_w_cv1[0], nsa_w_cv2[0], nsa_w_out[0], final_g)
```

```python
import functools
import math

import jax
import jax.numpy as jnp
import numpy as np
from jax import lax
from jax.experimental import pallas as pl
from jax.experimental.pallas import tpu as pltpu

F32 = jnp.float32
BF16 = jnp.bfloat16

D_MODEL = 1024
N_HEADS = 16
HEAD_DIM = 64
ATTN_WIDTH = N_HEADS * HEAD_DIM
NSA_GROUPS = 4
NSA_REP = N_HEADS // NSA_GROUPS
KV_WIDTH = NSA_GROUPS * HEAD_DIM
CMP_LEN = 32
CMP_STRIDE = 16
CMP_HIDDEN = 256
SEL_LEN = 64
N_SELECT = 8
WINDOW = 512
N_BRANCH = 3
ROPE_THETA = 500000.0
ROPE_DIM = HEAD_DIM // 4
ROPE_HALF = ROPE_DIM // 2
NORM_EPS = 1e-6
NEG = -1e30
FORCE = 1e6
LOG2E = math.log2(math.e)
Q_SCALE = HEAD_DIM ** -0.5 * LOG2E

LANES = 128
SUBLANES = 8
ONES_ROWS = 2 * SUBLANES
PROJ_ROWS = 512
OUT_ROWS = 1024
ATTN_TILE = 256
HEADS_PER_STEP = 8
GATE_ROWS = -(-N_BRANCH * HEADS_PER_STEP // SUBLANES) * SUBLANES
VMEM_LIMIT = 56 * 1024 * 1024


def _params(*sem):
    return pltpu.CompilerParams(dimension_semantics=sem, vmem_limit_bytes=VMEM_LIMIT)


def _iota(shape, dim):
    return lax.broadcasted_iota(jnp.int32, shape, dim)


def _split2(x):
    hi = x.astype(BF16)
    lo = (x - hi.astype(F32)).astype(BF16)
    return hi, lo


def _split3(x):
    hi = x.astype(BF16)
    r1 = x - hi.astype(F32)
    mid = r1.astype(BF16)
    lo = (r1 - mid.astype(F32)).astype(BF16)
    return hi, mid, lo


def _dot(a, b):
    return jnp.dot(a, b, preferred_element_type=F32)


def _dot_nt(a, b):
    return lax.dot_general(a, b, (((1,), (1,)), ((), ())), preferred_element_type=F32)


def _dot_tn(a, b):
    return lax.dot_general(a, b, (((0,), (0,)), ((), ())), preferred_element_type=F32)


def _rmsnorm(x, g):
    ms = jnp.mean(x * x, axis=-1, keepdims=True)
    return x * lax.rsqrt(ms + NORM_EPS) * g


def _silu(x):
    return x * (1.0 / (1.0 + jnp.exp(-x)))


def _col_reduce(x, op, reduce):
    rows = x.shape[0]
    if rows % (4 * SUBLANES) == 0 and rows >= 8 * SUBLANES:
        q = rows // 4
        x = op(op(x[:q], x[q:2 * q]), op(x[2 * q:3 * q], x[3 * q:]))
    return reduce(x, axis=0, keepdims=True)


def _col_max(x):
    return _col_reduce(x, jnp.maximum, jnp.max)


def _col_sum(x):
    return _col_reduce(x, jnp.add, jnp.sum)


def _pad_rows(x, rows):
    if rows == x.shape[0]:
        return x
    return jnp.concatenate([x, jnp.zeros((rows - x.shape[0], x.shape[1]), x.dtype)], axis=0)


def _tile(j, t):
    return pl.ds(j * t if isinstance(j, int) else pl.multiple_of(j * t, t), t)


def _normalised(acc):
    return acc[:HEAD_DIM] * (1.0 / acc[HEAD_DIM:HEAD_DIM + 1])


def _rope_lanes(x, cos, sin_lo, sin_hi):
    return (x * cos + pltpu.roll(x, LANES - ROPE_HALF, axis=1) * sin_lo
            + pltpu.roll(x, ROPE_HALF, axis=1) * sin_hi)


def _rope_rows(x, cos, sin):
    x1, x2 = x[:ROPE_HALF], x[ROPE_HALF:ROPE_DIM]
    return jnp.concatenate([x1 * cos - x2 * sin, x1 * sin + x2 * cos, x[ROPE_DIM:]], axis=0)


def _rope_angles(pos):
    inv_freq = np.power(np.float32(ROPE_THETA),
                        -np.arange(ROPE_HALF, dtype=np.float32) * np.float32(2.0 / ROPE_DIM))
    ang = pos.astype(np.float32)[:, None] * inv_freq[None, :].astype(np.float32)
    return np.cos(ang).astype(np.float32), np.sin(ang).astype(np.float32)


def _rope_lane_tables(pos):
    cos, sin = _rope_angles(pos)
    n = pos.shape[0]
    ones = np.ones((n, HEAD_DIM - ROPE_DIM), np.float32)
    zeros = np.zeros((n, HEAD_DIM - ROPE_DIM), np.float32)
    z8 = np.zeros((n, ROPE_HALF), np.float32)
    c = np.concatenate([cos, cos, ones], axis=1)
    s_lo = np.concatenate([-sin, z8, zeros], axis=1)
    s_hi = np.concatenate([z8, sin, zeros], axis=1)
    tile = lambda t: np.concatenate([t, t], axis=1)
    return tile(c), tile(s_lo), tile(s_hi)


def _fox_in_kernel(x_ref, g_ref, wk_ref, wt_ref, wfa_ref, wfb_ref, bf_ref, tri_ref,
                   qt_ref, k_ref, vt_ref, zt_ref, cx_ref, carry_ref, *, tk):
    s = pl.program_id(1)
    h = _rmsnorm(x_ref[0], g_ref[...])
    hb = h.astype(BF16)
    tm = hb.shape[0]
    W, H = ATTN_WIDTH, N_HEADS

    @pl.when(s == 0)
    def _():
        carry_ref[...] = jnp.zeros_like(carry_ref)

    h_lo = (h - hb.astype(F32)).astype(BF16)
    fa = _dot(hb, wfa_ref[...])
    f = fa + pltpu.roll(fa, LANES - 3 * H, axis=1) + _dot(h_lo, wfb_ref[...]) + bf_ref[...]

    for n0 in range(0, W, 512):
        k_ref[0, :, n0:n0 + 512] = _dot(hb, wk_ref[:, n0:n0 + 512]).astype(BF16)

    log_f = jnp.minimum(f, 0.0) - jnp.log1p(jnp.exp(-jnp.abs(f)))
    lane = _iota(log_f.shape, 1)
    a0, a1, a2 = _split3(log_f)
    pieces = jnp.where(lane < H, a0, jnp.where(lane < 2 * H, a1, a2))
    c = _dot(tri_ref[...], pieces)

    for r0 in range(0, 3 * W, 512):
        t = _dot_nt(wt_ref[r0:r0 + 512, :], hb).astype(BF16)
        if r0 < W:
            qt_ref[0, r0:r0 + 512, :] = t
        elif r0 < 2 * W:
            for ct in range(tm // tk):
                vt_ref[0, ct, r0 - W:r0 - W + 512, :] = t[:, ct * tk:(ct + 1) * tk]
        else:
            zt_ref[0, r0 - 2 * W:r0 - 2 * W + 512, :] = t

    c = c + pltpu.roll(c, LANES - H, axis=1) + pltpu.roll(c, LANES - 2 * H, axis=1)
    c = c + carry_ref[0:1, :]
    carry_ref[...] = jnp.broadcast_to(c[tm - 1:tm, :], carry_ref.shape)
    c = jnp.where(lane < H, c, jnp.where(lane < 2 * H, pltpu.roll(c, H, axis=1),
                                         pltpu.roll(c, 2 * H, axis=1)))

    hi, mid, lo = _split3(c * LOG2E)
    zero = jnp.zeros_like(hi)
    cx_ref[0] = jnp.where(lane < H, hi,
                          jnp.where(lane < 2 * H, mid, jnp.where(lane < 3 * H, lo, zero)))


def _fox_in(x, g, w_k, w_t, wf_a, wf_b, bf_row):
    B, S, _ = x.shape
    tm = min(PROJ_ROWS, S)
    tk = min(ATTN_TILE, S)
    W = ATTN_WIDTH
    tri = np.tril(np.ones((tm, tm), np.float32)).astype(BF16)
    const = lambda b, s: (0, 0)
    feat = pl.BlockSpec((1, W, tm), lambda b, s: (b, 0, s))
    return pl.pallas_call(
        functools.partial(_fox_in_kernel, tk=tk),
        name="fox_in",
        grid=(B, S // tm),
        in_specs=[
            pl.BlockSpec((1, tm, D_MODEL), lambda b, s: (b, s, 0)),
            pl.BlockSpec((1, D_MODEL), const),
            pl.BlockSpec(w_k.shape, const),
            pl.BlockSpec(w_t.shape, const),
            pl.BlockSpec((D_MODEL, LANES), const),
            pl.BlockSpec((D_MODEL, LANES), const),
            pl.BlockSpec((1, LANES), const),
            pl.BlockSpec((tm, tm), const),
        ],
        out_specs=[
            feat,
            pl.BlockSpec((1, tm, W), lambda b, s: (b, s, 0)),
            pl.BlockSpec((1, tm // tk, W, tk), lambda b, s: (b, s, 0, 0)),
            feat,
            pl.BlockSpec((1, tm, LANES), lambda b, s: (b, s, 0)),
        ],
        out_shape=[
            jax.ShapeDtypeStruct((B, W, S), BF16),
            jax.ShapeDtypeStruct((B, S, W), BF16),
            jax.ShapeDtypeStruct((B, S // tk, W, tk), BF16),
            jax.ShapeDtypeStruct((B, W, S), BF16),
            jax.ShapeDtypeStruct((B, S, LANES), BF16),
        ],
        scratch_shapes=[pltpu.VMEM((SUBLANES, LANES), F32)],
        compiler_params=_params("parallel", "arbitrary"),
    )(x, g, w_k, w_t, wf_a, wf_b, bf_row, tri)


def _causal_attention(n_q, t, load_q, key_tile, value_rows, emit, q_scr, s_scr, acc_scr):
    heads = range(q_scr.shape[1])
    ones = jnp.ones((ONES_ROWS, t), BF16)
    causal = _iota((t, t), 0) <= _iota((t, t), 1)

    def logits(slot, j):
        return [_dot(key_tile(j, hh), q_scr[slot, hh]) for hh in heads]

    def pv(j, hh, p):
        return _dot(jnp.concatenate([value_rows(j, hh), ones], axis=0), p.astype(BF16))

    def finish(i, slot, carry):
        accs = []
        for hh in heads:
            m, alpha = carry[hh]
            accs.append(alpha * acc_scr[hh] + pv(i - 1, hh, jnp.exp2(s_scr[slot, hh] - m)))
        emit(i, accs)

    pending = None
    for i in range(n_q):
        slot = i % 2
        load_q(i, slot)
        s_diag = logits(slot, i)
        s_first = logits(slot, 0) if i else None
        if pending is not None:
            finish(*pending)
        m_diag = []
        for hh in heads:
            s = jnp.where(causal, s_diag[hh], NEG)
            m = _col_max(s)
            acc_scr[hh] = pv(i, hh, jnp.exp2(s - m))
            m_diag.append(m)
        if i == 0:
            emit(0, [acc_scr[hh] for hh in heads])
            continue
        carry = []
        for hh in heads:
            s_scr[slot, hh] = s_first[hh]
            m_new = jnp.maximum(m_diag[hh], _col_max(s_first[hh]))
            carry.append((m_new, jnp.exp2(m_diag[hh] - m_new)))

        def body(j, carry, slot=slot):
            s_next, pvs = [], []
            for hh in heads:
                s_next.append(_dot(key_tile(j + 1, hh), q_scr[slot, hh]))
                pvs.append(pv(j, hh, jnp.exp2(s_scr[slot, hh] - carry[hh][0])))
            out = []
            for hh in heads:
                m = carry[hh][0]
                s_scr[slot, hh] = s_next[hh]
                m_new = jnp.maximum(m, _col_max(s_next[hh]))
                out.append((m_new, jnp.exp2(m - m_new)))
            for hh in heads:
                acc_scr[hh] = carry[hh][1] * acc_scr[hh] + pvs[hh]
            return tuple(out)

        carry = lax.fori_loop(0, i - 1, body, tuple(carry))
        pending = (i, slot, carry)
    if pending is not None:
        finish(*pending)


def _attn_scratch(nh, k_depth, t):
    return [pltpu.VMEM((2, nh, k_depth, t), BF16), pltpu.VMEM((2, nh, t, t), F32),
            pltpu.VMEM((nh, HEAD_DIM + ONES_ROWS, t), F32)]


def _fox_attn_kernel(q_ref, qx_ref, k_ref, kx_ref, v_ref, z_ref, o_ref, q_scr, s_scr, acc_scr, *, t):
    nh = q_scr.shape[1]
    n_q = q_ref.shape[2] // t
    row = _iota((LANES, t), 0)

    def load_q(i, slot):
        for hh in range(nh):
            pair, sub = divmod(hh, 2)
            qt = q_ref[0, pair * LANES:(pair + 1) * LANES, i * t:(i + 1) * t]
            own = jnp.logical_and(row >= sub * HEAD_DIM, row < (sub + 1) * HEAD_DIM)
            q_scr[slot, hh] = jnp.concatenate(
                [jnp.where(own, qt, jnp.zeros_like(qt)), qx_ref[hh]], axis=0)

    def key_tile(j, hh):
        pair = hh // 2
        return jnp.concatenate([k_ref[0, _tile(j, t), pair * LANES:(pair + 1) * LANES],
                                kx_ref[0, _tile(j, t), :]], axis=1)

    def value_rows(j, hh):
        return v_ref[0, j, hh * HEAD_DIM:(hh + 1) * HEAD_DIM, :]

    def emit(i, accs):
        for hh in range(nh):
            rows, cols = slice(hh * HEAD_DIM, (hh + 1) * HEAD_DIM), slice(i * t, (i + 1) * t)
            gated = _normalised(accs[hh]) * _silu(z_ref[0, rows, cols].astype(F32))
            o_ref[0, rows, cols] = gated.astype(o_ref.dtype)

    _causal_attention(n_q, t, load_q, key_tile, value_rows, emit, q_scr, s_scr, acc_scr)


def _fox_attn(qt, qx, k, kx, vt, zt):
    B, W, S = qt.shape
    t = min(ATTN_TILE, S)
    nh = HEADS_PER_STEP
    rows = nh * HEAD_DIM
    feat = pl.BlockSpec((1, rows, S), lambda b, p: (b, p, 0))
    return pl.pallas_call(
        functools.partial(_fox_attn_kernel, t=t),
        name="fox_attn",
        grid=(B, N_HEADS // nh),
        in_specs=[
            feat,
            pl.BlockSpec((nh, LANES, t), lambda b, p: (p, 0, 0)),
            pl.BlockSpec((1, S, rows), lambda b, p: (b, 0, p)),
            pl.BlockSpec((1, S, LANES), lambda b, p: (b, 0, 0)),
            pl.BlockSpec((1, S // t, rows, t), lambda b, p: (b, 0, p, 0)),
            feat,
        ],
        out_specs=feat,
        out_shape=jax.ShapeDtypeStruct((B, W, S), BF16),
        scratch_shapes=_attn_scratch(nh, 2 * LANES, t),
        compiler_params=_params("parallel", "parallel"),
    )(qt, qx, k, kx, vt, zt)


def _out_proj_kernel(*refs, final_norm):
    if final_norm:
        ot_ref, x_ref, w_ref, g_ref, y_ref = refs
    else:
        ot_ref, x_ref, w_ref, y_ref = refs
    y = x_ref[0] + _dot_tn(ot_ref[0], w_ref[...])
    y_ref[0] = _rmsnorm(y, g_ref[...]) if final_norm else y


def _out_proj(ot, x, w_out, final_g=None):
    B, S, _ = x.shape
    tm = min(OUT_ROWS, S)
    const = lambda b, s: (0, 0)
    tok = pl.BlockSpec((1, tm, D_MODEL), lambda b, s: (b, s, 0))
    in_specs = [pl.BlockSpec((1, ATTN_WIDTH, tm), lambda b, s: (b, 0, s)), tok,
                pl.BlockSpec((ATTN_WIDTH, D_MODEL), const)]
    args = [ot, x, w_out]
    if final_g is not None:
        in_specs.append(pl.BlockSpec((1, D_MODEL), const))
        args.append(final_g)
    return pl.pallas_call(
        functools.partial(_out_proj_kernel, final_norm=final_g is not None),
        name="out_proj",
        grid=(B, S // tm),
        in_specs=in_specs,
        out_specs=tok,
        out_shape=jax.ShapeDtypeStruct((B, S, D_MODEL), F32),
        compiler_params=_params("parallel", "parallel"),
    )(*args)


_TOK_SEL = 0
_TOK_WIN = KV_WIDTH
_TOK_CMP = 2 * KV_WIDTH
_TOK_COLS = 4 * KV_WIDTH
_FEAT_Q = 0
_FEAT_VSEL = ATTN_WIDTH
_FEAT_VWIN = _FEAT_VSEL + KV_WIDTH
_FEAT_Z = _FEAT_VWIN + KV_WIDTH
_FEAT_GATE = _FEAT_Z + ATTN_WIDTH
_FEAT_ROWS = _FEAT_GATE + N_BRANCH * N_HEADS


def _nsa_in_kernel(x_ref, g_ref, wk_ref, wt_ref, cos_ref, slo_ref, shi_ref, cost_ref, sint_ref,
                   qt_ref, ksel_ref, kwin_ref, chk_ref, chv_ref, vsel_ref, vwin_ref, zt_ref, gt_ref,
                   cmp_scr, *, tk):
    s = pl.program_id(1)
    hb = _rmsnorm(x_ref[0], g_ref[...]).astype(BF16)
    tm = hb.shape[0]
    G = NSA_GROUPS

    cos, slo, shi = cos_ref[...], slo_ref[...], shi_ref[...]
    tok = s * tm + _iota((tm, LANES), 0)
    lane = _iota((tm, LANES), 1)
    low = lane < HEAD_DIM
    block_id = jnp.where(lane - HEAD_DIM == tok // SEL_LEN, 1.0, 0.0)
    keys = _dot(hb, wk_ref[:, _TOK_SEL:_TOK_CMP])
    for col0, out_ref, extra in ((_TOK_SEL, ksel_ref, block_id), (_TOK_WIN, kwin_ref, 0.0)):
        for gp in range(G // 2):
            pair = _rope_lanes(keys[:, col0 + gp * LANES:col0 + (gp + 1) * LANES], cos, slo, shi)
            out_ref[0, 2 * gp] = jnp.where(low, pair, extra).astype(BF16)
            out_ref[0, 2 * gp + 1] = jnp.where(
                low, pltpu.roll(pair, HEAD_DIM, axis=1), extra).astype(BF16)

    raw = _dot(hb, wk_ref[:, _TOK_CMP:_TOK_COLS])
    for c in range(cmp_scr.shape[0]):
        cmp_scr[c] = raw[:, c * LANES:(c + 1) * LANES]

    cost, sint = cost_ref[...], sint_ref[...]
    for r0 in range(0, ATTN_WIDTH, 512):
        t = _dot_nt(wt_ref[_FEAT_Q + r0:_FEAT_Q + r0 + 512, :], hb)
        for h0 in range(0, 512, HEAD_DIM):
            qt_ref[0, r0 + h0:r0 + h0 + HEAD_DIM, :] = _rope_rows(
                t[h0:h0 + HEAD_DIM], cost, sint).astype(BF16)
        zt_ref[0, r0:r0 + 512, :] = _dot_nt(wt_ref[_FEAT_Z + r0:_FEAT_Z + r0 + 512, :], hb).astype(BF16)
    for row0, out_ref in ((_FEAT_VSEL, vsel_ref), (_FEAT_VWIN, vwin_ref)):
        t = _dot_nt(wt_ref[row0:row0 + KV_WIDTH, :], hb).astype(BF16)
        for c in range(tm // tk):
            out_ref[0, c] = t[:, c * tk:(c + 1) * tk]
    gate = _dot_nt(wt_ref[_FEAT_GATE:_FEAT_ROWS, :], hb)
    gate = 1.0 / (1.0 + jnp.exp(-gate))
    rows = N_BRANCH * HEADS_PER_STEP
    for hg in range(N_HEADS // HEADS_PER_STEP):
        gt_ref[0, hg] = _pad_rows(gate[hg * rows:(hg + 1) * rows], GATE_ROWS)

    n_ch = tm // CMP_STRIDE
    low_ch = _iota((n_ch, LANES), 1) < HEAD_DIM
    for kind, out_ref in enumerate((chk_ref, chv_ref)):
        for gp in range(G // 2):
            c = kind * (G // 2) + gp
            for l in range(0, CMP_STRIDE, 2):
                a0 = cmp_scr[c, pl.ds(l, n_ch, stride=CMP_STRIDE), :]
                a1 = cmp_scr[c, pl.ds(l + 1, n_ch, stride=CMP_STRIDE), :]
                dst = slice(l * HEAD_DIM, (l + 2) * HEAD_DIM)
                out_ref[0, 2 * gp, :, dst] = jnp.where(
                    low_ch, a0, pltpu.roll(a1, HEAD_DIM, axis=1)).astype(BF16)
                out_ref[0, 2 * gp + 1, :, dst] = jnp.where(
                    low_ch, pltpu.roll(a0, HEAD_DIM, axis=1), a1).astype(BF16)


def _nsa_in(x, g, w_k, w_t, lane_tables, row_tables):
    B, S, _ = x.shape
    tm = min(PROJ_ROWS, S)
    tk = min(ATTN_TILE, S)
    W, G = ATTN_WIDTH, NSA_GROUPS
    const = lambda b, s: (0, 0)
    feat = lambda rows: pl.BlockSpec((1, rows, tm), lambda b, s: (b, 0, s))
    tiles = pl.BlockSpec((1, tm // tk, KV_WIDTH, tk), lambda b, s: (b, s, 0, 0))
    keys = pl.BlockSpec((1, G, tm, LANES), lambda b, s: (b, 0, s, 0))
    chunk_w = CMP_STRIDE * HEAD_DIM
    chunk = pl.BlockSpec((1, G, tm // CMP_STRIDE, chunk_w), lambda b, s: (b, 0, s, 0))
    n_hg = N_HEADS // HEADS_PER_STEP
    return pl.pallas_call(
        functools.partial(_nsa_in_kernel, tk=tk),
        name="nsa_in",
        grid=(B, S // tm),
        in_specs=[
            pl.BlockSpec((1, tm, D_MODEL), lambda b, s: (b, s, 0)),
            pl.BlockSpec((1, D_MODEL), const),
            pl.BlockSpec(w_k.shape, const),
            pl.BlockSpec(w_t.shape, const),
        ] + [pl.BlockSpec((tm, LANES), lambda b, s: (s, 0))] * 3
          + [pl.BlockSpec((ROPE_HALF, tm), lambda b, s: (0, s))] * 2,
        out_specs=[feat(W), keys, keys, chunk, chunk, tiles, tiles, feat(W),
                   pl.BlockSpec((1, n_hg, GATE_ROWS, tm), lambda b, s: (b, 0, 0, s))],
        out_shape=[
            jax.ShapeDtypeStruct((B, W, S), BF16),
            jax.ShapeDtypeStruct((B, G, S, LANES), BF16),
            jax.ShapeDtypeStruct((B, G, S, LANES), BF16),
            jax.ShapeDtypeStruct((B, G, S // CMP_STRIDE, chunk_w), BF16),
            jax.ShapeDtypeStruct((B, G, S // CMP_STRIDE, chunk_w), BF16),
            jax.ShapeDtypeStruct((B, S // tk, KV_WIDTH, tk), BF16),
            jax.ShapeDtypeStruct((B, S // tk, KV_WIDTH, tk), BF16),
            jax.ShapeDtypeStruct((B, W, S), BF16),
            jax.ShapeDtypeStruct((B, n_hg, GATE_ROWS, S), F32),
        ],
        scratch_shapes=[pltpu.VMEM((2 * KV_WIDTH // LANES, tm, LANES), F32)],
        compiler_params=_params("parallel", "parallel"),
    )(x, g, w_k, w_t, *lane_tables, *row_tables)


def _compress_kernel(chk_ref, chv_ref, w1k_ref, w2k_ref, pek_ref, w1v_ref, w2vt_ref, pev_ref,
                     cos_ref, slo_ref, shi_ref, kc_ref, vct_ref):
    half = CMP_STRIDE * HEAD_DIM

    def hidden(ch, w1_ref, pe_ref):
        a = _dot(ch, w1_ref[:half, :])
        b = _dot(ch, w1_ref[half:, :])
        b = pltpu.roll(b, b.shape[0] - 1, axis=0)
        pe = _dot(pe_ref[...], w1_ref[...])[0:1, :]
        return _silu(a + b + pe).astype(BF16)

    kc = _dot(hidden(chk_ref[0, 0], w1k_ref, pek_ref), w2k_ref[...])
    kc_ref[0, 0] = _rope_lanes(kc, cos_ref[...], slo_ref[...], shi_ref[...]).astype(BF16)
    vct_ref[0, 0] = _dot_nt(w2vt_ref[...], hidden(chv_ref[0, 0], w1v_ref, pev_ref)).astype(BF16)


def _compress(chk, chv, w1k, w2k, pek, w1v, w2vt, pev, tables):
    B, G, n_chunk, width = chk.shape
    const = lambda b, g: (0, 0)
    chunk_spec = pl.BlockSpec((1, 1, n_chunk, width), lambda b, g: (b, g, 0, 0))
    specs = lambda ws: [pl.BlockSpec(w.shape, const) for w in ws]
    return pl.pallas_call(
        _compress_kernel,
        name="nsa_compress",
        grid=(B, G),
        in_specs=[chunk_spec, chunk_spec] + specs((w1k, w2k, pek, w1v, w2vt, pev))
                 + [pl.BlockSpec((n_chunk, LANES), const)] * 3,
        out_specs=[pl.BlockSpec((1, 1, n_chunk, LANES), lambda b, g: (b, g, 0, 0)),
                   pl.BlockSpec((1, 1, HEAD_DIM, n_chunk), lambda b, g: (b, g, 0, 0))],
        out_shape=[jax.ShapeDtypeStruct((B, G, n_chunk, LANES), BF16),
                   jax.ShapeDtypeStruct((B, G, HEAD_DIM, n_chunk), BF16)],
        compiler_params=_params("parallel", "parallel"),
    )(chk, chv, w1k, w2k, pek, w1v, w2vt, pev, *tables)


def _local_attn_kernel(q_ref, kc_ref, vct_ref, ovt_ref, kw_ref, vw_ref, oc_ref, bias_ref, ow_ref,
                       *, t, n_cmp, n_blk, n_back):
    n_q = q_ref.shape[2] // t
    heads = range(NSA_REP)
    kc = kc_ref[0, 0]
    vct = vct_ref[0, 0]
    ovt = ovt_ref[...]
    n_pad = kc.shape[0]
    above = _iota((t, t), 0) > _iota((t, t), 1)
    ones = jnp.ones((ONES_ROWS, t), BF16)
    blk_c = _iota((n_pad, t), 0)
    blk = _iota((n_blk, t), 0)
    n_rounds = min(N_SELECT, n_blk)
    per_head = -(-n_rounds // NSA_REP)

    def q_aug(i, r):
        return _pad_rows(q_ref[0, r * HEAD_DIM:(r + 1) * HEAD_DIM, i * t:(i + 1) * t], LANES)

    masks = {0: jnp.logical_not(above), n_back: above}

    def win_tiles(i):
        return [(i - d, masks.get(d)) for d in range(n_back + 1) if i - d >= 0]

    def win_logits(i, r):
        return [_dot(kw_ref[0, 0, _tile(j, t), :], q_aug(i, r)) for j, _ in win_tiles(i)]

    def cmp_logits(i):
        return [_dot(kc, q_aug(i, r)) for r in heads]

    def importance(i, logits):
        cols = slice(i * t, (i + 1) * t)
        qry = i * t + _iota((n_pad, t), 1)
        valid = jnp.logical_and(blk_c * CMP_STRIDE + (CMP_LEN - 1) <= qry, blk_c < n_cmp)
        p_sum = jnp.zeros((n_pad, t), F32)
        for r in heads:
            s = jnp.where(valid, logits[r], NEG)
            e = jnp.exp2(s - _col_max(s))
            p = jnp.where(valid, e * (1.0 / _col_sum(e)), 0.0)
            p_sum = p_sum + p
            oc_ref[0, r * HEAD_DIM:(r + 1) * HEAD_DIM, cols] = _dot(
                vct, p.astype(BF16)).astype(oc_ref.dtype)
        p_hi, p_lo = _split2(p_sum)
        imp = _dot(ovt, p_hi) + _dot(ovt, p_lo)
        cur = (i * t + _iota(imp.shape, 1)) // SEL_LEN
        forced = jnp.logical_or(blk == 0, jnp.logical_or(blk == cur, blk == cur - 1))
        return jnp.where(forced, FORCE, jnp.where(blk <= cur, imp, -1.0))

    items = [(i, r) for i in range(n_q) for r in heads]
    next_win = win_logits(*items[0])
    next_cmp = cmp_logits(0)
    for n, (i, r) in enumerate(items):
        logits, tile_cmp = next_win, next_cmp
        if n + 1 < len(items):
            next_win = win_logits(*items[n + 1])
            next_cmp = cmp_logits(items[n + 1][0]) if items[n + 1][1] == 0 else None
        if r == 0:
            imp = importance(i, tile_cmp)
            keep = jnp.zeros(imp.shape, jnp.bool_)
            done = 0
        ss = [s if mask is None else jnp.where(mask, s, NEG)
              for s, (_, mask) in zip(logits, win_tiles(i))]
        m = functools.reduce(jnp.maximum, [_col_max(s) for s in ss])
        acc = functools.reduce(jnp.add, [
            _dot(jnp.concatenate([vw_ref[0, j], ones], axis=0), jnp.exp2(s - m).astype(BF16))
            for s, (j, _) in zip(ss, win_tiles(i))])
        ow_ref[0, r * HEAD_DIM:(r + 1) * HEAD_DIM, i * t:(i + 1) * t] = _normalised(acc).astype(
            ow_ref.dtype)
        for _ in range(min(per_head, n_rounds - done)):
            top = _col_max(imp)
            first = jnp.min(jnp.where(imp == top, blk, n_blk), axis=0, keepdims=True)
            pick = blk == first
            keep = jnp.logical_or(keep, pick)
            imp = jnp.where(pick, -jnp.inf, imp)
            done += 1
        if r == NSA_REP - 1:
            bias_ref[0, 0, :, i * t:(i + 1) * t] = jnp.where(keep, 0.0, NEG).astype(bias_ref.dtype)


def _local_attn(qt, kc, vct, ovt, n_cmp, kw, vw):
    B, W, S = qt.shape
    G = kc.shape[1]
    t = min(ATTN_TILE, S)
    assert WINDOW % t == 0
    n_blk = ovt.shape[0]
    kern = functools.partial(_local_attn_kernel, t=t, n_cmp=n_cmp, n_blk=n_blk, n_back=WINDOW // t)
    feat = pl.BlockSpec((1, W // G, S), lambda b, g: (b, g, 0))
    return pl.pallas_call(
        kern,
        name="nsa_local_attn",
        grid=(B, G),
        in_specs=[
            feat,
            pl.BlockSpec((1, 1) + kc.shape[2:], lambda b, g: (b, g, 0, 0)),
            pl.BlockSpec((1, 1) + vct.shape[2:], lambda b, g: (b, g, 0, 0)),
            pl.BlockSpec(ovt.shape, lambda b, g: (0, 0)),
            pl.BlockSpec((1, 1, S, LANES), lambda b, g: (b, g, 0, 0)),
            pl.BlockSpec((1, S // t, HEAD_DIM, t), lambda b, g: (b, 0, g, 0)),
        ],
        out_specs=[feat, pl.BlockSpec((1, 1, n_blk, S), lambda b, g: (b, g, 0, 0)), feat],
        out_shape=[
            jax.ShapeDtypeStruct((B, W, S), BF16),
            jax.ShapeDtypeStruct((B, G, n_blk, S), BF16),
            jax.ShapeDtypeStruct((B, W, S), BF16),
        ],
        compiler_params=_params("parallel", "parallel"),
    )(qt, kc, vct, ovt, kw, vw)


def _sel_attn_kernel(q_ref, bias_ref, k_ref, v_ref, oc_ref, ow_ref, gt_ref, z_ref, o_ref,
                     q_scr, s_scr, acc_scr, *, t):
    nh = q_scr.shape[1]
    n_q = q_ref.shape[2] // t

    def load_q(i, slot):
        cols = slice(i * t, (i + 1) * t)
        for hh in range(nh):
            bias = _pad_rows(bias_ref[0, hh // NSA_REP, :, cols], HEAD_DIM)
            q_scr[slot, hh] = jnp.concatenate(
                [q_ref[0, hh * HEAD_DIM:(hh + 1) * HEAD_DIM, cols], bias], axis=0)

    def key_tile(j, hh):
        return k_ref[0, hh // NSA_REP, _tile(j, t), :]

    def value_rows(j, hh):
        g = hh // NSA_REP
        return v_ref[0, j, g * HEAD_DIM:(g + 1) * HEAD_DIM, :]

    def emit(i, accs):
        cols = slice(i * t, (i + 1) * t)
        for hh in range(nh):
            rows = slice(hh * HEAD_DIM, (hh + 1) * HEAD_DIM)
            gate = lambda b: gt_ref[0, 0, N_BRANCH * hh + b:N_BRANCH * hh + b + 1, cols]
            o = (gate(0) * oc_ref[0, rows, cols].astype(F32) + gate(1) * _normalised(accs[hh])
                 + gate(2) * ow_ref[0, rows, cols].astype(F32))
            o_ref[0, rows, cols] = (o * _silu(z_ref[0, rows, cols].astype(F32))).astype(o_ref.dtype)

    _causal_attention(n_q, t, load_q, key_tile, value_rows, emit, q_scr, s_scr, acc_scr)


def _sel_attn(qt, bias, k_aug, vt, oc, ow, gt, zt):
    B, W, S = qt.shape
    t = min(ATTN_TILE, S)
    ng = HEADS_PER_STEP // NSA_REP
    n_blk = bias.shape[2]
    feat = pl.BlockSpec((1, HEADS_PER_STEP * HEAD_DIM, S), lambda b, g: (b, g, 0))
    return pl.pallas_call(
        functools.partial(_sel_attn_kernel, t=t),
        name="nsa_sel_attn",
        grid=(B, NSA_GROUPS // ng),
        in_specs=[
            feat,
            pl.BlockSpec((1, ng, n_blk, S), lambda b, g: (b, g, 0, 0)),
            pl.BlockSpec((1, ng, S, LANES), lambda b, g: (b, g, 0, 0)),
            pl.BlockSpec((1, S // t, ng * HEAD_DIM, t), lambda b, g: (b, 0, g, 0)),
            feat, feat,
            pl.BlockSpec((1, 1, GATE_ROWS, S), lambda b, g: (b, g, 0, 0)),
            feat,
        ],
        out_specs=feat,
        out_shape=jax.ShapeDtypeStruct((B, W, S), BF16),
        scratch_shapes=_attn_scratch(HEADS_PER_STEP, LANES, t),
        compiler_params=_params("parallel", "parallel"),
    )(qt, bias, k_aug, vt, oc, ow, gt, zt)


def _fox_layer(x, g, w_in, b_f, w_out):
    B, S, _ = x.shape
    W, H = ATTN_WIDTH, N_HEADS
    w_k = w_in[:, W:2 * W].astype(BF16)
    w_t = jnp.concatenate([w_in[:, :W] * Q_SCALE, w_in[:, 2 * W:3 * W], w_in[:, 3 * W + H:]],
                          axis=1).T.astype(BF16)
    wf_hi, wf_lo = _split2(w_in[:, 3 * W:3 * W + H])
    pad = lambda n: jnp.zeros((D_MODEL, LANES - n * H), BF16)
    wf_a = jnp.concatenate([wf_hi] * 3 + [wf_lo] * 3 + [pad(6)], axis=1)
    wf_b = jnp.concatenate([wf_hi] * 3 + [pad(3)], axis=1)
    bf_row = jnp.concatenate([b_f, b_f, b_f, jnp.zeros((LANES - 3 * H,), F32)])[None, :]

    qt, k, vt, zt, cx = _fox_in(x, g[None, :], w_k, w_t, wf_a, wf_b, bf_row)

    t = min(ATTN_TILE, S)
    r = np.arange(LANES)
    sel = np.logical_and(r[None, :] % H == np.arange(H)[:, None], r[None, :] < 3 * H)
    qx = np.broadcast_to(np.where(sel, -1.0, 0.0).astype(np.float32)[:, :, None], (H, LANES, t))
    ot = _fox_attn(qt, jnp.asarray(qx, BF16), k, cx, vt, zt)
    return _out_proj(ot, x, w_out.astype(BF16))


def _nsa_layer(x, g, w_in, pe_k, w_ck1, w_ck2, pe_v, w_cv1, w_cv2, w_out, final_g):
    B, S, _ = x.shape
    W, KV, G = ATTN_WIDTH, KV_WIDTH, NSA_GROUPS
    sec = lambda j: w_in[:, W + j * KV:W + (j + 1) * KV]
    gate_off = W + 6 * KV
    n_gate = N_BRANCH * N_HEADS
    w_k = jnp.concatenate([sec(2), sec(4), sec(0), sec(1)], axis=1).astype(BF16)
    w_t = jnp.concatenate([w_in[:, :W] * Q_SCALE, sec(3), sec(5), w_in[:, gate_off + n_gate:],
                           w_in[:, gate_off:gate_off + n_gate]], axis=1).T.astype(BF16)
    pos = np.arange(S)
    cos, sin = _rope_angles(pos)
    qt, ksel, kwin, chk, chv, vsel, vwin, zt, gt = _nsa_in(
        x, g[None, :], w_k, w_t, _rope_lane_tables(pos),
        (np.ascontiguousarray(cos.T), np.ascontiguousarray(sin.T)))

    n_chunk = S // CMP_STRIDE
    n_cmp = n_chunk - CMP_LEN // CMP_STRIDE + 1

    def flat_pe(pe):
        pe = pe.reshape(1, CMP_LEN * HEAD_DIM)
        return jnp.broadcast_to(pe, (SUBLANES, CMP_LEN * HEAD_DIM)).astype(BF16)

    w2k = jnp.concatenate([w_ck2, jnp.zeros_like(w_ck2)], axis=1).astype(BF16)
    cmp_end = np.arange(n_chunk) * CMP_STRIDE + CMP_LEN - 1
    kc, vct = _compress(chk, chv, w_ck1.astype(BF16), w2k, flat_pe(pe_k),
                        w_cv1.astype(BF16), w_cv2.T.astype(BF16), flat_pe(pe_v),
                        _rope_lane_tables(cmp_end))

    n_blk = S // SEL_LEN
    ci = np.arange(n_chunk) * CMP_STRIDE
    sj = np.arange(n_blk) * SEL_LEN
    ovt = np.logical_and(ci[None, :] < sj[:, None] + SEL_LEN, ci[None, :] + CMP_LEN > sj[:, None])
    ovt = np.logical_and(ovt, np.arange(n_chunk)[None, :] < n_cmp).astype(np.float32)
    oc, bias, ow = _local_attn(qt, kc, vct, jnp.asarray(ovt, BF16), n_cmp, kwin, vwin)
    ot = _sel_attn(qt, bias, ksel, vsel, oc, ow, gt, zt)
    return _out_proj(ot, x, w_out.astype(BF16), final_g[None, :])


def kernel(x, norm_g, fox_w_in, fox_b_f, fox_w_out, nsa_w_in, nsa_pe_k, nsa_w_ck1, nsa_w_ck2,
           nsa_pe_v, nsa_w_cv1, nsa_w_cv2, nsa_w_out, final_g):
    x = _fox_layer(x, norm_g[0], fox_w_in[0], fox_b_f[0], fox_w_out[0])
    return _nsa_layer(x, norm_g[1], nsa_w_in[0], nsa_pe_k[0], nsa_w_ck1[0], nsa_w_ck2[0],
                      nsa_pe_v[0], nsa_w_cv1[0], nsa_w_cv2[0], nsa_w_out[0], final_g)
```

```python
import functools
import math

import jax
import jax.numpy as jnp
import numpy as np
from jax import lax
from jax.experimental import pallas as pl
from jax.experimental.pallas import tpu as pltpu

F32 = jnp.float32
BF16 = jnp.bfloat16

D_MODEL = 1024
N_HEADS = 16
HEAD_DIM = 64
ATTN_WIDTH = N_HEADS * HEAD_DIM
NSA_GROUPS = 4
NSA_REP = N_HEADS // NSA_GROUPS
KV_WIDTH = NSA_GROUPS * HEAD_DIM
CMP_LEN = 32
CMP_STRIDE = 16
CMP_HIDDEN = 256
SEL_LEN = 64
N_SELECT = 8
WINDOW = 512
N_BRANCH = 3
ROPE_THETA = 500000.0
ROPE_DIM = HEAD_DIM // 4
ROPE_HALF = ROPE_DIM // 2
NORM_EPS = 1e-6
NEG = -1e30
FORCE = 1e6
LOG2E = math.log2(math.e)
Q_SCALE = HEAD_DIM ** -0.5 * LOG2E

LANES = 128
SUBLANES = 8
ONES_ROWS = 2 * SUBLANES
PROJ_ROWS = 512
OUT_ROWS = 1024
ATTN_TILE = 256
HEADS_PER_STEP = 8
LOOKAHEAD = 1
GATE_ROWS = -(-N_BRANCH * HEADS_PER_STEP // SUBLANES) * SUBLANES
VMEM_LIMIT = 56 * 1024 * 1024


def _params(*sem):
    return pltpu.CompilerParams(dimension_semantics=sem, vmem_limit_bytes=VMEM_LIMIT)


def _iota(shape, dim):
    return lax.broadcasted_iota(jnp.int32, shape, dim)


def _split2(x):
    hi = x.astype(BF16)
    lo = (x - hi.astype(F32)).astype(BF16)
    return hi, lo


def _split3(x):
    hi = x.astype(BF16)
    r1 = x - hi.astype(F32)
    mid = r1.astype(BF16)
    lo = (r1 - mid.astype(F32)).astype(BF16)
    return hi, mid, lo


def _dot(a, b):
    return jnp.dot(a, b, preferred_element_type=F32)


def _dot_nt(a, b):
    return lax.dot_general(a, b, (((1,), (1,)), ((), ())), preferred_element_type=F32)


def _dot_tn(a, b):
    return lax.dot_general(a, b, (((0,), (0,)), ((), ())), preferred_element_type=F32)


def _rmsnorm(x, g):
    ms = jnp.mean(x * x, axis=-1, keepdims=True)
    return x * lax.rsqrt(ms + NORM_EPS) * g


def _silu(x):
    return x * (1.0 / (1.0 + jnp.exp(-x)))


def _col_reduce(x, op, reduce):
    rows = x.shape[0]
    if rows % (4 * SUBLANES) == 0 and rows >= 8 * SUBLANES:
        q = rows // 4
        x = op(op(x[:q], x[q:2 * q]), op(x[2 * q:3 * q], x[3 * q:]))
    return reduce(x, axis=0, keepdims=True)


def _col_max(x):
    return _col_reduce(x, jnp.maximum, jnp.max)


def _col_sum(x):
    return _col_reduce(x, jnp.add, jnp.sum)


def _pad_rows(x, rows):
    if rows == x.shape[0]:
        return x
    return jnp.concatenate([x, jnp.zeros((rows - x.shape[0], x.shape[1]), x.dtype)], axis=0)


def _tile(j, t):
    return pl.ds(j * t if isinstance(j, int) else pl.multiple_of(j * t, t), t)


def _normalised(acc):
    return acc[:HEAD_DIM] * (1.0 / acc[HEAD_DIM:HEAD_DIM + 1])


def _rope_lanes(x, cos, sin_lo, sin_hi):
    return (x * cos + pltpu.roll(x, LANES - ROPE_HALF, axis=1) * sin_lo
            + pltpu.roll(x, ROPE_HALF, axis=1) * sin_hi)


def _rope_rows(x, cos, sin):
    x1, x2 = x[:ROPE_HALF], x[ROPE_HALF:ROPE_DIM]
    return jnp.concatenate([x1 * cos - x2 * sin, x1 * sin + x2 * cos, x[ROPE_DIM:]], axis=0)


def _rope_angles(pos):
    inv_freq = np.power(np.float32(ROPE_THETA),
                        -np.arange(ROPE_HALF, dtype=np.float32) * np.float32(2.0 / ROPE_DIM))
    ang = pos.astype(np.float32)[:, None] * inv_freq[None, :].astype(np.float32)
    return np.cos(ang).astype(np.float32), np.sin(ang).astype(np.float32)


def _rope_lane_tables(pos):
    cos, sin = _rope_angles(pos)
    n = pos.shape[0]
    ones = np.ones((n, HEAD_DIM - ROPE_DIM), np.float32)
    zeros = np.zeros((n, HEAD_DIM - ROPE_DIM), np.float32)
    z8 = np.zeros((n, ROPE_HALF), np.float32)
    c = np.concatenate([cos, cos, ones], axis=1)
    s_lo = np.concatenate([-sin, z8, zeros], axis=1)
    s_hi = np.concatenate([z8, sin, zeros], axis=1)
    tile = lambda t: np.concatenate([t, t], axis=1)
    return tile(c), tile(s_lo), tile(s_hi)


def _fox_in_kernel(x_ref, g_ref, wk_ref, wt_ref, wfa_ref, wfb_ref, bf_ref, tri_ref,
                   qt_ref, k_ref, vt_ref, zt_ref, cx_ref, carry_ref, *, tk):
    s = pl.program_id(1)
    h = _rmsnorm(x_ref[0], g_ref[...])
    hb = h.astype(BF16)
    tm = hb.shape[0]
    W, H = ATTN_WIDTH, N_HEADS

    @pl.when(s == 0)
    def _():
        carry_ref[...] = jnp.zeros_like(carry_ref)

    h_lo = (h - hb.astype(F32)).astype(BF16)
    fa = _dot(hb, wfa_ref[...])
    f = fa + pltpu.roll(fa, LANES - 3 * H, axis=1) + _dot(h_lo, wfb_ref[...]) + bf_ref[...]

    for n0 in range(0, W, 512):
        k_ref[0, :, n0:n0 + 512] = _dot(hb, wk_ref[:, n0:n0 + 512]).astype(BF16)

    log_f = jnp.minimum(f, 0.0) - jnp.log1p(jnp.exp(-jnp.abs(f)))
    lane = _iota(log_f.shape, 1)
    a0, a1, a2 = _split3(log_f)
    pieces = jnp.where(lane < H, a0, jnp.where(lane < 2 * H, a1, a2))
    c = _dot(tri_ref[...], pieces)

    for r0 in range(0, 3 * W, 512):
        t = _dot_nt(wt_ref[r0:r0 + 512, :], hb).astype(BF16)
        if r0 < W:
            qt_ref[0, r0:r0 + 512, :] = t
        elif r0 < 2 * W:
            for ct in range(tm // tk):
                vt_ref[0, ct, r0 - W:r0 - W + 512, :] = t[:, ct * tk:(ct + 1) * tk]
        else:
            zt_ref[0, r0 - 2 * W:r0 - 2 * W + 512, :] = t

    c = c + pltpu.roll(c, LANES - H, axis=1) + pltpu.roll(c, LANES - 2 * H, axis=1)
    c = c + carry_ref[0:1, :]
    carry_ref[...] = jnp.broadcast_to(c[tm - 1:tm, :], carry_ref.shape)
    c = jnp.where(lane < H, c, jnp.where(lane < 2 * H, pltpu.roll(c, H, axis=1),
                                         pltpu.roll(c, 2 * H, axis=1)))

    hi, mid, lo = _split3(c * LOG2E)
    zero = jnp.zeros_like(hi)
    cx_ref[0] = jnp.where(lane < H, hi,
                          jnp.where(lane < 2 * H, mid, jnp.where(lane < 3 * H, lo, zero)))


def _fox_in(x, g, w_k, w_t, wf_a, wf_b, bf_row):
    B, S, _ = x.shape
    tm = min(PROJ_ROWS, S)
    tk = min(ATTN_TILE, S)
    W = ATTN_WIDTH
    tri = np.tril(np.ones((tm, tm), np.float32)).astype(BF16)
    const = lambda b, s: (0, 0)
    feat = pl.BlockSpec((1, W, tm), lambda b, s: (b, 0, s))
    return pl.pallas_call(
        functools.partial(_fox_in_kernel, tk=tk),
        name="fox_in",
        grid=(B, S // tm),
        in_specs=[
            pl.BlockSpec((1, tm, D_MODEL), lambda b, s: (b, s, 0)),
            pl.BlockSpec((1, D_MODEL), const),
            pl.BlockSpec(w_k.shape, const),
            pl.BlockSpec(w_t.shape, const),
            pl.BlockSpec((D_MODEL, LANES), const),
            pl.BlockSpec((D_MODEL, LANES), const),
            pl.BlockSpec((1, LANES), const),
            pl.BlockSpec((tm, tm), const),
        ],
        out_specs=[
            feat,
            pl.BlockSpec((1, tm, W), lambda b, s: (b, s, 0)),
            pl.BlockSpec((1, tm // tk, W, tk), lambda b, s: (b, s, 0, 0)),
            feat,
            pl.BlockSpec((1, tm, LANES), lambda b, s: (b, s, 0)),
        ],
        out_shape=[
            jax.ShapeDtypeStruct((B, W, S), BF16),
            jax.ShapeDtypeStruct((B, S, W), BF16),
            jax.ShapeDtypeStruct((B, S // tk, W, tk), BF16),
            jax.ShapeDtypeStruct((B, W, S), BF16),
            jax.ShapeDtypeStruct((B, S, LANES), BF16),
        ],
        scratch_shapes=[pltpu.VMEM((SUBLANES, LANES), F32)],
        compiler_params=_params("parallel", "arbitrary"),
    )(x, g, w_k, w_t, wf_a, wf_b, bf_row, tri)


def _causal_attention(n_q, t, load_q, key_tile, value_rows, emit, q_scr, s_scr, acc_scr):
    heads = range(q_scr.shape[1])
    ones = jnp.ones((ONES_ROWS, t), BF16)
    causal = _iota((t, t), 0) <= _iota((t, t), 1)

    def logits(slot, j):
        return [_dot(key_tile(j, hh), q_scr[slot, hh]) for hh in heads]

    def pv(j, hh, p):
        return _dot(jnp.concatenate([value_rows(j, hh), ones], axis=0), p.astype(BF16))

    def finish(i, slot, carry):
        accs = []
        for hh in heads:
            m, alpha = carry[hh]
            accs.append(alpha * acc_scr[hh] + pv(i - 1, hh, jnp.exp2(s_scr[slot, hh] - m)))
        emit(i, accs)

    pending = None
    for i in range(n_q):
        slot = i % 2
        load_q(i, slot)
        s_diag = logits(slot, i)
        s_first = logits(slot, 0) if i else None
        if pending is not None:
            finish(*pending)
        m_diag = []
        for hh in heads:
            s = jnp.where(causal, s_diag[hh], NEG)
            m = _col_max(s)
            acc_scr[hh] = pv(i, hh, jnp.exp2(s - m))
            m_diag.append(m)
        if i == 0:
            emit(0, [acc_scr[hh] for hh in heads])
            continue
        carry = []
        for hh in heads:
            s_scr[slot, hh] = s_first[hh]
            m_new = jnp.maximum(m_diag[hh], _col_max(s_first[hh]))
            carry.append((m_new, jnp.exp2(m_diag[hh] - m_new)))

        def body(j, carry, slot=slot):
            s_next, pvs = [], []
            for hh in heads:
                s_next.append(_dot(key_tile(j + 1, hh), q_scr[slot, hh]))
                pvs.append(pv(j, hh, jnp.exp2(s_scr[slot, hh] - carry[hh][0])))
            out = []
            for hh in heads:
                m = carry[hh][0]
                s_scr[slot, hh] = s_next[hh]
                m_new = jnp.maximum(m, _col_max(s_next[hh]))
                out.append((m_new, jnp.exp2(m - m_new)))
            for hh in heads:
                acc_scr[hh] = carry[hh][1] * acc_scr[hh] + pvs[hh]
            return tuple(out)

        carry = lax.fori_loop(0, i - 1, body, tuple(carry))
        pending = (i, slot, carry)
    if pending is not None:
        finish(*pending)


def _attn_scratch(nh, k_depth, t):
    return [pltpu.VMEM((2, nh, k_depth, t), BF16), pltpu.VMEM((2, nh, t, t), F32),
            pltpu.VMEM((nh, HEAD_DIM + ONES_ROWS, t), F32)]


def _fox_attn_kernel(q_ref, qx_ref, k_ref, kx_ref, v_ref, z_ref, o_ref, q_scr, s_scr, acc_scr, *, t):
    nh = q_scr.shape[1]
    n_q = q_ref.shape[2] // t
    row = _iota((LANES, t), 0)

    def load_q(i, slot):
        for hh in range(nh):
            pair, sub = divmod(hh, 2)
            qt = q_ref[0, pair * LANES:(pair + 1) * LANES, i * t:(i + 1) * t]
            own = jnp.logical_and(row >= sub * HEAD_DIM, row < (sub + 1) * HEAD_DIM)
            q_scr[slot, hh] = jnp.concatenate(
                [jnp.where(own, qt, jnp.zeros_like(qt)), qx_ref[hh]], axis=0)

    def key_tile(j, hh):
        pair = hh // 2
        return jnp.concatenate([k_ref[0, _tile(j, t), pair * LANES:(pair + 1) * LANES],
                                kx_ref[0, _tile(j, t), :]], axis=1)

    def value_rows(j, hh):
        return v_ref[0, j, hh * HEAD_DIM:(hh + 1) * HEAD_DIM, :]

    def emit(i, accs):
        for hh in range(nh):
            rows, cols = slice(hh * HEAD_DIM, (hh + 1) * HEAD_DIM), slice(i * t, (i + 1) * t)
            gated = _normalised(accs[hh]) * _silu(z_ref[0, rows, cols].astype(F32))
            o_ref[0, rows, cols] = gated.astype(o_ref.dtype)

    _causal_attention(n_q, t, load_q, key_tile, value_rows, emit, q_scr, s_scr, acc_scr)


def _fox_attn(qt, qx, k, kx, vt, zt):
    B, W, S = qt.shape
    t = min(ATTN_TILE, S)
    nh = HEADS_PER_STEP
    rows = nh * HEAD_DIM
    feat = pl.BlockSpec((1, rows, S), lambda b, p: (b, p, 0))
    return pl.pallas_call(
        functools.partial(_fox_attn_kernel, t=t),
        name="fox_attn",
        grid=(B, N_HEADS // nh),
        in_specs=[
            feat,
            pl.BlockSpec((nh, LANES, t), lambda b, p: (p, 0, 0)),
            pl.BlockSpec((1, S, rows), lambda b, p: (b, 0, p)),
            pl.BlockSpec((1, S, LANES), lambda b, p: (b, 0, 0)),
            pl.BlockSpec((1, S // t, rows, t), lambda b, p: (b, 0, p, 0)),
            feat,
        ],
        out_specs=feat,
        out_shape=jax.ShapeDtypeStruct((B, W, S), BF16),
        scratch_shapes=_attn_scratch(nh, 2 * LANES, t),
        compiler_params=_params("parallel", "parallel"),
    )(qt, qx, k, kx, vt, zt)


def _out_proj_kernel(*refs, final_norm):
    if final_norm:
        ot_ref, x_ref, w_ref, g_ref, y_ref = refs
    else:
        ot_ref, x_ref, w_ref, y_ref = refs
    y = x_ref[0] + _dot_tn(ot_ref[0], w_ref[...])
    y_ref[0] = _rmsnorm(y, g_ref[...]) if final_norm else y


def _out_proj(ot, x, w_out, final_g=None):
    B, S, _ = x.shape
    tm = min(OUT_ROWS, S)
    const = lambda b, s: (0, 0)
    tok = pl.BlockSpec((1, tm, D_MODEL), lambda b, s: (b, s, 0))
    in_specs = [pl.BlockSpec((1, ATTN_WIDTH, tm), lambda b, s: (b, 0, s)), tok,
                pl.BlockSpec((ATTN_WIDTH, D_MODEL), const)]
    args = [ot, x, w_out]
    if final_g is not None:
        in_specs.append(pl.BlockSpec((1, D_MODEL), const))
        args.append(final_g)
    return pl.pallas_call(
        functools.partial(_out_proj_kernel, final_norm=final_g is not None),
        name="out_proj",
        grid=(B, S // tm),
        in_specs=in_specs,
        out_specs=tok,
        out_shape=jax.ShapeDtypeStruct((B, S, D_MODEL), F32),
        compiler_params=_params("parallel", "parallel"),
    )(*args)


_TOK_SEL = 0
_TOK_WIN = KV_WIDTH
_TOK_CMP = 2 * KV_WIDTH
_TOK_COLS = 4 * KV_WIDTH
_FEAT_Q = 0
_FEAT_VSEL = ATTN_WIDTH
_FEAT_VWIN = _FEAT_VSEL + KV_WIDTH
_FEAT_Z = _FEAT_VWIN + KV_WIDTH
_FEAT_GATE = _FEAT_Z + ATTN_WIDTH
_FEAT_ROWS = _FEAT_GATE + N_BRANCH * N_HEADS


def _nsa_in_kernel(ot_ref, wo_ref, x_ref, g_ref, wk_ref, wt_ref, cos_ref, slo_ref, shi_ref,
                   cost_ref, sint_ref,
                   x1_ref, qt_ref, ksel_ref, kwin_ref, chk_ref, chv_ref, vsel_ref, vwin_ref, zt_ref,
                   gt_ref, cmp_scr, *, tk):
    s = pl.program_id(1)
    x1 = x_ref[0] + _dot_tn(ot_ref[0], wo_ref[...])
    x1_ref[0] = x1
    hb = _rmsnorm(x1, g_ref[...]).astype(BF16)
    tm = hb.shape[0]
    G = NSA_GROUPS

    cos, slo, shi = cos_ref[...], slo_ref[...], shi_ref[...]
    tok = s * tm + _iota((tm, LANES), 0)
    lane = _iota((tm, LANES), 1)
    low = lane < HEAD_DIM
    block_id = jnp.where(lane - HEAD_DIM == tok // SEL_LEN, 1.0, 0.0)
    keys = _dot(hb, wk_ref[:, _TOK_SEL:_TOK_CMP])
    for col0, out_ref, extra in ((_TOK_SEL, ksel_ref, block_id), (_TOK_WIN, kwin_ref, 0.0)):
        for gp in range(G // 2):
            pair = _rope_lanes(keys[:, col0 + gp * LANES:col0 + (gp + 1) * LANES], cos, slo, shi)
            out_ref[0, 2 * gp] = jnp.where(low, pair, extra).astype(BF16)
            out_ref[0, 2 * gp + 1] = jnp.where(
                low, pltpu.roll(pair, HEAD_DIM, axis=1), extra).astype(BF16)

    raw = _dot(hb, wk_ref[:, _TOK_CMP:_TOK_COLS])
    for c in range(cmp_scr.shape[0]):
        cmp_scr[c] = raw[:, c * LANES:(c + 1) * LANES]

    cost, sint = cost_ref[...], sint_ref[...]
    for r0 in range(0, ATTN_WIDTH, 512):
        t = _dot_nt(wt_ref[_FEAT_Q + r0:_FEAT_Q + r0 + 512, :], hb)
        for h0 in range(0, 512, HEAD_DIM):
            qt_ref[0, r0 + h0:r0 + h0 + HEAD_DIM, :] = _rope_rows(
                t[h0:h0 + HEAD_DIM], cost, sint).astype(BF16)
        zt_ref[0, r0:r0 + 512, :] = _dot_nt(wt_ref[_FEAT_Z + r0:_FEAT_Z + r0 + 512, :], hb).astype(BF16)
    for row0, out_ref in ((_FEAT_VSEL, vsel_ref), (_FEAT_VWIN, vwin_ref)):
        t = _dot_nt(wt_ref[row0:row0 + KV_WIDTH, :], hb).astype(BF16)
        for c in range(tm // tk):
            out_ref[0, c] = t[:, c * tk:(c + 1) * tk]
    gate = _dot_nt(wt_ref[_FEAT_GATE:_FEAT_ROWS, :], hb)
    gate = 1.0 / (1.0 + jnp.exp(-gate))
    rows = N_BRANCH * HEADS_PER_STEP
    for hg in range(N_HEADS // HEADS_PER_STEP):
        gt_ref[0, hg] = _pad_rows(gate[hg * rows:(hg + 1) * rows], GATE_ROWS)

    n_ch = tm // CMP_STRIDE
    low_ch = _iota((n_ch, LANES), 1) < HEAD_DIM
    for kind, out_ref in enumerate((chk_ref, chv_ref)):
        for gp in range(G // 2):
            c = kind * (G // 2) + gp
            for l in range(0, CMP_STRIDE, 2):
                a0 = cmp_scr[c, pl.ds(l, n_ch, stride=CMP_STRIDE), :]
                a1 = cmp_scr[c, pl.ds(l + 1, n_ch, stride=CMP_STRIDE), :]
                dst = slice(l * HEAD_DIM, (l + 2) * HEAD_DIM)
                out_ref[0, 2 * gp, :, dst] = jnp.where(
                    low_ch, a0, pltpu.roll(a1, HEAD_DIM, axis=1)).astype(BF16)
                out_ref[0, 2 * gp + 1, :, dst] = jnp.where(
                    low_ch, pltpu.roll(a0, HEAD_DIM, axis=1), a1).astype(BF16)


def _nsa_in(ot, w_out, x, g, w_k, w_t, lane_tables, row_tables):
    B, S, _ = x.shape
    tm = min(PROJ_ROWS, S)
    tk = min(ATTN_TILE, S)
    W, G = ATTN_WIDTH, NSA_GROUPS
    const = lambda b, s: (0, 0)
    feat = lambda rows: pl.BlockSpec((1, rows, tm), lambda b, s: (b, 0, s))
    tiles = pl.BlockSpec((1, tm // tk, KV_WIDTH, tk), lambda b, s: (b, s, 0, 0))
    keys = pl.BlockSpec((1, G, tm, LANES), lambda b, s: (b, 0, s, 0))
    chunk_w = CMP_STRIDE * HEAD_DIM
    chunk = pl.BlockSpec((1, G, tm // CMP_STRIDE, chunk_w), lambda b, s: (b, 0, s, 0))
    tok = pl.BlockSpec((1, tm, D_MODEL), lambda b, s: (b, s, 0))
    n_hg = N_HEADS // HEADS_PER_STEP
    return pl.pallas_call(
        functools.partial(_nsa_in_kernel, tk=tk),
        name="nsa_in",
        grid=(B, S // tm),
        in_specs=[
            feat(W),
            pl.BlockSpec(w_out.shape, const),
            tok,
            pl.BlockSpec((1, D_MODEL), const),
            pl.BlockSpec(w_k.shape, const),
            pl.BlockSpec(w_t.shape, const),
        ] + [pl.BlockSpec((tm, LANES), lambda b, s: (s, 0))] * 3
          + [pl.BlockSpec((ROPE_HALF, tm), lambda b, s: (0, s))] * 2,
        out_specs=[tok, feat(W), keys, keys, chunk, chunk, tiles, tiles, feat(W),
                   pl.BlockSpec((1, n_hg, GATE_ROWS, tm), lambda b, s: (b, 0, 0, s))],
        out_shape=[
            jax.ShapeDtypeStruct((B, S, D_MODEL), F32),
            jax.ShapeDtypeStruct((B, W, S), BF16),
            jax.ShapeDtypeStruct((B, G, S, LANES), BF16),
            jax.ShapeDtypeStruct((B, G, S, LANES), BF16),
            jax.ShapeDtypeStruct((B, G, S // CMP_STRIDE, chunk_w), BF16),
            jax.ShapeDtypeStruct((B, G, S // CMP_STRIDE, chunk_w), BF16),
            jax.ShapeDtypeStruct((B, S // tk, KV_WIDTH, tk), BF16),
            jax.ShapeDtypeStruct((B, S // tk, KV_WIDTH, tk), BF16),
            jax.ShapeDtypeStruct((B, W, S), BF16),
            jax.ShapeDtypeStruct((B, n_hg, GATE_ROWS, S), F32),
        ],
        scratch_shapes=[pltpu.VMEM((2 * KV_WIDTH // LANES, tm, LANES), F32)],
        compiler_params=_params("parallel", "parallel"),
    )(ot, w_out, x, g, w_k, w_t, *lane_tables, *row_tables)


def _compress_kernel(chk_ref, chv_ref, w1k_ref, w2k_ref, pek_ref, w1v_ref, w2vt_ref, pev_ref,
                     cos_ref, slo_ref, shi_ref, kc_ref, vct_ref):
    half = CMP_STRIDE * HEAD_DIM

    def hidden(ch, w1_ref, pe_ref):
        a = _dot(ch, w1_ref[:half, :])
        b = _dot(ch, w1_ref[half:, :])
        b = pltpu.roll(b, b.shape[0] - 1, axis=0)
        pe = _dot(pe_ref[...], w1_ref[...])[0:1, :]
        return _silu(a + b + pe).astype(BF16)

    kc = _dot(hidden(chk_ref[0, 0], w1k_ref, pek_ref), w2k_ref[...])
    kc_ref[0, 0] = _rope_lanes(kc, cos_ref[...], slo_ref[...], shi_ref[...]).astype(BF16)
    vct_ref[0, 0] = _dot_nt(w2vt_ref[...], hidden(chv_ref[0, 0], w1v_ref, pev_ref)).astype(BF16)


def _compress(chk, chv, w1k, w2k, pek, w1v, w2vt, pev, tables):
    B, G, n_chunk, width = chk.shape
    const = lambda b, g: (0, 0)
    chunk_spec = pl.BlockSpec((1, 1, n_chunk, width), lambda b, g: (b, g, 0, 0))
    specs = lambda ws: [pl.BlockSpec(w.shape, const) for w in ws]
    return pl.pallas_call(
        _compress_kernel,
        name="nsa_compress",
        grid=(B, G),
        in_specs=[chunk_spec, chunk_spec] + specs((w1k, w2k, pek, w1v, w2vt, pev))
                 + [pl.BlockSpec((n_chunk, LANES), const)] * 3,
        out_specs=[pl.BlockSpec((1, 1, n_chunk, LANES), lambda b, g: (b, g, 0, 0)),
                   pl.BlockSpec((1, 1, HEAD_DIM, n_chunk), lambda b, g: (b, g, 0, 0))],
        out_shape=[jax.ShapeDtypeStruct((B, G, n_chunk, LANES), BF16),
                   jax.ShapeDtypeStruct((B, G, HEAD_DIM, n_chunk), BF16)],
        compiler_params=_params("parallel", "parallel"),
    )(chk, chv, w1k, w2k, pek, w1v, w2vt, pev, *tables)


def _local_attn_kernel(q_ref, kc_ref, vct_ref, ovt_ref, kw_ref, vw_ref, oc_ref, bias_ref, ow_ref,
                       *, t, n_cmp, n_blk, n_back):
    n_q = q_ref.shape[2] // t
    heads = range(NSA_REP)
    kc = kc_ref[0, 0]
    vct = vct_ref[0, 0]
    ovt = ovt_ref[...]
    n_pad = kc.shape[0]
    above = _iota((t, t), 0) > _iota((t, t), 1)
    ones = jnp.ones((ONES_ROWS, t), BF16)
    blk_c = _iota((n_pad, t), 0)
    blk = _iota((n_blk, t), 0)
    n_rounds = min(N_SELECT, n_blk)
    per_head = -(-n_rounds // NSA_REP)

    def q_aug(i, r):
        return _pad_rows(q_ref[0, r * HEAD_DIM:(r + 1) * HEAD_DIM, i * t:(i + 1) * t], LANES)

    masks = {0: jnp.logical_not(above), n_back: above}

    def win_tiles(i):
        return [(i - d, masks.get(d)) for d in range(n_back + 1) if i - d >= 0]

    def win_logits(i, r):
        return [_dot(kw_ref[0, 0, _tile(j, t), :], q_aug(i, r)) for j, _ in win_tiles(i)]

    def cmp_logits(i):
        return [_dot(kc, q_aug(i, r)) for r in heads]

    def importance(i, logits):
        cols = slice(i * t, (i + 1) * t)
        qry = i * t + _iota((n_pad, t), 1)
        valid = jnp.logical_and(blk_c * CMP_STRIDE + (CMP_LEN - 1) <= qry, blk_c < n_cmp)
        p_sum = jnp.zeros((n_pad, t), F32)
        for r in heads:
            s = jnp.where(valid, logits[r], NEG)
            e = jnp.exp2(s - _col_max(s))
            p = jnp.where(valid, e * (1.0 / _col_sum(e)), 0.0)
            p_sum = p_sum + p
            oc_ref[0, r * HEAD_DIM:(r + 1) * HEAD_DIM, cols] = _dot(
                vct, p.astype(BF16)).astype(oc_ref.dtype)
        p_hi, p_lo = _split2(p_sum)
        imp = _dot(ovt, p_hi) + _dot(ovt, p_lo)
        cur = (i * t + _iota(imp.shape, 1)) // SEL_LEN
        forced = jnp.logical_or(blk == 0, jnp.logical_or(blk == cur, blk == cur - 1))
        return jnp.where(forced, FORCE, jnp.where(blk <= cur, imp, -1.0))

    items = [(i, r) for i in range(n_q) for r in heads]
    win_queue, cmp_queue = [], {}

    def issue(n):
        if n < len(items):
            win_queue.append(win_logits(*items[n]))
            if items[n][1] == 0:
                cmp_queue[items[n][0]] = cmp_logits(items[n][0])

    for n in range(LOOKAHEAD):
        issue(n)
    for n, (i, r) in enumerate(items):
        issue(n + LOOKAHEAD)
        logits = win_queue.pop(0)
        if r == 0:
            imp = importance(i, cmp_queue.pop(i))
            keep = jnp.zeros(imp.shape, jnp.bool_)
            done = 0
        ss = [s if mask is None else jnp.where(mask, s, NEG)
              for s, (_, mask) in zip(logits, win_tiles(i))]
        m = functools.reduce(jnp.maximum, [_col_max(s) for s in ss])
        acc = functools.reduce(jnp.add, [
            _dot(jnp.concatenate([vw_ref[0, j], ones], axis=0), jnp.exp2(s - m).astype(BF16))
            for s, (j, _) in zip(ss, win_tiles(i))])
        ow_ref[0, r * HEAD_DIM:(r + 1) * HEAD_DIM, i * t:(i + 1) * t] = _normalised(acc).astype(
            ow_ref.dtype)
        for _ in range(min(per_head, n_rounds - done)):
            top = _col_max(imp)
            first = jnp.min(jnp.where(imp == top, blk, n_blk), axis=0, keepdims=True)
            pick = blk == first
            keep = jnp.logical_or(keep, pick)
            imp = jnp.where(pick, -jnp.inf, imp)
            done += 1
        if r == NSA_REP - 1:
            bias_ref[0, 0, :, i * t:(i + 1) * t] = jnp.where(keep, 0.0, NEG).astype(bias_ref.dtype)


def _local_attn(qt, kc, vct, ovt, n_cmp, kw, vw):
    B, W, S = qt.shape
    G = kc.shape[1]
    t = min(ATTN_TILE, S)
    assert WINDOW % t == 0
    n_blk = ovt.shape[0]
    kern = functools.partial(_local_attn_kernel, t=t, n_cmp=n_cmp, n_blk=n_blk, n_back=WINDOW // t)
    feat = pl.BlockSpec((1, W // G, S), lambda b, g: (b, g, 0))
    return pl.pallas_call(
        kern,
        name="nsa_local_attn",
        grid=(B, G),
        in_specs=[
            feat,
            pl.BlockSpec((1, 1) + kc.shape[2:], lambda b, g: (b, g, 0, 0)),
            pl.BlockSpec((1, 1) + vct.shape[2:], lambda b, g: (b, g, 0, 0)),
            pl.BlockSpec(ovt.shape, lambda b, g: (0, 0)),
            pl.BlockSpec((1, 1, S, LANES), lambda b, g: (b, g, 0, 0)),
            pl.BlockSpec((1, S // t, HEAD_DIM, t), lambda b, g: (b, 0, g, 0)),
        ],
        out_specs=[feat, pl.BlockSpec((1, 1, n_blk, S), lambda b, g: (b, g, 0, 0)), feat],
        out_shape=[
            jax.ShapeDtypeStruct((B, W, S), BF16),
            jax.ShapeDtypeStruct((B, G, n_blk, S), BF16),
            jax.ShapeDtypeStruct((B, W, S), BF16),
        ],
        compiler_params=_params("parallel", "parallel"),
    )(qt, kc, vct, ovt, kw, vw)


def _sel_attn_kernel(q_ref, bias_ref, k_ref, v_ref, oc_ref, ow_ref, gt_ref, z_ref, o_ref,
                     q_scr, s_scr, acc_scr, *, t):
    nh = q_scr.shape[1]
    n_q = q_ref.shape[2] // t

    def load_q(i, slot):
        cols = slice(i * t, (i + 1) * t)
        for hh in range(nh):
            bias = _pad_rows(bias_ref[0, hh // NSA_REP, :, cols], HEAD_DIM)
            q_scr[slot, hh] = jnp.concatenate(
                [q_ref[0, hh * HEAD_DIM:(hh + 1) * HEAD_DIM, cols], bias], axis=0)

    def key_tile(j, hh):
        return k_ref[0, hh // NSA_REP, _tile(j, t), :]

    def value_rows(j, hh):
        g = hh // NSA_REP
        return v_ref[0, j, g * HEAD_DIM:(g + 1) * HEAD_DIM, :]

    def emit(i, accs):
        cols = slice(i * t, (i + 1) * t)
        for hh in range(nh):
            rows = slice(hh * HEAD_DIM, (hh + 1) * HEAD_DIM)
            gate = lambda b: gt_ref[0, 0, N_BRANCH * hh + b:N_BRANCH * hh + b + 1, cols]
            o = (gate(0) * oc_ref[0, rows, cols].astype(F32) + gate(1) * _normalised(accs[hh])
                 + gate(2) * ow_ref[0, rows, cols].astype(F32))
            o_ref[0, rows, cols] = (o * _silu(z_ref[0, rows, cols].astype(F32))).astype(o_ref.dtype)

    _causal_attention(n_q, t, load_q, key_tile, value_rows, emit, q_scr, s_scr, acc_scr)


def _sel_attn(qt, bias, k_aug, vt, oc, ow, gt, zt):
    B, W, S = qt.shape
    t = min(ATTN_TILE, S)
    ng = HEADS_PER_STEP // NSA_REP
    n_blk = bias.shape[2]
    feat = pl.BlockSpec((1, HEADS_PER_STEP * HEAD_DIM, S), lambda b, g: (b, g, 0))
    return pl.pallas_call(
        functools.partial(_sel_attn_kernel, t=t),
        name="nsa_sel_attn",
        grid=(B, NSA_GROUPS // ng),
        in_specs=[
            feat,
            pl.BlockSpec((1, ng, n_blk, S), lambda b, g: (b, g, 0, 0)),
            pl.BlockSpec((1, ng, S, LANES), lambda b, g: (b, g, 0, 0)),
            pl.BlockSpec((1, S // t, ng * HEAD_DIM, t), lambda b, g: (b, 0, g, 0)),
            feat, feat,
            pl.BlockSpec((1, 1, GATE_ROWS, S), lambda b, g: (b, g, 0, 0)),
            feat,
        ],
        out_specs=feat,
        out_shape=jax.ShapeDtypeStruct((B, W, S), BF16),
        scratch_shapes=_attn_scratch(HEADS_PER_STEP, LANES, t),
        compiler_params=_params("parallel", "parallel"),
    )(qt, bias, k_aug, vt, oc, ow, gt, zt)


def _fox_mixer(x, g, w_in, b_f):
    B, S, _ = x.shape
    W, H = ATTN_WIDTH, N_HEADS
    w_k = w_in[:, W:2 * W].astype(BF16)
    w_t = jnp.concatenate([w_in[:, :W] * Q_SCALE, w_in[:, 2 * W:3 * W], w_in[:, 3 * W + H:]],
                          axis=1).T.astype(BF16)
    wf_hi, wf_lo = _split2(w_in[:, 3 * W:3 * W + H])
    pad = lambda n: jnp.zeros((D_MODEL, LANES - n * H), BF16)
    wf_a = jnp.concatenate([wf_hi] * 3 + [wf_lo] * 3 + [pad(6)], axis=1)
    wf_b = jnp.concatenate([wf_hi] * 3 + [pad(3)], axis=1)
    bf_row = jnp.concatenate([b_f, b_f, b_f, jnp.zeros((LANES - 3 * H,), F32)])[None, :]

    qt, k, vt, zt, cx = _fox_in(x, g[None, :], w_k, w_t, wf_a, wf_b, bf_row)

    t = min(ATTN_TILE, S)
    r = np.arange(LANES)
    sel = np.logical_and(r[None, :] % H == np.arange(H)[:, None], r[None, :] < 3 * H)
    qx = np.broadcast_to(np.where(sel, -1.0, 0.0).astype(np.float32)[:, :, None], (H, LANES, t))
    return _fox_attn(qt, jnp.asarray(qx, BF16), k, cx, vt, zt)


def _nsa_layer(ot0, w_out0, x, g, w_in, pe_k, w_ck1, w_ck2, pe_v, w_cv1, w_cv2, w_out, final_g):
    B, S, _ = x.shape
    W, KV, G = ATTN_WIDTH, KV_WIDTH, NSA_GROUPS
    sec = lambda j: w_in[:, W + j * KV:W + (j + 1) * KV]
    gate_off = W + 6 * KV
    n_gate = N_BRANCH * N_HEADS
    w_k = jnp.concatenate([sec(2), sec(4), sec(0), sec(1)], axis=1).astype(BF16)
    w_t = jnp.concatenate([w_in[:, :W] * Q_SCALE, sec(3), sec(5), w_in[:, gate_off + n_gate:],
                           w_in[:, gate_off:gate_off + n_gate]], axis=1).T.astype(BF16)
    pos = np.arange(S)
    cos, sin = _rope_angles(pos)
    x, qt, ksel, kwin, chk, chv, vsel, vwin, zt, gt = _nsa_in(
        ot0, w_out0.astype(BF16), x, g[None, :], w_k, w_t, _rope_lane_tables(pos),
        (np.ascontiguousarray(cos.T), np.ascontiguousarray(sin.T)))

    n_chunk = S // CMP_STRIDE
    n_cmp = n_chunk - CMP_LEN // CMP_STRIDE + 1

    def flat_pe(pe):
        pe = pe.reshape(1, CMP_LEN * HEAD_DIM)
        return jnp.broadcast_to(pe, (SUBLANES, CMP_LEN * HEAD_DIM)).astype(BF16)

    w2k = jnp.concatenate([w_ck2, jnp.zeros_like(w_ck2)], axis=1).astype(BF16)
    cmp_end = np.arange(n_chunk) * CMP_STRIDE + CMP_LEN - 1
    kc, vct = _compress(chk, chv, w_ck1.astype(BF16), w2k, flat_pe(pe_k),
                        w_cv1.astype(BF16), w_cv2.T.astype(BF16), flat_pe(pe_v),
                        _rope_lane_tables(cmp_end))

    n_blk = S // SEL_LEN
    ci = np.arange(n_chunk) * CMP_STRIDE
    sj = np.arange(n_blk) * SEL_LEN
    ovt = np.logical_and(ci[None, :] < sj[:, None] + SEL_LEN, ci[None, :] + CMP_LEN > sj[:, None])
    ovt = np.logical_and(ovt, np.arange(n_chunk)[None, :] < n_cmp).astype(np.float32)
    oc, bias, ow = _local_attn(qt, kc, vct, jnp.asarray(ovt, BF16), n_cmp, kwin, vwin)
    ot = _sel_attn(qt, bias, ksel, vsel, oc, ow, gt, zt)
    return _out_proj(ot, x, w_out.astype(BF16), final_g[None, :])


def kernel(x, norm_g, fox_w_in, fox_b_f, fox_w_out, nsa_w_in, nsa_pe_k, nsa_w_ck1, nsa_w_ck2,
           nsa_pe_v, nsa_w_cv1, nsa_w_cv2, nsa_w_out, final_g):
    ot0 = _fox_mixer(x, norm_g[0], fox_w_in[0], fox_b_f[0])
    return _nsa_layer(ot0, fox_w_out[0], x, norm_g[1], nsa_w_in[0], nsa_pe_k[0], nsa_w_ck1[0],
                      nsa_w_ck2[0], nsa_pe_v[0], nsa_w_cv1[0], nsa_w_cv2[0], nsa_w_out[0], final_g)
```

```python
import functools
import math

import jax
import jax.numpy as jnp
import numpy as np
from jax import lax
from jax.experimental import pallas as pl
from jax.experimental.pallas import tpu as pltpu

F32 = jnp.float32
BF16 = jnp.bfloat16

D_MODEL = 1024
N_HEADS = 16
HEAD_DIM = 64
ATTN_WIDTH = N_HEADS * HEAD_DIM
NSA_GROUPS = 4
NSA_REP = N_HEADS // NSA_GROUPS
KV_WIDTH = NSA_GROUPS * HEAD_DIM
CMP_LEN = 32
CMP_STRIDE = 16
CMP_HIDDEN = 256
SEL_LEN = 64
N_SELECT = 8
WINDOW = 512
N_BRANCH = 3
ROPE_THETA = 500000.0
ROPE_DIM = HEAD_DIM // 4
ROPE_HALF = ROPE_DIM // 2
NORM_EPS = 1e-6
NEG = -1e30
FORCE = 1e6
LOG2E = math.log2(math.e)
Q_SCALE = HEAD_DIM ** -0.5 * LOG2E

LANES = 128
SUBLANES = 8
ONES_ROWS = 2 * SUBLANES
PROJ_ROWS = 512
OUT_ROWS = 1024
ATTN_TILE = 256
HEADS_PER_STEP = 8
LOOKAHEAD = 1
GATE_ROWS = -(-N_BRANCH * HEADS_PER_STEP // SUBLANES) * SUBLANES
VMEM_LIMIT = 56 * 1024 * 1024


def _params(*sem):
    return pltpu.CompilerParams(dimension_semantics=sem, vmem_limit_bytes=VMEM_LIMIT)


def _iota(shape, dim):
    return lax.broadcasted_iota(jnp.int32, shape, dim)


def _split2(x):
    hi = x.astype(BF16)
    lo = (x - hi.astype(F32)).astype(BF16)
    return hi, lo


def _split3(x):
    hi = x.astype(BF16)
    r1 = x - hi.astype(F32)
    mid = r1.astype(BF16)
    lo = (r1 - mid.astype(F32)).astype(BF16)
    return hi, mid, lo


def _dot(a, b):
    return jnp.dot(a, b, preferred_element_type=F32)


def _dot_nt(a, b):
    return lax.dot_general(a, b, (((1,), (1,)), ((), ())), preferred_element_type=F32)


def _dot_tn(a, b):
    return lax.dot_general(a, b, (((0,), (0,)), ((), ())), preferred_element_type=F32)


def _rmsnorm(x, g):
    ms = jnp.mean(x * x, axis=-1, keepdims=True)
    return x * lax.rsqrt(ms + NORM_EPS) * g


def _silu(x):
    return x * (1.0 / (1.0 + jnp.exp(-x)))


def _col_reduce(x, op, reduce):
    rows = x.shape[0]
    if rows % (4 * SUBLANES) == 0 and rows >= 8 * SUBLANES:
        q = rows // 4
        x = op(op(x[:q], x[q:2 * q]), op(x[2 * q:3 * q], x[3 * q:]))
    return reduce(x, axis=0, keepdims=True)


def _col_max(x):
    return _col_reduce(x, jnp.maximum, jnp.max)


def _col_sum(x):
    return _col_reduce(x, jnp.add, jnp.sum)


def _pad_rows(x, rows):
    if rows == x.shape[0]:
        return x
    return jnp.concatenate([x, jnp.zeros((rows - x.shape[0], x.shape[1]), x.dtype)], axis=0)


def _tile(j, t):
    return pl.ds(j * t if isinstance(j, int) else pl.multiple_of(j * t, t), t)


def _normalised(acc):
    return acc[:HEAD_DIM] * (1.0 / acc[HEAD_DIM:HEAD_DIM + 1])


def _rope_lanes(x, cos, sin_lo, sin_hi):
    return (x * cos + pltpu.roll(x, LANES - ROPE_HALF, axis=1) * sin_lo
            + pltpu.roll(x, ROPE_HALF, axis=1) * sin_hi)


def _rope_rows(x, cos, sin):
    x1, x2 = x[:ROPE_HALF], x[ROPE_HALF:ROPE_DIM]
    return jnp.concatenate([x1 * cos - x2 * sin, x1 * sin + x2 * cos, x[ROPE_DIM:]], axis=0)


def _rope_angles(pos):
    inv_freq = np.power(np.float32(ROPE_THETA),
                        -np.arange(ROPE_HALF, dtype=np.float32) * np.float32(2.0 / ROPE_DIM))
    ang = pos.astype(np.float32)[:, None] * inv_freq[None, :].astype(np.float32)
    return np.cos(ang).astype(np.float32), np.sin(ang).astype(np.float32)


def _rope_lane_tables(pos):
    cos, sin = _rope_angles(pos)
    n = pos.shape[0]
    ones = np.ones((n, HEAD_DIM - ROPE_DIM), np.float32)
    zeros = np.zeros((n, HEAD_DIM - ROPE_DIM), np.float32)
    z8 = np.zeros((n, ROPE_HALF), np.float32)
    c = np.concatenate([cos, cos, ones], axis=1)
    s_lo = np.concatenate([-sin, z8, zeros], axis=1)
    s_hi = np.concatenate([z8, sin, zeros], axis=1)
    tile = lambda t: np.concatenate([t, t], axis=1)
    return tile(c), tile(s_lo), tile(s_hi)


def _fox_in_kernel(x_ref, g_ref, wk_ref, wt_ref, wfa_ref, wfb_ref, bf_ref, tri_ref,
                   qt_ref, k_ref, vt_ref, zt_ref, cx_ref, carry_ref, *, tk):
    s = pl.program_id(1)
    h = _rmsnorm(x_ref[0], g_ref[...])
    hb = h.astype(BF16)
    tm = hb.shape[0]
    W, H = ATTN_WIDTH, N_HEADS

    @pl.when(s == 0)
    def _():
        carry_ref[...] = jnp.zeros_like(carry_ref)

    h_lo = (h - hb.astype(F32)).astype(BF16)
    fa = _dot(hb, wfa_ref[...])
    f = fa + pltpu.roll(fa, LANES - 3 * H, axis=1) + _dot(h_lo, wfb_ref[...]) + bf_ref[...]

    for n0 in range(0, W, 512):
        k_ref[0, :, n0:n0 + 512] = _dot(hb, wk_ref[:, n0:n0 + 512]).astype(BF16)

    log_f = jnp.minimum(f, 0.0) - jnp.log1p(jnp.exp(-jnp.abs(f)))
    lane = _iota(log_f.shape, 1)
    a0, a1, a2 = _split3(log_f)
    pieces = jnp.where(lane < H, a0, jnp.where(lane < 2 * H, a1, a2))
    c = _dot(tri_ref[...], pieces)

    for r0 in list(range(2 * W, 3 * W, 512)) + list(range(0, 2 * W, 512)):
        t = _dot_nt(wt_ref[r0:r0 + 512, :], hb)
        if r0 < W:
            qt_ref[0, r0:r0 + 512, :] = t.astype(BF16)
        elif r0 < 2 * W:
            t = t.astype(BF16)
            for ct in range(tm // tk):
                vt_ref[0, ct, r0 - W:r0 - W + 512, :] = t[:, ct * tk:(ct + 1) * tk]
        else:
            zt_ref[0, r0 - 2 * W:r0 - 2 * W + 512, :] = _silu(t).astype(BF16)

    c = c + pltpu.roll(c, LANES - H, axis=1) + pltpu.roll(c, LANES - 2 * H, axis=1)
    c = c + carry_ref[0:1, :]
    carry_ref[...] = jnp.broadcast_to(c[tm - 1:tm, :], carry_ref.shape)
    c = jnp.where(lane < H, c, jnp.where(lane < 2 * H, pltpu.roll(c, H, axis=1),
                                         pltpu.roll(c, 2 * H, axis=1)))

    hi, mid, lo = _split3(c * LOG2E)
    zero = jnp.zeros_like(hi)
    cx_ref[0] = jnp.where(lane < H, hi,
                          jnp.where(lane < 2 * H, mid, jnp.where(lane < 3 * H, lo, zero)))


def _fox_in(x, g, w_k, w_t, wf_a, wf_b, bf_row):
    B, S, _ = x.shape
    tm = min(PROJ_ROWS, S)
    tk = min(ATTN_TILE, S)
    W = ATTN_WIDTH
    tri = np.tril(np.ones((tm, tm), np.float32)).astype(BF16)
    const = lambda b, s: (0, 0)
    feat = pl.BlockSpec((1, W, tm), lambda b, s: (b, 0, s))
    return pl.pallas_call(
        functools.partial(_fox_in_kernel, tk=tk),
        name="fox_in",
        grid=(B, S // tm),
        in_specs=[
            pl.BlockSpec((1, tm, D_MODEL), lambda b, s: (b, s, 0)),
            pl.BlockSpec((1, D_MODEL), const),
            pl.BlockSpec(w_k.shape, const),
            pl.BlockSpec(w_t.shape, const),
            pl.BlockSpec((D_MODEL, LANES), const),
            pl.BlockSpec((D_MODEL, LANES), const),
            pl.BlockSpec((1, LANES), const),
            pl.BlockSpec((tm, tm), const),
        ],
        out_specs=[
            feat,
            pl.BlockSpec((1, tm, W), lambda b, s: (b, s, 0)),
            pl.BlockSpec((1, tm // tk, W, tk), lambda b, s: (b, s, 0, 0)),
            feat,
            pl.BlockSpec((1, tm, LANES), lambda b, s: (b, s, 0)),
        ],
        out_shape=[
            jax.ShapeDtypeStruct((B, W, S), BF16),
            jax.ShapeDtypeStruct((B, S, W), BF16),
            jax.ShapeDtypeStruct((B, S // tk, W, tk), BF16),
            jax.ShapeDtypeStruct((B, W, S), BF16),
            jax.ShapeDtypeStruct((B, S, LANES), BF16),
        ],
        scratch_shapes=[pltpu.VMEM((SUBLANES, LANES), F32)],
        compiler_params=_params("parallel", "arbitrary"),
    )(x, g, w_k, w_t, wf_a, wf_b, bf_row, tri)


def _causal_attention(n_q, t, load_q, key_tile, value_rows, emit, q_scr, s_scr, acc_scr):
    heads = range(q_scr.shape[1])
    ones = jnp.ones((ONES_ROWS, t), BF16)
    causal = _iota((t, t), 0) <= _iota((t, t), 1)

    def logits(slot, j):
        return [_dot(key_tile(j, hh), q_scr[slot, hh]) for hh in heads]

    def pv(j, hh, p):
        return _dot(jnp.concatenate([value_rows(j, hh), ones], axis=0), p.astype(BF16))

    def finish(i, slot, carry):
        accs = []
        for hh in heads:
            m, alpha = carry[hh]
            accs.append(alpha * acc_scr[hh] + pv(i - 1, hh, jnp.exp2(s_scr[slot, hh] - m)))
        emit(i, accs)

    pending = None
    for i in range(n_q):
        slot = i % 2
        load_q(i, slot)
        s_diag = logits(slot, i)
        s_first = logits(slot, 0) if i else None
        if pending is not None:
            finish(*pending)
        m_diag = []
        for hh in heads:
            s = jnp.where(causal, s_diag[hh], NEG)
            m = _col_max(s)
            acc_scr[hh] = pv(i, hh, jnp.exp2(s - m))
            m_diag.append(m)
        if i == 0:
            emit(0, [acc_scr[hh] for hh in heads])
            continue
        carry = []
        for hh in heads:
            s_scr[slot, hh] = s_first[hh]
            m_new = jnp.maximum(m_diag[hh], _col_max(s_first[hh]))
            carry.append((m_new, jnp.exp2(m_diag[hh] - m_new)))

        def body(j, carry, slot=slot):
            s_next, pvs = [], []
            for hh in heads:
                s_next.append(_dot(key_tile(j + 1, hh), q_scr[slot, hh]))
                pvs.append(pv(j, hh, jnp.exp2(s_scr[slot, hh] - carry[hh][0])))
            out = []
            for hh in heads:
                m = carry[hh][0]
                s_scr[slot, hh] = s_next[hh]
                m_new = jnp.maximum(m, _col_max(s_next[hh]))
                out.append((m_new, jnp.exp2(m - m_new)))
            for hh in heads:
                acc_scr[hh] = carry[hh][1] * acc_scr[hh] + pvs[hh]
            return tuple(out)

        carry = lax.fori_loop(0, i - 1, body, tuple(carry))
        pending = (i, slot, carry)
    if pending is not None:
        finish(*pending)


def _attn_scratch(nh, k_depth, t):
    return [pltpu.VMEM((2, nh, k_depth, t), BF16), pltpu.VMEM((2, nh, t, t), F32),
            pltpu.VMEM((nh, HEAD_DIM + ONES_ROWS, t), F32)]


def _fox_attn_kernel(q_ref, qx_ref, k_ref, kx_ref, v_ref, z_ref, o_ref, q_scr, s_scr, acc_scr, *, t):
    nh = q_scr.shape[1]
    n_q = q_ref.shape[2] // t
    row = _iota((LANES, t), 0)

    def load_q(i, slot):
        for hh in range(nh):
            pair, sub = divmod(hh, 2)
            qt = q_ref[0, pair * LANES:(pair + 1) * LANES, i * t:(i + 1) * t]
            own = jnp.logical_and(row >= sub * HEAD_DIM, row < (sub + 1) * HEAD_DIM)
            q_scr[slot, hh] = jnp.concatenate(
                [jnp.where(own, qt, jnp.zeros_like(qt)), qx_ref[hh]], axis=0)

    def key_tile(j, hh):
        pair = hh // 2
        return jnp.concatenate([k_ref[0, _tile(j, t), pair * LANES:(pair + 1) * LANES],
                                kx_ref[0, _tile(j, t), :]], axis=1)

    def value_rows(j, hh):
        return v_ref[0, j, hh * HEAD_DIM:(hh + 1) * HEAD_DIM, :]

    def emit(i, accs):
        for hh in range(nh):
            rows, cols = slice(hh * HEAD_DIM, (hh + 1) * HEAD_DIM), slice(i * t, (i + 1) * t)
            gated = _normalised(accs[hh]) * z_ref[0, rows, cols].astype(F32)
            o_ref[0, rows, cols] = gated.astype(o_ref.dtype)

    _causal_attention(n_q, t, load_q, key_tile, value_rows, emit, q_scr, s_scr, acc_scr)


def _fox_attn(qt, qx, k, kx, vt, zt):
    B, W, S = qt.shape
    t = min(ATTN_TILE, S)
    nh = HEADS_PER_STEP
    rows = nh * HEAD_DIM
    feat = pl.BlockSpec((1, rows, S), lambda b, p: (b, p, 0))
    return pl.pallas_call(
        functools.partial(_fox_attn_kernel, t=t),
        name="fox_attn",
        grid=(B, N_HEADS // nh),
        in_specs=[
            feat,
            pl.BlockSpec((nh, LANES, t), lambda b, p: (p, 0, 0)),
            pl.BlockSpec((1, S, rows), lambda b, p: (b, 0, p)),
            pl.BlockSpec((1, S, LANES), lambda b, p: (b, 0, 0)),
            pl.BlockSpec((1, S // t, rows, t), lambda b, p: (b, 0, p, 0)),
            feat,
        ],
        out_specs=feat,
        out_shape=jax.ShapeDtypeStruct((B, W, S), BF16),
        scratch_shapes=_attn_scratch(nh, 2 * LANES, t),
        compiler_params=_params("parallel", "parallel"),
    )(qt, qx, k, kx, vt, zt)


def _final_proj_kernel(ot_ref, x_ref, w_ref, g_ref, y_ref):
    y = x_ref[0] + _dot_tn(ot_ref[0], w_ref[...])
    y_ref[0] = _rmsnorm(y, g_ref[...])


def _final_proj(ot, x, w_out, final_g):
    B, S, _ = x.shape
    tm = min(OUT_ROWS, S)
    const = lambda b, s: (0, 0)
    tok = pl.BlockSpec((1, tm, D_MODEL), lambda b, s: (b, s, 0))
    return pl.pallas_call(
        _final_proj_kernel,
        name="final_proj",
        grid=(B, S // tm),
        in_specs=[pl.BlockSpec((1, ATTN_WIDTH, tm), lambda b, s: (b, 0, s)), tok,
                  pl.BlockSpec((ATTN_WIDTH, D_MODEL), const), pl.BlockSpec((1, D_MODEL), const)],
        out_specs=tok,
        out_shape=jax.ShapeDtypeStruct((B, S, D_MODEL), F32),
        compiler_params=_params("parallel", "parallel"),
    )(ot, x, w_out, final_g)


_TOK_SEL = 0
_TOK_WIN = KV_WIDTH
_TOK_CMP = 2 * KV_WIDTH
_TOK_COLS = 4 * KV_WIDTH
_FEAT_Q = 0
_FEAT_VSEL = ATTN_WIDTH
_FEAT_VWIN = _FEAT_VSEL + KV_WIDTH
_FEAT_Z = _FEAT_VWIN + KV_WIDTH
_FEAT_GATE = _FEAT_Z + ATTN_WIDTH
_FEAT_ROWS = _FEAT_GATE + N_BRANCH * N_HEADS


def _nsa_in_kernel(ot_ref, wo_ref, x_ref, g_ref, wk_ref, wt_ref, cos_ref, slo_ref, shi_ref,
                   cost_ref, sint_ref,
                   x1_ref, qt_ref, ksel_ref, kwin_ref, chk_ref, chv_ref, vsel_ref, vwin_ref, zt_ref,
                   gt_ref, cmp_scr, *, tk):
    s = pl.program_id(1)
    x1 = x_ref[0] + _dot_tn(ot_ref[0], wo_ref[...])
    x1_ref[0] = x1
    hb = _rmsnorm(x1, g_ref[...]).astype(BF16)
    tm = hb.shape[0]
    G = NSA_GROUPS

    cos, slo, shi = cos_ref[...], slo_ref[...], shi_ref[...]
    tok = s * tm + _iota((tm, LANES), 0)
    lane = _iota((tm, LANES), 1)
    low = lane < HEAD_DIM
    block_id = jnp.where(lane - HEAD_DIM == tok // SEL_LEN, 1.0, 0.0)
    keys = _dot(hb, wk_ref[:, _TOK_SEL:_TOK_CMP])
    for col0, out_ref, extra in ((_TOK_SEL, ksel_ref, block_id), (_TOK_WIN, kwin_ref, 0.0)):
        for gp in range(G // 2):
            pair = _rope_lanes(keys[:, col0 + gp * LANES:col0 + (gp + 1) * LANES], cos, slo, shi)
            out_ref[0, 2 * gp] = jnp.where(low, pair, extra).astype(BF16)
            out_ref[0, 2 * gp + 1] = jnp.where(
                low, pltpu.roll(pair, HEAD_DIM, axis=1), extra).astype(BF16)

    raw = _dot(hb, wk_ref[:, _TOK_CMP:_TOK_COLS])
    for c in range(cmp_scr.shape[0]):
        cmp_scr[c] = raw[:, c * LANES:(c + 1) * LANES]

    cost, sint = cost_ref[...], sint_ref[...]
    for r0 in range(0, ATTN_WIDTH, 512):
        t = _dot_nt(wt_ref[_FEAT_Q + r0:_FEAT_Q + r0 + 512, :], hb)
        for h0 in range(0, 512, HEAD_DIM):
            qt_ref[0, r0 + h0:r0 + h0 + HEAD_DIM, :] = _rope_rows(
                t[h0:h0 + HEAD_DIM], cost, sint).astype(BF16)
        zt_ref[0, r0:r0 + 512, :] = _silu(
            _dot_nt(wt_ref[_FEAT_Z + r0:_FEAT_Z + r0 + 512, :], hb)).astype(BF16)
    for row0, out_ref in ((_FEAT_VSEL, vsel_ref), (_FEAT_VWIN, vwin_ref)):
        t = _dot_nt(wt_ref[row0:row0 + KV_WIDTH, :], hb).astype(BF16)
        for c in range(tm // tk):
            out_ref[0, c] = t[:, c * tk:(c + 1) * tk]
    gate = _dot_nt(wt_ref[_FEAT_GATE:_FEAT_ROWS, :], hb)
    gate = 1.0 / (1.0 + jnp.exp(-gate))
    rows = N_BRANCH * HEADS_PER_STEP
    for hg in range(N_HEADS // HEADS_PER_STEP):
        gt_ref[0, hg] = _pad_rows(gate[hg * rows:(hg + 1) * rows], GATE_ROWS)

    n_ch = tm // CMP_STRIDE
    low_ch = _iota((n_ch, LANES), 1) < HEAD_DIM
    for kind, out_ref in enumerate((chk_ref, chv_ref)):
        for gp in range(G // 2):
            c = kind * (G // 2) + gp
            for l in range(0, CMP_STRIDE, 2):
                a0 = cmp_scr[c, pl.ds(l, n_ch, stride=CMP_STRIDE), :]
                a1 = cmp_scr[c, pl.ds(l + 1, n_ch, stride=CMP_STRIDE), :]
                dst = slice(l * HEAD_DIM, (l + 2) * HEAD_DIM)
                out_ref[0, 2 * gp, :, dst] = jnp.where(
                    low_ch, a0, pltpu.roll(a1, HEAD_DIM, axis=1)).astype(BF16)
                out_ref[0, 2 * gp + 1, :, dst] = jnp.where(
                    low_ch, pltpu.roll(a0, HEAD_DIM, axis=1), a1).astype(BF16)


def _nsa_in(ot, w_out, x, g, w_k, w_t, lane_tables, row_tables):
    B, S, _ = x.shape
    tm = min(PROJ_ROWS, S)
    tk = min(ATTN_TILE, S)
    W, G = ATTN_WIDTH, NSA_GROUPS
    const = lambda b, s: (0, 0)
    feat = lambda rows: pl.BlockSpec((1, rows, tm), lambda b, s: (b, 0, s))
    tiles = pl.BlockSpec((1, tm // tk, KV_WIDTH, tk), lambda b, s: (b, s, 0, 0))
    keys = pl.BlockSpec((1, G, tm, LANES), lambda b, s: (b, 0, s, 0))
    chunk_w = CMP_STRIDE * HEAD_DIM
    chunk = pl.BlockSpec((1, G, tm // CMP_STRIDE, chunk_w), lambda b, s: (b, 0, s, 0))
    tok = pl.BlockSpec((1, tm, D_MODEL), lambda b, s: (b, s, 0))
    n_hg = N_HEADS // HEADS_PER_STEP
    return pl.pallas_call(
        functools.partial(_nsa_in_kernel, tk=tk),
        name="nsa_in",
        grid=(B, S // tm),
        in_specs=[
            feat(W),
            pl.BlockSpec(w_out.shape, const),
            tok,
            pl.BlockSpec((1, D_MODEL), const),
            pl.BlockSpec(w_k.shape, const),
            pl.BlockSpec(w_t.shape, const),
        ] + [pl.BlockSpec((tm, LANES), lambda b, s: (s, 0))] * 3
          + [pl.BlockSpec((ROPE_HALF, tm), lambda b, s: (0, s))] * 2,
        out_specs=[tok, feat(W), keys, keys, chunk, chunk, tiles, tiles, feat(W),
                   pl.BlockSpec((1, n_hg, GATE_ROWS, tm), lambda b, s: (b, 0, 0, s))],
        out_shape=[
            jax.ShapeDtypeStruct((B, S, D_MODEL), F32),
            jax.ShapeDtypeStruct((B, W, S), BF16),
            jax.ShapeDtypeStruct((B, G, S, LANES), BF16),
            jax.ShapeDtypeStruct((B, G, S, LANES), BF16),
            jax.ShapeDtypeStruct((B, G, S // CMP_STRIDE, chunk_w), BF16),
            jax.ShapeDtypeStruct((B, G, S // CMP_STRIDE, chunk_w), BF16),
            jax.ShapeDtypeStruct((B, S // tk, KV_WIDTH, tk), BF16),
            jax.ShapeDtypeStruct((B, S // tk, KV_WIDTH, tk), BF16),
            jax.ShapeDtypeStruct((B, W, S), BF16),
            jax.ShapeDtypeStruct((B, n_hg, GATE_ROWS, S), F32),
        ],
        scratch_shapes=[pltpu.VMEM((2 * KV_WIDTH // LANES, tm, LANES), F32)],
        compiler_params=_params("parallel", "parallel"),
    )(ot, w_out, x, g, w_k, w_t, *lane_tables, *row_tables)


def _compress_kernel(chk_ref, chv_ref, w1k_ref, w2k_ref, pek_ref, w1v_ref, w2vt_ref, pev_ref,
                     cos_ref, slo_ref, shi_ref, kc_ref, vct_ref):
    half = CMP_STRIDE * HEAD_DIM

    def hidden(ch, w1_ref, pe_ref):
        a = _dot(ch, w1_ref[:half, :])
        b = _dot(ch, w1_ref[half:, :])
        b = pltpu.roll(b, b.shape[0] - 1, axis=0)
        pe = _dot(pe_ref[...], w1_ref[...])[0:1, :]
        return _silu(a + b + pe).astype(BF16)

    kc = _dot(hidden(chk_ref[0, 0], w1k_ref, pek_ref), w2k_ref[...])
    kc_ref[0, 0] = _rope_lanes(kc, cos_ref[...], slo_ref[...], shi_ref[...]).astype(BF16)
    vct_ref[0, 0] = _dot_nt(w2vt_ref[...], hidden(chv_ref[0, 0], w1v_ref, pev_ref)).astype(BF16)


def _compress(chk, chv, w1k, w2k, pek, w1v, w2vt, pev, tables):
    B, G, n_chunk, width = chk.shape
    const = lambda b, g: (0, 0)
    chunk_spec = pl.BlockSpec((1, 1, n_chunk, width), lambda b, g: (b, g, 0, 0))
    specs = lambda ws: [pl.BlockSpec(w.shape, const) for w in ws]
    return pl.pallas_call(
        _compress_kernel,
        name="nsa_compress",
        grid=(B, G),
        in_specs=[chunk_spec, chunk_spec] + specs((w1k, w2k, pek, w1v, w2vt, pev))
                 + [pl.BlockSpec((n_chunk, LANES), const)] * 3,
        out_specs=[pl.BlockSpec((1, 1, n_chunk, LANES), lambda b, g: (b, g, 0, 0)),
                   pl.BlockSpec((1, 1, HEAD_DIM, n_chunk), lambda b, g: (b, g, 0, 0))],
        out_shape=[jax.ShapeDtypeStruct((B, G, n_chunk, LANES), BF16),
                   jax.ShapeDtypeStruct((B, G, HEAD_DIM, n_chunk), BF16)],
        compiler_params=_params("parallel", "parallel"),
    )(chk, chv, w1k, w2k, pek, w1v, w2vt, pev, *tables)


def _local_attn_kernel(q_ref, kc_ref, vct_ref, ovt_ref, kw_ref, vw_ref, oc_ref, bias_ref, ow_ref,
                       *, t, n_cmp, n_blk, n_back):
    n_q = q_ref.shape[2] // t
    heads = range(NSA_REP)
    kc = kc_ref[0, 0]
    vct = vct_ref[0, 0]
    ovt = ovt_ref[...]
    n_pad = kc.shape[0]
    above = _iota((t, t), 0) > _iota((t, t), 1)
    ones = jnp.ones((ONES_ROWS, t), BF16)
    blk_c = _iota((n_pad, t), 0)
    blk = _iota((n_blk, t), 0)
    n_rounds = min(N_SELECT, n_blk)
    per_head = -(-n_rounds // NSA_REP)

    def q_aug(i, r):
        return _pad_rows(q_ref[0, r * HEAD_DIM:(r + 1) * HEAD_DIM, i * t:(i + 1) * t], LANES)

    masks = {0: jnp.logical_not(above), n_back: above}

    def win_tiles(i):
        return [(i - d, masks.get(d)) for d in range(n_back + 1) if i - d >= 0]

    def win_logits(i, r):
        return [_dot(kw_ref[0, 0, _tile(j, t), :], q_aug(i, r)) for j, _ in win_tiles(i)]

    def cmp_logits(i):
        return [_dot(kc, q_aug(i, r)) for r in heads]

    def importance(i, logits):
        cols = slice(i * t, (i + 1) * t)
        qry = i * t + _iota((n_pad, t), 1)
        valid = jnp.logical_and(blk_c * CMP_STRIDE + (CMP_LEN - 1) <= qry, blk_c < n_cmp)
        p_sum = jnp.zeros((n_pad, t), F32)
        for r in heads:
            s = jnp.where(valid, logits[r], NEG)
            e = jnp.exp2(s - _col_max(s))
            p = jnp.where(valid, e * (1.0 / _col_sum(e)), 0.0)
            p_sum = p_sum + p
            oc_ref[0, r * HEAD_DIM:(r + 1) * HEAD_DIM, cols] = _dot(
                vct, p.astype(BF16)).astype(oc_ref.dtype)
        p_hi, p_lo = _split2(p_sum)
        imp = _dot(ovt, p_hi) + _dot(ovt, p_lo)
        cur = (i * t + _iota(imp.shape, 1)) // SEL_LEN
        forced = jnp.logical_or(blk == 0, jnp.logical_or(blk == cur, blk == cur - 1))
        return jnp.where(forced, FORCE, jnp.where(blk <= cur, imp, -1.0))

    items = [(i, r) for i in range(n_q) for r in heads]
    win_queue, cmp_queue = [], {}

    def issue(n):
        if n < len(items):
            win_queue.append(win_logits(*items[n]))
            if items[n][1] == 0:
                cmp_queue[items[n][0]] = cmp_logits(items[n][0])

    for n in range(LOOKAHEAD):
        issue(n)
    for n, (i, r) in enumerate(items):
        issue(n + LOOKAHEAD)
        logits = win_queue.pop(0)
        if r == 0:
            imp = importance(i, cmp_queue.pop(i))
            keep = jnp.zeros(imp.shape, jnp.bool_)
            done = 0
        ss = [s if mask is None else jnp.where(mask, s, NEG)
              for s, (_, mask) in zip(logits, win_tiles(i))]
        m = functools.reduce(jnp.maximum, [_col_max(s) for s in ss])
        acc = functools.reduce(jnp.add, [
            _dot(jnp.concatenate([vw_ref[0, j], ones], axis=0), jnp.exp2(s - m).astype(BF16))
            for s, (j, _) in zip(ss, win_tiles(i))])
        ow_ref[0, r * HEAD_DIM:(r + 1) * HEAD_DIM, i * t:(i + 1) * t] = _normalised(acc).astype(
            ow_ref.dtype)
        for _ in range(min(per_head, n_rounds - done)):
            top = _col_max(imp)
            first = jnp.min(jnp.where(imp == top, blk, n_blk), axis=0, keepdims=True)
            pick = blk == first
            keep = jnp.logical_or(keep, pick)
            imp = jnp.where(pick, -jnp.inf, imp)
            done += 1
        if r == NSA_REP - 1:
            bias_ref[0, 0, :, i * t:(i + 1) * t] = jnp.where(keep, 0.0, NEG).astype(bias_ref.dtype)


def _local_attn(qt, kc, vct, ovt, n_cmp, kw, vw):
    B, W, S = qt.shape
    G = kc.shape[1]
    t = min(ATTN_TILE, S)
    assert WINDOW % t == 0
    n_blk = ovt.shape[0]
    kern = functools.partial(_local_attn_kernel, t=t, n_cmp=n_cmp, n_blk=n_blk, n_back=WINDOW // t)
    feat = pl.BlockSpec((1, W // G, S), lambda b, g: (b, g, 0))
    return pl.pallas_call(
        kern,
        name="nsa_local_attn",
        grid=(B, G),
        in_specs=[
            feat,
            pl.BlockSpec((1, 1) + kc.shape[2:], lambda b, g: (b, g, 0, 0)),
            pl.BlockSpec((1, 1) + vct.shape[2:], lambda b, g: (b, g, 0, 0)),
            pl.BlockSpec(ovt.shape, lambda b, g: (0, 0)),
            pl.BlockSpec((1, 1, S, LANES), lambda b, g: (b, g, 0, 0)),
            pl.BlockSpec((1, S // t, HEAD_DIM, t), lambda b, g: (b, 0, g, 0)),
        ],
        out_specs=[feat, pl.BlockSpec((1, 1, n_blk, S), lambda b, g: (b, g, 0, 0)), feat],
        out_shape=[
            jax.ShapeDtypeStruct((B, W, S), BF16),
            jax.ShapeDtypeStruct((B, G, n_blk, S), BF16),
            jax.ShapeDtypeStruct((B, W, S), BF16),
        ],
        compiler_params=_params("parallel", "parallel"),
    )(qt, kc, vct, ovt, kw, vw)


def _sel_attn_kernel(q_ref, bias_ref, k_ref, v_ref, oc_ref, ow_ref, gt_ref, z_ref, o_ref,
                     q_scr, s_scr, acc_scr, *, t):
    nh = q_scr.shape[1]
    n_q = q_ref.shape[2] // t

    def load_q(i, slot):
        cols = slice(i * t, (i + 1) * t)
        for hh in range(nh):
            bias = _pad_rows(bias_ref[0, hh // NSA_REP, :, cols], HEAD_DIM)
            q_scr[slot, hh] = jnp.concatenate(
                [q_ref[0, hh * HEAD_DIM:(hh + 1) * HEAD_DIM, cols], bias], axis=0)

    def key_tile(j, hh):
        return k_ref[0, hh // NSA_REP, _tile(j, t), :]

    def value_rows(j, hh):
        g = hh // NSA_REP
        return v_ref[0, j, g * HEAD_DIM:(g + 1) * HEAD_DIM, :]

    def emit(i, accs):
        cols = slice(i * t, (i + 1) * t)
        for hh in range(nh):
            rows = slice(hh * HEAD_DIM, (hh + 1) * HEAD_DIM)
            gate = lambda b: gt_ref[0, 0, N_BRANCH * hh + b:N_BRANCH * hh + b + 1, cols]
            o = (gate(0) * oc_ref[0, rows, cols].astype(F32) + gate(1) * _normalised(accs[hh])
                 + gate(2) * ow_ref[0, rows, cols].astype(F32))
            o_ref[0, rows, cols] = (o * z_ref[0, rows, cols].astype(F32)).astype(o_ref.dtype)

    _causal_attention(n_q, t, load_q, key_tile, value_rows, emit, q_scr, s_scr, acc_scr)


def _sel_attn(qt, bias, k_aug, vt, oc, ow, gt, zt):
    B, W, S = qt.shape
    t = min(ATTN_TILE, S)
    ng = HEADS_PER_STEP // NSA_REP
    n_blk = bias.shape[2]
    feat = pl.BlockSpec((1, HEADS_PER_STEP * HEAD_DIM, S), lambda b, g: (b, g, 0))
    return pl.pallas_call(
        functools.partial(_sel_attn_kernel, t=t),
        name="nsa_sel_attn",
        grid=(B, NSA_GROUPS // ng),
        in_specs=[
            feat,
            pl.BlockSpec((1, ng, n_blk, S), lambda b, g: (b, g, 0, 0)),
            pl.BlockSpec((1, ng, S, LANES), lambda b, g: (b, g, 0, 0)),
            pl.BlockSpec((1, S // t, ng * HEAD_DIM, t), lambda b, g: (b, 0, g, 0)),
            feat, feat,
            pl.BlockSpec((1, 1, GATE_ROWS, S), lambda b, g: (b, g, 0, 0)),
            feat,
        ],
        out_specs=feat,
        out_shape=jax.ShapeDtypeStruct((B, W, S), BF16),
        scratch_shapes=_attn_scratch(HEADS_PER_STEP, LANES, t),
        compiler_params=_params("parallel", "parallel"),
    )(qt, bias, k_aug, vt, oc, ow, gt, zt)


def _fox_mixer(x, g, w_in, b_f):
    B, S, _ = x.shape
    W, H = ATTN_WIDTH, N_HEADS
    w_k = w_in[:, W:2 * W].astype(BF16)
    w_t = jnp.concatenate([w_in[:, :W] * Q_SCALE, w_in[:, 2 * W:3 * W], w_in[:, 3 * W + H:]],
                          axis=1).T.astype(BF16)
    wf_hi, wf_lo = _split2(w_in[:, 3 * W:3 * W + H])
    pad = lambda n: jnp.zeros((D_MODEL, LANES - n * H), BF16)
    wf_a = jnp.concatenate([wf_hi] * 3 + [wf_lo] * 3 + [pad(6)], axis=1)
    wf_b = jnp.concatenate([wf_hi] * 3 + [pad(3)], axis=1)
    bf_row = jnp.concatenate([b_f, b_f, b_f, jnp.zeros((LANES - 3 * H,), F32)])[None, :]

    qt, k, vt, zt, cx = _fox_in(x, g[None, :], w_k, w_t, wf_a, wf_b, bf_row)

    t = min(ATTN_TILE, S)
    r = np.arange(LANES)
    sel = np.logical_and(r[None, :] % H == np.arange(H)[:, None], r[None, :] < 3 * H)
    qx = np.broadcast_to(np.where(sel, -1.0, 0.0).astype(np.float32)[:, :, None], (H, LANES, t))
    return _fox_attn(qt, jnp.asarray(qx, BF16), k, cx, vt, zt)


def _nsa_layer(ot0, w_out0, x, g, w_in, pe_k, w_ck1, w_ck2, pe_v, w_cv1, w_cv2, w_out, final_g):
    B, S, _ = x.shape
    W, KV, G = ATTN_WIDTH, KV_WIDTH, NSA_GROUPS
    sec = lambda j: w_in[:, W + j * KV:W + (j + 1) * KV]
    gate_off = W + 6 * KV
    n_gate = N_BRANCH * N_HEADS
    w_k = jnp.concatenate([sec(2), sec(4), sec(0), sec(1)], axis=1).astype(BF16)
    w_t = jnp.concatenate([w_in[:, :W] * Q_SCALE, sec(3), sec(5), w_in[:, gate_off + n_gate:],
                           w_in[:, gate_off:gate_off + n_gate]], axis=1).T.astype(BF16)
    pos = np.arange(S)
    cos, sin = _rope_angles(pos)
    x, qt, ksel, kwin, chk, chv, vsel, vwin, zt, gt = _nsa_in(
        ot0, w_out0.astype(BF16), x, g[None, :], w_k, w_t, _rope_lane_tables(pos),
        (np.ascontiguousarray(cos.T), np.ascontiguousarray(sin.T)))

    n_chunk = S // CMP_STRIDE
    n_cmp = n_chunk - CMP_LEN // CMP_STRIDE + 1

    def flat_pe(pe):
        pe = pe.reshape(1, CMP_LEN * HEAD_DIM)
        return jnp.broadcast_to(pe, (SUBLANES, CMP_LEN * HEAD_DIM)).astype(BF16)

    w2k = jnp.concatenate([w_ck2, jnp.zeros_like(w_ck2)], axis=1).astype(BF16)
    cmp_end = np.arange(n_chunk) * CMP_STRIDE + CMP_LEN - 1
    kc, vct = _compress(chk, chv, w_ck1.astype(BF16), w2k, flat_pe(pe_k),
                        w_cv1.astype(BF16), w_cv2.T.astype(BF16), flat_pe(pe_v),
                        _rope_lane_tables(cmp_end))

    n_blk = S // SEL_LEN
    ci = np.arange(n_chunk) * CMP_STRIDE
    sj = np.arange(n_blk) * SEL_LEN
    ovt = np.logical_and(ci[None, :] < sj[:, None] + SEL_LEN, ci[None, :] + CMP_LEN > sj[:, None])
    ovt = np.logical_and(ovt, np.arange(n_chunk)[None, :] < n_cmp).astype(np.float32)
    oc, bias, ow = _local_attn(qt, kc, vct, jnp.asarray(ovt, BF16), n_cmp, kwin, vwin)
    ot = _sel_attn(qt, bias, ksel, vsel, oc, ow, gt, zt)
    return _final_proj(ot, x, w_out.astype(BF16), final_g[None, :])


def kernel(x, norm_g, fox_w_in, fox_b_f, fox_w_out, nsa_w_in, nsa_pe_k, nsa_w_ck1, nsa_w_ck2,
           nsa_pe_v, nsa_w_cv1, nsa_w_cv2, nsa_w_out, final_g):
    ot0 = _fox_mixer(x, norm_g[0], fox_w_in[0], fox_b_f[0])
    return _nsa_layer(ot0, fox_w_out[0], x, norm_g[1], nsa_w_in[0], nsa_pe_k[0], nsa_w_ck1[0],
                      nsa_w_ck2[0], nsa_pe_v[0], nsa_w_cv1[0], nsa_w_cv2[0], nsa_w_out[0], final_g)
```

```python
import functools
import math

import jax
import jax.numpy as jnp
import numpy as np
from jax import lax
from jax.experimental import pallas as pl
from jax.experimental.pallas import tpu as pltpu

F32 = jnp.float32
BF16 = jnp.bfloat16

D_MODEL = 1024
N_HEADS = 16
HEAD_DIM = 64
ATTN_WIDTH = N_HEADS * HEAD_DIM
NSA_GROUPS = 4
NSA_REP = N_HEADS // NSA_GROUPS
KV_WIDTH = NSA_GROUPS * HEAD_DIM
CMP_LEN = 32
CMP_STRIDE = 16
CMP_HIDDEN = 256
SEL_LEN = 64
N_SELECT = 8
WINDOW = 512
N_BRANCH = 3
ROPE_THETA = 500000.0
ROPE_DIM = HEAD_DIM // 4
ROPE_HALF = ROPE_DIM // 2
NORM_EPS = 1e-6
NEG = -1e30
FORCE = 1e6
LOG2E = math.log2(math.e)
Q_SCALE = HEAD_DIM ** -0.5 * LOG2E

LANES = 128
SUBLANES = 8
ONES_ROWS = 2 * SUBLANES
PROJ_ROWS = 1024
OUT_ROWS = 1024
ATTN_TILE = 256
HEADS_PER_STEP = 8
LOOKAHEAD = 1
GATE_ROWS = -(-N_BRANCH * HEADS_PER_STEP // SUBLANES) * SUBLANES
VMEM_LIMIT = 56 * 1024 * 1024


def _params(*sem):
    return pltpu.CompilerParams(dimension_semantics=sem, vmem_limit_bytes=VMEM_LIMIT)


def _iota(shape, dim):
    return lax.broadcasted_iota(jnp.int32, shape, dim)


def _split2(x):
    hi = x.astype(BF16)
    lo = (x - hi.astype(F32)).astype(BF16)
    return hi, lo


def _split3(x):
    hi = x.astype(BF16)
    r1 = x - hi.astype(F32)
    mid = r1.astype(BF16)
    lo = (r1 - mid.astype(F32)).astype(BF16)
    return hi, mid, lo


def _dot(a, b):
    return jnp.dot(a, b, preferred_element_type=F32)


def _dot_nt(a, b):
    return lax.dot_general(a, b, (((1,), (1,)), ((), ())), preferred_element_type=F32)


def _dot_tn(a, b):
    return lax.dot_general(a, b, (((0,), (0,)), ((), ())), preferred_element_type=F32)


def _rmsnorm(x, g):
    ms = jnp.mean(x * x, axis=-1, keepdims=True)
    return x * lax.rsqrt(ms + NORM_EPS) * g


def _silu(x):
    return x * (1.0 / (1.0 + jnp.exp(-x)))


def _col_reduce(x, op, reduce):
    rows = x.shape[0]
    if rows % (4 * SUBLANES) == 0 and rows >= 8 * SUBLANES:
        q = rows // 4
        x = op(op(x[:q], x[q:2 * q]), op(x[2 * q:3 * q], x[3 * q:]))
    return reduce(x, axis=0, keepdims=True)


def _col_max(x):
    return _col_reduce(x, jnp.maximum, jnp.max)


def _col_sum(x):
    return _col_reduce(x, jnp.add, jnp.sum)


def _pad_rows(x, rows):
    if rows == x.shape[0]:
        return x
    return jnp.concatenate([x, jnp.zeros((rows - x.shape[0], x.shape[1]), x.dtype)], axis=0)


def _tile(j, t):
    return pl.ds(j * t if isinstance(j, int) else pl.multiple_of(j * t, t), t)


def _normalised(acc):
    return acc[:HEAD_DIM] * (1.0 / acc[HEAD_DIM:HEAD_DIM + 1])


def _rope_lanes(x, cos, sin_lo, sin_hi):
    return (x * cos + pltpu.roll(x, LANES - ROPE_HALF, axis=1) * sin_lo
            + pltpu.roll(x, ROPE_HALF, axis=1) * sin_hi)


def _rope_rows(x, cos, sin):
    x1, x2 = x[:ROPE_HALF], x[ROPE_HALF:ROPE_DIM]
    return jnp.concatenate([x1 * cos - x2 * sin, x1 * sin + x2 * cos, x[ROPE_DIM:]], axis=0)


def _rope_angles(pos):
    inv_freq = np.power(np.float32(ROPE_THETA),
                        -np.arange(ROPE_HALF, dtype=np.float32) * np.float32(2.0 / ROPE_DIM))
    ang = pos.astype(np.float32)[:, None] * inv_freq[None, :].astype(np.float32)
    return np.cos(ang).astype(np.float32), np.sin(ang).astype(np.float32)


def _rope_lane_tables(pos):
    cos, sin = _rope_angles(pos)
    n = pos.shape[0]
    ones = np.ones((n, HEAD_DIM - ROPE_DIM), np.float32)
    zeros = np.zeros((n, HEAD_DIM - ROPE_DIM), np.float32)
    z8 = np.zeros((n, ROPE_HALF), np.float32)
    c = np.concatenate([cos, cos, ones], axis=1)
    s_lo = np.concatenate([-sin, z8, zeros], axis=1)
    s_hi = np.concatenate([z8, sin, zeros], axis=1)
    tile = lambda t: np.concatenate([t, t], axis=1)
    return tile(c), tile(s_lo), tile(s_hi)


def _fox_in_kernel(x_ref, g_ref, wk_ref, wt_ref, wfa_ref, wfb_ref, bf_ref, tri_ref,
                   qt_ref, k_ref, vt_ref, zt_ref, cx_ref, carry_ref, *, tk):
    s = pl.program_id(1)
    h = _rmsnorm(x_ref[0], g_ref[...])
    hb = h.astype(BF16)
    tm = hb.shape[0]
    W, H = ATTN_WIDTH, N_HEADS

    @pl.when(s == 0)
    def _():
        carry_ref[...] = jnp.zeros_like(carry_ref)

    h_lo = (h - hb.astype(F32)).astype(BF16)
    fa = _dot(hb, wfa_ref[...])
    f = fa + pltpu.roll(fa, LANES - 3 * H, axis=1) + _dot(h_lo, wfb_ref[...]) + bf_ref[...]

    for n0 in range(0, W, 512):
        k_ref[0, :, n0:n0 + 512] = _dot(hb, wk_ref[:, n0:n0 + 512]).astype(BF16)

    log_f = jnp.minimum(f, 0.0) - jnp.log1p(jnp.exp(-jnp.abs(f)))
    lane = _iota(log_f.shape, 1)
    a0, a1, a2 = _split3(log_f)
    pieces = jnp.where(lane < H, a0, jnp.where(lane < 2 * H, a1, a2))
    c = _dot(tri_ref[...], pieces)

    for r0 in list(range(2 * W, 3 * W, 512)) + list(range(0, 2 * W, 512)):
        t = _dot_nt(wt_ref[r0:r0 + 512, :], hb)
        if r0 < W:
            qt_ref[0, r0:r0 + 512, :] = t.astype(BF16)
        elif r0 < 2 * W:
            t = t.astype(BF16)
            for ct in range(tm // tk):
                vt_ref[0, ct, r0 - W:r0 - W + 512, :] = t[:, ct * tk:(ct + 1) * tk]
        else:
            zt_ref[0, r0 - 2 * W:r0 - 2 * W + 512, :] = _silu(t).astype(BF16)

    c = c + pltpu.roll(c, LANES - H, axis=1) + pltpu.roll(c, LANES - 2 * H, axis=1)
    c = c + carry_ref[0:1, :]
    carry_ref[...] = jnp.broadcast_to(c[tm - 1:tm, :], carry_ref.shape)
    c = jnp.where(lane < H, c, jnp.where(lane < 2 * H, pltpu.roll(c, H, axis=1),
                                         pltpu.roll(c, 2 * H, axis=1)))

    hi, mid, lo = _split3(c * LOG2E)
    zero = jnp.zeros_like(hi)
    cx_ref[0] = jnp.where(lane < H, hi,
                          jnp.where(lane < 2 * H, mid, jnp.where(lane < 3 * H, lo, zero)))


def _fox_in(x, g, w_k, w_t, wf_a, wf_b, bf_row):
    B, S, _ = x.shape
    tm = min(PROJ_ROWS, S)
    tk = min(ATTN_TILE, S)
    W = ATTN_WIDTH
    tri = np.tril(np.ones((tm, tm), np.float32)).astype(BF16)
    const = lambda b, s: (0, 0)
    feat = pl.BlockSpec((1, W, tm), lambda b, s: (b, 0, s))
    return pl.pallas_call(
        functools.partial(_fox_in_kernel, tk=tk),
        name="fox_in",
        grid=(B, S // tm),
        in_specs=[
            pl.BlockSpec((1, tm, D_MODEL), lambda b, s: (b, s, 0)),
            pl.BlockSpec((1, D_MODEL), const),
            pl.BlockSpec(w_k.shape, const),
            pl.BlockSpec(w_t.shape, const),
            pl.BlockSpec((D_MODEL, LANES), const),
            pl.BlockSpec((D_MODEL, LANES), const),
            pl.BlockSpec((1, LANES), const),
            pl.BlockSpec((tm, tm), const),
        ],
        out_specs=[
            feat,
            pl.BlockSpec((1, tm, W), lambda b, s: (b, s, 0)),
            pl.BlockSpec((1, tm // tk, W, tk), lambda b, s: (b, s, 0, 0)),
            feat,
            pl.BlockSpec((1, tm, LANES), lambda b, s: (b, s, 0)),
        ],
        out_shape=[
            jax.ShapeDtypeStruct((B, W, S), BF16),
            jax.ShapeDtypeStruct((B, S, W), BF16),
            jax.ShapeDtypeStruct((B, S // tk, W, tk), BF16),
            jax.ShapeDtypeStruct((B, W, S), BF16),
            jax.ShapeDtypeStruct((B, S, LANES), BF16),
        ],
        scratch_shapes=[pltpu.VMEM((SUBLANES, LANES), F32)],
        compiler_params=_params("parallel", "arbitrary"),
    )(x, g, w_k, w_t, wf_a, wf_b, bf_row, tri)


def _causal_attention(n_q, t, prepare_q, query, key_tile, value_rows, emit, s_scr, acc_scr):
    heads = range(s_scr.shape[1])
    ones = jnp.ones((ONES_ROWS, t), BF16)
    causal = _iota((t, t), 0) <= _iota((t, t), 1)

    def logits(i, slot, j):
        return [_dot(key_tile(j, hh), query(i, slot, hh)) for hh in heads]

    def pv(j, hh, p):
        return _dot(jnp.concatenate([value_rows(j, hh), ones], axis=0), p.astype(BF16))

    def finish(i, slot, carry):
        accs = []
        for hh in heads:
            m, alpha = carry[hh]
            accs.append(alpha * acc_scr[hh] + pv(i - 1, hh, jnp.exp2(s_scr[slot, hh] - m)))
        emit(i, accs)

    pending = None
    for i in range(n_q):
        slot = i % 2
        prepare_q(i, slot)
        s_diag = logits(i, slot, i)
        s_first = logits(i, slot, 0) if i else None
        if pending is not None:
            finish(*pending)
        m_diag = []
        for hh in heads:
            s = jnp.where(causal, s_diag[hh], NEG)
            m = _col_max(s)
            acc_scr[hh] = pv(i, hh, jnp.exp2(s - m))
            m_diag.append(m)
        if i == 0:
            emit(0, [acc_scr[hh] for hh in heads])
            continue
        carry = []
        for hh in heads:
            s_scr[slot, hh] = s_first[hh]
            m_new = jnp.maximum(m_diag[hh], _col_max(s_first[hh]))
            carry.append((m_new, jnp.exp2(m_diag[hh] - m_new)))

        def body(j, carry, i=i, slot=slot):
            s_next, pvs = [], []
            for hh in heads:
                s_next.append(_dot(key_tile(j + 1, hh), query(i, slot, hh)))
                pvs.append(pv(j, hh, jnp.exp2(s_scr[slot, hh] - carry[hh][0])))
            out = []
            for hh in heads:
                m = carry[hh][0]
                s_scr[slot, hh] = s_next[hh]
                m_new = jnp.maximum(m, _col_max(s_next[hh]))
                out.append((m_new, jnp.exp2(m - m_new)))
            for hh in heads:
                acc_scr[hh] = carry[hh][1] * acc_scr[hh] + pvs[hh]
            return tuple(out)

        carry = lax.fori_loop(0, i - 1, body, tuple(carry))
        pending = (i, slot, carry)
    if pending is not None:
        finish(*pending)


def _attn_scratch(nh, t):
    return [pltpu.VMEM((2, nh, t, t), F32), pltpu.VMEM((nh, HEAD_DIM + ONES_ROWS, t), F32)]


def _fox_attn_kernel(q_ref, qx_ref, k_ref, kx_ref, v_ref, z_ref, o_ref, q_scr, s_scr, acc_scr, *, t):
    nh = q_scr.shape[1]
    n_q = q_ref.shape[2] // t
    row = _iota((LANES, t), 0)

    def load_q(i, slot):
        for hh in range(nh):
            pair, sub = divmod(hh, 2)
            qt = q_ref[0, pair * LANES:(pair + 1) * LANES, i * t:(i + 1) * t]
            own = jnp.logical_and(row >= sub * HEAD_DIM, row < (sub + 1) * HEAD_DIM)
            q_scr[slot, hh] = jnp.where(own, qt, jnp.zeros_like(qt))

    def query(i, slot, hh):
        return jnp.concatenate([q_scr[slot, hh], qx_ref[hh]], axis=0)

    def key_tile(j, hh):
        pair = hh // 2
        return jnp.concatenate([k_ref[0, _tile(j, t), pair * LANES:(pair + 1) * LANES],
                                kx_ref[0, _tile(j, t), :]], axis=1)

    def value_rows(j, hh):
        return v_ref[0, j, hh * HEAD_DIM:(hh + 1) * HEAD_DIM, :]

    def emit(i, accs):
        for hh in range(nh):
            rows, cols = slice(hh * HEAD_DIM, (hh + 1) * HEAD_DIM), slice(i * t, (i + 1) * t)
            gated = _normalised(accs[hh]) * z_ref[0, rows, cols].astype(F32)
            o_ref[0, rows, cols] = gated.astype(o_ref.dtype)

    _causal_attention(n_q, t, load_q, query, key_tile, value_rows, emit, s_scr, acc_scr)


def _fox_attn(qt, qx, k, kx, vt, zt):
    B, W, S = qt.shape
    t = min(ATTN_TILE, S)
    nh = HEADS_PER_STEP
    rows = nh * HEAD_DIM
    feat = pl.BlockSpec((1, rows, S), lambda b, p: (b, p, 0))
    return pl.pallas_call(
        functools.partial(_fox_attn_kernel, t=t),
        name="fox_attn",
        grid=(B, N_HEADS // nh),
        in_specs=[
            feat,
            pl.BlockSpec((nh, LANES, t), lambda b, p: (p, 0, 0)),
            pl.BlockSpec((1, S, rows), lambda b, p: (b, 0, p)),
            pl.BlockSpec((1, S, LANES), lambda b, p: (b, 0, 0)),
            pl.BlockSpec((1, S // t, rows, t), lambda b, p: (b, 0, p, 0)),
            feat,
        ],
        out_specs=feat,
        out_shape=jax.ShapeDtypeStruct((B, W, S), BF16),
        scratch_shapes=[pltpu.VMEM((2, nh, LANES, t), BF16)] + _attn_scratch(nh, t),
        compiler_params=_params("parallel", "parallel"),
    )(qt, qx, k, kx, vt, zt)


def _final_proj_kernel(ot_ref, x_ref, w_ref, g_ref, y_ref):
    y = x_ref[0] + _dot_tn(ot_ref[0], w_ref[...])
    y_ref[0] = _rmsnorm(y, g_ref[...])


def _final_proj(ot, x, w_out, final_g):
    B, S, _ = x.shape
    tm = min(OUT_ROWS, S)
    const = lambda b, s: (0, 0)
    tok = pl.BlockSpec((1, tm, D_MODEL), lambda b, s: (b, s, 0))
    return pl.pallas_call(
        _final_proj_kernel,
        name="final_proj",
        grid=(B, S // tm),
        in_specs=[pl.BlockSpec((1, ATTN_WIDTH, tm), lambda b, s: (b, 0, s)), tok,
                  pl.BlockSpec((ATTN_WIDTH, D_MODEL), const), pl.BlockSpec((1, D_MODEL), const)],
        out_specs=tok,
        out_shape=jax.ShapeDtypeStruct((B, S, D_MODEL), F32),
        compiler_params=_params("parallel", "parallel"),
    )(ot, x, w_out, final_g)


_TOK_SEL = 0
_TOK_WIN = KV_WIDTH
_TOK_CMP = 2 * KV_WIDTH
_TOK_COLS = 4 * KV_WIDTH
_FEAT_Q = 0
_FEAT_VSEL = ATTN_WIDTH
_FEAT_VWIN = _FEAT_VSEL + KV_WIDTH
_FEAT_Z = _FEAT_VWIN + KV_WIDTH
_FEAT_GATE = _FEAT_Z + ATTN_WIDTH
_FEAT_ROWS = _FEAT_GATE + N_BRANCH * N_HEADS


def _nsa_in_kernel(ot_ref, wo_ref, x_ref, g_ref, wk_ref, wt_ref, cos_ref, slo_ref, shi_ref,
                   cost_ref, sint_ref,
                   x1_ref, qt_ref, ksel_ref, kwin_ref, chk_ref, chv_ref, vsel_ref, vwin_ref, zt_ref,
                   gt_ref, cmp_scr, *, tk):
    s = pl.program_id(1)
    x1 = x_ref[0] + _dot_tn(ot_ref[0], wo_ref[...])
    x1_ref[0] = x1
    hb = _rmsnorm(x1, g_ref[...]).astype(BF16)
    tm = hb.shape[0]
    G = NSA_GROUPS

    cos, slo, shi = cos_ref[...], slo_ref[...], shi_ref[...]
    tok = s * tm + _iota((tm, LANES), 0)
    lane = _iota((tm, LANES), 1)
    low = lane < HEAD_DIM
    block_id = jnp.where(lane - HEAD_DIM == tok // SEL_LEN, 1.0, 0.0)
    keys = _dot(hb, wk_ref[:, _TOK_SEL:_TOK_CMP])
    for col0, out_ref, extra in ((_TOK_SEL, ksel_ref, block_id), (_TOK_WIN, kwin_ref, 0.0)):
        for gp in range(G // 2):
            pair = _rope_lanes(keys[:, col0 + gp * LANES:col0 + (gp + 1) * LANES], cos, slo, shi)
            out_ref[0, 2 * gp] = jnp.where(low, pair, extra).astype(BF16)
            out_ref[0, 2 * gp + 1] = jnp.where(
                low, pltpu.roll(pair, HEAD_DIM, axis=1), extra).astype(BF16)

    raw = _dot(hb, wk_ref[:, _TOK_CMP:_TOK_COLS])
    for c in range(cmp_scr.shape[0]):
        cmp_scr[c] = raw[:, c * LANES:(c + 1) * LANES]

    cost, sint = cost_ref[...], sint_ref[...]
    for r0 in range(0, ATTN_WIDTH, 512):
        t = _dot_nt(wt_ref[_FEAT_Q + r0:_FEAT_Q + r0 + 512, :], hb)
        for h0 in range(0, 512, HEAD_DIM):
            qt_ref[0, r0 + h0:r0 + h0 + HEAD_DIM, :] = _rope_rows(
                t[h0:h0 + HEAD_DIM], cost, sint).astype(BF16)
        zt_ref[0, r0:r0 + 512, :] = _silu(
            _dot_nt(wt_ref[_FEAT_Z + r0:_FEAT_Z + r0 + 512, :], hb)).astype(BF16)
    for row0, out_ref in ((_FEAT_VSEL, vsel_ref), (_FEAT_VWIN, vwin_ref)):
        t = _dot_nt(wt_ref[row0:row0 + KV_WIDTH, :], hb).astype(BF16)
        for c in range(tm // tk):
            out_ref[0, c] = t[:, c * tk:(c + 1) * tk]
    gate = _dot_nt(wt_ref[_FEAT_GATE:_FEAT_ROWS, :], hb)
    gate = 1.0 / (1.0 + jnp.exp(-gate))
    rows = N_BRANCH * HEADS_PER_STEP
    for hg in range(N_HEADS // HEADS_PER_STEP):
        gt_ref[0, hg] = _pad_rows(gate[hg * rows:(hg + 1) * rows], GATE_ROWS)

    n_ch = tm // CMP_STRIDE
    low_ch = _iota((n_ch, LANES), 1) < HEAD_DIM
    for kind, out_ref in enumerate((chk_ref, chv_ref)):
        for gp in range(G // 2):
            c = kind * (G // 2) + gp
            for l in range(0, CMP_STRIDE, 2):
                a0 = cmp_scr[c, pl.ds(l, n_ch, stride=CMP_STRIDE), :]
                a1 = cmp_scr[c, pl.ds(l + 1, n_ch, stride=CMP_STRIDE), :]
                dst = slice(l * HEAD_DIM, (l + 2) * HEAD_DIM)
                out_ref[0, 2 * gp, :, dst] = jnp.where(
                    low_ch, a0, pltpu.roll(a1, HEAD_DIM, axis=1)).astype(BF16)
                out_ref[0, 2 * gp + 1, :, dst] = jnp.where(
                    low_ch, pltpu.roll(a0, HEAD_DIM, axis=1), a1).astype(BF16)


def _nsa_in(ot, w_out, x, g, w_k, w_t, lane_tables, row_tables):
    B, S, _ = x.shape
    tm = min(PROJ_ROWS, S)
    tk = min(ATTN_TILE, S)
    W, G = ATTN_WIDTH, NSA_GROUPS
    const = lambda b, s: (0, 0)
    feat = lambda rows: pl.BlockSpec((1, rows, tm), lambda b, s: (b, 0, s))
    tiles = pl.BlockSpec((1, tm // tk, KV_WIDTH, tk), lambda b, s: (b, s, 0, 0))
    keys = pl.BlockSpec((1, G, tm, LANES), lambda b, s: (b, 0, s, 0))
    chunk_w = CMP_STRIDE * HEAD_DIM
    chunk = pl.BlockSpec((1, G, tm // CMP_STRIDE, chunk_w), lambda b, s: (b, 0, s, 0))
    tok = pl.BlockSpec((1, tm, D_MODEL), lambda b, s: (b, s, 0))
    n_hg = N_HEADS // HEADS_PER_STEP
    return pl.pallas_call(
        functools.partial(_nsa_in_kernel, tk=tk),
        name="nsa_in",
        grid=(B, S // tm),
        in_specs=[
            feat(W),
            pl.BlockSpec(w_out.shape, const),
            tok,
            pl.BlockSpec((1, D_MODEL), const),
            pl.BlockSpec(w_k.shape, const),
            pl.BlockSpec(w_t.shape, const),
        ] + [pl.BlockSpec((tm, LANES), lambda b, s: (s, 0))] * 3
          + [pl.BlockSpec((ROPE_HALF, tm), lambda b, s: (0, s))] * 2,
        out_specs=[tok, feat(W), keys, keys, chunk, chunk, tiles, tiles, feat(W),
                   pl.BlockSpec((1, n_hg, GATE_ROWS, tm), lambda b, s: (b, 0, 0, s))],
        out_shape=[
            jax.ShapeDtypeStruct((B, S, D_MODEL), F32),
            jax.ShapeDtypeStruct((B, W, S), BF16),
            jax.ShapeDtypeStruct((B, G, S, LANES), BF16),
            jax.ShapeDtypeStruct((B, G, S, LANES), BF16),
            jax.ShapeDtypeStruct((B, G, S // CMP_STRIDE, chunk_w), BF16),
            jax.ShapeDtypeStruct((B, G, S // CMP_STRIDE, chunk_w), BF16),
            jax.ShapeDtypeStruct((B, S // tk, KV_WIDTH, tk), BF16),
            jax.ShapeDtypeStruct((B, S // tk, KV_WIDTH, tk), BF16),
            jax.ShapeDtypeStruct((B, W, S), BF16),
            jax.ShapeDtypeStruct((B, n_hg, GATE_ROWS, S), F32),
        ],
        scratch_shapes=[pltpu.VMEM((2 * KV_WIDTH // LANES, tm, LANES), F32)],
        compiler_params=_params("parallel", "parallel"),
    )(ot, w_out, x, g, w_k, w_t, *lane_tables, *row_tables)


def _compress_kernel(chk_ref, chv_ref, w1k_ref, w2k_ref, pek_ref, w1v_ref, w2vt_ref, pev_ref,
                     cos_ref, slo_ref, shi_ref, kc_ref, vct_ref):
    half = CMP_STRIDE * HEAD_DIM

    def hidden(ch, w1_ref, pe_ref):
        a = _dot(ch, w1_ref[:half, :])
        b = _dot(ch, w1_ref[half:, :])
        b = pltpu.roll(b, b.shape[0] - 1, axis=0)
        pe = _dot(pe_ref[...], w1_ref[...])[0:1, :]
        return _silu(a + b + pe).astype(BF16)

    kc = _dot(hidden(chk_ref[0, 0], w1k_ref, pek_ref), w2k_ref[...])
    kc_ref[0, 0] = _rope_lanes(kc, cos_ref[...], slo_ref[...], shi_ref[...]).astype(BF16)
    vct_ref[0, 0] = _dot_nt(w2vt_ref[...], hidden(chv_ref[0, 0], w1v_ref, pev_ref)).astype(BF16)


def _compress(chk, chv, w1k, w2k, pek, w1v, w2vt, pev, tables):
    B, G, n_chunk, width = chk.shape
    const = lambda b, g: (0, 0)
    chunk_spec = pl.BlockSpec((1, 1, n_chunk, width), lambda b, g: (b, g, 0, 0))
    specs = lambda ws: [pl.BlockSpec(w.shape, const) for w in ws]
    return pl.pallas_call(
        _compress_kernel,
        name="nsa_compress",
        grid=(B, G),
        in_specs=[chunk_spec, chunk_spec] + specs((w1k, w2k, pek, w1v, w2vt, pev))
                 + [pl.BlockSpec((n_chunk, LANES), const)] * 3,
        out_specs=[pl.BlockSpec((1, 1, n_chunk, LANES), lambda b, g: (b, g, 0, 0)),
                   pl.BlockSpec((1, 1, HEAD_DIM, n_chunk), lambda b, g: (b, g, 0, 0))],
        out_shape=[jax.ShapeDtypeStruct((B, G, n_chunk, LANES), BF16),
                   jax.ShapeDtypeStruct((B, G, HEAD_DIM, n_chunk), BF16)],
        compiler_params=_params("parallel", "parallel"),
    )(chk, chv, w1k, w2k, pek, w1v, w2vt, pev, *tables)


def _local_attn_kernel(q_ref, kc_ref, vct_ref, ovt_ref, kw_ref, vw_ref, oc_ref, bias_ref, ow_ref,
                       *, t, n_cmp, n_blk, n_back):
    n_q = q_ref.shape[2] // t
    heads = range(NSA_REP)
    kc = kc_ref[0, 0]
    vct = vct_ref[0, 0]
    ovt = ovt_ref[...]
    n_pad = kc.shape[0]
    above = _iota((t, t), 0) > _iota((t, t), 1)
    ones = jnp.ones((ONES_ROWS, t), BF16)
    blk_c = _iota((n_pad, t), 0)
    blk = _iota((n_blk, t), 0)
    n_rounds = min(N_SELECT, n_blk)
    per_head = -(-n_rounds // NSA_REP)

    def q_aug(i, r):
        return _pad_rows(q_ref[0, r * HEAD_DIM:(r + 1) * HEAD_DIM, i * t:(i + 1) * t], LANES)

    masks = {0: jnp.logical_not(above), n_back: above}

    def win_tiles(i):
        return [(i - d, masks.get(d)) for d in range(n_back + 1) if i - d >= 0]

    def win_logits(i, r):
        return [_dot(kw_ref[0, 0, _tile(j, t), :], q_aug(i, r)) for j, _ in win_tiles(i)]

    def cmp_logits(i):
        return [_dot(kc, q_aug(i, r)) for r in heads]

    def importance(i, logits):
        cols = slice(i * t, (i + 1) * t)
        qry = i * t + _iota((n_pad, t), 1)
        valid = jnp.logical_and(blk_c * CMP_STRIDE + (CMP_LEN - 1) <= qry, blk_c < n_cmp)
        p_sum = jnp.zeros((n_pad, t), F32)
        for r in heads:
            s = jnp.where(valid, logits[r], NEG)
            e = jnp.exp2(s - _col_max(s))
            p = jnp.where(valid, e * (1.0 / _col_sum(e)), 0.0)
            p_sum = p_sum + p
            oc_ref[0, r * HEAD_DIM:(r + 1) * HEAD_DIM, cols] = _dot(
                vct, p.astype(BF16)).astype(oc_ref.dtype)
        p_hi, p_lo = _split2(p_sum)
        imp = _dot(ovt, p_hi) + _dot(ovt, p_lo)
        cur = (i * t + _iota(imp.shape, 1)) // SEL_LEN
        forced = jnp.logical_or(blk == 0, jnp.logical_or(blk == cur, blk == cur - 1))
        return jnp.where(forced, FORCE, jnp.where(blk <= cur, imp, -1.0))

    items = [(i, r) for i in range(n_q) for r in heads]
    win_queue, cmp_queue = [], {}

    def issue(n):
        if n < len(items):
            win_queue.append(win_logits(*items[n]))
            if items[n][1] == 0:
                cmp_queue[items[n][0]] = cmp_logits(items[n][0])

    for n in range(LOOKAHEAD):
        issue(n)
    for n, (i, r) in enumerate(items):
        issue(n + LOOKAHEAD)
        logits = win_queue.pop(0)
        if r == 0:
            imp = importance(i, cmp_queue.pop(i))
            keep = jnp.zeros(imp.shape, jnp.bool_)
            done = 0
        tiles = win_tiles(i)
        values = lambda j: jnp.concatenate([vw_ref[0, j], ones], axis=0)
        if len(tiles) == n_back + 1:
            diag = masks[0]
            s_fold = jnp.where(diag, logits[0], logits[-1])
            full = logits[1:-1]
            m = functools.reduce(jnp.maximum, [_col_max(s) for s in [s_fold] + full])
            p_fold = jnp.exp2(s_fold - m)
            zero = jnp.zeros_like(p_fold)
            parts = [(tiles[0][0], jnp.where(diag, p_fold, zero)),
                     (tiles[-1][0], jnp.where(diag, zero, p_fold))]
            parts += [(j, jnp.exp2(s - m)) for s, (j, _) in zip(full, tiles[1:-1])]
        else:
            ss = [s if mask is None else jnp.where(mask, s, NEG)
                  for s, (_, mask) in zip(logits, tiles)]
            m = functools.reduce(jnp.maximum, [_col_max(s) for s in ss])
            parts = [(j, jnp.exp2(s - m)) for s, (j, _) in zip(ss, tiles)]
        acc = functools.reduce(jnp.add, [_dot(values(j), p.astype(BF16)) for j, p in parts])
        ow_ref[0, r * HEAD_DIM:(r + 1) * HEAD_DIM, i * t:(i + 1) * t] = _normalised(acc).astype(
            ow_ref.dtype)
        for _ in range(min(per_head, n_rounds - done)):
            top = _col_max(imp)
            first = jnp.min(jnp.where(imp == top, blk, n_blk), axis=0, keepdims=True)
            pick = blk == first
            keep = jnp.logical_or(keep, pick)
            imp = jnp.where(pick, -jnp.inf, imp)
            done += 1
        if r == NSA_REP - 1:
            bias_ref[0, 0, :, i * t:(i + 1) * t] = jnp.where(keep, 0.0, NEG).astype(bias_ref.dtype)


def _local_attn(qt, kc, vct, ovt, n_cmp, kw, vw):
    B, W, S = qt.shape
    G = kc.shape[1]
    t = min(ATTN_TILE, S)
    assert WINDOW % t == 0
    n_blk = ovt.shape[0]
    kern = functools.partial(_local_attn_kernel, t=t, n_cmp=n_cmp, n_blk=n_blk, n_back=WINDOW // t)
    feat = pl.BlockSpec((1, W // G, S), lambda b, g: (b, g, 0))
    return pl.pallas_call(
        kern,
        name="nsa_local_attn",
        grid=(B, G),
        in_specs=[
            feat,
            pl.BlockSpec((1, 1) + kc.shape[2:], lambda b, g: (b, g, 0, 0)),
            pl.BlockSpec((1, 1) + vct.shape[2:], lambda b, g: (b, g, 0, 0)),
            pl.BlockSpec(ovt.shape, lambda b, g: (0, 0)),
            pl.BlockSpec((1, 1, S, LANES), lambda b, g: (b, g, 0, 0)),
            pl.BlockSpec((1, S // t, HEAD_DIM, t), lambda b, g: (b, 0, g, 0)),
        ],
        out_specs=[feat, pl.BlockSpec((1, 1, n_blk, S), lambda b, g: (b, g, 0, 0)), feat],
        out_shape=[
            jax.ShapeDtypeStruct((B, W, S), BF16),
            jax.ShapeDtypeStruct((B, G, n_blk, S), BF16),
            jax.ShapeDtypeStruct((B, W, S), BF16),
        ],
        compiler_params=_params("parallel", "parallel"),
    )(qt, kc, vct, ovt, kw, vw)


def _sel_attn_kernel(q_ref, bias_ref, k_ref, v_ref, oc_ref, ow_ref, gt_ref, z_ref, o_ref,
                     s_scr, acc_scr, *, t):
    nh = s_scr.shape[1]
    n_q = q_ref.shape[2] // t

    def query(i, slot, hh):
        cols = slice(i * t, (i + 1) * t)
        bias = _pad_rows(bias_ref[0, hh // NSA_REP, :, cols], HEAD_DIM)
        return jnp.concatenate([q_ref[0, hh * HEAD_DIM:(hh + 1) * HEAD_DIM, cols], bias], axis=0)

    def key_tile(j, hh):
        return k_ref[0, hh // NSA_REP, _tile(j, t), :]

    def value_rows(j, hh):
        g = hh // NSA_REP
        return v_ref[0, j, g * HEAD_DIM:(g + 1) * HEAD_DIM, :]

    def emit(i, accs):
        cols = slice(i * t, (i + 1) * t)
        for hh in range(nh):
            rows = slice(hh * HEAD_DIM, (hh + 1) * HEAD_DIM)
            gate = lambda b: gt_ref[0, 0, N_BRANCH * hh + b:N_BRANCH * hh + b + 1, cols]
            o = (gate(0) * oc_ref[0, rows, cols].astype(F32) + gate(1) * _normalised(accs[hh])
                 + gate(2) * ow_ref[0, rows, cols].astype(F32))
            o_ref[0, rows, cols] = (o * z_ref[0, rows, cols].astype(F32)).astype(o_ref.dtype)

    _causal_attention(n_q, t, lambda i, slot: None, query, key_tile, value_rows, emit,
                      s_scr, acc_scr)


def _sel_attn(qt, bias, k_aug, vt, oc, ow, gt, zt):
    B, W, S = qt.shape
    t = min(ATTN_TILE, S)
    ng = HEADS_PER_STEP // NSA_REP
    n_blk = bias.shape[2]
    feat = pl.BlockSpec((1, HEADS_PER_STEP * HEAD_DIM, S), lambda b, g: (b, g, 0))
    return pl.pallas_call(
        functools.partial(_sel_attn_kernel, t=t),
        name="nsa_sel_attn",
        grid=(B, NSA_GROUPS // ng),
        in_specs=[
            feat,
            pl.BlockSpec((1, ng, n_blk, S), lambda b, g: (b, g, 0, 0)),
            pl.BlockSpec((1, ng, S, LANES), lambda b, g: (b, g, 0, 0)),
            pl.BlockSpec((1, S // t, ng * HEAD_DIM, t), lambda b, g: (b, 0, g, 0)),
            feat, feat,
            pl.BlockSpec((1, 1, GATE_ROWS, S), lambda b, g: (b, g, 0, 0)),
            feat,
        ],
        out_specs=feat,
        out_shape=jax.ShapeDtypeStruct((B, W, S), BF16),
        scratch_shapes=_attn_scratch(HEADS_PER_STEP, t),
        compiler_params=_params("parallel", "parallel"),
    )(qt, bias, k_aug, vt, oc, ow, gt, zt)


def _fox_mixer(x, g, w_in, b_f):
    B, S, _ = x.shape
    W, H = ATTN_WIDTH, N_HEADS
    w_k = w_in[:, W:2 * W].astype(BF16)
    w_t = jnp.concatenate([w_in[:, :W] * Q_SCALE, w_in[:, 2 * W:3 * W], w_in[:, 3 * W + H:]],
                          axis=1).T.astype(BF16)
    wf_hi, wf_lo = _split2(w_in[:, 3 * W:3 * W + H])
    pad = lambda n: jnp.zeros((D_MODEL, LANES - n * H), BF16)
    wf_a = jnp.concatenate([wf_hi] * 3 + [wf_lo] * 3 + [pad(6)], axis=1)
    wf_b = jnp.concatenate([wf_hi] * 3 + [pad(3)], axis=1)
    bf_row = jnp.concatenate([b_f, b_f, b_f, jnp.zeros((LANES - 3 * H,), F32)])[None, :]

    qt, k, vt, zt, cx = _fox_in(x, g[None, :], w_k, w_t, wf_a, wf_b, bf_row)

    t = min(ATTN_TILE, S)
    r = np.arange(LANES)
    sel = np.logical_and(r[None, :] % H == np.arange(H)[:, None], r[None, :] < 3 * H)
    qx = np.broadcast_to(np.where(sel, -1.0, 0.0).astype(np.float32)[:, :, None], (H, LANES, t))
    return _fox_attn(qt, jnp.asarray(qx, BF16), k, cx, vt, zt)


def _nsa_layer(ot0, w_out0, x, g, w_in, pe_k, w_ck1, w_ck2, pe_v, w_cv1, w_cv2, w_out, final_g):
    B, S, _ = x.shape
    W, KV, G = ATTN_WIDTH, KV_WIDTH, NSA_GROUPS
    sec = lambda j: w_in[:, W + j * KV:W + (j + 1) * KV]
    gate_off = W + 6 * KV
    n_gate = N_BRANCH * N_HEADS
    w_k = jnp.concatenate([sec(2), sec(4), sec(0), sec(1)], axis=1).astype(BF16)
    w_t = jnp.concatenate([w_in[:, :W] * Q_SCALE, sec(3), sec(5), w_in[:, gate_off + n_gate:],
                           w_in[:, gate_off:gate_off + n_gate]], axis=1).T.astype(BF16)
    pos = np.arange(S)
    cos, sin = _rope_angles(pos)
    x, qt, ksel, kwin, chk, chv, vsel, vwin, zt, gt = _nsa_in(
        ot0, w_out0.astype(BF16), x, g[None, :], w_k, w_t, _rope_lane_tables(pos),
        (np.ascontiguousarray(cos.T), np.ascontiguousarray(sin.T)))

    n_chunk = S // CMP_STRIDE
    n_cmp = n_chunk - CMP_LEN // CMP_STRIDE + 1

    def flat_pe(pe):
        pe = pe.reshape(1, CMP_LEN * HEAD_DIM)
        return jnp.broadcast_to(pe, (SUBLANES, CMP_LEN * HEAD_DIM)).astype(BF16)

    w2k = jnp.concatenate([w_ck2, jnp.zeros_like(w_ck2)], axis=1).astype(BF16)
    cmp_end = np.arange(n_chunk) * CMP_STRIDE + CMP_LEN - 1
    kc, vct = _compress(chk, chv, w_ck1.astype(BF16), w2k, flat_pe(pe_k),
                        w_cv1.astype(BF16), w_cv2.T.astype(BF16), flat_pe(pe_v),
                        _rope_lane_tables(cmp_end))

    n_blk = S // SEL_LEN
    ci = np.arange(n_chunk) * CMP_STRIDE
    sj = np.arange(n_blk) * SEL_LEN
    ovt = np.logical_and(ci[None, :] < sj[:, None] + SEL_LEN, ci[None, :] + CMP_LEN > sj[:, None])
    ovt = np.logical_and(ovt, np.arange(n_chunk)[None, :] < n_cmp).astype(np.float32)
    oc, bias, ow = _local_attn(qt, kc, vct, jnp.asarray(ovt, BF16), n_cmp, kwin, vwin)
    ot = _sel_attn(qt, bias, ksel, vsel, oc, ow, gt, zt)
    return _final_proj(ot, x, w_out.astype(BF16), final_g[None, :])


def kernel(x, norm_g, fox_w_in, fox_b_f, fox_w_out, nsa_w_in, nsa_pe_k, nsa_w_ck1, nsa_w_ck2,
           nsa_pe_v, nsa_w_cv1, nsa_w_cv2, nsa_w_out, final_g):
    ot0 = _fox_mixer(x, norm_g[0], fox_w_in[0], fox_b_f[0])
    return _nsa_layer(ot0, fox_w_out[0], x, norm_g[1], nsa_w_in[0], nsa_pe_k[0], nsa_w_ck1[0],
                      nsa_w_ck2[0], nsa_pe_v[0], nsa_w_cv1[0], nsa_w_cv2[0], nsa_w_out[0], final_g)
```

```python
import functools
import math

import jax
import jax.numpy as jnp
import numpy as np
from jax import lax
from jax.experimental import pallas as pl
from jax.experimental.pallas import tpu as pltpu

F32 = jnp.float32
BF16 = jnp.bfloat16

D_MODEL = 1024
N_HEADS = 16
HEAD_DIM = 64
ATTN_WIDTH = N_HEADS * HEAD_DIM
NSA_GROUPS = 4
NSA_REP = N_HEADS // NSA_GROUPS
KV_WIDTH = NSA_GROUPS * HEAD_DIM
CMP_LEN = 32
CMP_STRIDE = 16
CMP_HIDDEN = 256
SEL_LEN = 64
N_SELECT = 8
WINDOW = 512
N_BRANCH = 3
ROPE_THETA = 500000.0
ROPE_DIM = HEAD_DIM // 4
ROPE_HALF = ROPE_DIM // 2
NORM_EPS = 1e-6
NEG = -1e30
FORCE = 1e6
LOG2E = math.log2(math.e)
Q_SCALE = HEAD_DIM ** -0.5 * LOG2E

LANES = 128
SUBLANES = 8
ONES_ROWS = 2 * SUBLANES
PROJ_ROWS = 1024
OUT_ROWS = 1024
ATTN_TILE = 256
HEADS_PER_STEP = 8
LOOKAHEAD = 1
GATE_ROWS = -(-N_BRANCH * HEADS_PER_STEP // SUBLANES) * SUBLANES
VMEM_LIMIT = 56 * 1024 * 1024


def _params(*sem):
    return pltpu.CompilerParams(dimension_semantics=sem, vmem_limit_bytes=VMEM_LIMIT)


def _iota(shape, dim):
    return lax.broadcasted_iota(jnp.int32, shape, dim)


def _split2(x):
    hi = x.astype(BF16)
    lo = (x - hi.astype(F32)).astype(BF16)
    return hi, lo


def _split3(x):
    hi = x.astype(BF16)
    r1 = x - hi.astype(F32)
    mid = r1.astype(BF16)
    lo = (r1 - mid.astype(F32)).astype(BF16)
    return hi, mid, lo


def _dot(a, b):
    return jnp.dot(a, b, preferred_element_type=F32)


def _dot_nt(a, b):
    return lax.dot_general(a, b, (((1,), (1,)), ((), ())), preferred_element_type=F32)


def _dot_tn(a, b):
    return lax.dot_general(a, b, (((0,), (0,)), ((), ())), preferred_element_type=F32)


def _rmsnorm(x, g):
    ms = jnp.mean(x * x, axis=-1, keepdims=True)
    return x * lax.rsqrt(ms + NORM_EPS) * g


def _silu(x):
    return x * (1.0 / (1.0 + jnp.exp(-x)))


def _col_reduce(x, op, reduce):
    rows = x.shape[0]
    if rows % (4 * SUBLANES) == 0 and rows >= 8 * SUBLANES:
        q = rows // 4
        x = op(op(x[:q], x[q:2 * q]), op(x[2 * q:3 * q], x[3 * q:]))
    return reduce(x, axis=0, keepdims=True)


def _col_max(x):
    return _col_reduce(x, jnp.maximum, jnp.max)


def _col_sum(x):
    return _col_reduce(x, jnp.add, jnp.sum)


def _pad_rows(x, rows):
    if rows == x.shape[0]:
        return x
    return jnp.concatenate([x, jnp.zeros((rows - x.shape[0], x.shape[1]), x.dtype)], axis=0)


def _tile(j, t):
    return pl.ds(j * t if isinstance(j, int) else pl.multiple_of(j * t, t), t)


def _normalised(acc):
    return acc[:HEAD_DIM] * (1.0 / acc[HEAD_DIM:HEAD_DIM + 1])


def _rope_lanes(x, cos, sin_lo, sin_hi):
    return (x * cos + pltpu.roll(x, LANES - ROPE_HALF, axis=1) * sin_lo
            + pltpu.roll(x, ROPE_HALF, axis=1) * sin_hi)


def _rope_rows(x, cos, sin):
    x1, x2 = x[:ROPE_HALF], x[ROPE_HALF:ROPE_DIM]
    return jnp.concatenate([x1 * cos - x2 * sin, x1 * sin + x2 * cos, x[ROPE_DIM:]], axis=0)


def _rope_angles(pos):
    inv_freq = np.power(np.float32(ROPE_THETA),
                        -np.arange(ROPE_HALF, dtype=np.float32) * np.float32(2.0 / ROPE_DIM))
    ang = pos.astype(np.float32)[:, None] * inv_freq[None, :].astype(np.float32)
    return np.cos(ang).astype(np.float32), np.sin(ang).astype(np.float32)


def _rope_lane_tables(pos):
    cos, sin = _rope_angles(pos)
    n = pos.shape[0]
    ones = np.ones((n, HEAD_DIM - ROPE_DIM), np.float32)
    zeros = np.zeros((n, HEAD_DIM - ROPE_DIM), np.float32)
    z8 = np.zeros((n, ROPE_HALF), np.float32)
    c = np.concatenate([cos, cos, ones], axis=1)
    s_lo = np.concatenate([-sin, z8, zeros], axis=1)
    s_hi = np.concatenate([z8, sin, zeros], axis=1)
    tile = lambda t: np.concatenate([t, t], axis=1)
    return tile(c), tile(s_lo), tile(s_hi)


def _fox_in_kernel(x_ref, g_ref, wk_ref, wt_ref, wfa_ref, wfb_ref, bf_ref, tri_ref,
                   qt_ref, k_ref, vt_ref, zt_ref, cx_ref, carry_ref, *, tk):
    s = pl.program_id(1)
    h = _rmsnorm(x_ref[0], g_ref[...])
    hb = h.astype(BF16)
    tm = hb.shape[0]
    W, H = ATTN_WIDTH, N_HEADS

    @pl.when(s == 0)
    def _():
        carry_ref[...] = jnp.zeros_like(carry_ref)

    h_lo = (h - hb.astype(F32)).astype(BF16)
    fa = _dot(hb, wfa_ref[...])
    f = fa + pltpu.roll(fa, LANES - 3 * H, axis=1) + _dot(h_lo, wfb_ref[...]) + bf_ref[...]

    for n0 in range(0, W, 512):
        k_ref[0, :, n0:n0 + 512] = _dot(hb, wk_ref[:, n0:n0 + 512]).astype(BF16)

    log_f = jnp.minimum(f, 0.0) - jnp.log1p(jnp.exp(-jnp.abs(f)))
    lane = _iota(log_f.shape, 1)
    a0, a1, a2 = _split3(log_f)
    pieces = jnp.where(lane < H, a0, jnp.where(lane < 2 * H, a1, a2))
    c = _dot(tri_ref[...], pieces)

    for r0 in list(range(2 * W, 3 * W, 512)) + list(range(0, 2 * W, 512)):
        t = _dot_nt(wt_ref[r0:r0 + 512, :], hb)
        if r0 < W:
            qt_ref[0, r0:r0 + 512, :] = t.astype(BF16)
        elif r0 < 2 * W:
            t = t.astype(BF16)
            for ct in range(tm // tk):
                vt_ref[0, ct, r0 - W:r0 - W + 512, :] = t[:, ct * tk:(ct + 1) * tk]
        else:
            zt_ref[0, r0 - 2 * W:r0 - 2 * W + 512, :] = _silu(t).astype(BF16)

    c = c + pltpu.roll(c, LANES - H, axis=1) + pltpu.roll(c, LANES - 2 * H, axis=1)
    c = c + carry_ref[0:1, :]
    carry_ref[...] = jnp.broadcast_to(c[tm - 1:tm, :], carry_ref.shape)
    c = jnp.where(lane < H, c, jnp.where(lane < 2 * H, pltpu.roll(c, H, axis=1),
                                         pltpu.roll(c, 2 * H, axis=1)))

    hi, mid, lo = _split3(c * LOG2E)
    zero = jnp.zeros_like(hi)
    cx_ref[0] = jnp.where(lane < H, hi,
                          jnp.where(lane < 2 * H, mid, jnp.where(lane < 3 * H, lo, zero)))


def _fox_in(x, g, w_k, w_t, wf_a, wf_b, bf_row):
    B, S, _ = x.shape
    tm = min(PROJ_ROWS, S)
    tk = min(ATTN_TILE, S)
    W = ATTN_WIDTH
    tri = np.tril(np.ones((tm, tm), np.float32)).astype(BF16)
    const = lambda b, s: (0, 0)
    feat = pl.BlockSpec((1, W, tm), lambda b, s: (b, 0, s))
    return pl.pallas_call(
        functools.partial(_fox_in_kernel, tk=tk),
        name="fox_in",
        grid=(B, S // tm),
        in_specs=[
            pl.BlockSpec((1, tm, D_MODEL), lambda b, s: (b, s, 0)),
            pl.BlockSpec((1, D_MODEL), const),
            pl.BlockSpec(w_k.shape, const),
            pl.BlockSpec(w_t.shape, const),
            pl.BlockSpec((D_MODEL, LANES), const),
            pl.BlockSpec((D_MODEL, LANES), const),
            pl.BlockSpec((1, LANES), const),
            pl.BlockSpec((tm, tm), const),
        ],
        out_specs=[
            feat,
            pl.BlockSpec((1, tm, W), lambda b, s: (b, s, 0)),
            pl.BlockSpec((1, tm // tk, W, tk), lambda b, s: (b, s, 0, 0)),
            feat,
            pl.BlockSpec((1, tm, LANES), lambda b, s: (b, s, 0)),
        ],
        out_shape=[
            jax.ShapeDtypeStruct((B, W, S), BF16),
            jax.ShapeDtypeStruct((B, S, W), BF16),
            jax.ShapeDtypeStruct((B, S // tk, W, tk), BF16),
            jax.ShapeDtypeStruct((B, W, S), BF16),
            jax.ShapeDtypeStruct((B, S, LANES), BF16),
        ],
        scratch_shapes=[pltpu.VMEM((SUBLANES, LANES), F32)],
        compiler_params=_params("parallel", "arbitrary"),
    )(x, g, w_k, w_t, wf_a, wf_b, bf_row, tri)


def _causal_attention(n_q, t, prepare_q, query, key_tile, value_rows, emit, s_scr, acc_scr):
    heads = range(s_scr.shape[1])
    ones = jnp.ones((ONES_ROWS, t), BF16)
    causal = _iota((t, t), 0) <= _iota((t, t), 1)

    def logits(i, slot, j):
        return [_dot(key_tile(j, hh), query(i, slot, hh)) for hh in heads]

    def pv(j, hh, p):
        return _dot(jnp.concatenate([value_rows(j, hh), ones], axis=0), p.astype(BF16))

    def finish(i, slot, carry):
        accs = []
        for hh in heads:
            m, alpha = carry[hh]
            accs.append(alpha * acc_scr[hh] + pv(i - 1, hh, jnp.exp2(s_scr[slot, hh] - m)))
        emit(i, accs)

    pending = None
    for i in range(n_q):
        slot = i % 2
        prepare_q(i, slot)
        s_diag = logits(i, slot, i)
        s_first = logits(i, slot, 0) if i else None
        if pending is not None:
            finish(*pending)
        m_diag = []
        for hh in heads:
            s = jnp.where(causal, s_diag[hh], NEG)
            m = _col_max(s)
            acc_scr[hh] = pv(i, hh, jnp.exp2(s - m))
            m_diag.append(m)
        if i == 0:
            emit(0, [acc_scr[hh] for hh in heads])
            continue
        carry = []
        for hh in heads:
            s_scr[slot, hh] = s_first[hh]
            m_new = jnp.maximum(m_diag[hh], _col_max(s_first[hh]))
            carry.append((m_new, jnp.exp2(m_diag[hh] - m_new)))

        def body(j, carry, i=i, slot=slot):
            s_next, pvs = [], []
            for hh in heads:
                s_next.append(_dot(key_tile(j + 1, hh), query(i, slot, hh)))
                pvs.append(pv(j, hh, jnp.exp2(s_scr[slot, hh] - carry[hh][0])))
            out = []
            for hh in heads:
                m = carry[hh][0]
                s_scr[slot, hh] = s_next[hh]
                m_new = jnp.maximum(m, _col_max(s_next[hh]))
                out.append((m_new, jnp.exp2(m - m_new)))
            for hh in heads:
                acc_scr[hh] = carry[hh][1] * acc_scr[hh] + pvs[hh]
            return tuple(out)

        carry = lax.fori_loop(0, i - 1, body, tuple(carry))
        pending = (i, slot, carry)
    if pending is not None:
        finish(*pending)


def _attn_scratch(nh, t):
    return [pltpu.VMEM((2, nh, t, t), F32), pltpu.VMEM((nh, HEAD_DIM + ONES_ROWS, t), F32)]


def _fox_attn_kernel(q_ref, qx_ref, k_ref, kx_ref, v_ref, z_ref, o_ref, q_scr, s_scr, acc_scr, *, t):
    nh = q_scr.shape[1]
    n_q = q_ref.shape[2] // t
    row = _iota((LANES, t), 0)

    def load_q(i, slot):
        for hh in range(nh):
            pair, sub = divmod(hh, 2)
            qt = q_ref[0, pair * LANES:(pair + 1) * LANES, i * t:(i + 1) * t]
            own = jnp.logical_and(row >= sub * HEAD_DIM, row < (sub + 1) * HEAD_DIM)
            q_scr[slot, hh] = jnp.where(own, qt, jnp.zeros_like(qt))

    def query(i, slot, hh):
        return jnp.concatenate([q_scr[slot, hh], qx_ref[hh]], axis=0)

    def key_tile(j, hh):
        pair = hh // 2
        return jnp.concatenate([k_ref[0, _tile(j, t), pair * LANES:(pair + 1) * LANES],
                                kx_ref[0, _tile(j, t), :]], axis=1)

    def value_rows(j, hh):
        return v_ref[0, j, hh * HEAD_DIM:(hh + 1) * HEAD_DIM, :]

    def emit(i, accs):
        for hh in range(nh):
            rows, cols = slice(hh * HEAD_DIM, (hh + 1) * HEAD_DIM), slice(i * t, (i + 1) * t)
            gated = _normalised(accs[hh]) * z_ref[0, rows, cols].astype(F32)
            o_ref[0, rows, cols] = gated.astype(o_ref.dtype)

    _causal_attention(n_q, t, load_q, query, key_tile, value_rows, emit, s_scr, acc_scr)


def _fox_attn(qt, qx, k, kx, vt, zt):
    B, W, S = qt.shape
    t = min(ATTN_TILE, S)
    nh = HEADS_PER_STEP
    rows = nh * HEAD_DIM
    feat = pl.BlockSpec((1, rows, S), lambda b, p: (b, p, 0))
    return pl.pallas_call(
        functools.partial(_fox_attn_kernel, t=t),
        name="fox_attn",
        grid=(B, N_HEADS // nh),
        in_specs=[
            feat,
            pl.BlockSpec((nh, LANES, t), lambda b, p: (p, 0, 0)),
            pl.BlockSpec((1, S, rows), lambda b, p: (b, 0, p)),
            pl.BlockSpec((1, S, LANES), lambda b, p: (b, 0, 0)),
            pl.BlockSpec((1, S // t, rows, t), lambda b, p: (b, 0, p, 0)),
            feat,
        ],
        out_specs=feat,
        out_shape=jax.ShapeDtypeStruct((B, W, S), BF16),
        scratch_shapes=[pltpu.VMEM((2, nh, LANES, t), BF16)] + _attn_scratch(nh, t),
        compiler_params=_params("parallel", "parallel"),
    )(qt, qx, k, kx, vt, zt)


def _final_proj_kernel(ot_ref, x_ref, w_ref, g_ref, y_ref):
    y = x_ref[0] + _dot_tn(ot_ref[0], w_ref[...])
    y_ref[0] = _rmsnorm(y, g_ref[...])


def _final_proj(ot, x, w_out, final_g):
    B, S, _ = x.shape
    tm = min(OUT_ROWS, S)
    const = lambda b, s: (0, 0)
    tok = pl.BlockSpec((1, tm, D_MODEL), lambda b, s: (b, s, 0))
    return pl.pallas_call(
        _final_proj_kernel,
        name="final_proj",
        grid=(B, S // tm),
        in_specs=[pl.BlockSpec((1, ATTN_WIDTH, tm), lambda b, s: (b, 0, s)), tok,
                  pl.BlockSpec((ATTN_WIDTH, D_MODEL), const), pl.BlockSpec((1, D_MODEL), const)],
        out_specs=tok,
        out_shape=jax.ShapeDtypeStruct((B, S, D_MODEL), F32),
        compiler_params=_params("parallel", "parallel"),
    )(ot, x, w_out, final_g)


_TOK_SEL = 0
_TOK_WIN = KV_WIDTH
_TOK_CMP = 2 * KV_WIDTH
_TOK_COLS = 4 * KV_WIDTH
_FEAT_Q = 0
_FEAT_VSEL = ATTN_WIDTH
_FEAT_VWIN = _FEAT_VSEL + KV_WIDTH
_FEAT_Z = _FEAT_VWIN + KV_WIDTH
_FEAT_GATE = _FEAT_Z + ATTN_WIDTH
_FEAT_ROWS = _FEAT_GATE + N_BRANCH * N_HEADS


def _nsa_in_kernel(ot_ref, wo_ref, x_ref, g_ref, wk_ref, wt_ref, cos_ref, slo_ref, shi_ref,
                   cost_ref, sint_ref,
                   x1_ref, qt_ref, ksel_ref, kwin_ref, chk_ref, chv_ref, vsel_ref, vwin_ref, zt_ref,
                   gt_ref, cmp_scr, *, tk):
    s = pl.program_id(1)
    x1 = x_ref[0] + _dot_tn(ot_ref[0], wo_ref[...])
    x1_ref[0] = x1
    hb = _rmsnorm(x1, g_ref[...]).astype(BF16)
    tm = hb.shape[0]
    G = NSA_GROUPS

    cos, slo, shi = cos_ref[...], slo_ref[...], shi_ref[...]
    tok = s * tm + _iota((tm, LANES), 0)
    lane = _iota((tm, LANES), 1)
    low = lane < HEAD_DIM
    block_id = jnp.where(lane - HEAD_DIM == tok // SEL_LEN, 1.0, 0.0)
    keys = _dot(hb, wk_ref[:, _TOK_SEL:_TOK_CMP])
    for col0, out_ref, extra in ((_TOK_SEL, ksel_ref, block_id), (_TOK_WIN, kwin_ref, 0.0)):
        for gp in range(G // 2):
            pair = _rope_lanes(keys[:, col0 + gp * LANES:col0 + (gp + 1) * LANES], cos, slo, shi)
            out_ref[0, 2 * gp] = jnp.where(low, pair, extra).astype(BF16)
            out_ref[0, 2 * gp + 1] = jnp.where(
                low, pltpu.roll(pair, HEAD_DIM, axis=1), extra).astype(BF16)

    raw = _dot(hb, wk_ref[:, _TOK_CMP:_TOK_COLS])
    for c in range(cmp_scr.shape[0]):
        cmp_scr[c] = raw[:, c * LANES:(c + 1) * LANES]

    cost, sint = cost_ref[...], sint_ref[...]
    for r0 in range(0, ATTN_WIDTH, 512):
        t = _dot_nt(wt_ref[_FEAT_Q + r0:_FEAT_Q + r0 + 512, :], hb)
        for h0 in range(0, 512, HEAD_DIM):
            qt_ref[0, r0 + h0:r0 + h0 + HEAD_DIM, :] = _rope_rows(
                t[h0:h0 + HEAD_DIM], cost, sint).astype(BF16)
        zt_ref[0, r0:r0 + 512, :] = _silu(
            _dot_nt(wt_ref[_FEAT_Z + r0:_FEAT_Z + r0 + 512, :], hb)).astype(BF16)
    for row0, out_ref in ((_FEAT_VSEL, vsel_ref), (_FEAT_VWIN, vwin_ref)):
        t = _dot_nt(wt_ref[row0:row0 + KV_WIDTH, :], hb).astype(BF16)
        for c in range(tm // tk):
            out_ref[0, c] = t[:, c * tk:(c + 1) * tk]
    gate = _dot_nt(wt_ref[_FEAT_GATE:_FEAT_ROWS, :], hb)
    gate = 1.0 / (1.0 + jnp.exp(-gate))
    rows = N_BRANCH * HEADS_PER_STEP
    for hg in range(N_HEADS // HEADS_PER_STEP):
        gt_ref[0, hg] = _pad_rows(gate[hg * rows:(hg + 1) * rows], GATE_ROWS)

    n_ch = tm // CMP_STRIDE
    low_ch = _iota((n_ch, LANES), 1) < HEAD_DIM
    for kind, out_ref in enumerate((chk_ref, chv_ref)):
        for gp in range(G // 2):
            c = kind * (G // 2) + gp
            for l in range(0, CMP_STRIDE, 2):
                a0 = cmp_scr[c, pl.ds(l, n_ch, stride=CMP_STRIDE), :]
                a1 = cmp_scr[c, pl.ds(l + 1, n_ch, stride=CMP_STRIDE), :]
                dst = slice(l * HEAD_DIM, (l + 2) * HEAD_DIM)
                out_ref[0, 2 * gp, :, dst] = jnp.where(
                    low_ch, a0, pltpu.roll(a1, HEAD_DIM, axis=1)).astype(BF16)
                out_ref[0, 2 * gp + 1, :, dst] = jnp.where(
                    low_ch, pltpu.roll(a0, HEAD_DIM, axis=1), a1).astype(BF16)


def _nsa_in(ot, w_out, x, g, w_k, w_t, lane_tables, row_tables):
    B, S, _ = x.shape
    tm = min(PROJ_ROWS, S)
    tk = min(ATTN_TILE, S)
    W, G = ATTN_WIDTH, NSA_GROUPS
    const = lambda b, s: (0, 0)
    feat = lambda rows: pl.BlockSpec((1, rows, tm), lambda b, s: (b, 0, s))
    tiles = pl.BlockSpec((1, tm // tk, KV_WIDTH, tk), lambda b, s: (b, s, 0, 0))
    keys = pl.BlockSpec((1, G, tm, LANES), lambda b, s: (b, 0, s, 0))
    chunk_w = CMP_STRIDE * HEAD_DIM
    chunk = pl.BlockSpec((1, G, tm // CMP_STRIDE, chunk_w), lambda b, s: (b, 0, s, 0))
    tok = pl.BlockSpec((1, tm, D_MODEL), lambda b, s: (b, s, 0))
    n_hg = N_HEADS // HEADS_PER_STEP
    return pl.pallas_call(
        functools.partial(_nsa_in_kernel, tk=tk),
        name="nsa_in",
        grid=(B, S // tm),
        in_specs=[
            feat(W),
            pl.BlockSpec(w_out.shape, const),
            tok,
            pl.BlockSpec((1, D_MODEL), const),
            pl.BlockSpec(w_k.shape, const),
            pl.BlockSpec(w_t.shape, const),
        ] + [pl.BlockSpec((tm, LANES), lambda b, s: (s, 0))] * 3
          + [pl.BlockSpec((ROPE_HALF, tm), lambda b, s: (0, s))] * 2,
        out_specs=[tok, feat(W), keys, keys, chunk, chunk, tiles, tiles, feat(W),
                   pl.BlockSpec((1, n_hg, GATE_ROWS, tm), lambda b, s: (b, 0, 0, s))],
        out_shape=[
            jax.ShapeDtypeStruct((B, S, D_MODEL), F32),
            jax.ShapeDtypeStruct((B, W, S), BF16),
            jax.ShapeDtypeStruct((B, G, S, LANES), BF16),
            jax.ShapeDtypeStruct((B, G, S, LANES), BF16),
            jax.ShapeDtypeStruct((B, G, S // CMP_STRIDE, chunk_w), BF16),
            jax.ShapeDtypeStruct((B, G, S // CMP_STRIDE, chunk_w), BF16),
            jax.ShapeDtypeStruct((B, S // tk, KV_WIDTH, tk), BF16),
            jax.ShapeDtypeStruct((B, S // tk, KV_WIDTH, tk), BF16),
            jax.ShapeDtypeStruct((B, W, S), BF16),
            jax.ShapeDtypeStruct((B, n_hg, GATE_ROWS, S), F32),
        ],
        scratch_shapes=[pltpu.VMEM((2 * KV_WIDTH // LANES, tm, LANES), F32)],
        compiler_params=_params("parallel", "parallel"),
    )(ot, w_out, x, g, w_k, w_t, *lane_tables, *row_tables)


def _compress_kernel(chk_ref, chv_ref, w1k_ref, w2k_ref, pek_ref, w1v_ref, w2vt_ref, pev_ref,
                     cos_ref, slo_ref, shi_ref, kc_ref, vct_ref, pe_scr):
    half = CMP_STRIDE * HEAD_DIM

    @pl.when(jnp.logical_and(pl.program_id(0) == 0, pl.program_id(1) == 0))
    def _():
        pe_scr[0] = _dot(pek_ref[...], w1k_ref[...])
        pe_scr[1] = _dot(pev_ref[...], w1v_ref[...])

    def hidden(ch, w1_ref, pe):
        a = _dot(ch, w1_ref[:half, :])
        b = _dot(ch, w1_ref[half:, :])
        b = pltpu.roll(b, b.shape[0] - 1, axis=0)
        return _silu(a + b + pe[0:1, :]).astype(BF16)

    hid_k = hidden(chk_ref[0, 0], w1k_ref, pe_scr[0])
    hid_v = hidden(chv_ref[0, 0], w1v_ref, pe_scr[1])
    kc = _dot(hid_k, w2k_ref[...])
    kc_ref[0, 0] = _rope_lanes(kc, cos_ref[...], slo_ref[...], shi_ref[...]).astype(BF16)
    vct_ref[0, 0] = _dot_nt(w2vt_ref[...], hid_v).astype(BF16)


def _compress(chk, chv, w1k, w2k, pek, w1v, w2vt, pev, tables):
    B, G, n_chunk, width = chk.shape
    const = lambda b, g: (0, 0)
    chunk_spec = pl.BlockSpec((1, 1, n_chunk, width), lambda b, g: (b, g, 0, 0))
    specs = lambda ws: [pl.BlockSpec(w.shape, const) for w in ws]
    return pl.pallas_call(
        _compress_kernel,
        name="nsa_compress",
        grid=(B, G),
        in_specs=[chunk_spec, chunk_spec] + specs((w1k, w2k, pek, w1v, w2vt, pev))
                 + [pl.BlockSpec((n_chunk, LANES), const)] * 3,
        out_specs=[pl.BlockSpec((1, 1, n_chunk, LANES), lambda b, g: (b, g, 0, 0)),
                   pl.BlockSpec((1, 1, HEAD_DIM, n_chunk), lambda b, g: (b, g, 0, 0))],
        out_shape=[jax.ShapeDtypeStruct((B, G, n_chunk, LANES), BF16),
                   jax.ShapeDtypeStruct((B, G, HEAD_DIM, n_chunk), BF16)],
        scratch_shapes=[pltpu.VMEM((2, SUBLANES, w1k.shape[1]), F32)],
        compiler_params=_params("arbitrary", "arbitrary"),
    )(chk, chv, w1k, w2k, pek, w1v, w2vt, pev, *tables)


def _local_attn_kernel(q_ref, kc_ref, vct_ref, ovt_ref, kw_ref, vw_ref, oc_ref, bias_ref, ow_ref,
                       *, t, n_cmp, n_blk, n_back):
    n_q = q_ref.shape[2] // t
    heads = range(NSA_REP)
    kc = kc_ref[0, 0]
    vct = vct_ref[0, 0]
    ovt = ovt_ref[...]
    n_pad = kc.shape[0]
    above = _iota((t, t), 0) > _iota((t, t), 1)
    ones = jnp.ones((ONES_ROWS, t), BF16)
    blk_c = _iota((n_pad, t), 0)
    blk = _iota((n_blk, t), 0)
    n_rounds = min(N_SELECT, n_blk)
    per_head = -(-n_rounds // NSA_REP)

    def q_aug(i, r):
        return _pad_rows(q_ref[0, r * HEAD_DIM:(r + 1) * HEAD_DIM, i * t:(i + 1) * t], LANES)

    masks = {0: jnp.logical_not(above), n_back: above}

    def win_tiles(i):
        return [(i - d, masks.get(d)) for d in range(n_back + 1) if i - d >= 0]

    def win_logits(i, r):
        return [_dot(kw_ref[0, 0, _tile(j, t), :], q_aug(i, r)) for j, _ in win_tiles(i)]

    def cmp_logits(i):
        return [_dot(kc, q_aug(i, r)) for r in heads]

    def importance(i, logits):
        cols = slice(i * t, (i + 1) * t)
        qry = i * t + _iota((n_pad, t), 1)
        valid = jnp.logical_and(blk_c * CMP_STRIDE + (CMP_LEN - 1) <= qry, blk_c < n_cmp)
        p_sum = jnp.zeros((n_pad, t), F32)
        for r in heads:
            s = jnp.where(valid, logits[r], NEG)
            e = jnp.exp2(s - _col_max(s))
            p = jnp.where(valid, e * (1.0 / _col_sum(e)), 0.0)
            p_sum = p_sum + p
            oc_ref[0, r * HEAD_DIM:(r + 1) * HEAD_DIM, cols] = _dot(
                vct, p.astype(BF16)).astype(oc_ref.dtype)
        p_hi, p_lo = _split2(p_sum)
        imp = _dot(ovt, p_hi) + _dot(ovt, p_lo)
        cur = (i * t + _iota(imp.shape, 1)) // SEL_LEN
        forced = jnp.logical_or(blk == 0, jnp.logical_or(blk == cur, blk == cur - 1))
        return jnp.where(forced, FORCE, jnp.where(blk <= cur, imp, -1.0))

    items = [(i, r) for i in range(n_q) for r in heads]
    win_queue, cmp_queue = [], {}

    def issue(n):
        if n < len(items):
            win_queue.append(win_logits(*items[n]))
            if items[n][1] == 0:
                cmp_queue[items[n][0]] = cmp_logits(items[n][0])

    for n in range(LOOKAHEAD):
        issue(n)
    for n, (i, r) in enumerate(items):
        issue(n + LOOKAHEAD)
        logits = win_queue.pop(0)
        if r == 0:
            imp = importance(i, cmp_queue.pop(i))
            keep = jnp.zeros(imp.shape, jnp.bool_)
            done = 0
        tiles = win_tiles(i)
        values = lambda j: jnp.concatenate([vw_ref[0, j], ones], axis=0)
        if len(tiles) == n_back + 1:
            diag = masks[0]
            s_fold = jnp.where(diag, logits[0], logits[-1])
            full = logits[1:-1]
            m = functools.reduce(jnp.maximum, [_col_max(s) for s in [s_fold] + full])
            p_fold = jnp.exp2(s_fold - m)
            zero = jnp.zeros_like(p_fold)
            parts = [(tiles[0][0], jnp.where(diag, p_fold, zero)),
                     (tiles[-1][0], jnp.where(diag, zero, p_fold))]
            parts += [(j, jnp.exp2(s - m)) for s, (j, _) in zip(full, tiles[1:-1])]
        else:
            ss = [s if mask is None else jnp.where(mask, s, NEG)
                  for s, (_, mask) in zip(logits, tiles)]
            m = functools.reduce(jnp.maximum, [_col_max(s) for s in ss])
            parts = [(j, jnp.exp2(s - m)) for s, (j, _) in zip(ss, tiles)]
        acc = functools.reduce(jnp.add, [_dot(values(j), p.astype(BF16)) for j, p in parts])
        ow_ref[0, r * HEAD_DIM:(r + 1) * HEAD_DIM, i * t:(i + 1) * t] = _normalised(acc).astype(
            ow_ref.dtype)
        for _ in range(min(per_head, n_rounds - done)):
            top = _col_max(imp)
            first = jnp.min(jnp.where(imp == top, blk, n_blk), axis=0, keepdims=True)
            pick = blk == first
            keep = jnp.logical_or(keep, pick)
            imp = jnp.where(pick, -jnp.inf, imp)
            done += 1
        if r == NSA_REP - 1:
            bias_ref[0, 0, :, i * t:(i + 1) * t] = jnp.where(keep, 0.0, NEG).astype(bias_ref.dtype)


def _local_attn(qt, kc, vct, ovt, n_cmp, kw, vw):
    B, W, S = qt.shape
    G = kc.shape[1]
    t = min(ATTN_TILE, S)
    assert WINDOW % t == 0
    n_blk = ovt.shape[0]
    kern = functools.partial(_local_attn_kernel, t=t, n_cmp=n_cmp, n_blk=n_blk, n_back=WINDOW // t)
    feat = pl.BlockSpec((1, W // G, S), lambda b, g: (b, g, 0))
    return pl.pallas_call(
        kern,
        name="nsa_local_attn",
        grid=(B, G),
        in_specs=[
            feat,
            pl.BlockSpec((1, 1) + kc.shape[2:], lambda b, g: (b, g, 0, 0)),
            pl.BlockSpec((1, 1) + vct.shape[2:], lambda b, g: (b, g, 0, 0)),
            pl.BlockSpec(ovt.shape, lambda b, g: (0, 0)),
            pl.BlockSpec((1, 1, S, LANES), lambda b, g: (b, g, 0, 0)),
            pl.BlockSpec((1, S // t, HEAD_DIM, t), lambda b, g: (b, 0, g, 0)),
        ],
        out_specs=[feat, pl.BlockSpec((1, 1, n_blk, S), lambda b, g: (b, g, 0, 0)), feat],
        out_shape=[
            jax.ShapeDtypeStruct((B, W, S), BF16),
            jax.ShapeDtypeStruct((B, G, n_blk, S), BF16),
            jax.ShapeDtypeStruct((B, W, S), BF16),
        ],
        compiler_params=_params("parallel", "parallel"),
    )(qt, kc, vct, ovt, kw, vw)


def _sel_attn_kernel(q_ref, bias_ref, k_ref, v_ref, oc_ref, ow_ref, gt_ref, z_ref, o_ref,
                     s_scr, acc_scr, *, t):
    nh = s_scr.shape[1]
    n_q = q_ref.shape[2] // t

    def query(i, slot, hh):
        cols = slice(i * t, (i + 1) * t)
        bias = _pad_rows(bias_ref[0, hh // NSA_REP, :, cols], HEAD_DIM)
        return jnp.concatenate([q_ref[0, hh * HEAD_DIM:(hh + 1) * HEAD_DIM, cols], bias], axis=0)

    def key_tile(j, hh):
        return k_ref[0, hh // NSA_REP, _tile(j, t), :]

    def value_rows(j, hh):
        g = hh // NSA_REP
        return v_ref[0, j, g * HEAD_DIM:(g + 1) * HEAD_DIM, :]

    def emit(i, accs):
        cols = slice(i * t, (i + 1) * t)
        for hh in range(nh):
            rows = slice(hh * HEAD_DIM, (hh + 1) * HEAD_DIM)
            gate = lambda b: gt_ref[0, 0, N_BRANCH * hh + b:N_BRANCH * hh + b + 1, cols]
            o = (gate(0) * oc_ref[0, rows, cols].astype(F32) + gate(1) * _normalised(accs[hh])
                 + gate(2) * ow_ref[0, rows, cols].astype(F32))
            o_ref[0, rows, cols] = (o * z_ref[0, rows, cols].astype(F32)).astype(o_ref.dtype)

    _causal_attention(n_q, t, lambda i, slot: None, query, key_tile, value_rows, emit,
                      s_scr, acc_scr)


def _sel_attn(qt, bias, k_aug, vt, oc, ow, gt, zt):
    B, W, S = qt.shape
    t = min(ATTN_TILE, S)
    ng = HEADS_PER_STEP // NSA_REP
    n_blk = bias.shape[2]
    feat = pl.BlockSpec((1, HEADS_PER_STEP * HEAD_DIM, S), lambda b, g: (b, g, 0))
    return pl.pallas_call(
        functools.partial(_sel_attn_kernel, t=t),
        name="nsa_sel_attn",
        grid=(B, NSA_GROUPS // ng),
        in_specs=[
            feat,
            pl.BlockSpec((1, ng, n_blk, S), lambda b, g: (b, g, 0, 0)),
            pl.BlockSpec((1, ng, S, LANES), lambda b, g: (b, g, 0, 0)),
            pl.BlockSpec((1, S // t, ng * HEAD_DIM, t), lambda b, g: (b, 0, g, 0)),
            feat, feat,
            pl.BlockSpec((1, 1, GATE_ROWS, S), lambda b, g: (b, g, 0, 0)),
            feat,
        ],
        out_specs=feat,
        out_shape=jax.ShapeDtypeStruct((B, W, S), BF16),
        scratch_shapes=_attn_scratch(HEADS_PER_STEP, t),
        compiler_params=_params("parallel", "parallel"),
    )(qt, bias, k_aug, vt, oc, ow, gt, zt)


def _fox_mixer(x, g, w_in, b_f):
    B, S, _ = x.shape
    W, H = ATTN_WIDTH, N_HEADS
    w_k = w_in[:, W:2 * W].astype(BF16)
    w_t = jnp.concatenate([w_in[:, :W] * Q_SCALE, w_in[:, 2 * W:3 * W], w_in[:, 3 * W + H:]],
                          axis=1).T.astype(BF16)
    wf_hi, wf_lo = _split2(w_in[:, 3 * W:3 * W + H])
    pad = lambda n: jnp.zeros((D_MODEL, LANES - n * H), BF16)
    wf_a = jnp.concatenate([wf_hi] * 3 + [wf_lo] * 3 + [pad(6)], axis=1)
    wf_b = jnp.concatenate([wf_hi] * 3 + [pad(3)], axis=1)
    bf_row = jnp.concatenate([b_f, b_f, b_f, jnp.zeros((LANES - 3 * H,), F32)])[None, :]

    qt, k, vt, zt, cx = _fox_in(x, g[None, :], w_k, w_t, wf_a, wf_b, bf_row)

    t = min(ATTN_TILE, S)
    r = np.arange(LANES)
    sel = np.logical_and(r[None, :] % H == np.arange(H)[:, None], r[None, :] < 3 * H)
    qx = np.broadcast_to(np.where(sel, -1.0, 0.0).astype(np.float32)[:, :, None], (H, LANES, t))
    return _fox_attn(qt, jnp.asarray(qx, BF16), k, cx, vt, zt)


def _nsa_layer(ot0, w_out0, x, g, w_in, pe_k, w_ck1, w_ck2, pe_v, w_cv1, w_cv2, w_out, final_g):
    B, S, _ = x.shape
    W, KV, G = ATTN_WIDTH, KV_WIDTH, NSA_GROUPS
    sec = lambda j: w_in[:, W + j * KV:W + (j + 1) * KV]
    gate_off = W + 6 * KV
    n_gate = N_BRANCH * N_HEADS
    w_k = jnp.concatenate([sec(2), sec(4), sec(0), sec(1)], axis=1).astype(BF16)
    w_t = jnp.concatenate([w_in[:, :W] * Q_SCALE, sec(3), sec(5), w_in[:, gate_off + n_gate:],
                           w_in[:, gate_off:gate_off + n_gate]], axis=1).T.astype(BF16)
    pos = np.arange(S)
    cos, sin = _rope_angles(pos)
    x, qt, ksel, kwin, chk, chv, vsel, vwin, zt, gt = _nsa_in(
        ot0, w_out0.astype(BF16), x, g[None, :], w_k, w_t, _rope_lane_tables(pos),
        (np.ascontiguousarray(cos.T), np.ascontiguousarray(sin.T)))

    n_chunk = S // CMP_STRIDE
    n_cmp = n_chunk - CMP_LEN // CMP_STRIDE + 1

    def flat_pe(pe):
        pe = pe.reshape(1, CMP_LEN * HEAD_DIM)
        return jnp.broadcast_to(pe, (SUBLANES, CMP_LEN * HEAD_DIM)).astype(BF16)

    w2k = jnp.concatenate([w_ck2, jnp.zeros_like(w_ck2)], axis=1).astype(BF16)
    cmp_end = np.arange(n_chunk) * CMP_STRIDE + CMP_LEN - 1
    kc, vct = _compress(chk, chv, w_ck1.astype(BF16), w2k, flat_pe(pe_k),
                        w_cv1.astype(BF16), w_cv2.T.astype(BF16), flat_pe(pe_v),
                        _rope_lane_tables(cmp_end))

    n_blk = S // SEL_LEN
    ci = np.arange(n_chunk) * CMP_STRIDE
    sj = np.arange(n_blk) * SEL_LEN
    ovt = np.logical_and(ci[None, :] < sj[:, None] + SEL_LEN, ci[None, :] + CMP_LEN > sj[:, None])
    ovt = np.logical_and(ovt, np.arange(n_chunk)[None, :] < n_cmp).astype(np.float32)
    oc, bias, ow = _local_attn(qt, kc, vct, jnp.asarray(ovt, BF16), n_cmp, kwin, vwin)
    ot = _sel_attn(qt, bias, ksel, vsel, oc, ow, gt, zt)
    return _final_proj(ot, x, w_out.astype(BF16), final_g[None, :])


def kernel(x, norm_g, fox_w_in, fox_b_f, fox_w_out, nsa_w_in, nsa_pe_k, nsa_w_ck1, nsa_w_ck2,
           nsa_pe_v, nsa_w_cv1, nsa_w_cv2, nsa_w_out, final_g):
    ot0 = _fox_mixer(x, norm_g[0], fox_w_in[0], fox_b_f[0])
    return _nsa_layer(ot0, fox_w_out[0], x, norm_g[1], nsa_w_in[0], nsa_pe_k[0], nsa_w_ck1[0],
                      nsa_w_ck2[0], nsa_pe_v[0], nsa_w_cv1[0], nsa_w_cv2[0], nsa_w_out[0], final_g)
```

```python
import functools
import math

import jax
import jax.numpy as jnp
import numpy as np
from jax import lax
from jax.experimental import pallas as pl
from jax.experimental.pallas import tpu as pltpu

F32 = jnp.float32
BF16 = jnp.bfloat16

D_MODEL = 1024
N_HEADS = 16
HEAD_DIM = 64
ATTN_WIDTH = N_HEADS * HEAD_DIM
NSA_GROUPS = 4
NSA_REP = N_HEADS // NSA_GROUPS
KV_WIDTH = NSA_GROUPS * HEAD_DIM
CMP_LEN = 32
CMP_STRIDE = 16
CMP_HIDDEN = 256
SEL_LEN = 64
N_SELECT = 8
WINDOW = 512
N_BRANCH = 3
ROPE_THETA = 500000.0
ROPE_DIM = HEAD_DIM // 4
ROPE_HALF = ROPE_DIM // 2
NORM_EPS = 1e-6
NEG = -1e30
FORCE = 1e6
LOG2E = math.log2(math.e)
Q_SCALE = HEAD_DIM ** -0.5 * LOG2E

LANES = 128
SUBLANES = 8
ONES_ROWS = 2 * SUBLANES
PROJ_ROWS = 1024
OUT_ROWS = 1024
ATTN_TILE = 256
HEADS_PER_STEP = 8
LOOKAHEAD = 1
GATE_ROWS = -(-N_BRANCH * HEADS_PER_STEP // SUBLANES) * SUBLANES
VMEM_LIMIT = 56 * 1024 * 1024


def _params(*sem):
    return pltpu.CompilerParams(dimension_semantics=sem, vmem_limit_bytes=VMEM_LIMIT)


def _iota(shape, dim):
    return lax.broadcasted_iota(jnp.int32, shape, dim)


def _split2(x):
    hi = x.astype(BF16)
    lo = (x - hi.astype(F32)).astype(BF16)
    return hi, lo


def _split3(x):
    hi = x.astype(BF16)
    r1 = x - hi.astype(F32)
    mid = r1.astype(BF16)
    lo = (r1 - mid.astype(F32)).astype(BF16)
    return hi, mid, lo


def _dot(a, b):
    return jnp.dot(a, b, preferred_element_type=F32)


def _dot_nt(a, b):
    return lax.dot_general(a, b, (((1,), (1,)), ((), ())), preferred_element_type=F32)


def _dot_tn(a, b):
    return lax.dot_general(a, b, (((0,), (0,)), ((), ())), preferred_element_type=F32)


def _rmsnorm(x, g):
    ms = jnp.mean(x * x, axis=-1, keepdims=True)
    return x * lax.rsqrt(ms + NORM_EPS) * g


def _silu(x):
    return x * (1.0 / (1.0 + jnp.exp(-x)))


def _col_reduce(x, op, reduce):
    rows = x.shape[0]
    if rows % (4 * SUBLANES) == 0 and rows >= 8 * SUBLANES:
        q = rows // 4
        x = op(op(x[:q], x[q:2 * q]), op(x[2 * q:3 * q], x[3 * q:]))
    return reduce(x, axis=0, keepdims=True)


def _col_max(x):
    return _col_reduce(x, jnp.maximum, jnp.max)


def _col_sum(x):
    return _col_reduce(x, jnp.add, jnp.sum)


def _pad_rows(x, rows):
    if rows == x.shape[0]:
        return x
    return jnp.concatenate([x, jnp.zeros((rows - x.shape[0], x.shape[1]), x.dtype)], axis=0)


def _tile(j, t):
    return pl.ds(j * t if isinstance(j, int) else pl.multiple_of(j * t, t), t)


def _normalised(acc):
    return acc[:HEAD_DIM] * (1.0 / acc[HEAD_DIM:HEAD_DIM + 1])


def _rope_lanes(x, cos, sin_lo, sin_hi):
    return (x * cos + pltpu.roll(x, LANES - ROPE_HALF, axis=1) * sin_lo
            + pltpu.roll(x, ROPE_HALF, axis=1) * sin_hi)


def _rope_rows(x, cos, sin):
    x1, x2 = x[:ROPE_HALF], x[ROPE_HALF:ROPE_DIM]
    return jnp.concatenate([x1 * cos - x2 * sin, x1 * sin + x2 * cos, x[ROPE_DIM:]], axis=0)


def _rope_angles(pos):
    inv_freq = np.power(np.float32(ROPE_THETA),
                        -np.arange(ROPE_HALF, dtype=np.float32) * np.float32(2.0 / ROPE_DIM))
    ang = pos.astype(np.float32)[:, None] * inv_freq[None, :].astype(np.float32)
    return np.cos(ang).astype(np.float32), np.sin(ang).astype(np.float32)


def _rope_lane_tables(pos):
    cos, sin = _rope_angles(pos)
    n = pos.shape[0]
    ones = np.ones((n, HEAD_DIM - ROPE_DIM), np.float32)
    zeros = np.zeros((n, HEAD_DIM - ROPE_DIM), np.float32)
    z8 = np.zeros((n, ROPE_HALF), np.float32)
    c = np.concatenate([cos, cos, ones], axis=1)
    s_lo = np.concatenate([-sin, z8, zeros], axis=1)
    s_hi = np.concatenate([z8, sin, zeros], axis=1)
    tile = lambda t: np.concatenate([t, t], axis=1)
    return tile(c), tile(s_lo), tile(s_hi)


def _fox_in_kernel(x_ref, g_ref, wk_ref, wt_ref, wfa_ref, wfb_ref, bf_ref, tri_ref,
                   qt_ref, k_ref, vt_ref, zt_ref, cx_ref, carry_ref, *, tk):
    s = pl.program_id(1)
    h = _rmsnorm(x_ref[0], g_ref[...])
    hb = h.astype(BF16)
    tm = hb.shape[0]
    W, H = ATTN_WIDTH, N_HEADS

    @pl.when(s == 0)
    def _():
        carry_ref[...] = jnp.zeros_like(carry_ref)

    h_lo = (h - hb.astype(F32)).astype(BF16)
    fa = _dot(hb, wfa_ref[...])
    f = fa + pltpu.roll(fa, LANES - 3 * H, axis=1) + _dot(h_lo, wfb_ref[...]) + bf_ref[...]

    for n0 in range(0, W, 512):
        k_ref[0, :, n0:n0 + 512] = _dot(hb, wk_ref[:, n0:n0 + 512]).astype(BF16)

    log_f = jnp.minimum(f, 0.0) - jnp.log1p(jnp.exp(-jnp.abs(f)))
    lane = _iota(log_f.shape, 1)
    a0, a1, a2 = _split3(log_f)
    pieces = jnp.where(lane < H, a0, jnp.where(lane < 2 * H, a1, a2))
    c = _dot(tri_ref[...], pieces)

    for r0 in list(range(2 * W, 3 * W, 512)) + list(range(0, 2 * W, 512)):
        t = _dot_nt(wt_ref[r0:r0 + 512, :], hb)
        if r0 < W:
            qt_ref[0, r0:r0 + 512, :] = t.astype(BF16)
        elif r0 < 2 * W:
            t = t.astype(BF16)
            for ct in range(tm // tk):
                vt_ref[0, ct, r0 - W:r0 - W + 512, :] = t[:, ct * tk:(ct + 1) * tk]
        else:
            zt_ref[0, r0 - 2 * W:r0 - 2 * W + 512, :] = _silu(t).astype(BF16)

    c = c + pltpu.roll(c, LANES - H, axis=1) + pltpu.roll(c, LANES - 2 * H, axis=1)
    c = c + carry_ref[0:1, :]
    carry_ref[...] = jnp.broadcast_to(c[tm - 1:tm, :], carry_ref.shape)
    c = jnp.where(lane < H, c, jnp.where(lane < 2 * H, pltpu.roll(c, H, axis=1),
                                         pltpu.roll(c, 2 * H, axis=1)))

    hi, mid, lo = _split3(c * LOG2E)
    zero = jnp.zeros_like(hi)
    cx_ref[0] = jnp.where(lane < H, hi,
                          jnp.where(lane < 2 * H, mid, jnp.where(lane < 3 * H, lo, zero)))


def _fox_in(x, g, w_k, w_t, wf_a, wf_b, bf_row):
    B, S, _ = x.shape
    tm = min(PROJ_ROWS, S)
    tk = min(ATTN_TILE, S)
    W = ATTN_WIDTH
    tri = np.tril(np.ones((tm, tm), np.float32)).astype(BF16)
    const = lambda b, s: (0, 0)
    feat = pl.BlockSpec((1, W, tm), lambda b, s: (b, 0, s))
    return pl.pallas_call(
        functools.partial(_fox_in_kernel, tk=tk),
        name="fox_in",
        grid=(B, S // tm),
        in_specs=[
            pl.BlockSpec((1, tm, D_MODEL), lambda b, s: (b, s, 0)),
            pl.BlockSpec((1, D_MODEL), const),
            pl.BlockSpec(w_k.shape, const),
            pl.BlockSpec(w_t.shape, const),
            pl.BlockSpec((D_MODEL, LANES), const),
            pl.BlockSpec((D_MODEL, LANES), const),
            pl.BlockSpec((1, LANES), const),
            pl.BlockSpec((tm, tm), const),
        ],
        out_specs=[
            feat,
            pl.BlockSpec((1, tm, W), lambda b, s: (b, s, 0)),
            pl.BlockSpec((1, tm // tk, W, tk), lambda b, s: (b, s, 0, 0)),
            feat,
            pl.BlockSpec((1, tm, LANES), lambda b, s: (b, s, 0)),
        ],
        out_shape=[
            jax.ShapeDtypeStruct((B, W, S), BF16),
            jax.ShapeDtypeStruct((B, S, W), BF16),
            jax.ShapeDtypeStruct((B, S // tk, W, tk), BF16),
            jax.ShapeDtypeStruct((B, W, S), BF16),
            jax.ShapeDtypeStruct((B, S, LANES), BF16),
        ],
        scratch_shapes=[pltpu.VMEM((SUBLANES, LANES), F32)],
        compiler_params=_params("parallel", "arbitrary"),
    )(x, g, w_k, w_t, wf_a, wf_b, bf_row, tri)


def _causal_attention(n_q, t, prepare_q, query, key_tile, value_rows, emit, s_scr, acc_scr):
    heads = range(s_scr.shape[1])
    ones = jnp.ones((ONES_ROWS, t), BF16)
    causal = _iota((t, t), 0) <= _iota((t, t), 1)

    def logits(i, slot, j):
        return [_dot(key_tile(j, hh), query(i, slot, hh)) for hh in heads]

    def pv(j, hh, p):
        return _dot(jnp.concatenate([value_rows(j, hh), ones], axis=0), p.astype(BF16))

    def finish(i, slot, carry):
        accs = []
        for hh in heads:
            m, alpha = carry[hh]
            accs.append(alpha * acc_scr[hh] + pv(i - 1, hh, jnp.exp2(s_scr[slot, hh] - m)))
        emit(i, accs)

    pending = None
    for i in range(n_q):
        slot = i % 2
        prepare_q(i, slot)
        s_diag = logits(i, slot, i)
        s_first = logits(i, slot, 0) if i else None
        if pending is not None:
            finish(*pending)
        m_diag = []
        for hh in heads:
            s = jnp.where(causal, s_diag[hh], NEG)
            m = _col_max(s)
            acc_scr[hh] = pv(i, hh, jnp.exp2(s - m))
            m_diag.append(m)
        if i == 0:
            emit(0, [acc_scr[hh] for hh in heads])
            continue
        carry = []
        for hh in heads:
            s_scr[slot, hh] = s_first[hh]
            m_new = jnp.maximum(m_diag[hh], _col_max(s_first[hh]))
            carry.append((m_new, jnp.exp2(m_diag[hh] - m_new)))

        def body(j, carry, i=i, slot=slot):
            s_next, pvs = [], []
            for hh in heads:
                s_next.append(_dot(key_tile(j + 1, hh), query(i, slot, hh)))
                pvs.append(pv(j, hh, jnp.exp2(s_scr[slot, hh] - carry[hh][0])))
            out = []
            for hh in heads:
                m = carry[hh][0]
                s_scr[slot, hh] = s_next[hh]
                m_new = jnp.maximum(m, _col_max(s_next[hh]))
                out.append((m_new, jnp.exp2(m - m_new)))
            for hh in heads:
                acc_scr[hh] = carry[hh][1] * acc_scr[hh] + pvs[hh]
            return tuple(out)

        carry = lax.fori_loop(0, i - 1, body, tuple(carry))
        pending = (i, slot, carry)
    if pending is not None:
        finish(*pending)


def _attn_scratch(nh, t):
    return [pltpu.VMEM((2, nh, t, t), F32), pltpu.VMEM((nh, HEAD_DIM + ONES_ROWS, t), F32)]


def _fox_attn_kernel(q_ref, qx_ref, k_ref, kx_ref, v_ref, z_ref, o_ref, q_scr, s_scr, acc_scr, *, t):
    nh = q_scr.shape[1]
    n_q = q_ref.shape[2] // t
    row = _iota((LANES, t), 0)

    def load_q(i, slot):
        for hh in range(nh):
            pair, sub = divmod(hh, 2)
            qt = q_ref[0, pair * LANES:(pair + 1) * LANES, i * t:(i + 1) * t]
            own = jnp.logical_and(row >= sub * HEAD_DIM, row < (sub + 1) * HEAD_DIM)
            q_scr[slot, hh] = jnp.where(own, qt, jnp.zeros_like(qt))

    def query(i, slot, hh):
        return jnp.concatenate([q_scr[slot, hh], qx_ref[hh]], axis=0)

    def key_tile(j, hh):
        pair = hh // 2
        return jnp.concatenate([k_ref[0, _tile(j, t), pair * LANES:(pair + 1) * LANES],
                                kx_ref[0, _tile(j, t), :]], axis=1)

    def value_rows(j, hh):
        return v_ref[0, j, hh * HEAD_DIM:(hh + 1) * HEAD_DIM, :]

    def emit(i, accs):
        for hh in range(nh):
            rows, cols = slice(hh * HEAD_DIM, (hh + 1) * HEAD_DIM), slice(i * t, (i + 1) * t)
            gated = _normalised(accs[hh]) * z_ref[0, rows, cols].astype(F32)
            o_ref[0, rows, cols] = gated.astype(o_ref.dtype)

    _causal_attention(n_q, t, load_q, query, key_tile, value_rows, emit, s_scr, acc_scr)


def _fox_attn(qt, qx, k, kx, vt, zt):
    B, W, S = qt.shape
    t = min(ATTN_TILE, S)
    nh = HEADS_PER_STEP
    rows = nh * HEAD_DIM
    feat = pl.BlockSpec((1, rows, S), lambda b, p: (b, p, 0))
    return pl.pallas_call(
        functools.partial(_fox_attn_kernel, t=t),
        name="fox_attn",
        grid=(B, N_HEADS // nh),
        in_specs=[
            feat,
            pl.BlockSpec((nh, LANES, t), lambda b, p: (p, 0, 0)),
            pl.BlockSpec((1, S, rows), lambda b, p: (b, 0, p)),
            pl.BlockSpec((1, S, LANES), lambda b, p: (b, 0, 0)),
            pl.BlockSpec((1, S // t, rows, t), lambda b, p: (b, 0, p, 0)),
            feat,
        ],
        out_specs=feat,
        out_shape=jax.ShapeDtypeStruct((B, W, S), BF16),
        scratch_shapes=[pltpu.VMEM((2, nh, LANES, t), BF16)] + _attn_scratch(nh, t),
        compiler_params=_params("parallel", "parallel"),
    )(qt, qx, k, kx, vt, zt)


def _final_proj_kernel(ot_ref, x_ref, w_ref, g_ref, y_ref):
    y = x_ref[0] + _dot_tn(ot_ref[0], w_ref[...])
    y_ref[0] = _rmsnorm(y, g_ref[...])


def _final_proj(ot, x, w_out, final_g):
    B, S, _ = x.shape
    tm = min(OUT_ROWS, S)
    const = lambda b, s: (0, 0)
    tok = pl.BlockSpec((1, tm, D_MODEL), lambda b, s: (b, s, 0))
    return pl.pallas_call(
        _final_proj_kernel,
        name="final_proj",
        grid=(B, S // tm),
        in_specs=[pl.BlockSpec((1, ATTN_WIDTH, tm), lambda b, s: (b, 0, s)), tok,
                  pl.BlockSpec((ATTN_WIDTH, D_MODEL), const), pl.BlockSpec((1, D_MODEL), const)],
        out_specs=tok,
        out_shape=jax.ShapeDtypeStruct((B, S, D_MODEL), F32),
        compiler_params=_params("parallel", "parallel"),
    )(ot, x, w_out, final_g)


_TOK_SEL = 0
_TOK_WIN = KV_WIDTH
_TOK_CMP = 2 * KV_WIDTH
_TOK_COLS = 4 * KV_WIDTH
_FEAT_Q = 0
_FEAT_VSEL = ATTN_WIDTH
_FEAT_VWIN = _FEAT_VSEL + KV_WIDTH
_FEAT_Z = _FEAT_VWIN + KV_WIDTH
_FEAT_GATE = _FEAT_Z + ATTN_WIDTH
_FEAT_ROWS = _FEAT_GATE + N_BRANCH * N_HEADS


def _nsa_in_kernel(ot_ref, wo_ref, x_ref, g_ref, wk_ref, wt_ref, cos_ref, slo_ref, shi_ref,
                   cost_ref, sint_ref,
                   x1_ref, qt_ref, ksel_ref, kwin_ref, chk_ref, chv_ref, vsel_ref, vwin_ref, zt_ref,
                   gt_ref, cmp_scr, *, tk):
    s = pl.program_id(1)
    x1 = x_ref[0] + _dot_tn(ot_ref[0], wo_ref[...])
    x1_ref[0] = x1
    hb = _rmsnorm(x1, g_ref[...]).astype(BF16)
    tm = hb.shape[0]
    G = NSA_GROUPS

    cos, slo, shi = cos_ref[...], slo_ref[...], shi_ref[...]
    tok = s * tm + _iota((tm, LANES), 0)
    lane = _iota((tm, LANES), 1)
    low = lane < HEAD_DIM
    block_id = jnp.where(lane - HEAD_DIM == tok // SEL_LEN, 1.0, 0.0)
    keys = _dot(hb, wk_ref[:, _TOK_SEL:_TOK_CMP])
    for col0, out_ref, extra in ((_TOK_SEL, ksel_ref, block_id), (_TOK_WIN, kwin_ref, 0.0)):
        for gp in range(G // 2):
            pair = _rope_lanes(keys[:, col0 + gp * LANES:col0 + (gp + 1) * LANES], cos, slo, shi)
            out_ref[0, 2 * gp] = jnp.where(low, pair, extra).astype(BF16)
            out_ref[0, 2 * gp + 1] = jnp.where(
                low, pltpu.roll(pair, HEAD_DIM, axis=1), extra).astype(BF16)

    raw = _dot(hb, wk_ref[:, _TOK_CMP:_TOK_COLS])
    for c in range(cmp_scr.shape[0]):
        cmp_scr[c] = raw[:, c * LANES:(c + 1) * LANES]

    cost, sint = cost_ref[...], sint_ref[...]
    for r0 in range(0, ATTN_WIDTH, 512):
        t = _dot_nt(wt_ref[_FEAT_Q + r0:_FEAT_Q + r0 + 512, :], hb)
        for h0 in range(0, 512, HEAD_DIM):
            qt_ref[0, r0 + h0:r0 + h0 + HEAD_DIM, :] = _rope_rows(
                t[h0:h0 + HEAD_DIM], cost, sint).astype(BF16)
        zt_ref[0, r0:r0 + 512, :] = _silu(
            _dot_nt(wt_ref[_FEAT_Z + r0:_FEAT_Z + r0 + 512, :], hb)).astype(BF16)
    for row0, out_ref in ((_FEAT_VSEL, vsel_ref), (_FEAT_VWIN, vwin_ref)):
        t = _dot_nt(wt_ref[row0:row0 + KV_WIDTH, :], hb).astype(BF16)
        for c in range(tm // tk):
            out_ref[0, c] = t[:, c * tk:(c + 1) * tk]
    gate = _dot_nt(wt_ref[_FEAT_GATE:_FEAT_ROWS, :], hb)
    gate = 1.0 / (1.0 + jnp.exp(-gate))
    rows = N_BRANCH * HEADS_PER_STEP
    for hg in range(N_HEADS // HEADS_PER_STEP):
        gt_ref[0, hg] = _pad_rows(gate[hg * rows:(hg + 1) * rows], GATE_ROWS)

    n_ch = tm // CMP_STRIDE
    low_ch = _iota((n_ch, LANES), 1) < HEAD_DIM
    for kind, out_ref in enumerate((chk_ref, chv_ref)):
        for gp in range(G // 2):
            c = kind * (G // 2) + gp
            for l in range(0, CMP_STRIDE, 2):
                a0 = cmp_scr[c, pl.ds(l, n_ch, stride=CMP_STRIDE), :]
                a1 = cmp_scr[c, pl.ds(l + 1, n_ch, stride=CMP_STRIDE), :]
                dst = slice(l * HEAD_DIM, (l + 2) * HEAD_DIM)
                out_ref[0, 2 * gp, :, dst] = jnp.where(
                    low_ch, a0, pltpu.roll(a1, HEAD_DIM, axis=1)).astype(BF16)
                out_ref[0, 2 * gp + 1, :, dst] = jnp.where(
                    low_ch, pltpu.roll(a0, HEAD_DIM, axis=1), a1).astype(BF16)


def _nsa_in(ot, w_out, x, g, w_k, w_t, lane_tables, row_tables):
    B, S, _ = x.shape
    tm = min(PROJ_ROWS, S)
    tk = min(ATTN_TILE, S)
    W, G = ATTN_WIDTH, NSA_GROUPS
    const = lambda b, s: (0, 0)
    feat = lambda rows: pl.BlockSpec((1, rows, tm), lambda b, s: (b, 0, s))
    tiles = pl.BlockSpec((1, tm // tk, KV_WIDTH, tk), lambda b, s: (b, s, 0, 0))
    keys = pl.BlockSpec((1, G, tm, LANES), lambda b, s: (b, 0, s, 0))
    chunk_w = CMP_STRIDE * HEAD_DIM
    chunk = pl.BlockSpec((1, G, tm // CMP_STRIDE, chunk_w), lambda b, s: (b, 0, s, 0))
    tok = pl.BlockSpec((1, tm, D_MODEL), lambda b, s: (b, s, 0))
    n_hg = N_HEADS // HEADS_PER_STEP
    return pl.pallas_call(
        functools.partial(_nsa_in_kernel, tk=tk),
        name="nsa_in",
        grid=(B, S // tm),
        in_specs=[
            feat(W),
            pl.BlockSpec(w_out.shape, const),
            tok,
            pl.BlockSpec((1, D_MODEL), const),
            pl.BlockSpec(w_k.shape, const),
            pl.BlockSpec(w_t.shape, const),
        ] + [pl.BlockSpec((tm, LANES), lambda b, s: (s, 0))] * 3
          + [pl.BlockSpec((ROPE_HALF, tm), lambda b, s: (0, s))] * 2,
        out_specs=[tok, feat(W), keys, keys, chunk, chunk, tiles, tiles, feat(W),
                   pl.BlockSpec((1, n_hg, GATE_ROWS, tm), lambda b, s: (b, 0, 0, s))],
        out_shape=[
            jax.ShapeDtypeStruct((B, S, D_MODEL), F32),
            jax.ShapeDtypeStruct((B, W, S), BF16),
            jax.ShapeDtypeStruct((B, G, S, LANES), BF16),
            jax.ShapeDtypeStruct((B, G, S, LANES), BF16),
            jax.ShapeDtypeStruct((B, G, S // CMP_STRIDE, chunk_w), BF16),
            jax.ShapeDtypeStruct((B, G, S // CMP_STRIDE, chunk_w), BF16),
            jax.ShapeDtypeStruct((B, S // tk, KV_WIDTH, tk), BF16),
            jax.ShapeDtypeStruct((B, S // tk, KV_WIDTH, tk), BF16),
            jax.ShapeDtypeStruct((B, W, S), BF16),
            jax.ShapeDtypeStruct((B, n_hg, GATE_ROWS, S), F32),
        ],
        scratch_shapes=[pltpu.VMEM((2 * KV_WIDTH // LANES, tm, LANES), F32)],
        compiler_params=_params("parallel", "parallel"),
    )(ot, w_out, x, g, w_k, w_t, *lane_tables, *row_tables)


def _compress_kernel(chk_ref, chv_ref, w1k_ref, w2k_ref, pek_ref, w1v_ref, w2vt_ref, pev_ref,
                     cos_ref, slo_ref, shi_ref, kc_ref, vct_ref, pe_scr):
    half = CMP_STRIDE * HEAD_DIM

    @pl.when(jnp.logical_and(pl.program_id(0) == 0, pl.program_id(1) == 0))
    def _():
        pe_scr[0] = _dot(pek_ref[...], w1k_ref[...])
        pe_scr[1] = _dot(pev_ref[...], w1v_ref[...])

    def hidden(ch, w1_ref, pe):
        a = _dot(ch, w1_ref[:half, :])
        b = _dot(ch, w1_ref[half:, :])
        b = pltpu.roll(b, b.shape[0] - 1, axis=0)
        return _silu(a + b + pe[0:1, :]).astype(BF16)

    hid_k = hidden(chk_ref[0, 0], w1k_ref, pe_scr[0])
    hid_v = hidden(chv_ref[0, 0], w1v_ref, pe_scr[1])
    kc = _dot(hid_k, w2k_ref[...])
    kc_ref[0, 0] = _rope_lanes(kc, cos_ref[...], slo_ref[...], shi_ref[...]).astype(BF16)
    vct_ref[0, 0] = _dot_nt(w2vt_ref[...], hid_v).astype(BF16)


def _compress(chk, chv, w1k, w2k, pek, w1v, w2vt, pev, tables):
    B, G, n_chunk, width = chk.shape
    const = lambda b, g: (0, 0)
    chunk_spec = pl.BlockSpec((1, 1, n_chunk, width), lambda b, g: (b, g, 0, 0))
    specs = lambda ws: [pl.BlockSpec(w.shape, const) for w in ws]
    return pl.pallas_call(
        _compress_kernel,
        name="nsa_compress",
        grid=(B, G),
        in_specs=[chunk_spec, chunk_spec] + specs((w1k, w2k, pek, w1v, w2vt, pev))
                 + [pl.BlockSpec((n_chunk, LANES), const)] * 3,
        out_specs=[pl.BlockSpec((1, 1, n_chunk, LANES), lambda b, g: (b, g, 0, 0)),
                   pl.BlockSpec((1, 1, HEAD_DIM, n_chunk), lambda b, g: (b, g, 0, 0))],
        out_shape=[jax.ShapeDtypeStruct((B, G, n_chunk, LANES), BF16),
                   jax.ShapeDtypeStruct((B, G, HEAD_DIM, n_chunk), BF16)],
        scratch_shapes=[pltpu.VMEM((2, SUBLANES, w1k.shape[1]), F32)],
        compiler_params=_params("arbitrary", "arbitrary"),
    )(chk, chv, w1k, w2k, pek, w1v, w2vt, pev, *tables)


def _local_attn_kernel(q_ref, kc_ref, vct_ref, ovt_ref, kw_ref, vw_ref, oc_ref, bias_ref, ow_ref,
                       *, t, n_cmp, n_blk, n_back):
    n_q = q_ref.shape[2] // t
    heads = range(NSA_REP)
    kc = kc_ref[0, 0]
    vct = vct_ref[0, 0]
    ovt = ovt_ref[...]
    n_pad = kc.shape[0]
    above = _iota((t, t), 0) > _iota((t, t), 1)
    ones = jnp.ones((ONES_ROWS, t), BF16)
    blk_c = _iota((n_pad, t), 0)
    blk = _iota((n_blk, t), 0)
    n_rounds = min(N_SELECT, n_blk)
    per_head = -(-n_rounds // NSA_REP)

    def q_aug(i, r):
        return _pad_rows(q_ref[0, r * HEAD_DIM:(r + 1) * HEAD_DIM, i * t:(i + 1) * t], LANES)

    masks = {0: jnp.logical_not(above), n_back: above}

    def win_tiles(i):
        return [(i - d, masks.get(d)) for d in range(n_back + 1) if i - d >= 0]

    def win_logits(i, r):
        return [_dot(kw_ref[0, 0, _tile(j, t), :], q_aug(i, r)) for j, _ in win_tiles(i)]

    def cmp_logits(i):
        return [_dot(kc, q_aug(i, r)) for r in heads]

    def importance(i, logits):
        cols = slice(i * t, (i + 1) * t)
        qry = i * t + _iota((n_pad, t), 1)
        valid = jnp.logical_and(blk_c * CMP_STRIDE + (CMP_LEN - 1) <= qry, blk_c < n_cmp)
        p_sum = jnp.zeros((n_pad, t), F32)
        for r in heads:
            s = jnp.where(valid, logits[r], NEG)
            e = jnp.exp2(s - _col_max(s))
            p = e * (1.0 / _col_sum(e))
            if i * t < CMP_LEN - 1:
                p = jnp.where(valid, p, 0.0)
            p_sum = p_sum + p
            oc_ref[0, r * HEAD_DIM:(r + 1) * HEAD_DIM, cols] = _dot(
                vct, p.astype(BF16)).astype(oc_ref.dtype)
        p_hi, p_lo = _split2(p_sum)
        imp = _dot(ovt, p_hi) + _dot(ovt, p_lo)
        cur = (i * t + _iota(imp.shape, 1)) // SEL_LEN
        forced = jnp.logical_or(blk == 0, jnp.logical_or(blk == cur, blk == cur - 1))
        return jnp.where(forced, FORCE, jnp.where(blk <= cur, imp, -1.0))

    items = [(i, r) for i in range(n_q) for r in heads]
    win_queue, cmp_queue = [], {}

    def issue(n):
        if n < len(items):
            win_queue.append(win_logits(*items[n]))
            if items[n][1] == 0:
                cmp_queue[items[n][0]] = cmp_logits(items[n][0])

    for n in range(LOOKAHEAD):
        issue(n)
    for n, (i, r) in enumerate(items):
        issue(n + LOOKAHEAD)
        logits = win_queue.pop(0)
        if r == 0:
            imp = importance(i, cmp_queue.pop(i))
            keep = jnp.zeros(imp.shape, jnp.bool_)
            done = 0
        tiles = win_tiles(i)
        values = lambda j: jnp.concatenate([vw_ref[0, j], ones], axis=0)
        if len(tiles) == n_back + 1:
            diag = masks[0]
            s_fold = jnp.where(diag, logits[0], logits[-1])
            full = logits[1:-1]
            m = functools.reduce(jnp.maximum, [_col_max(s) for s in [s_fold] + full])
            p_fold = jnp.exp2(s_fold - m)
            zero = jnp.zeros_like(p_fold)
            parts = [(tiles[0][0], jnp.where(diag, p_fold, zero)),
                     (tiles[-1][0], jnp.where(diag, zero, p_fold))]
            parts += [(j, jnp.exp2(s - m)) for s, (j, _) in zip(full, tiles[1:-1])]
        else:
            ss = [s if mask is None else jnp.where(mask, s, NEG)
                  for s, (_, mask) in zip(logits, tiles)]
            m = functools.reduce(jnp.maximum, [_col_max(s) for s in ss])
            parts = [(j, jnp.exp2(s - m)) for s, (j, _) in zip(ss, tiles)]
        acc = functools.reduce(jnp.add, [_dot(values(j), p.astype(BF16)) for j, p in parts])
        ow_ref[0, r * HEAD_DIM:(r + 1) * HEAD_DIM, i * t:(i + 1) * t] = _normalised(acc).astype(
            ow_ref.dtype)
        for _ in range(min(per_head, n_rounds - done)):
            top = _col_max(imp)
            first = jnp.min(jnp.where(imp == top, blk, n_blk), axis=0, keepdims=True)
            pick = blk == first
            keep = jnp.logical_or(keep, pick)
            imp = jnp.where(pick, -jnp.inf, imp)
            done += 1
        if r == NSA_REP - 1:
            bias_ref[0, 0, :, i * t:(i + 1) * t] = jnp.where(keep, 0.0, NEG).astype(bias_ref.dtype)


def _local_attn(qt, kc, vct, ovt, n_cmp, kw, vw):
    B, W, S = qt.shape
    G = kc.shape[1]
    t = min(ATTN_TILE, S)
    assert WINDOW % t == 0
    n_blk = ovt.shape[0]
    kern = functools.partial(_local_attn_kernel, t=t, n_cmp=n_cmp, n_blk=n_blk, n_back=WINDOW // t)
    feat = pl.BlockSpec((1, W // G, S), lambda b, g: (b, g, 0))
    return pl.pallas_call(
        kern,
        name="nsa_local_attn",
        grid=(B, G),
        in_specs=[
            feat,
            pl.BlockSpec((1, 1) + kc.shape[2:], lambda b, g: (b, g, 0, 0)),
            pl.BlockSpec((1, 1) + vct.shape[2:], lambda b, g: (b, g, 0, 0)),
            pl.BlockSpec(ovt.shape, lambda b, g: (0, 0)),
            pl.BlockSpec((1, 1, S, LANES), lambda b, g: (b, g, 0, 0)),
            pl.BlockSpec((1, S // t, HEAD_DIM, t), lambda b, g: (b, 0, g, 0)),
        ],
        out_specs=[feat, pl.BlockSpec((1, 1, n_blk, S), lambda b, g: (b, g, 0, 0)), feat],
        out_shape=[
            jax.ShapeDtypeStruct((B, W, S), BF16),
            jax.ShapeDtypeStruct((B, G, n_blk, S), BF16),
            jax.ShapeDtypeStruct((B, W, S), BF16),
        ],
        compiler_params=_params("parallel", "parallel"),
    )(qt, kc, vct, ovt, kw, vw)


def _sel_attn_kernel(q_ref, bias_ref, k_ref, v_ref, oc_ref, ow_ref, gt_ref, z_ref, o_ref,
                     s_scr, acc_scr, *, t):
    nh = s_scr.shape[1]
    n_q = q_ref.shape[2] // t

    def query(i, slot, hh):
        cols = slice(i * t, (i + 1) * t)
        bias = _pad_rows(bias_ref[0, hh // NSA_REP, :, cols], HEAD_DIM)
        return jnp.concatenate([q_ref[0, hh * HEAD_DIM:(hh + 1) * HEAD_DIM, cols], bias], axis=0)

    def key_tile(j, hh):
        return k_ref[0, hh // NSA_REP, _tile(j, t), :]

    def value_rows(j, hh):
        g = hh // NSA_REP
        return v_ref[0, j, g * HEAD_DIM:(g + 1) * HEAD_DIM, :]

    def emit(i, accs):
        cols = slice(i * t, (i + 1) * t)
        for hh in range(nh):
            rows = slice(hh * HEAD_DIM, (hh + 1) * HEAD_DIM)
            gate = lambda b: gt_ref[0, 0, N_BRANCH * hh + b:N_BRANCH * hh + b + 1, cols]
            o = (gate(0) * oc_ref[0, rows, cols].astype(F32) + gate(1) * _normalised(accs[hh])
                 + gate(2) * ow_ref[0, rows, cols].astype(F32))
            o_ref[0, rows, cols] = (o * z_ref[0, rows, cols].astype(F32)).astype(o_ref.dtype)

    _causal_attention(n_q, t, lambda i, slot: None, query, key_tile, value_rows, emit,
                      s_scr, acc_scr)


def _sel_attn(qt, bias, k_aug, vt, oc, ow, gt, zt):
    B, W, S = qt.shape
    t = min(ATTN_TILE, S)
    ng = HEADS_PER_STEP // NSA_REP
    n_blk = bias.shape[2]
    feat = pl.BlockSpec((1, HEADS_PER_STEP * HEAD_DIM, S), lambda b, g: (b, g, 0))
    return pl.pallas_call(
        functools.partial(_sel_attn_kernel, t=t),
        name="nsa_sel_attn",
        grid=(B, NSA_GROUPS // ng),
        in_specs=[
            feat,
            pl.BlockSpec((1, ng, n_blk, S), lambda b, g: (b, g, 0, 0)),
            pl.BlockSpec((1, ng, S, LANES), lambda b, g: (b, g, 0, 0)),
            pl.BlockSpec((1, S // t, ng * HEAD_DIM, t), lambda b, g: (b, 0, g, 0)),
            feat, feat,
            pl.BlockSpec((1, 1, GATE_ROWS, S), lambda b, g: (b, g, 0, 0)),
            feat,
        ],
        out_specs=feat,
        out_shape=jax.ShapeDtypeStruct((B, W, S), BF16),
        scratch_shapes=_attn_scratch(HEADS_PER_STEP, t),
        compiler_params=_params("parallel", "parallel"),
    )(qt, bias, k_aug, vt, oc, ow, gt, zt)


def _fox_mixer(x, g, w_in, b_f):
    B, S, _ = x.shape
    W, H = ATTN_WIDTH, N_HEADS
    w_k = w_in[:, W:2 * W].astype(BF16)
    w_t = jnp.concatenate([w_in[:, :W] * Q_SCALE, w_in[:, 2 * W:3 * W], w_in[:, 3 * W + H:]],
                          axis=1).T.astype(BF16)
    wf_hi, wf_lo = _split2(w_in[:, 3 * W:3 * W + H])
    pad = lambda n: jnp.zeros((D_MODEL, LANES - n * H), BF16)
    wf_a = jnp.concatenate([wf_hi] * 3 + [wf_lo] * 3 + [pad(6)], axis=1)
    wf_b = jnp.concatenate([wf_hi] * 3 + [pad(3)], axis=1)
    bf_row = jnp.concatenate([b_f, b_f, b_f, jnp.zeros((LANES - 3 * H,), F32)])[None, :]

    qt, k, vt, zt, cx = _fox_in(x, g[None, :], w_k, w_t, wf_a, wf_b, bf_row)

    t = min(ATTN_TILE, S)
    r = np.arange(LANES)
    sel = np.logical_and(r[None, :] % H == np.arange(H)[:, None], r[None, :] < 3 * H)
    qx = np.broadcast_to(np.where(sel, -1.0, 0.0).astype(np.float32)[:, :, None], (H, LANES, t))
    return _fox_attn(qt, jnp.asarray(qx, BF16), k, cx, vt, zt)


def _nsa_layer(ot0, w_out0, x, g, w_in, pe_k, w_ck1, w_ck2, pe_v, w_cv1, w_cv2, w_out, final_g):
    B, S, _ = x.shape
    W, KV, G = ATTN_WIDTH, KV_WIDTH, NSA_GROUPS
    sec = lambda j: w_in[:, W + j * KV:W + (j + 1) * KV]
    gate_off = W + 6 * KV
    n_gate = N_BRANCH * N_HEADS
    w_k = jnp.concatenate([sec(2), sec(4), sec(0), sec(1)], axis=1).astype(BF16)
    w_t = jnp.concatenate([w_in[:, :W] * Q_SCALE, sec(3), sec(5), w_in[:, gate_off + n_gate:],
                           w_in[:, gate_off:gate_off + n_gate]], axis=1).T.astype(BF16)
    pos = np.arange(S)
    cos, sin = _rope_angles(pos)
    x, qt, ksel, kwin, chk, chv, vsel, vwin, zt, gt = _nsa_in(
        ot0, w_out0.astype(BF16), x, g[None, :], w_k, w_t, _rope_lane_tables(pos),
        (np.ascontiguousarray(cos.T), np.ascontiguousarray(sin.T)))

    n_chunk = S // CMP_STRIDE
    n_cmp = n_chunk - CMP_LEN // CMP_STRIDE + 1

    def flat_pe(pe):
        pe = pe.reshape(1, CMP_LEN * HEAD_DIM)
        return jnp.broadcast_to(pe, (SUBLANES, CMP_LEN * HEAD_DIM)).astype(BF16)

    w2k = jnp.concatenate([w_ck2, jnp.zeros_like(w_ck2)], axis=1).astype(BF16)
    cmp_end = np.arange(n_chunk) * CMP_STRIDE + CMP_LEN - 1
    kc, vct = _compress(chk, chv, w_ck1.astype(BF16), w2k, flat_pe(pe_k),
                        w_cv1.astype(BF16), w_cv2.T.astype(BF16), flat_pe(pe_v),
                        _rope_lane_tables(cmp_end))

    n_blk = S // SEL_LEN
    ci = np.arange(n_chunk) * CMP_STRIDE
    sj = np.arange(n_blk) * SEL_LEN
    ovt = np.logical_and(ci[None, :] < sj[:, None] + SEL_LEN, ci[None, :] + CMP_LEN > sj[:, None])
    ovt = np.logical_and(ovt, np.arange(n_chunk)[None, :] < n_cmp).astype(np.float32)
    oc, bias, ow = _local_attn(qt, kc, vct, jnp.asarray(ovt, BF16), n_cmp, kwin, vwin)
    ot = _sel_attn(qt, bias, ksel, vsel, oc, ow, gt, zt)
    return _final_proj(ot, x, w_out.astype(BF16), final_g[None, :])


def kernel(x, norm_g, fox_w_in, fox_b_f, fox_w_out, nsa_w_in, nsa_pe_k, nsa_w_ck1, nsa_w_ck2,
           nsa_pe_v, nsa_w_cv1, nsa_w_cv2, nsa_w_out, final_g):
    ot0 = _fox_mixer(x, norm_g[0], fox_w_in[0], fox_b_f[0])
    return _nsa_layer(ot0, fox_w_out[0], x, norm_g[1], nsa_w_in[0], nsa_pe_k[0], nsa_w_ck1[0],
                      nsa_w_ck2[0], nsa_pe_v[0], nsa_w_cv1[0], nsa_w_cv2[0], nsa_w_out[0], final_g)
```

```python
import functools
import math

import jax
import jax.numpy as jnp
import numpy as np
from jax import lax
from jax.experimental import pallas as pl
from jax.experimental.pallas import tpu as pltpu

F32 = jnp.float32
BF16 = jnp.bfloat16

D_MODEL = 1024
N_HEADS = 16
HEAD_DIM = 64
ATTN_WIDTH = N_HEADS * HEAD_DIM
NSA_GROUPS = 4
NSA_REP = N_HEADS // NSA_GROUPS
KV_WIDTH = NSA_GROUPS * HEAD_DIM
CMP_LEN = 32
CMP_STRIDE = 16
CMP_HIDDEN = 256
SEL_LEN = 64
N_SELECT = 8
WINDOW = 512
N_BRANCH = 3
ROPE_THETA = 500000.0
ROPE_DIM = HEAD_DIM // 4
ROPE_HALF = ROPE_DIM // 2
NORM_EPS = 1e-6
NEG = -1e30
FORCE = 1e6
LOG2E = math.log2(math.e)
Q_SCALE = HEAD_DIM ** -0.5 * LOG2E

LANES = 128
SUBLANES = 8
ONES_ROWS = 2 * SUBLANES
PROJ_ROWS = 1024
OUT_ROWS = 1024
ATTN_TILE = 256
HEADS_PER_STEP = 8
LOOKAHEAD = 1
GATE_ROWS = -(-N_BRANCH * HEADS_PER_STEP // SUBLANES) * SUBLANES
VMEM_LIMIT = 56 * 1024 * 1024


def _params(*sem):
    return pltpu.CompilerParams(dimension_semantics=sem, vmem_limit_bytes=VMEM_LIMIT)


def _iota(shape, dim):
    return lax.broadcasted_iota(jnp.int32, shape, dim)


def _split2(x):
    hi = x.astype(BF16)
    lo = (x - hi.astype(F32)).astype(BF16)
    return hi, lo


def _split3(x):
    hi = x.astype(BF16)
    r1 = x - hi.astype(F32)
    mid = r1.astype(BF16)
    lo = (r1 - mid.astype(F32)).astype(BF16)
    return hi, mid, lo


def _dot(a, b):
    return jnp.dot(a, b, preferred_element_type=F32)


def _dot_nt(a, b):
    return lax.dot_general(a, b, (((1,), (1,)), ((), ())), preferred_element_type=F32)


def _dot_tn(a, b):
    return lax.dot_general(a, b, (((0,), (0,)), ((), ())), preferred_element_type=F32)


def _rmsnorm(x, g):
    ms = jnp.mean(x * x, axis=-1, keepdims=True)
    return x * lax.rsqrt(ms + NORM_EPS) * g


def _silu(x):
    return x * (1.0 / (1.0 + jnp.exp(-x)))


def _col_reduce(x, op, reduce):
    rows = x.shape[0]
    if rows % (4 * SUBLANES) == 0 and rows >= 8 * SUBLANES:
        q = rows // 4
        x = op(op(x[:q], x[q:2 * q]), op(x[2 * q:3 * q], x[3 * q:]))
    return reduce(x, axis=0, keepdims=True)


def _col_max(x):
    return _col_reduce(x, jnp.maximum, jnp.max)


def _col_sum(x):
    return _col_reduce(x, jnp.add, jnp.sum)


def _pad_rows(x, rows):
    if rows == x.shape[0]:
        return x
    return jnp.concatenate([x, jnp.zeros((rows - x.shape[0], x.shape[1]), x.dtype)], axis=0)


def _tile(j, t):
    return pl.ds(j * t if isinstance(j, int) else pl.multiple_of(j * t, t), t)


def _normalised(acc):
    return acc[:HEAD_DIM] * (1.0 / acc[HEAD_DIM:HEAD_DIM + 1])


def _rope_lanes(x, cos, sin_lo, sin_hi):
    return (x * cos + pltpu.roll(x, LANES - ROPE_HALF, axis=1) * sin_lo
            + pltpu.roll(x, ROPE_HALF, axis=1) * sin_hi)


def _rope_rows(x, cos, sin):
    x1, x2 = x[:ROPE_HALF], x[ROPE_HALF:ROPE_DIM]
    return jnp.concatenate([x1 * cos - x2 * sin, x1 * sin + x2 * cos, x[ROPE_DIM:]], axis=0)


def _rope_angles(pos):
    inv_freq = np.power(np.float32(ROPE_THETA),
                        -np.arange(ROPE_HALF, dtype=np.float32) * np.float32(2.0 / ROPE_DIM))
    ang = pos.astype(np.float32)[:, None] * inv_freq[None, :].astype(np.float32)
    return np.cos(ang).astype(np.float32), np.sin(ang).astype(np.float32)


def _rope_lane_tables(pos):
    cos, sin = _rope_angles(pos)
    n = pos.shape[0]
    ones = np.ones((n, HEAD_DIM - ROPE_DIM), np.float32)
    zeros = np.zeros((n, HEAD_DIM - ROPE_DIM), np.float32)
    z8 = np.zeros((n, ROPE_HALF), np.float32)
    c = np.concatenate([cos, cos, ones], axis=1)
    s_lo = np.concatenate([-sin, z8, zeros], axis=1)
    s_hi = np.concatenate([z8, sin, zeros], axis=1)
    tile = lambda t: np.concatenate([t, t], axis=1)
    return tile(c), tile(s_lo), tile(s_hi)


def _fox_in_kernel(x_ref, g_ref, wk_ref, wt_ref, wfa_ref, wfb_ref, bf_ref, tri_ref,
                   qt_ref, k_ref, vt_ref, zt_ref, cx_ref, carry_ref, *, tk):
    s = pl.program_id(1)
    h = _rmsnorm(x_ref[0], g_ref[...])
    hb = h.astype(BF16)
    tm = hb.shape[0]
    W, H = ATTN_WIDTH, N_HEADS

    @pl.when(s == 0)
    def _():
        carry_ref[...] = jnp.zeros_like(carry_ref)

    h_lo = (h - hb.astype(F32)).astype(BF16)
    fa = _dot(hb, wfa_ref[...])
    f = fa + pltpu.roll(fa, LANES - 3 * H, axis=1) + _dot(h_lo, wfb_ref[...]) + bf_ref[...]

    for n0 in range(0, W, 512):
        k_ref[0, :, n0:n0 + 512] = _dot(hb, wk_ref[:, n0:n0 + 512]).astype(BF16)

    log_f = jnp.minimum(f, 0.0) - jnp.log1p(jnp.exp(-jnp.abs(f)))
    lane = _iota(log_f.shape, 1)
    a0, a1, a2 = _split3(log_f)
    pieces = jnp.where(lane < H, a0, jnp.where(lane < 2 * H, a1, a2))
    c = _dot(tri_ref[...], pieces)

    for r0 in list(range(2 * W, 3 * W, 512)) + list(range(0, 2 * W, 512)):
        t = _dot_nt(wt_ref[r0:r0 + 512, :], hb)
        if r0 < W:
            qt_ref[0, r0:r0 + 512, :] = t.astype(BF16)
        elif r0 < 2 * W:
            t = t.astype(BF16)
            for ct in range(tm // tk):
                vt_ref[0, ct, r0 - W:r0 - W + 512, :] = t[:, ct * tk:(ct + 1) * tk]
        else:
            zt_ref[0, r0 - 2 * W:r0 - 2 * W + 512, :] = _silu(t).astype(BF16)

    c = c + pltpu.roll(c, LANES - H, axis=1) + pltpu.roll(c, LANES - 2 * H, axis=1)
    c = c + carry_ref[0:1, :]
    carry_ref[...] = jnp.broadcast_to(c[tm - 1:tm, :], carry_ref.shape)
    c = jnp.where(lane < H, c, jnp.where(lane < 2 * H, pltpu.roll(c, H, axis=1),
                                         pltpu.roll(c, 2 * H, axis=1)))

    hi, mid, lo = _split3(c * LOG2E)
    zero = jnp.zeros_like(hi)
    cx_ref[0] = jnp.where(lane < H, hi,
                          jnp.where(lane < 2 * H, mid, jnp.where(lane < 3 * H, lo, zero)))


def _fox_in(x, g, w_k, w_t, wf_a, wf_b, bf_row):
    B, S, _ = x.shape
    tm = min(PROJ_ROWS, S)
    tk = min(ATTN_TILE, S)
    W = ATTN_WIDTH
    tri = np.tril(np.ones((tm, tm), np.float32)).astype(BF16)
    const = lambda b, s: (0, 0)
    feat = pl.BlockSpec((1, W, tm), lambda b, s: (b, 0, s))
    return pl.pallas_call(
        functools.partial(_fox_in_kernel, tk=tk),
        name="fox_in",
        grid=(B, S // tm),
        in_specs=[
            pl.BlockSpec((1, tm, D_MODEL), lambda b, s: (b, s, 0)),
            pl.BlockSpec((1, D_MODEL), const),
            pl.BlockSpec(w_k.shape, const),
            pl.BlockSpec(w_t.shape, const),
            pl.BlockSpec((D_MODEL, LANES), const),
            pl.BlockSpec((D_MODEL, LANES), const),
            pl.BlockSpec((1, LANES), const),
            pl.BlockSpec((tm, tm), const),
        ],
        out_specs=[
            feat,
            pl.BlockSpec((1, tm, W), lambda b, s: (b, s, 0)),
            pl.BlockSpec((1, tm // tk, W, tk), lambda b, s: (b, s, 0, 0)),
            feat,
            pl.BlockSpec((1, tm, LANES), lambda b, s: (b, s, 0)),
        ],
        out_shape=[
            jax.ShapeDtypeStruct((B, W, S), BF16),
            jax.ShapeDtypeStruct((B, S, W), BF16),
            jax.ShapeDtypeStruct((B, S // tk, W, tk), BF16),
            jax.ShapeDtypeStruct((B, W, S), BF16),
            jax.ShapeDtypeStruct((B, S, LANES), BF16),
        ],
        scratch_shapes=[pltpu.VMEM((SUBLANES, LANES), F32)],
        compiler_params=_params("parallel", "arbitrary"),
    )(x, g, w_k, w_t, wf_a, wf_b, bf_row, tri)


def _causal_attention(n_q, t, prepare_q, query, key_tile, value_rows, emit, s_scr, acc_scr):
    heads = range(s_scr.shape[1])
    ones = jnp.ones((ONES_ROWS, t), BF16)
    causal = _iota((t, t), 0) <= _iota((t, t), 1)

    def logits(i, slot, j):
        return [_dot(key_tile(j, hh), query(i, slot, hh)) for hh in heads]

    def pv(j, hh, p):
        return _dot(jnp.concatenate([value_rows(j, hh), ones], axis=0), p.astype(BF16))

    def finish(i, slot, carry):
        accs = []
        for hh in heads:
            m, alpha = carry[hh]
            accs.append(alpha * acc_scr[hh] + pv(i - 1, hh, jnp.exp2(s_scr[slot, hh] - m)))
        emit(i, accs)

    pending = None
    for i in range(n_q):
        slot = i % 2
        prepare_q(i, slot)
        s_diag = logits(i, slot, i)
        s_first = logits(i, slot, 0) if i else None
        if pending is not None:
            finish(*pending)
        m_diag = []
        for hh in heads:
            s = jnp.where(causal, s_diag[hh], NEG)
            m = _col_max(s)
            acc_scr[hh] = pv(i, hh, jnp.exp2(s - m))
            m_diag.append(m)
        if i == 0:
            emit(0, [acc_scr[hh] for hh in heads])
            continue
        carry = []
        for hh in heads:
            s_scr[slot, hh] = s_first[hh]
            m_new = jnp.maximum(m_diag[hh], _col_max(s_first[hh]))
            carry.append((m_new, jnp.exp2(m_diag[hh] - m_new)))

        def body(j, carry, i=i, slot=slot):
            s_next, pvs = [], []
            for hh in heads:
                s_next.append(_dot(key_tile(j + 1, hh), query(i, slot, hh)))
                pvs.append(pv(j, hh, jnp.exp2(s_scr[slot, hh] - carry[hh][0])))
            out = []
            for hh in heads:
                m = carry[hh][0]
                s_scr[slot, hh] = s_next[hh]
                m_new = jnp.maximum(m, _col_max(s_next[hh]))
                out.append((m_new, jnp.exp2(m - m_new)))
            for hh in heads:
                acc_scr[hh] = carry[hh][1] * acc_scr[hh] + pvs[hh]
            return tuple(out)

        carry = lax.fori_loop(0, i - 1, body, tuple(carry))
        pending = (i, slot, carry)
    if pending is not None:
        finish(*pending)


def _attn_scratch(nh, t):
    return [pltpu.VMEM((2, nh, t, t), F32), pltpu.VMEM((nh, HEAD_DIM + ONES_ROWS, t), F32)]


def _fox_attn_kernel(q_ref, qx_ref, k_ref, kx_ref, v_ref, z_ref, o_ref, q_scr, s_scr, acc_scr, *, t):
    nh = q_scr.shape[1]
    n_q = q_ref.shape[2] // t
    row = _iota((LANES, t), 0)

    def load_q(i, slot):
        for hh in range(nh):
            pair, sub = divmod(hh, 2)
            qt = q_ref[0, pair * LANES:(pair + 1) * LANES, i * t:(i + 1) * t]
            own = jnp.logical_and(row >= sub * HEAD_DIM, row < (sub + 1) * HEAD_DIM)
            q_scr[slot, hh] = jnp.where(own, qt, jnp.zeros_like(qt))

    def query(i, slot, hh):
        return jnp.concatenate([q_scr[slot, hh], qx_ref[hh]], axis=0)

    def key_tile(j, hh):
        pair = hh // 2
        return jnp.concatenate([k_ref[0, _tile(j, t), pair * LANES:(pair + 1) * LANES],
                                kx_ref[0, _tile(j, t), :]], axis=1)

    def value_rows(j, hh):
        return v_ref[0, j, hh * HEAD_DIM:(hh + 1) * HEAD_DIM, :]

    def emit(i, accs):
        for hh in range(nh):
            rows, cols = slice(hh * HEAD_DIM, (hh + 1) * HEAD_DIM), slice(i * t, (i + 1) * t)
            gated = _normalised(accs[hh]) * z_ref[0, rows, cols].astype(F32)
            o_ref[0, rows, cols] = gated.astype(o_ref.dtype)

    _causal_attention(n_q, t, load_q, query, key_tile, value_rows, emit, s_scr, acc_scr)


def _fox_attn(qt, qx, k, kx, vt, zt):
    B, W, S = qt.shape
    t = min(ATTN_TILE, S)
    nh = HEADS_PER_STEP
    rows = nh * HEAD_DIM
    feat = pl.BlockSpec((1, rows, S), lambda b, p: (b, p, 0))
    return pl.pallas_call(
        functools.partial(_fox_attn_kernel, t=t),
        name="fox_attn",
        grid=(B, N_HEADS // nh),
        in_specs=[
            feat,
            pl.BlockSpec((nh, LANES, t), lambda b, p: (p, 0, 0)),
            pl.BlockSpec((1, S, rows), lambda b, p: (b, 0, p)),
            pl.BlockSpec((1, S, LANES), lambda b, p: (b, 0, 0)),
            pl.BlockSpec((1, S // t, rows, t), lambda b, p: (b, 0, p, 0)),
            feat,
        ],
        out_specs=feat,
        out_shape=jax.ShapeDtypeStruct((B, W, S), BF16),
        scratch_shapes=[pltpu.VMEM((2, nh, LANES, t), BF16)] + _attn_scratch(nh, t),
        compiler_params=_params("parallel", "parallel"),
    )(qt, qx, k, kx, vt, zt)


def _final_proj_kernel(ot_ref, x_ref, w_ref, g_ref, y_ref):
    y = x_ref[0] + _dot_tn(ot_ref[0], w_ref[...])
    y_ref[0] = _rmsnorm(y, g_ref[...])


def _final_proj(ot, x, w_out, final_g):
    B, S, _ = x.shape
    tm = min(OUT_ROWS, S)
    const = lambda b, s: (0, 0)
    tok = pl.BlockSpec((1, tm, D_MODEL), lambda b, s: (b, s, 0))
    return pl.pallas_call(
        _final_proj_kernel,
        name="final_proj",
        grid=(B, S // tm),
        in_specs=[pl.BlockSpec((1, ATTN_WIDTH, tm), lambda b, s: (b, 0, s)), tok,
                  pl.BlockSpec((ATTN_WIDTH, D_MODEL), const), pl.BlockSpec((1, D_MODEL), const)],
        out_specs=tok,
        out_shape=jax.ShapeDtypeStruct((B, S, D_MODEL), F32),
        compiler_params=_params("parallel", "parallel"),
    )(ot, x, w_out, final_g)


_TOK_SEL = 0
_TOK_WIN = KV_WIDTH
_TOK_CMP = 2 * KV_WIDTH
_TOK_COLS = 4 * KV_WIDTH
_FEAT_Q = 0
_FEAT_VSEL = ATTN_WIDTH
_FEAT_VWIN = _FEAT_VSEL + KV_WIDTH
_FEAT_Z = _FEAT_VWIN + KV_WIDTH
_FEAT_GATE = _FEAT_Z + ATTN_WIDTH
_FEAT_ROWS = _FEAT_GATE + N_BRANCH * N_HEADS


def _nsa_in_kernel(ot_ref, wo_ref, x_ref, g_ref, wk_ref, wt_ref, cos_ref, slo_ref, shi_ref,
                   cost_ref, sint_ref,
                   x1_ref, qt_ref, ksel_ref, kwin_ref, chk_ref, chv_ref, vsel_ref, vwin_ref, zt_ref,
                   gt_ref, cmp_scr, *, tk):
    s = pl.program_id(1)
    x1 = x_ref[0] + _dot_tn(ot_ref[0], wo_ref[...])
    x1_ref[0] = x1
    hb = _rmsnorm(x1, g_ref[...]).astype(BF16)
    tm = hb.shape[0]
    G = NSA_GROUPS

    cos, slo, shi = cos_ref[...], slo_ref[...], shi_ref[...]
    tok = s * tm + _iota((tm, LANES), 0)
    lane = _iota((tm, LANES), 1)
    low = lane < HEAD_DIM
    block_id = jnp.where(lane - HEAD_DIM == tok // SEL_LEN, 1.0, 0.0)
    keys = _dot(hb, wk_ref[:, _TOK_SEL:_TOK_CMP])
    for col0, out_ref, extra in ((_TOK_SEL, ksel_ref, block_id), (_TOK_WIN, kwin_ref, 0.0)):
        for gp in range(G // 2):
            pair = _rope_lanes(keys[:, col0 + gp * LANES:col0 + (gp + 1) * LANES], cos, slo, shi)
            out_ref[0, 2 * gp] = jnp.where(low, pair, extra).astype(BF16)
            out_ref[0, 2 * gp + 1] = jnp.where(
                low, pltpu.roll(pair, HEAD_DIM, axis=1), extra).astype(BF16)

    raw = _dot(hb, wk_ref[:, _TOK_CMP:_TOK_COLS])
    for c in range(cmp_scr.shape[0]):
        cmp_scr[c] = raw[:, c * LANES:(c + 1) * LANES]

    cost, sint = cost_ref[...], sint_ref[...]
    for r0 in range(0, ATTN_WIDTH, 512):
        t = _dot_nt(wt_ref[_FEAT_Q + r0:_FEAT_Q + r0 + 512, :], hb)
        for h0 in range(0, 512, HEAD_DIM):
            qt_ref[0, r0 + h0:r0 + h0 + HEAD_DIM, :] = _rope_rows(
                t[h0:h0 + HEAD_DIM], cost, sint).astype(BF16)
        zt_ref[0, r0:r0 + 512, :] = _silu(
            _dot_nt(wt_ref[_FEAT_Z + r0:_FEAT_Z + r0 + 512, :], hb)).astype(BF16)
    for row0, out_ref in ((_FEAT_VSEL, vsel_ref), (_FEAT_VWIN, vwin_ref)):
        t = _dot_nt(wt_ref[row0:row0 + KV_WIDTH, :], hb).astype(BF16)
        for c in range(tm // tk):
            out_ref[0, c] = t[:, c * tk:(c + 1) * tk]
    gate = _dot_nt(wt_ref[_FEAT_GATE:_FEAT_ROWS, :], hb)
    gate = 1.0 / (1.0 + jnp.exp(-gate))
    rows = N_BRANCH * HEADS_PER_STEP
    for hg in range(N_HEADS // HEADS_PER_STEP):
        gt_ref[0, hg] = _pad_rows(gate[hg * rows:(hg + 1) * rows], GATE_ROWS)

    n_ch = tm // CMP_STRIDE
    low_ch = _iota((n_ch, LANES), 1) < HEAD_DIM
    for kind, out_ref in enumerate((chk_ref, chv_ref)):
        for gp in range(G // 2):
            c = kind * (G // 2) + gp
            for l in range(0, CMP_STRIDE, 2):
                a0 = cmp_scr[c, pl.ds(l, n_ch, stride=CMP_STRIDE), :]
                a1 = cmp_scr[c, pl.ds(l + 1, n_ch, stride=CMP_STRIDE), :]
                dst = slice(l * HEAD_DIM, (l + 2) * HEAD_DIM)
                out_ref[0, 2 * gp, :, dst] = jnp.where(
                    low_ch, a0, pltpu.roll(a1, HEAD_DIM, axis=1)).astype(BF16)
                out_ref[0, 2 * gp + 1, :, dst] = jnp.where(
                    low_ch, pltpu.roll(a0, HEAD_DIM, axis=1), a1).astype(BF16)


def _nsa_in(ot, w_out, x, g, w_k, w_t, lane_tables, row_tables):
    B, S, _ = x.shape
    tm = min(PROJ_ROWS, S)
    tk = min(ATTN_TILE, S)
    W, G = ATTN_WIDTH, NSA_GROUPS
    const = lambda b, s: (0, 0)
    feat = lambda rows: pl.BlockSpec((1, rows, tm), lambda b, s: (b, 0, s))
    tiles = pl.BlockSpec((1, tm // tk, KV_WIDTH, tk), lambda b, s: (b, s, 0, 0))
    keys = pl.BlockSpec((1, G, tm, LANES), lambda b, s: (b, 0, s, 0))
    chunk_w = CMP_STRIDE * HEAD_DIM
    chunk = pl.BlockSpec((1, G, tm // CMP_STRIDE, chunk_w), lambda b, s: (b, 0, s, 0))
    tok = pl.BlockSpec((1, tm, D_MODEL), lambda b, s: (b, s, 0))
    n_hg = N_HEADS // HEADS_PER_STEP
    return pl.pallas_call(
        functools.partial(_nsa_in_kernel, tk=tk),
        name="nsa_in",
        grid=(B, S // tm),
        in_specs=[
            feat(W),
            pl.BlockSpec(w_out.shape, const),
            tok,
            pl.BlockSpec((1, D_MODEL), const),
            pl.BlockSpec(w_k.shape, const),
            pl.BlockSpec(w_t.shape, const),
        ] + [pl.BlockSpec((tm, LANES), lambda b, s: (s, 0))] * 3
          + [pl.BlockSpec((ROPE_HALF, tm), lambda b, s: (0, s))] * 2,
        out_specs=[tok, feat(W), keys, keys, chunk, chunk, tiles, tiles, feat(W),
                   pl.BlockSpec((1, n_hg, GATE_ROWS, tm), lambda b, s: (b, 0, 0, s))],
        out_shape=[
            jax.ShapeDtypeStruct((B, S, D_MODEL), F32),
            jax.ShapeDtypeStruct((B, W, S), BF16),
            jax.ShapeDtypeStruct((B, G, S, LANES), BF16),
            jax.ShapeDtypeStruct((B, G, S, LANES), BF16),
            jax.ShapeDtypeStruct((B, G, S // CMP_STRIDE, chunk_w), BF16),
            jax.ShapeDtypeStruct((B, G, S // CMP_STRIDE, chunk_w), BF16),
            jax.ShapeDtypeStruct((B, S // tk, KV_WIDTH, tk), BF16),
            jax.ShapeDtypeStruct((B, S // tk, KV_WIDTH, tk), BF16),
            jax.ShapeDtypeStruct((B, W, S), BF16),
            jax.ShapeDtypeStruct((B, n_hg, GATE_ROWS, S), F32),
        ],
        scratch_shapes=[pltpu.VMEM((2 * KV_WIDTH // LANES, tm, LANES), F32)],
        compiler_params=_params("parallel", "parallel"),
    )(ot, w_out, x, g, w_k, w_t, *lane_tables, *row_tables)


def _compress_kernel(chk_ref, chv_ref, w1k_ref, w2k_ref, pek_ref, w1v_ref, w2vt_ref, pev_ref,
                     cos_ref, slo_ref, shi_ref, kc_ref, vct_ref, pe_scr):
    half = CMP_STRIDE * HEAD_DIM

    @pl.when(jnp.logical_and(pl.program_id(0) == 0, pl.program_id(1) == 0))
    def _():
        pe_scr[0] = _dot(pek_ref[...], w1k_ref[...])
        pe_scr[1] = _dot(pev_ref[...], w1v_ref[...])

    def hidden(ch, w1_ref, pe):
        a = _dot(ch, w1_ref[:half, :])
        b = _dot(ch, w1_ref[half:, :])
        b = pltpu.roll(b, b.shape[0] - 1, axis=0)
        return _silu(a + b + pe[0:1, :]).astype(BF16)

    hid_k = hidden(chk_ref[0, 0], w1k_ref, pe_scr[0])
    hid_v = hidden(chv_ref[0, 0], w1v_ref, pe_scr[1])
    kc = _dot(hid_k, w2k_ref[...])
    kc_ref[0, 0] = _rope_lanes(kc, cos_ref[...], slo_ref[...], shi_ref[...]).astype(BF16)
    vct_ref[0, 0] = _dot_nt(w2vt_ref[...], hid_v).astype(BF16)


def _compress(chk, chv, w1k, w2k, pek, w1v, w2vt, pev, tables):
    B, G, n_chunk, width = chk.shape
    const = lambda b, g: (0, 0)
    chunk_spec = pl.BlockSpec((1, 1, n_chunk, width), lambda b, g: (b, g, 0, 0))
    specs = lambda ws: [pl.BlockSpec(w.shape, const) for w in ws]
    return pl.pallas_call(
        _compress_kernel,
        name="nsa_compress",
        grid=(B, G),
        in_specs=[chunk_spec, chunk_spec] + specs((w1k, w2k, pek, w1v, w2vt, pev))
                 + [pl.BlockSpec((n_chunk, LANES), const)] * 3,
        out_specs=[pl.BlockSpec((1, 1, n_chunk, LANES), lambda b, g: (b, g, 0, 0)),
                   pl.BlockSpec((1, 1, HEAD_DIM, n_chunk), lambda b, g: (b, g, 0, 0))],
        out_shape=[jax.ShapeDtypeStruct((B, G, n_chunk, LANES), BF16),
                   jax.ShapeDtypeStruct((B, G, HEAD_DIM, n_chunk), BF16)],
        scratch_shapes=[pltpu.VMEM((2, SUBLANES, w1k.shape[1]), F32)],
        compiler_params=_params("arbitrary", "arbitrary"),
    )(chk, chv, w1k, w2k, pek, w1v, w2vt, pev, *tables)


def _local_attn_kernel(q_ref, kc_ref, vct_ref, ovt_ref, kw_ref, vw_ref, oc_ref, bias_ref, ow_ref,
                       *, t, n_cmp, n_blk, n_back):
    n_q = q_ref.shape[2] // t
    heads = range(NSA_REP)
    kc = kc_ref[0, 0]
    vct = vct_ref[0, 0]
    ovt = ovt_ref[...]
    n_pad = kc.shape[0]
    above = _iota((t, t), 0) > _iota((t, t), 1)
    ones = jnp.ones((ONES_ROWS, t), BF16)
    blk_c = _iota((n_pad, t), 0)
    blk = _iota((n_blk, t), 0)
    n_rounds = min(N_SELECT, n_blk)
    per_head = -(-n_rounds // NSA_REP)

    def q_aug(i, r):
        return _pad_rows(q_ref[0, r * HEAD_DIM:(r + 1) * HEAD_DIM, i * t:(i + 1) * t], LANES)

    masks = {0: jnp.logical_not(above), n_back: above}

    def win_tiles(i):
        return [(i - d, masks.get(d)) for d in range(n_back + 1) if i - d >= 0]

    def win_logits(i, r):
        return [_dot(kw_ref[0, 0, _tile(j, t), :], q_aug(i, r)) for j, _ in win_tiles(i)]

    def cmp_logits(i):
        return [_dot(kc, q_aug(i, r)) for r in heads]

    def importance(i, logits):
        cols = slice(i * t, (i + 1) * t)
        qry = i * t + _iota((n_pad, t), 1)
        valid = jnp.logical_and(blk_c * CMP_STRIDE + (CMP_LEN - 1) <= qry, blk_c < n_cmp)
        p_sum = jnp.zeros((n_pad, t), F32)
        safe = min(max((i * t - CMP_LEN) // CMP_STRIDE + 1, 0), n_cmp) // SUBLANES * SUBLANES
        for r in heads:
            s = logits[r]
            if safe:
                s = jnp.concatenate([s[:safe], jnp.where(valid[safe:], s[safe:], NEG)], axis=0)
            else:
                s = jnp.where(valid, s, NEG)
            e = jnp.exp2(s - _col_max(s))
            p = e * (1.0 / _col_sum(e))
            if i * t < CMP_LEN - 1:
                p = jnp.where(valid, p, 0.0)
            p_sum = p_sum + p
            oc_ref[0, r * HEAD_DIM:(r + 1) * HEAD_DIM, cols] = _dot(
                vct, p.astype(BF16)).astype(oc_ref.dtype)
        p_hi, p_lo = _split2(p_sum)
        imp = _dot(ovt, p_hi) + _dot(ovt, p_lo)
        cur = (i * t + _iota(imp.shape, 1)) // SEL_LEN
        forced = jnp.logical_or(blk == 0, jnp.logical_or(blk == cur, blk == cur - 1))
        return jnp.where(forced, FORCE, jnp.where(blk <= cur, imp, -1.0))

    items = [(i, r) for i in range(n_q) for r in heads]
    win_queue, cmp_queue = [], {}

    def issue(n):
        if n < len(items):
            win_queue.append(win_logits(*items[n]))
            if items[n][1] == 0:
                cmp_queue[items[n][0]] = cmp_logits(items[n][0])

    for n in range(LOOKAHEAD):
        issue(n)
    for n, (i, r) in enumerate(items):
        issue(n + LOOKAHEAD)
        logits = win_queue.pop(0)
        if r == 0:
            imp = importance(i, cmp_queue.pop(i))
            done = 0
        tiles = win_tiles(i)
        values = lambda j: jnp.concatenate([vw_ref[0, j], ones], axis=0)
        if len(tiles) == n_back + 1:
            diag = masks[0]
            s_fold = jnp.where(diag, logits[0], logits[-1])
            full = logits[1:-1]
            m = functools.reduce(jnp.maximum, [_col_max(s) for s in [s_fold] + full])
            p_fold = jnp.exp2(s_fold - m)
            zero = jnp.zeros_like(p_fold)
            parts = [(tiles[0][0], jnp.where(diag, p_fold, zero)),
                     (tiles[-1][0], jnp.where(diag, zero, p_fold))]
            parts += [(j, jnp.exp2(s - m)) for s, (j, _) in zip(full, tiles[1:-1])]
        else:
            ss = [s if mask is None else jnp.where(mask, s, NEG)
                  for s, (_, mask) in zip(logits, tiles)]
            m = functools.reduce(jnp.maximum, [_col_max(s) for s in ss])
            parts = [(j, jnp.exp2(s - m)) for s, (j, _) in zip(ss, tiles)]
        acc = functools.reduce(jnp.add, [_dot(values(j), p.astype(BF16)) for j, p in parts])
        ow_ref[0, r * HEAD_DIM:(r + 1) * HEAD_DIM, i * t:(i + 1) * t] = _normalised(acc).astype(
            ow_ref.dtype)
        for _ in range(min(per_head, n_rounds - done)):
            top = _col_max(imp)
            first = jnp.min(jnp.where(imp == top, blk, n_blk), axis=0, keepdims=True)
            pick = blk == first
            imp = jnp.where(pick, -jnp.inf, imp)
            done += 1
        if r == NSA_REP - 1:
            bias_ref[0, 0, :, i * t:(i + 1) * t] = jnp.where(
                imp == -jnp.inf, 0.0, NEG).astype(bias_ref.dtype)


def _local_attn(qt, kc, vct, ovt, n_cmp, kw, vw):
    B, W, S = qt.shape
    G = kc.shape[1]
    t = min(ATTN_TILE, S)
    assert WINDOW % t == 0
    n_blk = ovt.shape[0]
    kern = functools.partial(_local_attn_kernel, t=t, n_cmp=n_cmp, n_blk=n_blk, n_back=WINDOW // t)
    feat = pl.BlockSpec((1, W // G, S), lambda b, g: (b, g, 0))
    return pl.pallas_call(
        kern,
        name="nsa_local_attn",
        grid=(B, G),
        in_specs=[
            feat,
            pl.BlockSpec((1, 1) + kc.shape[2:], lambda b, g: (b, g, 0, 0)),
            pl.BlockSpec((1, 1) + vct.shape[2:], lambda b, g: (b, g, 0, 0)),
            pl.BlockSpec(ovt.shape, lambda b, g: (0, 0)),
            pl.BlockSpec((1, 1, S, LANES), lambda b, g: (b, g, 0, 0)),
            pl.BlockSpec((1, S // t, HEAD_DIM, t), lambda b, g: (b, 0, g, 0)),
        ],
        out_specs=[feat, pl.BlockSpec((1, 1, n_blk, S), lambda b, g: (b, g, 0, 0)), feat],
        out_shape=[
            jax.ShapeDtypeStruct((B, W, S), BF16),
            jax.ShapeDtypeStruct((B, G, n_blk, S), BF16),
            jax.ShapeDtypeStruct((B, W, S), BF16),
        ],
        compiler_params=_params("parallel", "parallel"),
    )(qt, kc, vct, ovt, kw, vw)


def _sel_attn_kernel(q_ref, bias_ref, k_ref, v_ref, oc_ref, ow_ref, gt_ref, z_ref, o_ref,
                     s_scr, acc_scr, *, t):
    nh = s_scr.shape[1]
    n_q = q_ref.shape[2] // t

    def query(i, slot, hh):
        cols = slice(i * t, (i + 1) * t)
        bias = _pad_rows(bias_ref[0, hh // NSA_REP, :, cols], HEAD_DIM)
        return jnp.concatenate([q_ref[0, hh * HEAD_DIM:(hh + 1) * HEAD_DIM, cols], bias], axis=0)

    def key_tile(j, hh):
        return k_ref[0, hh // NSA_REP, _tile(j, t), :]

    def value_rows(j, hh):
        g = hh // NSA_REP
        return v_ref[0, j, g * HEAD_DIM:(g + 1) * HEAD_DIM, :]

    def emit(i, accs):
        cols = slice(i * t, (i + 1) * t)
        for hh in range(nh):
            rows = slice(hh * HEAD_DIM, (hh + 1) * HEAD_DIM)
            gate = lambda b: gt_ref[0, 0, N_BRANCH * hh + b:N_BRANCH * hh + b + 1, cols]
            o = (gate(0) * oc_ref[0, rows, cols].astype(F32) + gate(1) * _normalised(accs[hh])
                 + gate(2) * ow_ref[0, rows, cols].astype(F32))
            o_ref[0, rows, cols] = (o * z_ref[0, rows, cols].astype(F32)).astype(o_ref.dtype)

    _causal_attention(n_q, t, lambda i, slot: None, query, key_tile, value_rows, emit,
                      s_scr, acc_scr)


def _sel_attn(qt, bias, k_aug, vt, oc, ow, gt, zt):
    B, W, S = qt.shape
    t = min(ATTN_TILE, S)
    ng = HEADS_PER_STEP // NSA_REP
    n_blk = bias.shape[2]
    feat = pl.BlockSpec((1, HEADS_PER_STEP * HEAD_DIM, S), lambda b, g: (b, g, 0))
    return pl.pallas_call(
        functools.partial(_sel_attn_kernel, t=t),
        name="nsa_sel_attn",
        grid=(B, NSA_GROUPS // ng),
        in_specs=[
            feat,
            pl.BlockSpec((1, ng, n_blk, S), lambda b, g: (b, g, 0, 0)),
            pl.BlockSpec((1, ng, S, LANES), lambda b, g: (b, g, 0, 0)),
            pl.BlockSpec((1, S // t, ng * HEAD_DIM, t), lambda b, g: (b, 0, g, 0)),
            feat, feat,
            pl.BlockSpec((1, 1, GATE_ROWS, S), lambda b, g: (b, g, 0, 0)),
            feat,
        ],
        out_specs=feat,
        out_shape=jax.ShapeDtypeStruct((B, W, S), BF16),
        scratch_shapes=_attn_scratch(HEADS_PER_STEP, t),
        compiler_params=_params("parallel", "parallel"),
    )(qt, bias, k_aug, vt, oc, ow, gt, zt)


def _fox_mixer(x, g, w_in, b_f):
    B, S, _ = x.shape
    W, H = ATTN_WIDTH, N_HEADS
    w_k = w_in[:, W:2 * W].astype(BF16)
    w_t = jnp.concatenate([w_in[:, :W] * Q_SCALE, w_in[:, 2 * W:3 * W], w_in[:, 3 * W + H:]],
                          axis=1).T.astype(BF16)
    wf_hi, wf_lo = _split2(w_in[:, 3 * W:3 * W + H])
    pad = lambda n: jnp.zeros((D_MODEL, LANES - n * H), BF16)
    wf_a = jnp.concatenate([wf_hi] * 3 + [wf_lo] * 3 + [pad(6)], axis=1)
    wf_b = jnp.concatenate([wf_hi] * 3 + [pad(3)], axis=1)
    bf_row = jnp.concatenate([b_f, b_f, b_f, jnp.zeros((LANES - 3 * H,), F32)])[None, :]

    qt, k, vt, zt, cx = _fox_in(x, g[None, :], w_k, w_t, wf_a, wf_b, bf_row)

    t = min(ATTN_TILE, S)
    r = np.arange(LANES)
    sel = np.logical_and(r[None, :] % H == np.arange(H)[:, None], r[None, :] < 3 * H)
    qx = np.broadcast_to(np.where(sel, -1.0, 0.0).astype(np.float32)[:, :, None], (H, LANES, t))
    return _fox_attn(qt, jnp.asarray(qx, BF16), k, cx, vt, zt)


def _nsa_layer(ot0, w_out0, x, g, w_in, pe_k, w_ck1, w_ck2, pe_v, w_cv1, w_cv2, w_out, final_g):
    B, S, _ = x.shape
    W, KV, G = ATTN_WIDTH, KV_WIDTH, NSA_GROUPS
    sec = lambda j: w_in[:, W + j * KV:W + (j + 1) * KV]
    gate_off = W + 6 * KV
    n_gate = N_BRANCH * N_HEADS
    w_k = jnp.concatenate([sec(2), sec(4), sec(0), sec(1)], axis=1).astype(BF16)
    w_t = jnp.concatenate([w_in[:, :W] * Q_SCALE, sec(3), sec(5), w_in[:, gate_off + n_gate:],
                           w_in[:, gate_off:gate_off + n_gate]], axis=1).T.astype(BF16)
    pos = np.arange(S)
    cos, sin = _rope_angles(pos)
    x, qt, ksel, kwin, chk, chv, vsel, vwin, zt, gt = _nsa_in(
        ot0, w_out0.astype(BF16), x, g[None, :], w_k, w_t, _rope_lane_tables(pos),
        (np.ascontiguousarray(cos.T), np.ascontiguousarray(sin.T)))

    n_chunk = S // CMP_STRIDE
    n_cmp = n_chunk - CMP_LEN // CMP_STRIDE + 1

    def flat_pe(pe):
        pe = pe.reshape(1, CMP_LEN * HEAD_DIM)
        return jnp.broadcast_to(pe, (SUBLANES, CMP_LEN * HEAD_DIM)).astype(BF16)

    w2k = jnp.concatenate([w_ck2, jnp.zeros_like(w_ck2)], axis=1).astype(BF16)
    cmp_end = np.arange(n_chunk) * CMP_STRIDE + CMP_LEN - 1
    kc, vct = _compress(chk, chv, w_ck1.astype(BF16), w2k, flat_pe(pe_k),
                        w_cv1.astype(BF16), w_cv2.T.astype(BF16), flat_pe(pe_v),
                        _rope_lane_tables(cmp_end))

    n_blk = S // SEL_LEN
    ci = np.arange(n_chunk) * CMP_STRIDE
    sj = np.arange(n_blk) * SEL_LEN
    ovt = np.logical_and(ci[None, :] < sj[:, None] + SEL_LEN, ci[None, :] + CMP_LEN > sj[:, None])
    ovt = np.logical_and(ovt, np.arange(n_chunk)[None, :] < n_cmp).astype(np.float32)
    oc, bias, ow = _local_attn(qt, kc, vct, jnp.asarray(ovt, BF16), n_cmp, kwin, vwin)
    ot = _sel_attn(qt, bias, ksel, vsel, oc, ow, gt, zt)
    return _final_proj(ot, x, w_out.astype(BF16), final_g[None, :])


def kernel(x, norm_g, fox_w_in, fox_b_f, fox_w_out, nsa_w_in, nsa_pe_k, nsa_w_ck1, nsa_w_ck2,
           nsa_pe_v, nsa_w_cv1, nsa_w_cv2, nsa_w_out, final_g):
    ot0 = _fox_mixer(x, norm_g[0], fox_w_in[0], fox_b_f[0])
    return _nsa_layer(ot0, fox_w_out[0], x, norm_g[1], nsa_w_in[0], nsa_pe_k[0], nsa_w_ck1[0],
                      nsa_w_ck2[0], nsa_pe_v[0], nsa_w_cv1[0], nsa_w_cv2[0], nsa_w_out[0], final_g)
```

```python
import functools
import math

import jax
import jax.numpy as jnp
import numpy as np
from jax import lax
from jax.experimental import pallas as pl
from jax.experimental.pallas import tpu as pltpu

F32 = jnp.float32
BF16 = jnp.bfloat16

D_MODEL = 1024
N_HEADS = 16
HEAD_DIM = 64
ATTN_WIDTH = N_HEADS * HEAD_DIM
NSA_GROUPS = 4
NSA_REP = N_HEADS // NSA_GROUPS
KV_WIDTH = NSA_GROUPS * HEAD_DIM
CMP_LEN = 32
CMP_STRIDE = 16
CMP_HIDDEN = 256
SEL_LEN = 64
N_SELECT = 8
WINDOW = 512
N_BRANCH = 3
ROPE_THETA = 500000.0
ROPE_DIM = HEAD_DIM // 4
ROPE_HALF = ROPE_DIM // 2
NORM_EPS = 1e-6
NEG = -1e30
FORCE = 1e6
LOG2E = math.log2(math.e)
Q_SCALE = HEAD_DIM ** -0.5 * LOG2E

LANES = 128
SUBLANES = 8
ONES_ROWS = 2 * SUBLANES
PROJ_ROWS = 1024
OUT_ROWS = 1024
ATTN_TILE = 256
HEADS_PER_STEP = 8
LOOKAHEAD = 1
GATE_ROWS = -(-N_BRANCH * HEADS_PER_STEP // SUBLANES) * SUBLANES
VMEM_LIMIT = 56 * 1024 * 1024


def _params(*sem):
    return pltpu.CompilerParams(dimension_semantics=sem, vmem_limit_bytes=VMEM_LIMIT)


def _iota(shape, dim):
    return lax.broadcasted_iota(jnp.int32, shape, dim)


def _split2(x):
    hi = x.astype(BF16)
    lo = (x - hi.astype(F32)).astype(BF16)
    return hi, lo


def _split3(x):
    hi = x.astype(BF16)
    r1 = x - hi.astype(F32)
    mid = r1.astype(BF16)
    lo = (r1 - mid.astype(F32)).astype(BF16)
    return hi, mid, lo


def _dot(a, b):
    return jnp.dot(a, b, preferred_element_type=F32)


def _dot_nt(a, b):
    return lax.dot_general(a, b, (((1,), (1,)), ((), ())), preferred_element_type=F32)


def _dot_tn(a, b):
    return lax.dot_general(a, b, (((0,), (0,)), ((), ())), preferred_element_type=F32)


def _rmsnorm(x, g):
    ms = jnp.mean(x * x, axis=-1, keepdims=True)
    return x * lax.rsqrt(ms + NORM_EPS) * g


def _silu(x):
    return x * (1.0 / (1.0 + jnp.exp(-x)))


def _col_reduce(x, op, reduce):
    rows = x.shape[0]
    if rows % (4 * SUBLANES) == 0 and rows >= 8 * SUBLANES:
        q = rows // 4
        x = op(op(x[:q], x[q:2 * q]), op(x[2 * q:3 * q], x[3 * q:]))
    return reduce(x, axis=0, keepdims=True)


def _col_max(x):
    return _col_reduce(x, jnp.maximum, jnp.max)


def _col_sum(x):
    return _col_reduce(x, jnp.add, jnp.sum)


def _pad_rows(x, rows):
    if rows == x.shape[0]:
        return x
    return jnp.concatenate([x, jnp.zeros((rows - x.shape[0], x.shape[1]), x.dtype)], axis=0)


def _tile(j, t):
    return pl.ds(j * t if isinstance(j, int) else pl.multiple_of(j * t, t), t)


def _normalised(acc):
    return acc[:HEAD_DIM] * (1.0 / acc[HEAD_DIM:HEAD_DIM + 1])


def _rope_lanes(x, cos, sin_lo, sin_hi):
    return (x * cos + pltpu.roll(x, LANES - ROPE_HALF, axis=1) * sin_lo
            + pltpu.roll(x, ROPE_HALF, axis=1) * sin_hi)


def _rope_rows(x, cos, sin):
    x1, x2 = x[:ROPE_HALF], x[ROPE_HALF:ROPE_DIM]
    return jnp.concatenate([x1 * cos - x2 * sin, x1 * sin + x2 * cos, x[ROPE_DIM:]], axis=0)


def _rope_angles(pos):
    inv_freq = np.power(np.float32(ROPE_THETA),
                        -np.arange(ROPE_HALF, dtype=np.float32) * np.float32(2.0 / ROPE_DIM))
    ang = pos.astype(np.float32)[:, None] * inv_freq[None, :].astype(np.float32)
    return np.cos(ang).astype(np.float32), np.sin(ang).astype(np.float32)


def _rope_lane_tables(pos):
    cos, sin = _rope_angles(pos)
    n = pos.shape[0]
    ones = np.ones((n, HEAD_DIM - ROPE_DIM), np.float32)
    zeros = np.zeros((n, HEAD_DIM - ROPE_DIM), np.float32)
    z8 = np.zeros((n, ROPE_HALF), np.float32)
    c = np.concatenate([cos, cos, ones], axis=1)
    s_lo = np.concatenate([-sin, z8, zeros], axis=1)
    s_hi = np.concatenate([z8, sin, zeros], axis=1)
    tile = lambda t: np.concatenate([t, t], axis=1)
    return tile(c), tile(s_lo), tile(s_hi)


def _fox_in_kernel(x_ref, g_ref, wk_ref, wt_ref, wfa_ref, wfb_ref, bf_ref, tri_ref,
                   qt_ref, k_ref, vt_ref, zt_ref, cx_ref, carry_ref, *, tk):
    s = pl.program_id(1)
    h = _rmsnorm(x_ref[0], g_ref[...])
    hb = h.astype(BF16)
    tm = hb.shape[0]
    W, H = ATTN_WIDTH, N_HEADS

    @pl.when(s == 0)
    def _():
        carry_ref[...] = jnp.zeros_like(carry_ref)

    h_lo = (h - hb.astype(F32)).astype(BF16)
    fa = _dot(hb, wfa_ref[...])
    f = fa + pltpu.roll(fa, LANES - 3 * H, axis=1) + _dot(h_lo, wfb_ref[...]) + bf_ref[...]

    for n0 in range(0, W, 512):
        k_ref[0, :, n0:n0 + 512] = _dot(hb, wk_ref[:, n0:n0 + 512]).astype(BF16)

    log_f = jnp.minimum(f, 0.0) - jnp.log1p(jnp.exp(-jnp.abs(f)))
    lane = _iota(log_f.shape, 1)
    a0, a1, a2 = _split3(log_f)
    pieces = jnp.where(lane < H, a0, jnp.where(lane < 2 * H, a1, a2))
    c = _dot(tri_ref[...], pieces)

    for r0 in list(range(2 * W, 3 * W, 512)) + list(range(0, 2 * W, 512)):
        t = _dot_nt(wt_ref[r0:r0 + 512, :], hb)
        if r0 < W:
            qt_ref[0, r0:r0 + 512, :] = t.astype(BF16)
        elif r0 < 2 * W:
            t = t.astype(BF16)
            for ct in range(tm // tk):
                vt_ref[0, ct, r0 - W:r0 - W + 512, :] = t[:, ct * tk:(ct + 1) * tk]
        else:
            zt_ref[0, r0 - 2 * W:r0 - 2 * W + 512, :] = _silu(t).astype(BF16)

    c = c + pltpu.roll(c, LANES - H, axis=1) + pltpu.roll(c, LANES - 2 * H, axis=1)
    c = c + carry_ref[0:1, :]
    carry_ref[...] = jnp.broadcast_to(c[tm - 1:tm, :], carry_ref.shape)
    c = jnp.where(lane < H, c, jnp.where(lane < 2 * H, pltpu.roll(c, H, axis=1),
                                         pltpu.roll(c, 2 * H, axis=1)))

    hi, mid, lo = _split3(c * LOG2E)
    zero = jnp.zeros_like(hi)
    cx_ref[0] = jnp.where(lane < H, hi,
                          jnp.where(lane < 2 * H, mid, jnp.where(lane < 3 * H, lo, zero)))


def _fox_in(x, g, w_k, w_t, wf_a, wf_b, bf_row):
    B, S, _ = x.shape
    tm = min(PROJ_ROWS, S)
    tk = min(ATTN_TILE, S)
    W = ATTN_WIDTH
    tri = np.tril(np.ones((tm, tm), np.float32)).astype(BF16)
    const = lambda b, s: (0, 0)
    feat = pl.BlockSpec((1, W, tm), lambda b, s: (b, 0, s))
    return pl.pallas_call(
        functools.partial(_fox_in_kernel, tk=tk),
        name="fox_in",
        grid=(B, S // tm),
        in_specs=[
            pl.BlockSpec((1, tm, D_MODEL), lambda b, s: (b, s, 0)),
            pl.BlockSpec((1, D_MODEL), const),
            pl.BlockSpec(w_k.shape, const),
            pl.BlockSpec(w_t.shape, const),
            pl.BlockSpec((D_MODEL, LANES), const),
            pl.BlockSpec((D_MODEL, LANES), const),
            pl.BlockSpec((1, LANES), const),
            pl.BlockSpec((tm, tm), const),
        ],
        out_specs=[
            feat,
            pl.BlockSpec((1, tm, W), lambda b, s: (b, s, 0)),
            pl.BlockSpec((1, tm // tk, W, tk), lambda b, s: (b, s, 0, 0)),
            feat,
            pl.BlockSpec((1, tm, LANES), lambda b, s: (b, s, 0)),
        ],
        out_shape=[
            jax.ShapeDtypeStruct((B, W, S), BF16),
            jax.ShapeDtypeStruct((B, S, W), BF16),
            jax.ShapeDtypeStruct((B, S // tk, W, tk), BF16),
            jax.ShapeDtypeStruct((B, W, S), BF16),
            jax.ShapeDtypeStruct((B, S, LANES), BF16),
        ],
        scratch_shapes=[pltpu.VMEM((SUBLANES, LANES), F32)],
        compiler_params=_params("parallel", "arbitrary"),
    )(x, g, w_k, w_t, wf_a, wf_b, bf_row, tri)


def _causal_attention(n_q, t, prepare_q, query, key_tile, value_rows, emit, s_scr, acc_scr):
    heads = range(s_scr.shape[1])
    ones = jnp.ones((ONES_ROWS, t), BF16)
    causal = _iota((t, t), 0) <= _iota((t, t), 1)

    def logits(i, slot, j):
        return [_dot(key_tile(j, hh), query(i, slot, hh)) for hh in heads]

    def pv(j, hh, p):
        return _dot(jnp.concatenate([value_rows(j, hh), ones], axis=0), p.astype(BF16))

    def finish(i, slot, carry):
        accs = []
        for hh in heads:
            m, alpha = carry[hh]
            accs.append(alpha * acc_scr[hh] + pv(i - 1, hh, jnp.exp2(s_scr[slot, hh] - m)))
        emit(i, accs)

    pending = None
    for i in range(n_q):
        slot = i % 2
        prepare_q(i, slot)
        s_diag = logits(i, slot, i)
        s_first = logits(i, slot, 0) if i else None
        if pending is not None:
            finish(*pending)
        m_diag = []
        for hh in heads:
            s = jnp.where(causal, s_diag[hh], NEG)
            m = _col_max(s)
            acc_scr[hh] = pv(i, hh, jnp.exp2(s - m))
            m_diag.append(m)
        if i == 0:
            emit(0, [acc_scr[hh] for hh in heads])
            continue
        carry = []
        for hh in heads:
            s_scr[slot, hh] = s_first[hh]
            m_new = jnp.maximum(m_diag[hh], _col_max(s_first[hh]))
            carry.append((m_new, jnp.exp2(m_diag[hh] - m_new)))

        def body(j, carry, i=i, slot=slot):
            s_next, pvs = [], []
            for hh in heads:
                s_next.append(_dot(key_tile(j + 1, hh), query(i, slot, hh)))
                pvs.append(pv(j, hh, jnp.exp2(s_scr[slot, hh] - carry[hh][0])))
            out = []
            for hh in heads:
                m = carry[hh][0]
                s_scr[slot, hh] = s_next[hh]
                m_new = jnp.maximum(m, _col_max(s_next[hh]))
                out.append((m_new, jnp.exp2(m - m_new)))
            for hh in heads:
                acc_scr[hh] = carry[hh][1] * acc_scr[hh] + pvs[hh]
            return tuple(out)

        carry = lax.fori_loop(0, i - 1, body, tuple(carry))
        pending = (i, slot, carry)
    if pending is not None:
        finish(*pending)


def _attn_scratch(nh, t):
    return [pltpu.VMEM((2, nh, t, t), F32), pltpu.VMEM((nh, HEAD_DIM + ONES_ROWS, t), F32)]


def _fox_attn_kernel(q_ref, qx_ref, k_ref, kx_ref, v_ref, z_ref, o_ref, q_scr, s_scr, acc_scr, *, t):
    nh = q_scr.shape[1]
    n_q = q_ref.shape[2] // t
    row = _iota((LANES, t), 0)

    def load_q(i, slot):
        for hh in range(nh):
            pair, sub = divmod(hh, 2)
            qt = q_ref[0, pair * LANES:(pair + 1) * LANES, i * t:(i + 1) * t]
            own = jnp.logical_and(row >= sub * HEAD_DIM, row < (sub + 1) * HEAD_DIM)
            q_scr[slot, hh] = jnp.where(own, qt, jnp.zeros_like(qt))

    def query(i, slot, hh):
        return jnp.concatenate([q_scr[slot, hh], qx_ref[hh]], axis=0)

    def key_tile(j, hh):
        pair = hh // 2
        return jnp.concatenate([k_ref[0, _tile(j, t), pair * LANES:(pair + 1) * LANES],
                                kx_ref[0, _tile(j, t), :]], axis=1)

    def value_rows(j, hh):
        return v_ref[0, j, hh * HEAD_DIM:(hh + 1) * HEAD_DIM, :]

    def emit(i, accs):
        for hh in range(nh):
            rows, cols = slice(hh * HEAD_DIM, (hh + 1) * HEAD_DIM), slice(i * t, (i + 1) * t)
            gated = _normalised(accs[hh]) * z_ref[0, rows, cols].astype(F32)
            o_ref[0, rows, cols] = gated.astype(o_ref.dtype)

    _causal_attention(n_q, t, load_q, query, key_tile, value_rows, emit, s_scr, acc_scr)


def _fox_attn(qt, qx, k, kx, vt, zt):
    B, W, S = qt.shape
    t = min(ATTN_TILE, S)
    nh = HEADS_PER_STEP
    rows = nh * HEAD_DIM
    feat = pl.BlockSpec((1, rows, S), lambda b, p: (b, p, 0))
    return pl.pallas_call(
        functools.partial(_fox_attn_kernel, t=t),
        name="fox_attn",
        grid=(B, N_HEADS // nh),
        in_specs=[
            feat,
            pl.BlockSpec((nh, LANES, t), lambda b, p: (p, 0, 0)),
            pl.BlockSpec((1, S, rows), lambda b, p: (b, 0, p)),
            pl.BlockSpec((1, S, LANES), lambda b, p: (b, 0, 0)),
            pl.BlockSpec((1, S // t, rows, t), lambda b, p: (b, 0, p, 0)),
            feat,
        ],
        out_specs=feat,
        out_shape=jax.ShapeDtypeStruct((B, W, S), BF16),
        scratch_shapes=[pltpu.VMEM((2, nh, LANES, t), BF16)] + _attn_scratch(nh, t),
        compiler_params=_params("parallel", "parallel"),
    )(qt, qx, k, kx, vt, zt)


def _final_proj_kernel(ot_ref, x_ref, w_ref, g_ref, y_ref):
    y = x_ref[0] + _dot_tn(ot_ref[0], w_ref[...])
    y_ref[0] = _rmsnorm(y, g_ref[...])


def _final_proj(ot, x, w_out, final_g):
    B, S, _ = x.shape
    tm = min(OUT_ROWS, S)
    const = lambda b, s: (0, 0)
    tok = pl.BlockSpec((1, tm, D_MODEL), lambda b, s: (b, s, 0))
    return pl.pallas_call(
        _final_proj_kernel,
        name="final_proj",
        grid=(B, S // tm),
        in_specs=[pl.BlockSpec((1, ATTN_WIDTH, tm), lambda b, s: (b, 0, s)), tok,
                  pl.BlockSpec((ATTN_WIDTH, D_MODEL), const), pl.BlockSpec((1, D_MODEL), const)],
        out_specs=tok,
        out_shape=jax.ShapeDtypeStruct((B, S, D_MODEL), F32),
        compiler_params=_params("parallel", "parallel"),
    )(ot, x, w_out, final_g)


_TOK_SEL = 0
_TOK_WIN = KV_WIDTH
_TOK_CMP = 2 * KV_WIDTH
_TOK_COLS = 4 * KV_WIDTH
_FEAT_Q = 0
_FEAT_VSEL = ATTN_WIDTH
_FEAT_VWIN = _FEAT_VSEL + KV_WIDTH
_FEAT_Z = _FEAT_VWIN + KV_WIDTH
_FEAT_GATE = _FEAT_Z + ATTN_WIDTH
_FEAT_ROWS = _FEAT_GATE + N_BRANCH * N_HEADS


def _nsa_in_kernel(ot_ref, wo_ref, x_ref, g_ref, wk_ref, wt_ref, cos_ref, slo_ref, shi_ref,
                   cost_ref, sint_ref,
                   x1_ref, qt_ref, ksel_ref, kwin_ref, chk_ref, chv_ref, vsel_ref, vwin_ref, zt_ref,
                   gt_ref, cmp_scr, *, tk):
    s = pl.program_id(1)
    x1 = x_ref[0] + _dot_tn(ot_ref[0], wo_ref[...])
    x1_ref[0] = x1
    hb = _rmsnorm(x1, g_ref[...]).astype(BF16)
    tm = hb.shape[0]
    G = NSA_GROUPS

    cos, slo, shi = cos_ref[...], slo_ref[...], shi_ref[...]
    tok = s * tm + _iota((tm, LANES), 0)
    lane = _iota((tm, LANES), 1)
    low = lane < HEAD_DIM
    block_id = jnp.where(lane - HEAD_DIM == tok // SEL_LEN, 1.0, 0.0)
    keys = _dot(hb, wk_ref[:, _TOK_SEL:_TOK_CMP])
    for col0, out_ref, extra in ((_TOK_SEL, ksel_ref, block_id), (_TOK_WIN, kwin_ref, 0.0)):
        for gp in range(G // 2):
            pair = _rope_lanes(keys[:, col0 + gp * LANES:col0 + (gp + 1) * LANES], cos, slo, shi)
            out_ref[0, 2 * gp] = jnp.where(low, pair, extra).astype(BF16)
            out_ref[0, 2 * gp + 1] = jnp.where(
                low, pltpu.roll(pair, HEAD_DIM, axis=1), extra).astype(BF16)

    raw = _dot(hb, wk_ref[:, _TOK_CMP:_TOK_COLS])
    for c in range(cmp_scr.shape[0]):
        cmp_scr[c] = raw[:, c * LANES:(c + 1) * LANES]

    cost, sint = cost_ref[...], sint_ref[...]
    for r0 in range(0, ATTN_WIDTH, 512):
        t = _dot_nt(wt_ref[_FEAT_Q + r0:_FEAT_Q + r0 + 512, :], hb)
        for h0 in range(0, 512, HEAD_DIM):
            qt_ref[0, r0 + h0:r0 + h0 + HEAD_DIM, :] = _rope_rows(
                t[h0:h0 + HEAD_DIM], cost, sint).astype(BF16)
        zt_ref[0, r0:r0 + 512, :] = _silu(
            _dot_nt(wt_ref[_FEAT_Z + r0:_FEAT_Z + r0 + 512, :], hb)).astype(BF16)
    for row0, out_ref in ((_FEAT_VSEL, vsel_ref), (_FEAT_VWIN, vwin_ref)):
        t = _dot_nt(wt_ref[row0:row0 + KV_WIDTH, :], hb).astype(BF16)
        for c in range(tm // tk):
            out_ref[0, c] = t[:, c * tk:(c + 1) * tk]
    gate = _dot_nt(wt_ref[_FEAT_GATE:_FEAT_ROWS, :], hb)
    gate = 1.0 / (1.0 + jnp.exp(-gate))
    rows = N_BRANCH * HEADS_PER_STEP
    for hg in range(N_HEADS // HEADS_PER_STEP):
        gt_ref[0, hg] = _pad_rows(gate[hg * rows:(hg + 1) * rows], GATE_ROWS)

    n_ch = tm // CMP_STRIDE
    low_ch = _iota((n_ch, LANES), 1) < HEAD_DIM
    for kind, out_ref in enumerate((chk_ref, chv_ref)):
        for gp in range(G // 2):
            c = kind * (G // 2) + gp
            for l in range(0, CMP_STRIDE, 2):
                a0 = cmp_scr[c, pl.ds(l, n_ch, stride=CMP_STRIDE), :]
                a1 = cmp_scr[c, pl.ds(l + 1, n_ch, stride=CMP_STRIDE), :]
                dst = slice(l * HEAD_DIM, (l + 2) * HEAD_DIM)
                out_ref[0, 2 * gp, :, dst] = jnp.where(
                    low_ch, a0, pltpu.roll(a1, HEAD_DIM, axis=1)).astype(BF16)
                out_ref[0, 2 * gp + 1, :, dst] = jnp.where(
                    low_ch, pltpu.roll(a0, HEAD_DIM, axis=1), a1).astype(BF16)


def _nsa_in(ot, w_out, x, g, w_k, w_t, lane_tables, row_tables):
    B, S, _ = x.shape
    tm = min(PROJ_ROWS, S)
    tk = min(ATTN_TILE, S)
    W, G = ATTN_WIDTH, NSA_GROUPS
    const = lambda b, s: (0, 0)
    feat = lambda rows: pl.BlockSpec((1, rows, tm), lambda b, s: (b, 0, s))
    tiles = pl.BlockSpec((1, tm // tk, KV_WIDTH, tk), lambda b, s: (b, s, 0, 0))
    keys = pl.BlockSpec((1, G, tm, LANES), lambda b, s: (b, 0, s, 0))
    chunk_w = CMP_STRIDE * HEAD_DIM
    chunk = pl.BlockSpec((1, G, tm // CMP_STRIDE, chunk_w), lambda b, s: (b, 0, s, 0))
    tok = pl.BlockSpec((1, tm, D_MODEL), lambda b, s: (b, s, 0))
    n_hg = N_HEADS // HEADS_PER_STEP
    return pl.pallas_call(
        functools.partial(_nsa_in_kernel, tk=tk),
        name="nsa_in",
        grid=(B, S // tm),
        in_specs=[
            feat(W),
            pl.BlockSpec(w_out.shape, const),
            tok,
            pl.BlockSpec((1, D_MODEL), const),
            pl.BlockSpec(w_k.shape, const),
            pl.BlockSpec(w_t.shape, const),
        ] + [pl.BlockSpec((tm, LANES), lambda b, s: (s, 0))] * 3
          + [pl.BlockSpec((ROPE_HALF, tm), lambda b, s: (0, s))] * 2,
        out_specs=[tok, feat(W), keys, keys, chunk, chunk, tiles, tiles, feat(W),
                   pl.BlockSpec((1, n_hg, GATE_ROWS, tm), lambda b, s: (b, 0, 0, s))],
        out_shape=[
            jax.ShapeDtypeStruct((B, S, D_MODEL), F32),
            jax.ShapeDtypeStruct((B, W, S), BF16),
            jax.ShapeDtypeStruct((B, G, S, LANES), BF16),
            jax.ShapeDtypeStruct((B, G, S, LANES), BF16),
            jax.ShapeDtypeStruct((B, G, S // CMP_STRIDE, chunk_w), BF16),
            jax.ShapeDtypeStruct((B, G, S // CMP_STRIDE, chunk_w), BF16),
            jax.ShapeDtypeStruct((B, S // tk, KV_WIDTH, tk), BF16),
            jax.ShapeDtypeStruct((B, S // tk, KV_WIDTH, tk), BF16),
            jax.ShapeDtypeStruct((B, W, S), BF16),
            jax.ShapeDtypeStruct((B, n_hg, GATE_ROWS, S), F32),
        ],
        scratch_shapes=[pltpu.VMEM((2 * KV_WIDTH // LANES, tm, LANES), F32)],
        compiler_params=_params("parallel", "parallel"),
    )(ot, w_out, x, g, w_k, w_t, *lane_tables, *row_tables)


def _compressed_kv(chk_ref, chv_ref, w1k_ref, w2k_ref, pek_ref, w1v_ref, w2vt_ref, pev_ref,
                   cos_ref, slo_ref, shi_ref, pe_scr):
    half = CMP_STRIDE * HEAD_DIM

    @pl.when(jnp.logical_and(pl.program_id(0) == 0, pl.program_id(1) == 0))
    def _():
        pe_scr[0] = _dot(pek_ref[...], w1k_ref[...])
        pe_scr[1] = _dot(pev_ref[...], w1v_ref[...])

    def hidden(ch, w1_ref, pe):
        a = _dot(ch, w1_ref[:half, :])
        b = _dot(ch, w1_ref[half:, :])
        b = pltpu.roll(b, b.shape[0] - 1, axis=0)
        return _silu(a + b + pe[0:1, :]).astype(BF16)

    hid_k = hidden(chk_ref[0, 0], w1k_ref, pe_scr[0])
    hid_v = hidden(chv_ref[0, 0], w1v_ref, pe_scr[1])
    kc = _dot(hid_k, w2k_ref[...])
    kc = _rope_lanes(kc, cos_ref[...], slo_ref[...], shi_ref[...]).astype(BF16)
    return kc, _dot_nt(w2vt_ref[...], hid_v).astype(BF16)


def _local_attn_kernel(q_ref, *refs, t, n_cmp, n_blk, n_back):
    compress_refs, (ovt_ref, kw_ref, vw_ref, oc_ref, bias_ref, ow_ref, pe_scr) = refs[:11], refs[11:]
    n_q = q_ref.shape[2] // t
    heads = range(NSA_REP)
    kc, vct = _compressed_kv(*compress_refs, pe_scr)
    ovt = ovt_ref[...]
    n_pad = kc.shape[0]
    above = _iota((t, t), 0) > _iota((t, t), 1)
    ones = jnp.ones((ONES_ROWS, t), BF16)
    blk_c = _iota((n_pad, t), 0)
    blk = _iota((n_blk, t), 0)
    n_rounds = min(N_SELECT, n_blk)
    per_head = -(-n_rounds // NSA_REP)

    def q_aug(i, r):
        return _pad_rows(q_ref[0, r * HEAD_DIM:(r + 1) * HEAD_DIM, i * t:(i + 1) * t], LANES)

    masks = {0: jnp.logical_not(above), n_back: above}

    def win_tiles(i):
        return [(i - d, masks.get(d)) for d in range(n_back + 1) if i - d >= 0]

    def win_logits(i, r):
        return [_dot(kw_ref[0, 0, _tile(j, t), :], q_aug(i, r)) for j, _ in win_tiles(i)]

    def cmp_logits(i):
        return [_dot(kc, q_aug(i, r)) for r in heads]

    def importance(i, logits):
        cols = slice(i * t, (i + 1) * t)
        qry = i * t + _iota((n_pad, t), 1)
        valid = jnp.logical_and(blk_c * CMP_STRIDE + (CMP_LEN - 1) <= qry, blk_c < n_cmp)
        p_sum = jnp.zeros((n_pad, t), F32)
        safe = min(max((i * t - CMP_LEN) // CMP_STRIDE + 1, 0), n_cmp) // SUBLANES * SUBLANES
        for r in heads:
            s = logits[r]
            if safe:
                s = jnp.concatenate([s[:safe], jnp.where(valid[safe:], s[safe:], NEG)], axis=0)
            else:
                s = jnp.where(valid, s, NEG)
            e = jnp.exp2(s - _col_max(s))
            p = e * (1.0 / _col_sum(e))
            if i * t < CMP_LEN - 1:
                p = jnp.where(valid, p, 0.0)
            p_sum = p_sum + p
            oc_ref[0, r * HEAD_DIM:(r + 1) * HEAD_DIM, cols] = _dot(
                vct, p.astype(BF16)).astype(oc_ref.dtype)
        p_hi, p_lo = _split2(p_sum)
        imp = _dot(ovt, p_hi) + _dot(ovt, p_lo)
        cur = (i * t + _iota(imp.shape, 1)) // SEL_LEN
        forced = jnp.logical_or(blk == 0, jnp.logical_or(blk == cur, blk == cur - 1))
        return jnp.where(forced, FORCE, jnp.where(blk <= cur, imp, -1.0))

    items = [(i, r) for i in range(n_q) for r in heads]
    win_queue, cmp_queue = [], {}

    def issue(n):
        if n < len(items):
            win_queue.append(win_logits(*items[n]))
            if items[n][1] == 0:
                cmp_queue[items[n][0]] = cmp_logits(items[n][0])

    for n in range(LOOKAHEAD):
        issue(n)
    for n, (i, r) in enumerate(items):
        issue(n + LOOKAHEAD)
        logits = win_queue.pop(0)
        if r == 0:
            imp = importance(i, cmp_queue.pop(i))
            done = 0
        tiles = win_tiles(i)
        values = lambda j: jnp.concatenate([vw_ref[0, j], ones], axis=0)
        if len(tiles) == n_back + 1:
            diag = masks[0]
            s_fold = jnp.where(diag, logits[0], logits[-1])
            full = logits[1:-1]
            m = functools.reduce(jnp.maximum, [_col_max(s) for s in [s_fold] + full])
            p_fold = jnp.exp2(s_fold - m)
            zero = jnp.zeros_like(p_fold)
            parts = [(tiles[0][0], jnp.where(diag, p_fold, zero)),
                     (tiles[-1][0], jnp.where(diag, zero, p_fold))]
            parts += [(j, jnp.exp2(s - m)) for s, (j, _) in zip(full, tiles[1:-1])]
        else:
            ss = [s if mask is None else jnp.where(mask, s, NEG)
                  for s, (_, mask) in zip(logits, tiles)]
            m = functools.reduce(jnp.maximum, [_col_max(s) for s in ss])
            parts = [(j, jnp.exp2(s - m)) for s, (j, _) in zip(ss, tiles)]
        acc = functools.reduce(jnp.add, [_dot(values(j), p.astype(BF16)) for j, p in parts])
        ow_ref[0, r * HEAD_DIM:(r + 1) * HEAD_DIM, i * t:(i + 1) * t] = _normalised(acc).astype(
            ow_ref.dtype)
        for _ in range(min(per_head, n_rounds - done)):
            top = _col_max(imp)
            first = jnp.min(jnp.where(imp == top, blk, n_blk), axis=0, keepdims=True)
            pick = blk == first
            imp = jnp.where(pick, -jnp.inf, imp)
            done += 1
        if r == NSA_REP - 1:
            bias_ref[0, 0, :, i * t:(i + 1) * t] = jnp.where(
                imp == -jnp.inf, 0.0, NEG).astype(bias_ref.dtype)


def _local_attn(qt, compress_args, ovt, n_cmp, kw, vw):
    B, W, S = qt.shape
    chk, chv, w1k = compress_args[:3]
    G, n_chunk = chk.shape[1], chk.shape[2]
    chunk_spec = pl.BlockSpec((1, 1, n_chunk, chk.shape[3]), lambda b, g: (b, g, 0, 0))
    whole = lambda a: pl.BlockSpec(a.shape, lambda b, g: (0, 0))
    t = min(ATTN_TILE, S)
    assert WINDOW % t == 0
    n_blk = ovt.shape[0]
    kern = functools.partial(_local_attn_kernel, t=t, n_cmp=n_cmp, n_blk=n_blk, n_back=WINDOW // t)
    feat = pl.BlockSpec((1, W // G, S), lambda b, g: (b, g, 0))
    return pl.pallas_call(
        kern,
        name="nsa_local_attn",
        grid=(B, G),
        in_specs=[feat, chunk_spec, chunk_spec] + [whole(a) for a in compress_args[2:]] + [
            pl.BlockSpec(ovt.shape, lambda b, g: (0, 0)),
            pl.BlockSpec((1, 1, S, LANES), lambda b, g: (b, g, 0, 0)),
            pl.BlockSpec((1, S // t, HEAD_DIM, t), lambda b, g: (b, 0, g, 0)),
        ],
        out_specs=[feat, pl.BlockSpec((1, 1, n_blk, S), lambda b, g: (b, g, 0, 0)), feat],
        out_shape=[
            jax.ShapeDtypeStruct((B, W, S), BF16),
            jax.ShapeDtypeStruct((B, G, n_blk, S), BF16),
            jax.ShapeDtypeStruct((B, W, S), BF16),
        ],
        scratch_shapes=[pltpu.VMEM((2, SUBLANES, w1k.shape[1]), F32)],
        compiler_params=_params("arbitrary", "arbitrary"),
    )(qt, *compress_args, ovt, kw, vw)


def _sel_attn_kernel(q_ref, bias_ref, k_ref, v_ref, oc_ref, ow_ref, gt_ref, z_ref, o_ref,
                     s_scr, acc_scr, *, t):
    nh = s_scr.shape[1]
    n_q = q_ref.shape[2] // t

    def query(i, slot, hh):
        cols = slice(i * t, (i + 1) * t)
        bias = _pad_rows(bias_ref[0, hh // NSA_REP, :, cols], HEAD_DIM)
        return jnp.concatenate([q_ref[0, hh * HEAD_DIM:(hh + 1) * HEAD_DIM, cols], bias], axis=0)

    def key_tile(j, hh):
        return k_ref[0, hh // NSA_REP, _tile(j, t), :]

    def value_rows(j, hh):
        g = hh // NSA_REP
        return v_ref[0, j, g * HEAD_DIM:(g + 1) * HEAD_DIM, :]

    def emit(i, accs):
        cols = slice(i * t, (i + 1) * t)
        for hh in range(nh):
            rows = slice(hh * HEAD_DIM, (hh + 1) * HEAD_DIM)
            gate = lambda b: gt_ref[0, 0, N_BRANCH * hh + b:N_BRANCH * hh + b + 1, cols]
            o = (gate(0) * oc_ref[0, rows, cols].astype(F32) + gate(1) * _normalised(accs[hh])
                 + gate(2) * ow_ref[0, rows, cols].astype(F32))
            o_ref[0, rows, cols] = (o * z_ref[0, rows, cols].astype(F32)).astype(o_ref.dtype)

    _causal_attention(n_q, t, lambda i, slot: None, query, key_tile, value_rows, emit,
                      s_scr, acc_scr)


def _sel_attn(qt, bias, k_aug, vt, oc, ow, gt, zt):
    B, W, S = qt.shape
    t = min(ATTN_TILE, S)
    ng = HEADS_PER_STEP // NSA_REP
    n_blk = bias.shape[2]
    feat = pl.BlockSpec((1, HEADS_PER_STEP * HEAD_DIM, S), lambda b, g: (b, g, 0))
    return pl.pallas_call(
        functools.partial(_sel_attn_kernel, t=t),
        name="nsa_sel_attn",
        grid=(B, NSA_GROUPS // ng),
        in_specs=[
            feat,
            pl.BlockSpec((1, ng, n_blk, S), lambda b, g: (b, g, 0, 0)),
            pl.BlockSpec((1, ng, S, LANES), lambda b, g: (b, g, 0, 0)),
            pl.BlockSpec((1, S // t, ng * HEAD_DIM, t), lambda b, g: (b, 0, g, 0)),
            feat, feat,
            pl.BlockSpec((1, 1, GATE_ROWS, S), lambda b, g: (b, g, 0, 0)),
            feat,
        ],
        out_specs=feat,
        out_shape=jax.ShapeDtypeStruct((B, W, S), BF16),
        scratch_shapes=_attn_scratch(HEADS_PER_STEP, t),
        compiler_params=_params("parallel", "parallel"),
    )(qt, bias, k_aug, vt, oc, ow, gt, zt)


def _fox_mixer(x, g, w_in, b_f):
    B, S, _ = x.shape
    W, H = ATTN_WIDTH, N_HEADS
    w_k = w_in[:, W:2 * W].astype(BF16)
    w_t = jnp.concatenate([w_in[:, :W] * Q_SCALE, w_in[:, 2 * W:3 * W], w_in[:, 3 * W + H:]],
                          axis=1).T.astype(BF16)
    wf_hi, wf_lo = _split2(w_in[:, 3 * W:3 * W + H])
    pad = lambda n: jnp.zeros((D_MODEL, LANES - n * H), BF16)
    wf_a = jnp.concatenate([wf_hi] * 3 + [wf_lo] * 3 + [pad(6)], axis=1)
    wf_b = jnp.concatenate([wf_hi] * 3 + [pad(3)], axis=1)
    bf_row = jnp.concatenate([b_f, b_f, b_f, jnp.zeros((LANES - 3 * H,), F32)])[None, :]

    qt, k, vt, zt, cx = _fox_in(x, g[None, :], w_k, w_t, wf_a, wf_b, bf_row)

    t = min(ATTN_TILE, S)
    r = np.arange(LANES)
    sel = np.logical_and(r[None, :] % H == np.arange(H)[:, None], r[None, :] < 3 * H)
    qx = np.broadcast_to(np.where(sel, -1.0, 0.0).astype(np.float32)[:, :, None], (H, LANES, t))
    return _fox_attn(qt, jnp.asarray(qx, BF16), k, cx, vt, zt)


def _nsa_layer(ot0, w_out0, x, g, w_in, pe_k, w_ck1, w_ck2, pe_v, w_cv1, w_cv2, w_out, final_g):
    B, S, _ = x.shape
    W, KV, G = ATTN_WIDTH, KV_WIDTH, NSA_GROUPS
    sec = lambda j: w_in[:, W + j * KV:W + (j + 1) * KV]
    gate_off = W + 6 * KV
    n_gate = N_BRANCH * N_HEADS
    w_k = jnp.concatenate([sec(2), sec(4), sec(0), sec(1)], axis=1).astype(BF16)
    w_t = jnp.concatenate([w_in[:, :W] * Q_SCALE, sec(3), sec(5), w_in[:, gate_off + n_gate:],
                           w_in[:, gate_off:gate_off + n_gate]], axis=1).T.astype(BF16)
    pos = np.arange(S)
    cos, sin = _rope_angles(pos)
    x, qt, ksel, kwin, chk, chv, vsel, vwin, zt, gt = _nsa_in(
        ot0, w_out0.astype(BF16), x, g[None, :], w_k, w_t, _rope_lane_tables(pos),
        (np.ascontiguousarray(cos.T), np.ascontiguousarray(sin.T)))

    n_chunk = S // CMP_STRIDE
    n_cmp = n_chunk - CMP_LEN // CMP_STRIDE + 1

    def flat_pe(pe):
        pe = pe.reshape(1, CMP_LEN * HEAD_DIM)
        return jnp.broadcast_to(pe, (SUBLANES, CMP_LEN * HEAD_DIM)).astype(BF16)

    w2k = jnp.concatenate([w_ck2, jnp.zeros_like(w_ck2)], axis=1).astype(BF16)
    cmp_end = np.arange(n_chunk) * CMP_STRIDE + CMP_LEN - 1
    compress_args = (chk, chv, w_ck1.astype(BF16), w2k, flat_pe(pe_k), w_cv1.astype(BF16),
                     w_cv2.T.astype(BF16), flat_pe(pe_v), *_rope_lane_tables(cmp_end))

    n_blk = S // SEL_LEN
    ci = np.arange(n_chunk) * CMP_STRIDE
    sj = np.arange(n_blk) * SEL_LEN
    ovt = np.logical_and(ci[None, :] < sj[:, None] + SEL_LEN, ci[None, :] + CMP_LEN > sj[:, None])
    ovt = np.logical_and(ovt, np.arange(n_chunk)[None, :] < n_cmp).astype(np.float32)
    oc, bias, ow = _local_attn(qt, compress_args, jnp.asarray(ovt, BF16), n_cmp, kwin, vwin)
    ot = _sel_attn(qt, bias, ksel, vsel, oc, ow, gt, zt)
    return _final_proj(ot, x, w_out.astype(BF16), final_g[None, :])


def kernel(x, norm_g, fox_w_in, fox_b_f, fox_w_out, nsa_w_in, nsa_pe_k, nsa_w_ck1, nsa_w_ck2,
           nsa_pe_v, nsa_w_cv1, nsa_w_cv2, nsa_w_out, final_g):
    ot0 = _fox_mixer(x, norm_g[0], fox_w_in[0], fox_b_f[0])
    return _nsa_layer(ot0, fox_w_out[0], x, norm_g[1], nsa_w_in[0], nsa_pe_k[0], nsa_w_ck1[0],
                      nsa_w_ck2[0], nsa_pe_v[0], nsa_w_cv1[0], nsa_w_cv2[0], nsa_w_out[0], final_g)
```
